```python
import math
import jax, jax.numpy as jnp
from jax import lax
import numpy as np

D_MODEL = 1024
BATCH = 2
SEQ = 8192
DEPTH = 2

GRID_W = 64
CTX_LEN = 256
N_HEADS = 8
N_KV_HEADS = 2
HEAD_DIM = 64
Q_GROUP = N_HEADS // N_KV_HEADS
WINDOW = 128
ATTN_BLOCK = 128
ROPE_THETA = 10000.0
D_SSM = D_MODEL // 2
SSM_GROUP = 16
N_SSM_GROUPS = D_SSM // SSM_GROUP
SSM_STATE = 64
N_EXPERT_GROUPS = 4
EXPERTS_PER_GROUP = 8
N_EXPERTS = N_EXPERT_GROUPS * EXPERTS_PER_GROUP
TOP_K = 2
D_EXPERT = D_MODEL // 2
MOE_BLOCK = 128
Q_W = N_HEADS * HEAD_DIM
KV_W = N_KV_HEADS * HEAD_DIM
D_IN = Q_W + 2 * KV_W + D_SSM + 2 * D_MODEL
EPS = 1e-6
NEG_INF = -1e30
F32 = jnp.float32

kernel_name = 'hybrid_s5_swa_hmoe_prefix'


def rms_norm(x, g):
    xf = x.astype(F32)
    y = xf * lax.rsqrt(jnp.mean(xf * xf, axis=-1, keepdims=True) + EPS)
    return (y * g.astype(F32)).astype(x.dtype)


def modulate(h, shift, scale):
    return h * (1 + scale) + shift


def rope_2d(x, rows, cols):
    half = HEAD_DIM // 2
    quarter = half // 2
    freqs = ROPE_THETA ** (-jnp.arange(quarter, dtype=F32) / quarter)

    def rot(xa, pos):
        ang = pos[:, None] * freqs[None, :]
        cos = jnp.cos(ang)[None, :, None, :].astype(x.dtype)
        sin = jnp.sin(ang)[None, :, None, :].astype(x.dtype)
        x1, x2 = xa[..., :quarter], xa[..., quarter:]
        return jnp.concatenate([x1 * cos - x2 * sin, x1 * sin + x2 * cos], axis=-1)

    return jnp.concatenate([rot(x[..., :half], rows), rot(x[..., half:], cols)], axis=-1)


def latent_attention(q, k, v, kc, vc, sink):
    b, s = q.shape[0], q.shape[1]
    nb = s // ATTN_BLOCK
    scale = HEAD_DIM ** -0.5
    qb = q.reshape(b, nb, ATTN_BLOCK, N_KV_HEADS, Q_GROUP, HEAD_DIM)

    def band(t):
        tp = jnp.pad(t, ((0, 0), (ATTN_BLOCK, ATTN_BLOCK), (0, 0), (0, 0)))
        tp = tp.reshape(b, nb + 2, ATTN_BLOCK, N_KV_HEADS, HEAD_DIM)
        return jnp.concatenate([tp[:, :-2], tp[:, 1:-1], tp[:, 2:]], axis=2)

    kb, vb = band(k), band(v)
    s_loc = jnp.einsum('bnqhgd,bnjhd->bnhgqj', qb, kb).astype(F32) * scale
    s_ctx = jnp.einsum('bnqhgd,bchd->bnhgqc', qb, kc).astype(F32) * scale
    qpos = jnp.arange(nb)[:, None] * ATTN_BLOCK + jnp.arange(ATTN_BLOCK)[None, :]
    kpos = (jnp.arange(nb)[:, None] - 1) * ATTN_BLOCK + jnp.arange(3 * ATTN_BLOCK)[None, :]
    kp = kpos[:, None, :]
    valid = (jnp.abs(qpos[:, :, None] - kp) <= WINDOW) & (kp >= 0) & (kp < s)
    s_loc = jnp.where(valid[None, :, None, None], s_loc, NEG_INF)
    sink_l = jnp.broadcast_to(sink.astype(F32).reshape(N_KV_HEADS, Q_GROUP, 1, 1), s_loc.shape[:-1] + (1,))
    p = jax.nn.softmax(jnp.concatenate([s_loc, s_ctx, sink_l], axis=-1), axis=-1).astype(v.dtype)
    n_loc = 3 * ATTN_BLOCK
    n_ctx = kc.shape[1]
    o = (jnp.einsum('bnhgqj,bnjhd->bnqhgd', p[..., :n_loc], vb)
         + jnp.einsum('bnhgqc,bchd->bnqhgd', p[..., n_loc:n_loc + n_ctx], vc))
    return o.reshape(b, s, Q_W)


def context_attention(q, k, v, sink):
    b, n = q.shape[0], q.shape[1]
    qg = q.reshape(b, n, N_KV_HEADS, Q_GROUP, HEAD_DIM)
    sc = jnp.einsum('bqhgd,bjhd->bhgqj', qg, k).astype(F32) * HEAD_DIM ** -0.5
    sink_c = jnp.broadcast_to(sink.astype(F32).reshape(N_KV_HEADS, Q_GROUP, 1, 1), sc.shape[:-1] + (1,))
    p = jax.nn.softmax(jnp.concatenate([sc, sink_c], axis=-1), axis=-1)[..., :-1].astype(v.dtype)
    return jnp.einsum('bhgqj,bjhd->bqhgd', p, v).reshape(b, n, Q_W)


def ssm_discretise(lam_re, lam_im, log_dt, b_re, b_im):
    lr, li = lam_re.astype(F32), lam_im.astype(F32)
    dt = jnp.exp(log_dt.astype(F32))[:, None]
    mag = jnp.exp(lr * dt)
    a_re = mag * jnp.cos(li * dt)
    a_im = mag * jnp.sin(li * dt)
    den = lr * lr + li * li
    nr = a_re - 1.0
    f_re = (nr * lr + a_im * li) / den
    f_im = (a_im * lr - nr * li) / den
    br, bi = b_re.astype(F32), b_im.astype(F32)
    bb_re = f_re[..., None] * br - f_im[..., None] * bi
    bb_im = f_re[..., None] * bi + f_im[..., None] * br
    return a_re, a_im, bb_re, bb_im


def complex_affine_combine(e1, e2):
    a1r, a1i, b1r, b1i = e1
    a2r, a2i, b2r, b2i = e2
    return (a2r * a1r - a2i * a1i, a2r * a1i + a2i * a1r,
            a2r * b1r - a2i * b1i + b2r, a2r * b1i + a2i * b1r + b2i)


def ssm_scan(u, a_re, a_im, bb_re, bb_im, h0, reverse):
    bu_re = jnp.einsum('blgm,gpm->blgp', u, bb_re)
    bu_im = jnp.einsum('blgm,gpm->blgp', u, bb_im)
    if h0 is not None:
        h_re, h_im = h0
        idx = -1 if reverse else 0
        bu_re = bu_re.at[:, idx].add(a_re * h_re - a_im * h_im)
        bu_im = bu_im.at[:, idx].add(a_re * h_im + a_im * h_re)
    ar = jnp.broadcast_to(a_re, bu_re.shape)
    ai = jnp.broadcast_to(a_im, bu_im.shape)
    _, _, x_re, x_im = lax.associative_scan(complex_affine_combine, (ar, ai, bu_re, bu_im), reverse=reverse, axis=1)
    return x_re, x_im


def ssm_readout(x_re, x_im, c_re, c_im):
    return (jnp.einsum('blgp,gmp->blgm', x_re, c_re.astype(F32))
            - jnp.einsum('blgp,gmp->blgm', x_im, c_im.astype(F32)))


def ssm_branch(u, uc, ctx_out, lam_re, lam_im, log_dt, b_re, b_im, c_re, c_im, d_skip, w_glu):
    b, s, _ = u.shape
    n_ctx = uc.shape[1]
    ug = u.astype(F32).reshape(b, s, N_SSM_GROUPS, SSM_GROUP)
    ucg = uc.astype(F32).reshape(b, n_ctx, N_SSM_GROUPS, SSM_GROUP)
    dg = d_skip.astype(F32).reshape(N_SSM_GROUPS, SSM_GROUP)
    y = ug * dg
    yc = ucg * dg if ctx_out else None
    for direction, reverse in ((0, False), (1, True)):
        a_re, a_im, bb_re, bb_im = ssm_discretise(lam_re[direction], lam_im[direction], log_dt[direction],
                                                  b_re[direction], b_im[direction])
        xc_re, xc_im = ssm_scan(ucg, a_re, a_im, bb_re, bb_im, None, reverse)
        end = 0 if reverse else -1
        x_re, x_im = ssm_scan(ug, a_re, a_im, bb_re, bb_im, (xc_re[:, end], xc_im[:, end]), reverse)
        y = y + ssm_readout(x_re, x_im, c_re[direction], c_im[direction])
        if ctx_out:
            yc = yc + ssm_readout(xc_re, xc_im, c_re[direction], c_im[direction])

    def glu(z):
        z = jax.nn.gelu(z.reshape(z.shape[0], z.shape[1], D_SSM)).astype(u.dtype)
        return z * jax.nn.sigmoid(z @ w_glu)

    return glu(y), (glu(yc) if ctx_out else None)


def mixer(h, hc, rows, cols, ctx_out, w_in, sink, lam_re, lam_im, log_dt, b_re, b_im, c_re, c_im,
          d_skip, w_glu, w_br_attn, w_br_ssm, w_out):
    b, s, _ = h.shape
    n_ctx = hc.shape[1]
    o_k = Q_W
    o_v = o_k + KV_W
    o_u = o_v + KV_W
    o_ga = o_u + D_SSM
    o_gs = o_ga + D_MODEL
    p = h @ w_in
    q = rope_2d(p[..., :o_k].reshape(b, s, N_HEADS, HEAD_DIM), rows, cols)
    k = rope_2d(p[..., o_k:o_v].reshape(b, s, N_KV_HEADS, HEAD_DIM), rows, cols)
    v = p[..., o_v:o_u].reshape(b, s, N_KV_HEADS, HEAD_DIM)
    u = p[..., o_u:o_ga]
    ga = p[..., o_ga:o_gs]
    gs = p[..., o_gs:]
    w_ctx = w_in if ctx_out else w_in[:, o_k:o_ga]
    off = 0 if ctx_out else o_k
    pc = hc @ w_ctx
    kc = pc[..., o_k - off:o_v - off].reshape(b, n_ctx, N_KV_HEADS, HEAD_DIM)
    vc = pc[..., o_v - off:o_u - off].reshape(b, n_ctx, N_KV_HEADS, HEAD_DIM)
    uc = pc[..., o_u - off:o_ga - off]
    attn = latent_attention(q, k, v, kc, vc, sink)
    ssm, ssm_c = ssm_branch(u, uc, ctx_out, lam_re, lam_im, log_dt, b_re, b_im, c_re, c_im, d_skip, w_glu)
    y = (jax.nn.sigmoid(ga) * (attn @ w_br_attn) + jax.nn.sigmoid(gs) * (ssm @ w_br_ssm)) @ w_out
    if not ctx_out:
        return y, None
    qc = pc[..., :o_k].reshape(b, n_ctx, N_HEADS, HEAD_DIM)
    attn_c = context_attention(qc, kc, vc, sink)
    yc = (jax.nn.sigmoid(pc[..., o_ga:o_gs]) * (attn_c @ w_br_attn)
          + jax.nn.sigmoid(pc[..., o_gs:]) * (ssm_c @ w_br_ssm)) @ w_out
    return y, yc


def hier_moe(h, w_rg, w_re, w_gate, w_up, w_down):
    t, d = h.shape
    p_grp = jax.nn.softmax((h @ w_rg).astype(F32), axis=-1)
    g_prob, g_idx = lax.top_k(p_grp, 1)
    logits_e = (h @ w_re).astype(F32).reshape(t, N_EXPERT_GROUPS, EXPERTS_PER_GROUP)
    logits_sel = jnp.take_along_axis(logits_e, g_idx[:, :, None], axis=1)[:, 0]
    e_prob, e_idx = lax.top_k(jax.nn.softmax(logits_sel, axis=-1), TOP_K)
    e_prob = e_prob / jnp.sum(e_prob, axis=-1, keepdims=True)
    weight = (g_prob * e_prob).reshape(-1)
    expert = (g_idx * EXPERTS_PER_GROUP + e_idx).reshape(-1)
    token = jnp.repeat(jnp.arange(t), TOP_K)
    n_assign = t * TOP_K
    order = jnp.argsort(expert)
    s_exp, s_tok, s_w = expert[order], token[order], weight[order]
    counts = jnp.bincount(expert, length=N_EXPERTS)
    start = jnp.cumsum(counts) - counts
    pcounts = (counts + MOE_BLOCK - 1) // MOE_BLOCK * MOE_BLOCK
    pend = jnp.cumsum(pcounts)
    pstart = pend - pcounts
    dest = pstart[s_exp] + jnp.arange(n_assign) - start[s_exp]
    n_pad = -(-n_assign // MOE_BLOCK) * MOE_BLOCK + N_EXPERTS * MOE_BLOCK
    n_blk = n_pad // MOE_BLOCK
    buf_tok = jnp.full((n_pad,), t, jnp.int32).at[dest].set(s_tok.astype(jnp.int32))
    buf_w = jnp.zeros((n_pad,), h.dtype).at[dest].set(s_w.astype(h.dtype))
    blk_exp = jnp.minimum(jnp.searchsorted(pend, jnp.arange(n_blk) * MOE_BLOCK, side='right'), N_EXPERTS - 1)
    h_pad = jnp.concatenate([h, jnp.zeros((1, d), h.dtype)], axis=0)
    xb = h_pad[buf_tok].reshape(n_blk, MOE_BLOCK, d)

    def expert_block(args):
        xblk, e = args
        return (jax.nn.silu(xblk @ w_gate[e]) * (xblk @ w_up[e])) @ w_down[e]

    yb = lax.map(expert_block, (xb, blk_exp)).reshape(n_pad, d)
    out = jax.ops.segment_sum(yb * buf_w[:, None], buf_tok, num_segments=t + 1)
    return out[:t]


def setup_inputs(seed: int = 0) -> dict:
    key = jax.random.key(seed)
    ks = iter(jax.random.split(key, 40))
    nrm = lambda shape: jax.random.normal(next(ks), shape, F32)
    L2 = (DEPTH, 2, N_SSM_GROUPS)
    return {
        'x': nrm((BATCH, SEQ, D_MODEL)),
        'c': nrm((BATCH, D_MODEL)),
        'ctx': nrm((BATCH, CTX_LEN, D_MODEL)),
        'c_ctx': nrm((D_MODEL,)),
        'w_mod': nrm((DEPTH, D_MODEL, 6 * D_MODEL)) * (0.5 * D_MODEL ** -0.5),
        'b_mod': nrm((DEPTH, 6 * D_MODEL)) * 0.01,
        'g_norm1': 1.0 + 0.01 * nrm((DEPTH, D_MODEL)),
        'g_norm2': 1.0 + 0.01 * nrm((DEPTH, D_MODEL)),
        'w_in': nrm((DEPTH, D_MODEL, D_IN)) * D_MODEL ** -0.5,
        'attn_sink': 0.1 * nrm((DEPTH, N_HEADS)),
        'ssm_lam_re': -0.5 + 0.01 * nrm(L2 + (SSM_STATE,)),
        'ssm_lam_im': jnp.pi * jnp.arange(SSM_STATE, dtype=F32) + 0.01 * nrm(L2 + (SSM_STATE,)),
        'ssm_log_dt': jax.random.uniform(next(ks), L2, F32, minval=math.log(1e-3), maxval=math.log(1e-1)),
        'ssm_b_re': nrm(L2 + (SSM_STATE, SSM_GROUP)) * (2 * SSM_GROUP) ** -0.5,
        'ssm_b_im': nrm(L2 + (SSM_STATE, SSM_GROUP)) * (2 * SSM_GROUP) ** -0.5,
        'ssm_c_re': nrm(L2 + (SSM_GROUP, SSM_STATE)) * SSM_STATE ** -0.5,
        'ssm_c_im': nrm(L2 + (SSM_GROUP, SSM_STATE)) * SSM_STATE ** -0.5,
        'ssm_d': nrm((DEPTH, D_SSM)),
        'w_glu': nrm((DEPTH, D_SSM, D_SSM)) * D_SSM ** -0.5,
        'w_br_attn': nrm((DEPTH, Q_W, D_MODEL)) * Q_W ** -0.5,
        'w_br_ssm': nrm((DEPTH, D_SSM, D_MODEL)) * D_SSM ** -0.5,
        'w_out': nrm((DEPTH, D_MODEL, D_MODEL)) * D_MODEL ** -0.5,
        'w_router_group': nrm((DEPTH, D_MODEL, N_EXPERT_GROUPS)) * D_MODEL ** -0.5,
        'w_router_expert': nrm((DEPTH, D_MODEL, N_EXPERTS)) * D_MODEL ** -0.5,
        'w_exp_gate': nrm((DEPTH, N_EXPERTS, D_MODEL, D_EXPERT)) * D_MODEL ** -0.5,
        'w_exp_up': nrm((DEPTH, N_EXPERTS, D_MODEL, D_EXPERT)) * D_MODEL ** -0.5,
        'w_exp_down': nrm((DEPTH, N_EXPERTS, D_EXPERT, D_MODEL)) * D_EXPERT ** -0.5,
        'g_final': 1.0 + 0.01 * nrm((D_MODEL,)),
    }


def reference(x, c, ctx, c_ctx, w_mod, b_mod, g_norm1, g_norm2, w_in, attn_sink, ssm_lam_re, ssm_lam_im,
              ssm_log_dt, ssm_b_re, ssm_b_im, ssm_c_re, ssm_c_im, ssm_d, w_glu, w_br_attn, w_br_ssm, w_out,
              w_router_group, w_router_expert, w_exp_gate, w_exp_up, w_exp_down, g_final):
    b, s, d = x.shape
    n_rows = s // GRID_W
    rows = jnp.repeat(jnp.arange(n_rows), GRID_W).astype(F32)
    cols = jnp.tile(jnp.arange(GRID_W), n_rows).astype(F32)
    silu_c = jax.nn.silu(c)
    silu_cc = jax.nn.silu(c_ctx)
    for l in range(DEPTH):
        ctx_out = l < DEPTH - 1
        mod = silu_c @ w_mod[l] + b_mod[l]
        mod_c = silu_cc @ w_mod[l] + b_mod[l]
        sh1, sc1, gt1, sh2, sc2, gt2 = jnp.split(mod[:, None, :], 6, axis=-1)
        csh1, csc1, cgt1, csh2, csc2, cgt2 = jnp.split(mod_c, 6, axis=-1)
        h = modulate(rms_norm(x, g_norm1[l]), sh1, sc1)
        hc = modulate(rms_norm(ctx, g_norm1[l]), csh1, csc1)
        y, yc = mixer(h, hc, rows, cols, ctx_out, w_in[l], attn_sink[l], ssm_lam_re[l], ssm_lam_im[l],
                      ssm_log_dt[l], ssm_b_re[l], ssm_b_im[l], ssm_c_re[l], ssm_c_im[l], ssm_d[l], w_glu[l],
                      w_br_attn[l], w_br_ssm[l], w_out[l])
        x = x + gt1 * y
        h2 = modulate(rms_norm(x, g_norm2[l]), sh2, sc2).reshape(b * s, d)
        if ctx_out:
            ctx = ctx + cgt1 * yc
            hc2 = modulate(rms_norm(ctx, g_norm2[l]), csh2, csc2).reshape(-1, d)
            f = hier_moe(jnp.concatenate([h2, hc2], axis=0), w_router_group[l], w_router_expert[l],
                         w_exp_gate[l], w_exp_up[l], w_exp_down[l])
            x = x + gt2 * f[:b * s].reshape(b, s, d)
            ctx = ctx + cgt2 * f[b * s:].reshape(ctx.shape)
        else:
            f = hier_moe(h2, w_router_group[l], w_router_expert[l], w_exp_gate[l], w_exp_up[l], w_exp_down[l])
            x = x + gt2 * f.reshape(b, s, d)
    return rms_norm(x, g_final)
```

```python
import functools
import math

import jax
import jax.numpy as jnp
from jax import lax
from jax.experimental import pallas as pl
from jax.experimental.pallas import tpu as pltpu

F32 = jnp.float32
BF16 = jnp.bfloat16

D_MODEL = 1024
GRID_W = 64
N_HEADS = 8
N_KV_HEADS = 2
HEAD_DIM = 64
Q_GROUP = N_HEADS // N_KV_HEADS
ATTN_BLOCK = 128
ROPE_THETA = 10000.0
D_SSM = D_MODEL // 2
SSM_GROUP = 16
N_SSM_GROUPS = D_SSM // SSM_GROUP
SSM_STATE = 64
N_EXPERT_GROUPS = 4
EXPERTS_PER_GROUP = 8
N_EXPERTS = N_EXPERT_GROUPS * EXPERTS_PER_GROUP
TOP_K = 2
D_EXPERT = D_MODEL // 2
Q_W = N_HEADS * HEAD_DIM
KV_W = N_KV_HEADS * HEAD_DIM
O_K = Q_W
O_V = O_K + KV_W
O_U = O_V + KV_W
O_GA = O_U + D_SSM
O_GS = O_GA + D_MODEL
D_IN = O_GS + D_MODEL
EPS = 1e-6
NEG_INF = -1e30

ROW_TILE = 256
SSM_CHUNK = 16
SSM_CW = SSM_CHUNK * SSM_GROUP
SSM_SCAN_GROUPS = 8
MOE_BLOCK = 128
ROUTER_PAD = 128
MOD_ROWS = 8
VMEM_LIMIT = 48 * 1024 * 1024


def _cparams(sem):
    return pltpu.CompilerParams(dimension_semantics=sem, vmem_limit_bytes=VMEM_LIMIT)


def _dot(a, b):
    return jnp.dot(a, b, preferred_element_type=F32)


def _split_bf16(a):
    hi = a.astype(BF16)
    lo = (a - hi.astype(F32)).astype(BF16)
    return hi, lo


def _rms_mod(x, g, shift, scale):
    y = x * lax.rsqrt(jnp.mean(x * x, axis=-1, keepdims=True) + EPS) * g
    return y * (1.0 + scale) + shift


def _mod_kernel(c_ref, w_ref, b_ref, o_ref):
    c = c_ref[...]
    s_hi, s_lo = _split_bf16(c * jax.nn.sigmoid(c))
    w_hi, w_lo = _split_bf16(w_ref[0])
    o_ref[0] = _dot(s_hi, w_hi) + _dot(s_lo, w_hi) + _dot(s_hi, w_lo) + b_ref[0]


def _modulation(c_rows, w_mod, b_mod):
    depth, d, n = w_mod.shape
    nb = n // 4
    return pl.pallas_call(
        _mod_kernel,
        grid=(depth, n // nb),
        in_specs=[
            pl.BlockSpec((MOD_ROWS, d), lambda l, j: (0, 0)),
            pl.BlockSpec((1, d, nb), lambda l, j: (l, 0, j)),
            pl.BlockSpec((1, 1, nb), lambda l, j: (l, 0, j)),
        ],
        out_specs=pl.BlockSpec((1, MOD_ROWS, nb), lambda l, j: (l, 0, j)),
        out_shape=jax.ShapeDtypeStruct((depth, MOD_ROWS, n), F32),
        compiler_params=_cparams(("arbitrary", "arbitrary")),
        name="modulation",
    )(c_rows, w_mod, b_mod.reshape(depth, 1, n))


def _inproj_kernel(has_f, *refs):
    if has_f:
        (x_ref, f_ref, modp_ref, mod_ref, g_ref, cos_ref, sin_ref, w_ref,
         xo_ref, q_ref, k_ref, v_ref, u_ref, ga_ref, gs_ref) = refs
        x = x_ref[...] + modp_ref[0, 5:6, :] * f_ref[...]
        xo_ref[...] = x
    else:
        (x_ref, mod_ref, g_ref, cos_ref, sin_ref, w_ref,
         q_ref, k_ref, v_ref, u_ref, ga_ref, gs_ref) = refs
        x = x_ref[...]
    m = mod_ref[0]
    h = _rms_mod(x, g_ref[...], m[0:1], m[1:2]).astype(BF16)
    cos = cos_ref[...]
    sin = sin_ref[...]
    lane = lax.broadcasted_iota(jnp.int32, cos.shape, 1)
    first = (lane % (HEAD_DIM // 2)) < (HEAD_DIM // 4)

    def rope(t):
        sw = jnp.where(first, pltpu.roll(t, 128 - HEAD_DIM // 4, 1), pltpu.roll(t, HEAD_DIM // 4, 1))
        return t * cos + sw * sin

    def proj(lo, hi):
        return _dot(h, w_ref[:, lo:hi])

    q = proj(0, O_K)
    for j in range(Q_W // 128):
        q_ref[:, 128 * j:128 * (j + 1)] = (rope(q[:, 128 * j:128 * (j + 1)]) * HEAD_DIM ** -0.5).astype(BF16)
    k_ref[...] = rope(proj(O_K, O_V)).astype(BF16)
    v_ref[...] = proj(O_V, O_U).astype(BF16)
    u_ref[...] = proj(O_U, O_GA).astype(BF16)
    ga_ref[...] = jax.nn.sigmoid(proj(O_GA, O_GS)).astype(BF16)
    gs_ref[...] = jax.nn.sigmoid(proj(O_GS, D_IN)).astype(BF16)


def _inproj(x_all, f_all, mod_prev, mod, g1, cos_t, sin_t, w_in, n_tiles, tiles_per_seq, n_lat_tiles, batch):
    t_all, d = x_all.shape
    has_f = f_all is not None
    row = lambda i: (i, 0)
    modi = lambda i: (jnp.minimum(i // tiles_per_seq, batch), 0, 0)
    const = lambda i: (0, 0)
    ropei = lambda i: (jnp.where(i < n_lat_tiles, i % tiles_per_seq, tiles_per_seq), 0)
    in_specs = [pl.BlockSpec((ROW_TILE, d), row)]
    args = [x_all]
    if has_f:
        in_specs += [pl.BlockSpec((ROW_TILE, d), row), pl.BlockSpec((1, 6, d), modi)]
        args += [f_all, mod_prev]
    in_specs += [
        pl.BlockSpec((1, 6, d), modi),
        pl.BlockSpec((1, d), const),
        pl.BlockSpec((ROW_TILE, 128), ropei),
        pl.BlockSpec((ROW_TILE, 128), ropei),
        pl.BlockSpec((d, D_IN), const),
    ]
    args += [mod, g1, cos_t, sin_t, w_in]
    widths = [Q_W, KV_W, KV_W, D_SSM, D_MODEL, D_MODEL]
    out_specs = [pl.BlockSpec((ROW_TILE, w), row) for w in widths]
    out_shape = [jax.ShapeDtypeStruct((n_tiles * ROW_TILE, w), BF16) for w in widths]
    if has_f:
        out_specs = [pl.BlockSpec((ROW_TILE, d), row)] + out_specs
        out_shape = [jax.ShapeDtypeStruct((n_tiles * ROW_TILE, d), F32)] + out_shape
    outs = pl.pallas_call(
        functools.partial(_inproj_kernel, has_f),
        grid=(n_tiles,),
        in_specs=in_specs,
        out_specs=out_specs,
        out_shape=out_shape,
        compiler_params=_cparams(("parallel",)),
        name="inproj",
    )(*args)
    if has_f:
        return outs[0], outs[1:]
    return x_all, outs


def _attn_kernel(band, nb_per_seq, sink_ref, *refs):
    if band:
        q_ref, kp_ref, kc_ref, kn_ref, vp_ref, vc_ref, vn_ref, kx_ref, vx_ref, o_ref = refs
    else:
        q_ref, kx_ref, vx_ref, o_ref = refs
    blk = q_ref.shape[0]
    if band:
        j = pl.program_id(0) % nb_per_seq
        r = lax.broadcasted_iota(jnp.int32, (blk, blk), 0)
        c = lax.broadcasted_iota(jnp.int32, (blk, blk), 1)
        edge_p = jnp.where(j > 0, 0.0, NEG_INF).astype(F32)
        edge_n = jnp.where(j < nb_per_seq - 1, 0.0, NEG_INF).astype(F32)
        bias_p = jnp.where(c >= r, edge_p, NEG_INF).astype(F32)
        bias_n = jnp.where(c <= r, edge_n, NEG_INF).astype(F32)
        bias = jnp.concatenate(
            [bias_p, jnp.zeros((blk, blk), F32), bias_n, jnp.zeros((blk, kx_ref.shape[0]), F32)], axis=1)
    outs = []
    for g in range(N_KV_HEADS):
        gs = slice(g * HEAD_DIM, (g + 1) * HEAD_DIM)
        if band:
            k_all = jnp.concatenate([kp_ref[:, gs], kc_ref[:, gs], kn_ref[:, gs], kx_ref[:, gs]], axis=0)
            v_all = jnp.concatenate([vp_ref[:, gs], vc_ref[:, gs], vn_ref[:, gs], vx_ref[:, gs]], axis=0)
        else:
            k_all = kx_ref[:, gs]
            v_all = vx_ref[:, gs]
        qg = jnp.concatenate(
            [q_ref[:, (g * Q_GROUP + h) * HEAD_DIM:(g * Q_GROUP + h + 1) * HEAD_DIM] for h in range(Q_GROUP)],
            axis=0)
        s_all = lax.dot_general(qg, k_all, (((1,), (1,)), ((), ())), preferred_element_type=F32)
        for h in range(Q_GROUP):
            s = s_all[h * blk:(h + 1) * blk]
            if band:
                s = s + bias
            sink = sink_ref[g * Q_GROUP + h]
            m = jnp.maximum(jnp.max(s, axis=-1, keepdims=True), sink)
            p = jnp.exp(s - m)
            denom = jnp.sum(p, axis=-1, keepdims=True) + jnp.exp(sink - m)
            o = _dot(p.astype(BF16), v_all) * (1.0 / denom)
            outs.append(o.astype(BF16))
    o_ref[...] = jnp.concatenate(outs, axis=1)


def _attention(sink, q, k, v, n_blocks, nb_per_seq, kx_block0, ctx_len, band, q_block0):
    blk = ATTN_BLOCK
    qi = lambda i, s: (q_block0 + i, 0)
    cur = lambda i, s: (i, 0)
    prv = lambda i, s: (jnp.maximum(i - 1, 0), 0)
    nxt = lambda i, s: (jnp.minimum(i + 1, n_blocks - 1), 0)
    kxi = lambda i, s: (kx_block0 + i // nb_per_seq, 0)
    kspec = lambda f: pl.BlockSpec((blk, KV_W), f)
    xspec = pl.BlockSpec((ctx_len, KV_W), kxi)
    if band:
        in_specs = [pl.BlockSpec((blk, Q_W), qi), kspec(prv), kspec(cur), kspec(nxt),
                    kspec(prv), kspec(cur), kspec(nxt), xspec, xspec]
        args = (q, k, k, k, v, v, v, k, v)
    else:
        in_specs = [pl.BlockSpec((blk, Q_W), qi), xspec, xspec]
        args = (q, k, v)
    return pl.pallas_call(
        functools.partial(_attn_kernel, band, nb_per_seq),
        grid_spec=pltpu.PrefetchScalarGridSpec(
            num_scalar_prefetch=1,
            grid=(n_blocks,),
            in_specs=in_specs,
            out_specs=pl.BlockSpec((blk, Q_W), lambda i, s: (i, 0)),
        ),
        out_shape=jax.ShapeDtypeStruct((n_blocks * blk, Q_W), BF16),
        compiler_params=_cparams(("parallel",)),
        name="band_attention" if band else "context_attention",
    )(sink, *args)


def _ssm_weights(lam_re, lam_im, log_dt, b_re, b_im, c_re, c_im, d_skip, batch):
    hp = lax.Precision.HIGHEST
    g, p, mm, lc = N_SSM_GROUPS, SSM_STATE, SSM_GROUP, SSM_CHUNK
    lr, li = lam_re.astype(F32), lam_im.astype(F32)
    dt = jnp.exp(log_dt.astype(F32))[..., None]
    mag = jnp.exp(lr * dt)
    a_re = mag * jnp.cos(li * dt)
    a_im = mag * jnp.sin(li * dt)
    den = lr * lr + li * li
    nr = a_re - 1.0
    f_re = (nr * lr + a_im * li) / den
    f_im = (a_im * lr - nr * li) / den
    br, bi = b_re.astype(F32), b_im.astype(F32)
    bb_re = f_re[..., None] * br - f_im[..., None] * bi
    bb_im = f_re[..., None] * bi + f_im[..., None] * br
    jj = jnp.arange(lc + 1, dtype=F32)[None, None, :, None]
    pmag = jnp.exp(lr[:, :, None, :] * dt[:, :, None, :] * jj)
    pang = li[:, :, None, :] * dt[:, :, None, :] * jj
    pw_re = pmag * jnp.cos(pang)
    pw_im = pmag * jnp.sin(pang)
    cr, ci = c_re.astype(F32), c_im.astype(F32)
    cp_re = cr[:, :, None] * pw_re[:, :, :, None, :] - ci[:, :, None] * pw_im[:, :, :, None, :]
    cp_im = cr[:, :, None] * pw_im[:, :, :, None, :] + ci[:, :, None] * pw_re[:, :, :, None, :]
    kk = (jnp.einsum('dgjmp,dgpn->dgjmn', cp_re, bb_re, precision=hp)
          - jnp.einsum('dgjmp,dgpn->dgjmn', cp_im, bb_im, precision=hp))
    s_i = jnp.arange(lc)[:, None]
    t_i = jnp.arange(lc)[None, :]
    kf = kk[0][:, jnp.clip(t_i - s_i, 0, lc)] * (t_i >= s_i)[None, :, :, None, None].astype(F32)
    kb = kk[1][:, jnp.clip(s_i - t_i, 0, lc)] * (s_i >= t_i)[None, :, :, None, None].astype(F32)
    dg = d_skip.astype(F32).reshape(g, mm)
    eye_t = jnp.eye(lc, dtype=F32)[None, :, :, None, None]
    eye_m = jnp.eye(mm, dtype=F32)[None, None, None, :, :]
    tm = kf + kb + eye_t * eye_m * dg[:, None, None, :, None]
    tmat = tm.transpose(0, 1, 4, 2, 3).reshape(g, lc * mm, lc * mm)

    def state_in(d, pidx):
        pr = pw_re[d][:, pidx]
        pi = pw_im[d][:, pidx]
        w_re = pr[:, :, None, :] * bb_re[d].transpose(0, 2, 1)[:, None] - pi[:, :, None, :] * bb_im[d].transpose(0, 2, 1)[:, None]
        w_im = pr[:, :, None, :] * bb_im[d].transpose(0, 2, 1)[:, None] + pi[:, :, None, :] * bb_re[d].transpose(0, 2, 1)[:, None]
        return w_re, w_im

    wf_re, wf_im = state_in(0, lc - 1 - jnp.arange(lc))
    wb_re, wb_im = state_in(1, jnp.arange(lc))
    base = jnp.stack([wf_re, wf_im, wb_re, wb_im], axis=3).reshape(g, lc * mm, 4, p)
    eye_b = jnp.eye(batch, dtype=F32)
    w1 = base[:, None, :, :, None, :] * eye_b[None, :, None, None, :, None]
    w1 = w1.reshape(g, batch, lc * mm, 4 * batch * p)

    def state_out(d, pidx):
        pr = pw_re[d][:, pidx].transpose(0, 2, 1)[:, :, :, None]
        pi = pw_im[d][:, pidx].transpose(0, 2, 1)[:, :, :, None]
        crd = cr[d].transpose(0, 2, 1)[:, :, None, :]
        cid = ci[d].transpose(0, 2, 1)[:, :, None, :]
        return crd * pr - cid * pi, -crd * pi - cid * pr

    of_re, of_im = state_out(0, jnp.arange(lc) + 1)
    ob_re, ob_im = state_out(1, lc - jnp.arange(lc))
    obase = jnp.stack([of_re, of_im, ob_re, ob_im], axis=1).reshape(g, 4, p, lc * mm)
    wout = obase[:, None, :, None, :, :] * eye_b[None, :, None, :, None, None]
    wout = wout.reshape(g, batch, 4 * batch * p, lc * mm)

    def plane(x):
        return jnp.tile(x, (1, batch))

    abar = jnp.stack([plane(pw_re[0][:, lc]), plane(pw_im[0][:, lc]),
                      plane(pw_re[1][:, lc]), plane(pw_im[1][:, lc])], axis=0)
    return w1.astype(BF16), abar, tmat.astype(BF16), wout.astype(BF16)


def _ssm_p1_kernel(u_ref, w_ref, *s_refs):
    acc = _dot(u_ref[0, 0], w_ref[0, 0])
    for b in range(1, u_ref.shape[1]):
        acc = acc + _dot(u_ref[0, b], w_ref[0, b])
    pw = s_refs[0].shape[1]
    for k, s_ref in enumerate(s_refs):
        s_ref[...] = acc[:, k * pw:(k + 1) * pw]


def _ssm_scan_kernel(ncc, ncl, a_ref, sfr_ref, sfi_ref, sbr_ref, sbi_ref, xfr_ref, xfi_ref, xbr_ref, xbi_ref):
    nc = ncc + ncl
    gb = a_ref.shape[1]
    pw = a_ref.shape[2]
    afr, afi, abr, abi = a_ref[0], a_ref[1], a_ref[2], a_ref[3]

    def rows(r):
        return pl.ds(r, gb, stride=nc)

    def step(rf, rb, carry):
        xfr, xfi, xbr, xbi = carry
        xfr_ref[rows(rf), :] = xfr
        xfi_ref[rows(rf), :] = xfi
        xbr_ref[rows(rb), :] = xbr
        xbi_ref[rows(rb), :] = xbi
        sfr = sfr_ref[rows(rf), :]
        sfi = sfi_ref[rows(rf), :]
        sbr = sbr_ref[rows(rb), :]
        sbi = sbi_ref[rows(rb), :]
        return (afr * xfr - afi * xfi + sfr, afr * xfi + afi * xfr + sfi,
                abr * xbr - abi * xbi + sbr, abr * xbi + abi * xbr + sbi)

    zero = jnp.zeros((gb, pw), F32)
    carry = lax.fori_loop(0, ncc, lambda t, c: step(t, ncc - 1 - t, c), (zero, zero, zero, zero))
    lax.fori_loop(0, ncl, lambda t, c: step(ncc + t, nc - 1 - t, c), carry)


def _ssm_p3_kernel(u_ref, xfr_ref, xfi_ref, xbr_ref, xbi_ref, t_ref, wo_ref, z_ref):
    xin = jnp.concatenate([xfr_ref[...], xfi_ref[...], xbr_ref[...], xbi_ref[...]], axis=1).astype(BF16)
    for b in range(u_ref.shape[1]):
        y = _dot(u_ref[0, b], t_ref[0]) + _dot(xin, wo_ref[0, b])
        z_ref[0, b] = jax.nn.gelu(y, approximate=True).astype(BF16)


def _ssm_mixer(u_chunks, w1, abar, tmat, wout, ncc, ncl):
    g, batch, nc, cw = u_chunks.shape
    sw = w1.shape[-1]
    pw = sw // 4
    plane_shape = [jax.ShapeDtypeStruct((g * nc, pw), F32)] * 4
    s_planes = pl.pallas_call(
        _ssm_p1_kernel,
        grid=(g,),
        in_specs=[pl.BlockSpec((1, batch, nc, cw), lambda i: (i, 0, 0, 0)),
                  pl.BlockSpec((1, batch, cw, sw), lambda i: (i, 0, 0, 0))],
        out_specs=[pl.BlockSpec((nc, pw), lambda i: (i, 0))] * 4,
        out_shape=plane_shape,
        compiler_params=_cparams(("parallel",)),
        name="ssm_chunk_states",
    )(u_chunks, w1)
    gb = SSM_SCAN_GROUPS
    x_planes = pl.pallas_call(
        functools.partial(_ssm_scan_kernel, ncc, ncl),
        grid=(g // gb,),
        in_specs=[pl.BlockSpec((4, gb, pw), lambda i: (0, i, 0))]
        + [pl.BlockSpec((gb * nc, pw), lambda i: (i, 0))] * 4,
        out_specs=[pl.BlockSpec((gb * nc, pw), lambda i: (i, 0))] * 4,
        out_shape=plane_shape,
        compiler_params=_cparams(("parallel",)),
        name="ssm_chunk_scan",
    )(abar, *s_planes)
    return pl.pallas_call(
        _ssm_p3_kernel,
        grid=(g,),
        in_specs=[pl.BlockSpec((1, batch, nc, cw), lambda i: (i, 0, 0, 0))]
        + [pl.BlockSpec((nc, pw), lambda i: (i, 0))] * 4
        + [pl.BlockSpec((1, cw, cw), lambda i: (i, 0, 0)),
           pl.BlockSpec((1, batch, sw, cw), lambda i: (i, 0, 0, 0))],
        out_specs=pl.BlockSpec((1, batch, nc, cw), lambda i: (i, 0, 0, 0)),
        out_shape=jax.ShapeDtypeStruct((g, batch, nc, cw), BF16),
        compiler_params=_cparams(("parallel",)),
        name="ssm_chunk_outputs",
    )(u_chunks, *x_planes, tmat, wout)


def _to_chunks(u_all, batch, seq, ctx_len):
    def part(t, n):
        t = t.reshape(batch, n // SSM_CHUNK, SSM_CHUNK, N_SSM_GROUPS, SSM_GROUP)
        return t.transpose(3, 0, 1, 2, 4).reshape(N_SSM_GROUPS, batch, n // SSM_CHUNK, SSM_CW)
    lat = part(u_all[:batch * seq], seq)
    cx = part(u_all[batch * seq:], ctx_len)
    return jnp.concatenate([cx, lat], axis=2)


def _from_chunks(z, batch, seq, ctx_len):
    ncc = ctx_len // SSM_CHUNK

    def part(t, n):
        t = t.reshape(N_SSM_GROUPS, batch, n // SSM_CHUNK, SSM_CHUNK, SSM_GROUP)
        return t.transpose(1, 2, 3, 0, 4).reshape(batch * n, D_SSM)
    return jnp.concatenate([part(z[:, :, ncc:], seq), part(z[:, :, :ncc], ctx_len)], axis=0)


def _merge_kernel(attn_ref, z_ref, ga_ref, gs_ref, x_ref, mod_ref, g2_ref, wglu_ref, wba_ref, wbs_ref,
                  wout_ref, wrh_ref, wrl_ref, xo_ref, h2_ref, lg_ref):
    z = z_ref[...]
    glu = (z.astype(F32) * jax.nn.sigmoid(_dot(z, wglu_ref[...]))).astype(BF16)
    mix = (ga_ref[...].astype(F32) * _dot(attn_ref[...], wba_ref[...])
           + gs_ref[...].astype(F32) * _dot(glu, wbs_ref[...])).astype(BF16)
    m = mod_ref[0]
    x = x_ref[...] + m[2:3] * _dot(mix, wout_ref[...])
    xo_ref[...] = x
    h2 = _rms_mod(x, g2_ref[...], m[3:4], m[4:5])
    hi, lo = _split_bf16(h2)
    h2_ref[...] = hi
    lg_ref[...] = _dot(hi, wrh_ref[...]) + _dot(lo, wrh_ref[...]) + _dot(hi, wrl_ref[...])


def _merge(attn, z, ga, gs, x_all, mod, g2, w_glu, w_ba, w_bs, w_out, wr_hi, wr_lo, n_tiles, tiles_per_seq, batch):
    d = D_MODEL
    row = lambda i: (i, 0)
    const = lambda i: (0, 0)
    modi = lambda i: (jnp.minimum(i // tiles_per_seq, batch), 0, 0)
    n = n_tiles * ROW_TILE
    return pl.pallas_call(
        _merge_kernel,
        grid=(n_tiles,),
        in_specs=[
            pl.BlockSpec((ROW_TILE, Q_W), row),
            pl.BlockSpec((ROW_TILE, D_SSM), row),
            pl.BlockSpec((ROW_TILE, d), row),
            pl.BlockSpec((ROW_TILE, d), row),
            pl.BlockSpec((ROW_TILE, d), row),
            pl.BlockSpec((1, 6, d), modi),
            pl.BlockSpec((1, d), const),
            pl.BlockSpec((D_SSM, D_SSM), const),
            pl.BlockSpec((Q_W, d), const),
            pl.BlockSpec((D_SSM, d), const),
            pl.BlockSpec((d, d), const),
            pl.BlockSpec((d, ROUTER_PAD), const),
            pl.BlockSpec((d, ROUTER_PAD), const),
        ],
        out_specs=[pl.BlockSpec((ROW_TILE, d), row), pl.BlockSpec((ROW_TILE, d), row),
                   pl.BlockSpec((ROW_TILE, ROUTER_PAD), row)],
        out_shape=[jax.ShapeDtypeStruct((n, d), F32), jax.ShapeDtypeStruct((n, d), BF16),
                   jax.ShapeDtypeStruct((n, ROUTER_PAD), F32)],
        compiler_params=_cparams(("parallel",)),
        name="merge_router",
    )(attn, z, ga, gs, x_all, mod, g2, w_glu, w_ba, w_bs, w_out, wr_hi, wr_lo)


def _expert_kernel(be_ref, nu_ref, x_ref, wg_ref, wu_ref, wd_ref, y_ref, wg_s, wu_s, wd_s):
    i = pl.program_id(0)
    changed = jnp.logical_or(i == 0, be_ref[i] != be_ref[jnp.maximum(i - 1, 0)])

    @pl.when(changed)
    def _():
        wg_s[...] = wg_ref[0].astype(BF16)
        wu_s[...] = wu_ref[0].astype(BF16)
        wd_s[...] = wd_ref[0].astype(BF16)

    @pl.when(i < nu_ref[0])
    def _():
        x = x_ref[...]
        gate = _dot(x, wg_s[...])
        up = _dot(x, wu_s[...])
        hid = (gate * jax.nn.sigmoid(gate) * up).astype(BF16)
        y_ref[...] = _dot(hid, wd_s[...]).astype(y_ref.dtype)

    @pl.when(i >= nu_ref[0])
    def _():
        y_ref[...] = jnp.zeros_like(y_ref)


def _experts(blk_exp, n_used, xs, w_gate, w_up, w_down):
    n_slots, d = xs.shape
    n_blk = n_slots // MOE_BLOCK
    de = w_gate.shape[-1]
    return pl.pallas_call(
        _expert_kernel,
        grid_spec=pltpu.PrefetchScalarGridSpec(
            num_scalar_prefetch=2,
            grid=(n_blk,),
            in_specs=[
                pl.BlockSpec((MOE_BLOCK, d), lambda i, be, nu: (i, 0)),
                pl.BlockSpec((1, d, de), lambda i, be, nu: (be[i], 0, 0)),
                pl.BlockSpec((1, d, de), lambda i, be, nu: (be[i], 0, 0)),
                pl.BlockSpec((1, de, d), lambda i, be, nu: (be[i], 0, 0)),
            ],
            out_specs=pl.BlockSpec((MOE_BLOCK, d), lambda i, be, nu: (i, 0)),
            scratch_shapes=[pltpu.VMEM((d, de), BF16), pltpu.VMEM((d, de), BF16), pltpu.VMEM((de, d), BF16)],
        ),
        out_shape=jax.ShapeDtypeStruct((n_slots, d), BF16),
        compiler_params=_cparams(("arbitrary",)),
        name="expert_mlp",
    )(blk_exp, n_used, xs, w_gate, w_up, w_down)


def _route(logits):
    t = logits.shape[0]
    p_grp = jax.nn.softmax(logits[:, :N_EXPERT_GROUPS], axis=-1)
    g_prob, g_idx = lax.top_k(p_grp, 1)
    le = logits[:, N_EXPERT_GROUPS:N_EXPERT_GROUPS + N_EXPERTS].reshape(t, N_EXPERT_GROUPS, EXPERTS_PER_GROUP)
    sel = jnp.take_along_axis(le, g_idx[:, :, None], axis=1)[:, 0]
    e_prob, e_idx = lax.top_k(jax.nn.softmax(sel, axis=-1), TOP_K)
    e_prob = e_prob / jnp.sum(e_prob, axis=-1, keepdims=True)
    return g_idx * EXPERTS_PER_GROUP + e_idx, g_prob * e_prob


def _dispatch_plan(expert):
    t = expert.shape[0]
    flat = expert.reshape(-1)
    n_assign = flat.shape[0]
    onehot = (flat[:, None] == jnp.arange(N_EXPERTS)[None, :]).astype(jnp.int32)
    csum = jnp.cumsum(onehot, axis=0)
    rank = jnp.sum(csum * onehot, axis=-1) - 1
    counts = csum[-1]
    pcounts = (counts + MOE_BLOCK - 1) // MOE_BLOCK * MOE_BLOCK
    pend = jnp.cumsum(pcounts)
    pstart = pend - pcounts
    dest = pstart[flat] + rank
    n_blk = -(-n_assign // MOE_BLOCK) + N_EXPERTS
    blk_exp = jnp.minimum(jnp.searchsorted(pend, jnp.arange(n_blk) * MOE_BLOCK, side='right'), N_EXPERTS - 1)
    token = jnp.repeat(jnp.arange(t, dtype=jnp.int32), TOP_K)
    slot_tok = jnp.zeros((n_blk * MOE_BLOCK,), jnp.int32).at[dest].set(token)
    n_used = (pend[-1] // MOE_BLOCK).astype(jnp.int32).reshape(1)
    return dest.reshape(t, TOP_K), slot_tok, blk_exp.astype(jnp.int32), n_used


def _moe(h2, logits, w_gate, w_up, w_down):
    expert, weight = _route(logits)
    dest, slot_tok, blk_exp, n_used = _dispatch_plan(expert)
    xs = h2[slot_tok]
    ys = _experts(blk_exp, n_used, xs, w_gate, w_up, w_down)
    yk = ys[dest].astype(F32)
    return jnp.sum(yk * weight[:, :, None], axis=1)


def _final_kernel(x_ref, f_ref, mod_ref, g_ref, o_ref):
    x = x_ref[...] + mod_ref[0, 5:6, :] * f_ref[...]
    o_ref[...] = x * lax.rsqrt(jnp.mean(x * x, axis=-1, keepdims=True) + EPS) * g_ref[...]


def _final(x_lat, f_lat, mod, g_final, n_tiles, tiles_per_seq, batch):
    d = D_MODEL
    row = lambda i: (i, 0)
    return pl.pallas_call(
        _final_kernel,
        grid=(n_tiles,),
        in_specs=[pl.BlockSpec((ROW_TILE, d), row), pl.BlockSpec((ROW_TILE, d), row),
                  pl.BlockSpec((1, 6, d), lambda i: (jnp.minimum(i // tiles_per_seq, batch), 0, 0)),
                  pl.BlockSpec((1, d), lambda i: (0, 0))],
        out_specs=pl.BlockSpec((ROW_TILE, d), row),
        out_shape=jax.ShapeDtypeStruct((n_tiles * ROW_TILE, d), F32),
        compiler_params=_cparams(("parallel",)),
        name="final_norm",
    )(x_lat, f_lat, mod, g_final)


def _rope_tables(seq):
    quarter = HEAD_DIM // 4
    freqs = ROPE_THETA ** (-jnp.arange(quarter, dtype=F32) / quarter)
    pos = jnp.arange(seq)
    ang_r = (pos // GRID_W).astype(F32)[:, None] * freqs[None, :]
    ang_c = (pos % GRID_W).astype(F32)[:, None] * freqs[None, :]
    cos = jnp.concatenate([jnp.cos(ang_r)] * 2 + [jnp.cos(ang_c)] * 2, axis=-1)
    sin = jnp.concatenate([-jnp.sin(ang_r), jnp.sin(ang_r), -jnp.sin(ang_c), jnp.sin(ang_c)], axis=-1)
    reps = 128 // HEAD_DIM
    cos = jnp.tile(cos, (1, reps))
    sin = jnp.tile(sin, (1, reps))
    cos = jnp.concatenate([cos, jnp.ones((ROW_TILE, 128), F32)], axis=0)
    sin = jnp.concatenate([sin, jnp.zeros((ROW_TILE, 128), F32)], axis=0)
    return cos, sin


def kernel(x, c, ctx, c_ctx, w_mod, b_mod, g_norm1, g_norm2, w_in, attn_sink, ssm_lam_re, ssm_lam_im,
           ssm_log_dt, ssm_b_re, ssm_b_im, ssm_c_re, ssm_c_im, ssm_d, w_glu, w_br_attn, w_br_ssm, w_out,
           w_router_group, w_router_expert, w_exp_gate, w_exp_up, w_exp_down, g_final):
    batch, seq, d = x.shape
    ctx_len = ctx.shape[1]
    depth = w_mod.shape[0]
    assert d == D_MODEL and batch * SSM_STATE == 128
    assert seq % ROW_TILE == 0 and ctx_len % ROW_TILE == 0 and (batch * seq) % ctx_len == 0
    assert batch + 1 <= MOD_ROWS
    t_lat, t_ctx = batch * seq, batch * ctx_len
    tiles_per_seq = seq // ROW_TILE
    n_lat_tiles = t_lat // ROW_TILE
    n_all_tiles = (t_lat + t_ctx) // ROW_TILE
    nb_per_seq = seq // ATTN_BLOCK
    ncc, ncl = ctx_len // SSM_CHUNK, seq // SSM_CHUNK

    c_rows = jnp.zeros((MOD_ROWS, d), F32).at[:batch].set(c).at[batch].set(c_ctx)
    mod_all = _modulation(c_rows, w_mod, b_mod).reshape(depth, MOD_ROWS, 6, d)
    cos_t, sin_t = _rope_tables(seq)
    x_all = jnp.concatenate([x.reshape(t_lat, d), ctx.reshape(t_ctx, d)], axis=0)
    f_all = None
    for l in range(depth):
        ctx_out = l < depth - 1
        mod = mod_all[l]
        x_all, (q, k, v, u, ga, gs) = _inproj(
            x_all, f_all, mod_all[l - 1] if l > 0 else None, mod, g_norm1[l].reshape(1, d), cos_t, sin_t,
            w_in[l].astype(BF16), n_all_tiles, tiles_per_seq, n_lat_tiles, batch)
        sink = attn_sink[l].astype(F32)
        attn = _attention(sink, q, k, v, t_lat // ATTN_BLOCK, nb_per_seq, t_lat // ctx_len, ctx_len, True, 0)
        if ctx_out:
            attn_c = _attention(sink, q, k, v, t_ctx // ATTN_BLOCK, ctx_len // ATTN_BLOCK, t_lat // ctx_len,
                                ctx_len, False, t_lat // ATTN_BLOCK)
            attn = jnp.concatenate([attn, attn_c], axis=0)
        w1, abar, tmat, wout = _ssm_weights(ssm_lam_re[l], ssm_lam_im[l], ssm_log_dt[l], ssm_b_re[l], ssm_b_im[l],
                                            ssm_c_re[l], ssm_c_im[l], ssm_d[l], batch)
        z = _from_chunks(_ssm_mixer(_to_chunks(u, batch, seq, ctx_len), w1, abar, tmat, wout, ncc, ncl),
                         batch, seq, ctx_len)
        w_r = jnp.zeros((d, ROUTER_PAD), F32)
        w_r = w_r.at[:, :N_EXPERT_GROUPS].set(w_router_group[l])
        w_r = w_r.at[:, N_EXPERT_GROUPS:N_EXPERT_GROUPS + N_EXPERTS].set(w_router_expert[l])
        wr_hi, wr_lo = _split_bf16(w_r)
        n_tiles = n_all_tiles if ctx_out else n_lat_tiles
        x_all, h2, logits = _merge(attn, z, ga, gs, x_all, mod, g_norm2[l].reshape(1, d), w_glu[l].astype(BF16),
                                   w_br_attn[l].astype(BF16), w_br_ssm[l].astype(BF16), w_out[l].astype(BF16),
                                   wr_hi, wr_lo, n_tiles, tiles_per_seq, batch)
        f_all = _moe(h2, logits, w_exp_gate[l], w_exp_up[l], w_exp_down[l])
    out = _final(x_all, f_all, mod_all[depth - 1], g_final.reshape(1, d), n_lat_tiles, tiles_per_seq, batch)
    return out.reshape(batch, seq, d)
```

```python
import functools
import math

import jax
import jax.numpy as jnp
from jax import lax
from jax.experimental import pallas as pl
from jax.experimental.pallas import tpu as pltpu

F32 = jnp.float32
BF16 = jnp.bfloat16

D_MODEL = 1024
GRID_W = 64
N_HEADS = 8
N_KV_HEADS = 2
HEAD_DIM = 64
Q_GROUP = N_HEADS // N_KV_HEADS
ATTN_BLOCK = 128
ROPE_THETA = 10000.0
D_SSM = D_MODEL // 2
SSM_GROUP = 16
N_SSM_GROUPS = D_SSM // SSM_GROUP
SSM_STATE = 64
N_EXPERT_GROUPS = 4
EXPERTS_PER_GROUP = 8
N_EXPERTS = N_EXPERT_GROUPS * EXPERTS_PER_GROUP
TOP_K = 2
D_EXPERT = D_MODEL // 2
Q_W = N_HEADS * HEAD_DIM
KV_W = N_KV_HEADS * HEAD_DIM
O_K = Q_W
O_V = O_K + KV_W
O_U = O_V + KV_W
O_GA = O_U + D_SSM
O_GS = O_GA + D_MODEL
D_IN = O_GS + D_MODEL
EPS = 1e-6
NEG_INF = -1e30

ROW_TILE = 256
SSM_CHUNK = 16
SSM_CW = SSM_CHUNK * SSM_GROUP
SSM_SCAN_GROUPS = 8
MOE_BLOCK = 128
ROUTER_PAD = 128
MOD_ROWS = 8
VMEM_LIMIT = 48 * 1024 * 1024


def _cparams(sem):
    return pltpu.CompilerParams(dimension_semantics=sem, vmem_limit_bytes=VMEM_LIMIT)


def _dot(a, b):
    return jnp.dot(a, b, preferred_element_type=F32)


def _split_bf16(a):
    hi = a.astype(BF16)
    lo = (a - hi.astype(F32)).astype(BF16)
    return hi, lo


def _rms_mod(x, g, shift, scale):
    y = x * lax.rsqrt(jnp.mean(x * x, axis=-1, keepdims=True) + EPS) * g
    return y * (1.0 + scale) + shift


def _mod_kernel(c_ref, w_ref, b_ref, o_ref):
    c = c_ref[...]
    s_hi, s_lo = _split_bf16(c * jax.nn.sigmoid(c))
    w_hi, w_lo = _split_bf16(w_ref[0])
    o_ref[0] = _dot(s_hi, w_hi) + _dot(s_lo, w_hi) + _dot(s_hi, w_lo) + b_ref[0]


def _modulation(c_rows, w_mod, b_mod):
    depth, d, n = w_mod.shape
    nb = n // 4
    return pl.pallas_call(
        _mod_kernel,
        grid=(depth, n // nb),
        in_specs=[
            pl.BlockSpec((MOD_ROWS, d), lambda l, j: (0, 0)),
            pl.BlockSpec((1, d, nb), lambda l, j: (l, 0, j)),
            pl.BlockSpec((1, 1, nb), lambda l, j: (l, 0, j)),
        ],
        out_specs=pl.BlockSpec((1, MOD_ROWS, nb), lambda l, j: (l, 0, j)),
        out_shape=jax.ShapeDtypeStruct((depth, MOD_ROWS, n), F32),
        compiler_params=_cparams(("arbitrary", "arbitrary")),
        name="modulation",
    )(c_rows, w_mod, b_mod.reshape(depth, 1, n))


def _moe_residual(x_ref, y0_ref, y1_ref, rt_ref, mod_ref):
    rt = rt_ref[...]
    f = rt[:, 2:3] * y0_ref[...].astype(F32) + rt[:, 3:4] * y1_ref[...].astype(F32)
    return x_ref[...] + mod_ref[0, 5:6, :] * f


def _inproj_kernel(has_f, *refs):
    if has_f:
        (x_ref, y0_ref, y1_ref, rt_ref, modp_ref, mod_ref, g_ref, cos_ref, sin_ref, w_ref,
         xo_ref, q_ref, k_ref, v_ref, u_ref, ga_ref, gs_ref) = refs
        x = _moe_residual(x_ref, y0_ref, y1_ref, rt_ref, modp_ref)
        xo_ref[...] = x
    else:
        (x_ref, mod_ref, g_ref, cos_ref, sin_ref, w_ref,
         q_ref, k_ref, v_ref, u_ref, ga_ref, gs_ref) = refs
        x = x_ref[...]
    m = mod_ref[0]
    h = _rms_mod(x, g_ref[...], m[0:1], m[1:2]).astype(BF16)
    cos = cos_ref[...]
    sin = sin_ref[...]
    lane = lax.broadcasted_iota(jnp.int32, cos.shape, 1)
    first = (lane % (HEAD_DIM // 2)) < (HEAD_DIM // 4)

    def rope(t):
        sw = jnp.where(first, pltpu.roll(t, 128 - HEAD_DIM // 4, 1), pltpu.roll(t, HEAD_DIM // 4, 1))
        return t * cos + sw * sin

    def proj(lo, hi):
        return _dot(h, w_ref[:, lo:hi])

    q = proj(0, O_K)
    for j in range(Q_W // 128):
        q_ref[:, 128 * j:128 * (j + 1)] = (rope(q[:, 128 * j:128 * (j + 1)]) * HEAD_DIM ** -0.5).astype(BF16)
    k_ref[...] = rope(proj(O_K, O_V)).astype(BF16)
    v_ref[...] = proj(O_V, O_U).astype(BF16)
    u_ref[...] = proj(O_U, O_GA).astype(BF16)
    ga_ref[...] = jax.nn.sigmoid(proj(O_GA, O_GS)).astype(BF16)
    gs_ref[...] = jax.nn.sigmoid(proj(O_GS, D_IN)).astype(BF16)


def _inproj(x_all, moe_out, mod_prev, mod, g1, cos_t, sin_t, w_in, n_tiles, tiles_per_seq, n_lat_tiles, batch):
    t_all, d = x_all.shape
    has_f = moe_out is not None
    row = lambda i: (i, 0)
    modi = lambda i: (jnp.minimum(i // tiles_per_seq, batch), 0, 0)
    const = lambda i: (0, 0)
    ropei = lambda i: (jnp.where(i < n_lat_tiles, i % tiles_per_seq, tiles_per_seq), 0)
    in_specs = [pl.BlockSpec((ROW_TILE, d), row)]
    args = [x_all]
    if has_f:
        in_specs += [pl.BlockSpec((ROW_TILE, d), row), pl.BlockSpec((ROW_TILE, d), row),
                     pl.BlockSpec((ROW_TILE, ROUTER_PAD), row), pl.BlockSpec((1, 6, d), modi)]
        args += [*moe_out, mod_prev]
    in_specs += [
        pl.BlockSpec((1, 6, d), modi),
        pl.BlockSpec((1, d), const),
        pl.BlockSpec((ROW_TILE, 128), ropei),
        pl.BlockSpec((ROW_TILE, 128), ropei),
        pl.BlockSpec((d, D_IN), const),
    ]
    args += [mod, g1, cos_t, sin_t, w_in]
    widths = [Q_W, KV_W, KV_W, D_SSM, D_MODEL, D_MODEL]
    out_specs = [pl.BlockSpec((ROW_TILE, w), row) for w in widths]
    out_shape = [jax.ShapeDtypeStruct((n_tiles * ROW_TILE, w), BF16) for w in widths]
    if has_f:
        out_specs = [pl.BlockSpec((ROW_TILE, d), row)] + out_specs
        out_shape = [jax.ShapeDtypeStruct((n_tiles * ROW_TILE, d), F32)] + out_shape
    outs = pl.pallas_call(
        functools.partial(_inproj_kernel, has_f),
        grid=(n_tiles,),
        in_specs=in_specs,
        out_specs=out_specs,
        out_shape=out_shape,
        compiler_params=_cparams(("parallel",)),
        name="inproj",
    )(*args)
    if has_f:
        return outs[0], outs[1:]
    return x_all, outs


def _attn_kernel(band, nb_per_seq, sink_ref, *refs):
    if band:
        q_ref, kp_ref, kc_ref, kn_ref, vp_ref, vc_ref, vn_ref, kx_ref, vx_ref, o_ref = refs
    else:
        q_ref, kx_ref, vx_ref, o_ref = refs
    blk = q_ref.shape[0]
    if band:
        j = pl.program_id(0) % nb_per_seq
        r = lax.broadcasted_iota(jnp.int32, (blk, blk), 0)
        c = lax.broadcasted_iota(jnp.int32, (blk, blk), 1)
        edge_p = jnp.where(j > 0, 0.0, NEG_INF).astype(F32)
        edge_n = jnp.where(j < nb_per_seq - 1, 0.0, NEG_INF).astype(F32)
        bias_p = jnp.where(c >= r, edge_p, NEG_INF).astype(F32)
        bias_n = jnp.where(c <= r, edge_n, NEG_INF).astype(F32)
        bias = jnp.concatenate(
            [bias_p, jnp.zeros((blk, blk), F32), bias_n, jnp.zeros((blk, kx_ref.shape[0]), F32)], axis=1)
    outs = []
    for g in range(N_KV_HEADS):
        gs = slice(g * HEAD_DIM, (g + 1) * HEAD_DIM)
        if band:
            k_all = jnp.concatenate([kp_ref[:, gs], kc_ref[:, gs], kn_ref[:, gs], kx_ref[:, gs]], axis=0)
            v_all = jnp.concatenate([vp_ref[:, gs], vc_ref[:, gs], vn_ref[:, gs], vx_ref[:, gs]], axis=0)
        else:
            k_all = kx_ref[:, gs]
            v_all = vx_ref[:, gs]
        qg = jnp.concatenate(
            [q_ref[:, (g * Q_GROUP + h) * HEAD_DIM:(g * Q_GROUP + h + 1) * HEAD_DIM] for h in range(Q_GROUP)],
            axis=0)
        s_all = lax.dot_general(qg, k_all, (((1,), (1,)), ((), ())), preferred_element_type=F32)
        for h in range(Q_GROUP):
            s = s_all[h * blk:(h + 1) * blk]
            if band:
                s = s + bias
            sink = sink_ref[g * Q_GROUP + h]
            m = jnp.maximum(jnp.max(s, axis=-1, keepdims=True), sink)
            p = jnp.exp(s - m)
            denom = jnp.sum(p, axis=-1, keepdims=True) + jnp.exp(sink - m)
            o = _dot(p.astype(BF16), v_all) * (1.0 / denom)
            outs.append(o.astype(BF16))
    o_ref[...] = jnp.concatenate(outs, axis=1)


def _attention(sink, q, k, v, n_blocks, nb_per_seq, kx_block0, ctx_len, band, q_block0):
    blk = ATTN_BLOCK
    qi = lambda i, s: (q_block0 + i, 0)
    cur = lambda i, s: (i, 0)
    prv = lambda i, s: (jnp.maximum(i - 1, 0), 0)
    nxt = lambda i, s: (jnp.minimum(i + 1, n_blocks - 1), 0)
    kxi = lambda i, s: (kx_block0 + i // nb_per_seq, 0)
    kspec = lambda f: pl.BlockSpec((blk, KV_W), f)
    xspec = pl.BlockSpec((ctx_len, KV_W), kxi)
    if band:
        in_specs = [pl.BlockSpec((blk, Q_W), qi), kspec(prv), kspec(cur), kspec(nxt),
                    kspec(prv), kspec(cur), kspec(nxt), xspec, xspec]
        args = (q, k, k, k, v, v, v, k, v)
    else:
        in_specs = [pl.BlockSpec((blk, Q_W), qi), xspec, xspec]
        args = (q, k, v)
    return pl.pallas_call(
        functools.partial(_attn_kernel, band, nb_per_seq),
        grid_spec=pltpu.PrefetchScalarGridSpec(
            num_scalar_prefetch=1,
            grid=(n_blocks,),
            in_specs=in_specs,
            out_specs=pl.BlockSpec((blk, Q_W), lambda i, s: (i, 0)),
        ),
        out_shape=jax.ShapeDtypeStruct((n_blocks * blk, Q_W), BF16),
        compiler_params=_cparams(("parallel",)),
        name="band_attention" if band else "context_attention",
    )(sink, *args)


def _ssm_weights(lam_re, lam_im, log_dt, b_re, b_im, c_re, c_im, d_skip, batch):
    hp = lax.Precision.HIGHEST
    g, p, mm, lc = N_SSM_GROUPS, SSM_STATE, SSM_GROUP, SSM_CHUNK
    lr, li = lam_re.astype(F32), lam_im.astype(F32)
    dt = jnp.exp(log_dt.astype(F32))[..., None]
    mag = jnp.exp(lr * dt)
    a_re = mag * jnp.cos(li * dt)
    a_im = mag * jnp.sin(li * dt)
    den = lr * lr + li * li
    nr = a_re - 1.0
    f_re = (nr * lr + a_im * li) / den
    f_im = (a_im * lr - nr * li) / den
    br, bi = b_re.astype(F32), b_im.astype(F32)
    bb_re = f_re[..., None] * br - f_im[..., None] * bi
    bb_im = f_re[..., None] * bi + f_im[..., None] * br
    jj = jnp.arange(lc + 1, dtype=F32)[None, None, :, None]
    pmag = jnp.exp(lr[:, :, None, :] * dt[:, :, None, :] * jj)
    pang = li[:, :, None, :] * dt[:, :, None, :] * jj
    pw_re = pmag * jnp.cos(pang)
    pw_im = pmag * jnp.sin(pang)
    cr, ci = c_re.astype(F32), c_im.astype(F32)
    cp_re = cr[:, :, None] * pw_re[:, :, :, None, :] - ci[:, :, None] * pw_im[:, :, :, None, :]
    cp_im = cr[:, :, None] * pw_im[:, :, :, None, :] + ci[:, :, None] * pw_re[:, :, :, None, :]
    kk = (jnp.einsum('dgjmp,dgpn->dgjmn', cp_re, bb_re, precision=hp)
          - jnp.einsum('dgjmp,dgpn->dgjmn', cp_im, bb_im, precision=hp))
    s_i = jnp.arange(lc)[:, None]
    t_i = jnp.arange(lc)[None, :]
    kf = kk[0][:, jnp.clip(t_i - s_i, 0, lc)] * (t_i >= s_i)[None, :, :, None, None].astype(F32)
    kb = kk[1][:, jnp.clip(s_i - t_i, 0, lc)] * (s_i >= t_i)[None, :, :, None, None].astype(F32)
    dg = d_skip.astype(F32).reshape(g, mm)
    eye_t = jnp.eye(lc, dtype=F32)[None, :, :, None, None]
    eye_m = jnp.eye(mm, dtype=F32)[None, None, None, :, :]
    tm = kf + kb + eye_t * eye_m * dg[:, None, None, :, None]
    tmat = tm.transpose(0, 1, 4, 2, 3).reshape(g, lc * mm, lc * mm)

    def state_in(d, pidx):
        pr = pw_re[d][:, pidx]
        pi = pw_im[d][:, pidx]
        w_re = pr[:, :, None, :] * bb_re[d].transpose(0, 2, 1)[:, None] - pi[:, :, None, :] * bb_im[d].transpose(0, 2, 1)[:, None]
        w_im = pr[:, :, None, :] * bb_im[d].transpose(0, 2, 1)[:, None] + pi[:, :, None, :] * bb_re[d].transpose(0, 2, 1)[:, None]
        return w_re, w_im

    wf_re, wf_im = state_in(0, lc - 1 - jnp.arange(lc))
    wb_re, wb_im = state_in(1, jnp.arange(lc))
    base = jnp.stack([wf_re, wf_im, wb_re, wb_im], axis=3).reshape(g, lc * mm, 4, p)
    eye_b = jnp.eye(batch, dtype=F32)
    w1 = base[:, None, :, :, None, :] * eye_b[None, :, None, None, :, None]
    w1 = w1.reshape(g, batch, lc * mm, 4 * batch * p)

    def state_out(d, pidx):
        pr = pw_re[d][:, pidx].transpose(0, 2, 1)[:, :, :, None]
        pi = pw_im[d][:, pidx].transpose(0, 2, 1)[:, :, :, None]
        crd = cr[d].transpose(0, 2, 1)[:, :, None, :]
        cid = ci[d].transpose(0, 2, 1)[:, :, None, :]
        return crd * pr - cid * pi, -crd * pi - cid * pr

    of_re, of_im = state_out(0, jnp.arange(lc) + 1)
    ob_re, ob_im = state_out(1, lc - jnp.arange(lc))
    obase = jnp.stack([of_re, of_im, ob_re, ob_im], axis=1).reshape(g, 4, p, lc * mm)
    wout = obase[:, None, :, None, :, :] * eye_b[None, :, None, :, None, None]
    wout = wout.reshape(g, batch, 4 * batch * p, lc * mm)

    def plane(x):
        return jnp.tile(x, (1, batch))

    abar = jnp.stack([plane(pw_re[0][:, lc]), plane(pw_im[0][:, lc]),
                      plane(pw_re[1][:, lc]), plane(pw_im[1][:, lc])], axis=0)
    return w1.astype(BF16), abar, tmat.astype(BF16), wout.astype(BF16)


def _ssm_p1_kernel(u_ref, w_ref, *s_refs):
    acc = _dot(u_ref[0, 0], w_ref[0, 0])
    for b in range(1, u_ref.shape[1]):
        acc = acc + _dot(u_ref[0, b], w_ref[0, b])
    pw = s_refs[0].shape[1]
    for k, s_ref in enumerate(s_refs):
        s_ref[...] = acc[:, k * pw:(k + 1) * pw]


def _ssm_scan_kernel(ncc, ncl, a_ref, sfr_ref, sfi_ref, sbr_ref, sbi_ref, xfr_ref, xfi_ref, xbr_ref, xbi_ref):
    nc = ncc + ncl
    gb = a_ref.shape[1]
    pw = a_ref.shape[2]
    afr, afi, abr, abi = a_ref[0], a_ref[1], a_ref[2], a_ref[3]

    def rows(r):
        return pl.ds(r, gb, stride=nc)

    def step(rf, rb, carry):
        xfr, xfi, xbr, xbi = carry
        xfr_ref[rows(rf), :] = xfr
        xfi_ref[rows(rf), :] = xfi
        xbr_ref[rows(rb), :] = xbr
        xbi_ref[rows(rb), :] = xbi
        sfr = sfr_ref[rows(rf), :]
        sfi = sfi_ref[rows(rf), :]
        sbr = sbr_ref[rows(rb), :]
        sbi = sbi_ref[rows(rb), :]
        return (afr * xfr - afi * xfi + sfr, afr * xfi + afi * xfr + sfi,
                abr * xbr - abi * xbi + sbr, abr * xbi + abi * xbr + sbi)

    zero = jnp.zeros((gb, pw), F32)
    carry = lax.fori_loop(0, ncc, lambda t, c: step(t, ncc - 1 - t, c), (zero, zero, zero, zero))
    lax.fori_loop(0, ncl, lambda t, c: step(ncc + t, nc - 1 - t, c), carry)


def _ssm_p3_kernel(u_ref, xfr_ref, xfi_ref, xbr_ref, xbi_ref, t_ref, wo_ref, z_ref):
    xin = jnp.concatenate([xfr_ref[...], xfi_ref[...], xbr_ref[...], xbi_ref[...]], axis=1).astype(BF16)
    for b in range(u_ref.shape[1]):
        y = _dot(u_ref[0, b], t_ref[0]) + _dot(xin, wo_ref[0, b])
        z_ref[0, b] = jax.nn.gelu(y, approximate=True).astype(BF16)


def _ssm_mixer(u_chunks, w1, abar, tmat, wout, ncc, ncl):
    g, batch, nc, cw = u_chunks.shape
    sw = w1.shape[-1]
    pw = sw // 4
    plane_shape = [jax.ShapeDtypeStruct((g * nc, pw), F32)] * 4
    s_planes = pl.pallas_call(
        _ssm_p1_kernel,
        grid=(g,),
        in_specs=[pl.BlockSpec((1, batch, nc, cw), lambda i: (i, 0, 0, 0)),
                  pl.BlockSpec((1, batch, cw, sw), lambda i: (i, 0, 0, 0))],
        out_specs=[pl.BlockSpec((nc, pw), lambda i: (i, 0))] * 4,
        out_shape=plane_shape,
        compiler_params=_cparams(("parallel",)),
        name="ssm_chunk_states",
    )(u_chunks, w1)
    gb = SSM_SCAN_GROUPS
    x_planes = pl.pallas_call(
        functools.partial(_ssm_scan_kernel, ncc, ncl),
        grid=(g // gb,),
        in_specs=[pl.BlockSpec((4, gb, pw), lambda i: (0, i, 0))]
        + [pl.BlockSpec((gb * nc, pw), lambda i: (i, 0))] * 4,
        out_specs=[pl.BlockSpec((gb * nc, pw), lambda i: (i, 0))] * 4,
        out_shape=plane_shape,
        compiler_params=_cparams(("parallel",)),
        name="ssm_chunk_scan",
    )(abar, *s_planes)
    return pl.pallas_call(
        _ssm_p3_kernel,
        grid=(g,),
        in_specs=[pl.BlockSpec((1, batch, nc, cw), lambda i: (i, 0, 0, 0))]
        + [pl.BlockSpec((nc, pw), lambda i: (i, 0))] * 4
        + [pl.BlockSpec((1, cw, cw), lambda i: (i, 0, 0)),
           pl.BlockSpec((1, batch, sw, cw), lambda i: (i, 0, 0, 0))],
        out_specs=pl.BlockSpec((1, batch, nc, cw), lambda i: (i, 0, 0, 0)),
        out_shape=jax.ShapeDtypeStruct((g, batch, nc, cw), BF16),
        compiler_params=_cparams(("parallel",)),
        name="ssm_chunk_outputs",
    )(u_chunks, *x_planes, tmat, wout)


def _to_chunks(u_all, batch, seq, ctx_len):
    def part(t, n):
        t = t.reshape(batch, n // SSM_CHUNK, SSM_CHUNK, N_SSM_GROUPS, SSM_GROUP)
        return t.transpose(3, 0, 1, 2, 4).reshape(N_SSM_GROUPS, batch, n // SSM_CHUNK, SSM_CW)
    lat = part(u_all[:batch * seq], seq)
    cx = part(u_all[batch * seq:], ctx_len)
    return jnp.concatenate([cx, lat], axis=2)


def _from_chunks(z, batch, seq, ctx_len):
    ncc = ctx_len // SSM_CHUNK

    def part(t, n):
        t = t.reshape(N_SSM_GROUPS, batch, n // SSM_CHUNK, SSM_CHUNK, SSM_GROUP)
        return t.transpose(1, 2, 3, 0, 4).reshape(batch * n, D_SSM)
    return jnp.concatenate([part(z[:, :, ncc:], seq), part(z[:, :, :ncc], ctx_len)], axis=0)


def _merge_kernel(attn_ref, z_ref, ga_ref, gs_ref, x_ref, mod_ref, g2_ref, wglu_ref, wba_ref, wbs_ref,
                  wout_ref, wrh_ref, wrl_ref, xo_ref, h2_ref, rt_ref):
    z = z_ref[...]
    glu = (z.astype(F32) * jax.nn.sigmoid(_dot(z, wglu_ref[...]))).astype(BF16)
    mix = (ga_ref[...].astype(F32) * _dot(attn_ref[...], wba_ref[...])
           + gs_ref[...].astype(F32) * _dot(glu, wbs_ref[...])).astype(BF16)
    m = mod_ref[0]
    x = x_ref[...] + m[2:3] * _dot(mix, wout_ref[...])
    xo_ref[...] = x
    h2 = _rms_mod(x, g2_ref[...], m[3:4], m[4:5])
    hi, lo = _split_bf16(h2)
    h2_ref[...] = hi.astype(F32)
    logits = _dot(hi, wrh_ref[...]) + _dot(lo, wrh_ref[...]) + _dot(hi, wrl_ref[...])
    rt_ref[...] = _route(logits)


def _route(lg):
    ng, epg = N_EXPERT_GROUPS, EXPERTS_PER_GROUP
    lane_i = lax.broadcasted_iota(jnp.int32, lg.shape, 1)
    lane = lane_i.astype(F32)
    big = float(ROUTER_PAD)

    def rmax(mask_val):
        return jnp.max(mask_val, axis=-1, keepdims=True)

    def first_lane(mask, val, mx):
        return jnp.min(jnp.where(mask, jnp.where(val == mx, lane, big), big), axis=-1, keepdims=True)

    gmask = lane_i < ng
    lgm = jnp.where(gmask, lg, NEG_INF)
    mg = rmax(lgm)
    g_prob = 1.0 / jnp.sum(jnp.exp(lgm - mg), axis=-1, keepdims=True)
    g_idx = first_lane(gmask, lg, mg)
    egroup = jnp.floor((lane - float(ng)) * (1.0 / epg))
    emask = egroup == g_idx
    l1 = jnp.where(emask, lg, NEG_INF)
    m1 = rmax(l1)
    i1 = first_lane(emask, lg, m1)
    l2 = jnp.where(lane == i1, NEG_INF, l1)
    m2 = rmax(l2)
    i2 = jnp.min(jnp.where(l2 == m2, jnp.where(emask, lane, big), big), axis=-1, keepdims=True)
    r = jnp.exp(m2 - m1)
    w1 = g_prob / (1.0 + r)
    w2 = w1 * r
    return jnp.where(lane_i == 0, i1 - float(ng),
                     jnp.where(lane_i == 1, i2 - float(ng),
                               jnp.where(lane_i == 2, w1, jnp.where(lane_i == 3, w2, 0.0))))


def _merge(attn, z, ga, gs, x_all, mod, g2, w_glu, w_ba, w_bs, w_out, wr_hi, wr_lo, n_tiles, tiles_per_seq, batch):
    d = D_MODEL
    row = lambda i: (i, 0)
    const = lambda i: (0, 0)
    modi = lambda i: (jnp.minimum(i // tiles_per_seq, batch), 0, 0)
    n = n_tiles * ROW_TILE
    return pl.pallas_call(
        _merge_kernel,
        grid=(n_tiles,),
        in_specs=[
            pl.BlockSpec((ROW_TILE, Q_W), row),
            pl.BlockSpec((ROW_TILE, D_SSM), row),
            pl.BlockSpec((ROW_TILE, d), row),
            pl.BlockSpec((ROW_TILE, d), row),
            pl.BlockSpec((ROW_TILE, d), row),
            pl.BlockSpec((1, 6, d), modi),
            pl.BlockSpec((1, d), const),
            pl.BlockSpec((D_SSM, D_SSM), const),
            pl.BlockSpec((Q_W, d), const),
            pl.BlockSpec((D_SSM, d), const),
            pl.BlockSpec((d, d), const),
            pl.BlockSpec((d, ROUTER_PAD), const),
            pl.BlockSpec((d, ROUTER_PAD), const),
        ],
        out_specs=[pl.BlockSpec((ROW_TILE, d), row), pl.BlockSpec((ROW_TILE, d), row),
                   pl.BlockSpec((ROW_TILE, ROUTER_PAD), row)],
        out_shape=[jax.ShapeDtypeStruct((n, d), F32), jax.ShapeDtypeStruct((n, d), F32),
                   jax.ShapeDtypeStruct((n, ROUTER_PAD), F32)],
        compiler_params=_cparams(("parallel",)),
        name="merge_router",
    )(attn, z, ga, gs, x_all, mod, g2, w_glu, w_ba, w_bs, w_out, wr_hi, wr_lo)


def _expert_kernel(be_ref, nu_ref, x_ref, wg_ref, wu_ref, wd_ref, y_ref, wg_s, wu_s, wd_s):
    i = pl.program_id(0)
    changed = jnp.logical_or(i == 0, be_ref[i] != be_ref[jnp.maximum(i - 1, 0)])

    @pl.when(changed)
    def _():
        wg_s[...] = wg_ref[0, 0].astype(BF16)
        wu_s[...] = wu_ref[0, 0].astype(BF16)
        wd_s[...] = wd_ref[0, 0].astype(BF16)

    @pl.when(i < nu_ref[0])
    def _():
        x = x_ref[...].astype(BF16)
        gate = _dot(x, wg_s[...])
        up = _dot(x, wu_s[...])
        hid = (gate * jax.nn.sigmoid(gate) * up).astype(BF16)
        y_ref[...] = _dot(hid, wd_s[...]).astype(y_ref.dtype)

    @pl.when(i >= nu_ref[0])
    def _():
        y_ref[...] = jnp.zeros_like(y_ref)


def _experts(blk_exp, n_used, xs, layer, w_gate, w_up, w_down):
    n_slots, d = xs.shape
    n_blk = n_slots // MOE_BLOCK
    de = w_gate.shape[-1]
    return pl.pallas_call(
        _expert_kernel,
        grid_spec=pltpu.PrefetchScalarGridSpec(
            num_scalar_prefetch=2,
            grid=(n_blk,),
            in_specs=[
                pl.BlockSpec((MOE_BLOCK, d), lambda i, be, nu: (i, 0)),
                pl.BlockSpec((1, 1, d, de), lambda i, be, nu: (layer, be[i], 0, 0)),
                pl.BlockSpec((1, 1, d, de), lambda i, be, nu: (layer, be[i], 0, 0)),
                pl.BlockSpec((1, 1, de, d), lambda i, be, nu: (layer, be[i], 0, 0)),
            ],
            out_specs=pl.BlockSpec((MOE_BLOCK, d), lambda i, be, nu: (i, 0)),
            scratch_shapes=[pltpu.VMEM((d, de), BF16), pltpu.VMEM((d, de), BF16), pltpu.VMEM((de, d), BF16)],
        ),
        out_shape=jax.ShapeDtypeStruct((n_slots, d), BF16),
        compiler_params=_cparams(("arbitrary",)),
        name="expert_mlp",
    )(blk_exp, n_used, xs, w_gate, w_up, w_down)


def _plan_kernel(rt_ref, dest_ref, cnt_ref, run_s):
    p = pl.program_id(0)
    i = pl.program_id(1)
    rows = rt_ref.shape[0]
    lane = lax.broadcasted_iota(jnp.int32, (rows, ROUTER_PAD), 1)
    lane_f = lane.astype(F32)
    rt = rt_ref[...]
    oh0 = jnp.where(lane_f == rt[:, 0:1], 1.0, 0.0)
    oh1 = jnp.where(lane_f == rt[:, 1:2], 1.0, 0.0)
    tot0 = jnp.sum(oh0, axis=0, keepdims=True)
    tot1 = jnp.sum(oh1, axis=0, keepdims=True)

    @pl.when(jnp.logical_and(p == 0, i == 0))
    def _():
        run_s[...] = jnp.zeros_like(run_s)

    @pl.when(p == 0)
    def _():
        run_s[0:1, :] = run_s[0:1, :] + tot0 + tot1

    @pl.when(jnp.logical_and(p == 1, i == 0))
    def _():
        counts = run_s[...]
        padded = jnp.floor((counts + float(MOE_BLOCK - 1)) * (1.0 / MOE_BLOCK)) * float(MOE_BLOCK)
        r = lax.broadcasted_iota(jnp.int32, (ROUTER_PAD, ROUTER_PAD), 0)
        c = lax.broadcasted_iota(jnp.int32, (ROUTER_PAD, ROUTER_PAD), 1)
        incl = jnp.where(r <= c, 1.0, 0.0).astype(BF16)
        hi, lo = _split_bf16(padded)
        pend = _dot(hi, incl) + _dot(lo, incl)
        pstart = pend - padded
        cnt_ref[...] = jnp.where(lax.broadcasted_iota(jnp.int32, counts.shape, 0) == 0, pend, 0.0)
        run_s[...] = pstart

    @pl.when(p == 1)
    def _():
        r = lax.broadcasted_iota(jnp.int32, (rows, rows), 0)
        c = lax.broadcasted_iota(jnp.int32, (rows, rows), 1)
        before = jnp.where(c < r, 1.0, 0.0).astype(BF16)
        run = run_s[0:1, :]
        pos0 = jnp.sum(oh0 * (run + _dot(before, oh0.astype(BF16))), axis=-1, keepdims=True)
        pos1 = jnp.sum(oh1 * (run + tot0 + _dot(before, oh1.astype(BF16))), axis=-1, keepdims=True)
        run_s[0:1, :] = run + tot0 + tot1
        dest_ref[...] = jnp.where(lane == 0, pos0, jnp.where(lane == 1, pos1, 0.0)).astype(jnp.int32)


def _plan(route, n_tiles):
    return pl.pallas_call(
        _plan_kernel,
        grid=(2, n_tiles),
        in_specs=[pl.BlockSpec((ROW_TILE, ROUTER_PAD), lambda p, i: (i, 0))],
        out_specs=[pl.BlockSpec((ROW_TILE, ROUTER_PAD), lambda p, i: (i * p, 0)),
                   pl.BlockSpec((8, ROUTER_PAD), lambda p, i: (0, 0))],
        out_shape=[jax.ShapeDtypeStruct((n_tiles * ROW_TILE, ROUTER_PAD), jnp.int32),
                   jax.ShapeDtypeStruct((8, ROUTER_PAD), F32)],
        scratch_shapes=[pltpu.VMEM((8, ROUTER_PAD), F32)],
        compiler_params=_cparams(("arbitrary", "arbitrary")),
        name="dispatch_plan",
    )(route)


def _dispatch_kernel(dest_ref, h_ref, xs_in_ref, xs_ref, sem):
    del xs_in_ref
    rows = h_ref.shape[0]

    def row_copy(r, d):
        return pltpu.make_async_copy(h_ref.at[pl.ds(r, 1)], xs_ref.at[pl.ds(d, 1)], sem)

    def issue(r, carry):
        for k in range(TOP_K):
            row_copy(r, dest_ref[0, 0, k * rows + r]).start()
        return carry

    lax.fori_loop(0, rows, issue, 0, unroll=8)
    for _ in range(TOP_K * rows):
        row_copy(0, 0).wait()


def _dispatch(dest_tiles, h2, xs_init, n_tiles):
    d = h2.shape[1]
    return pl.pallas_call(
        _dispatch_kernel,
        grid=(n_tiles,),
        in_specs=[pl.BlockSpec((1, 1, TOP_K * ROW_TILE), lambda i: (i, 0, 0), memory_space=pltpu.SMEM),
                  pl.BlockSpec((ROW_TILE, d), lambda i: (i, 0)),
                  pl.BlockSpec(memory_space=pl.ANY)],
        out_specs=pl.BlockSpec(memory_space=pl.ANY),
        out_shape=jax.ShapeDtypeStruct(xs_init.shape, xs_init.dtype),
        scratch_shapes=[pltpu.SemaphoreType.DMA],
        input_output_aliases={2: 0},
        compiler_params=_cparams(("arbitrary",)),
        name="dispatch_rows",
    )(dest_tiles, h2, xs_init)


def _moe(h2, route, n_tiles, layer, w_gate, w_up, w_down):
    t, d = h2.shape
    dest, cnt = _plan(route, n_tiles)
    pend = cnt[0, :N_EXPERTS].astype(jnp.int32)
    n_blk = -(-(t * TOP_K) // MOE_BLOCK) + N_EXPERTS
    blk_start = jnp.arange(n_blk, dtype=jnp.int32) * MOE_BLOCK
    blk_exp = jnp.minimum(jnp.sum((pend[None, :] <= blk_start[:, None]).astype(jnp.int32), axis=1), N_EXPERTS - 1)
    n_used = (pend[N_EXPERTS - 1] // MOE_BLOCK).reshape(1)
    dest2 = dest[:, :TOP_K]
    dest_tiles = dest2.reshape(n_tiles, ROW_TILE, TOP_K).transpose(0, 2, 1).reshape(n_tiles, 1, TOP_K * ROW_TILE)
    xs = _dispatch(dest_tiles, h2, jnp.zeros((n_blk * MOE_BLOCK, d), F32), n_tiles)
    ys = _experts(blk_exp, n_used, xs, layer, w_gate, w_up, w_down)
    return ys[dest2[:, 0]], ys[dest2[:, 1]]


def _final_kernel(x_ref, y0_ref, y1_ref, rt_ref, mod_ref, g_ref, o_ref):
    x = _moe_residual(x_ref, y0_ref, y1_ref, rt_ref, mod_ref)
    o_ref[...] = x * lax.rsqrt(jnp.mean(x * x, axis=-1, keepdims=True) + EPS) * g_ref[...]


def _final(x_lat, moe_out, mod, g_final, n_tiles, tiles_per_seq, batch):
    d = D_MODEL
    row = lambda i: (i, 0)
    return pl.pallas_call(
        _final_kernel,
        grid=(n_tiles,),
        in_specs=[pl.BlockSpec((ROW_TILE, d), row), pl.BlockSpec((ROW_TILE, d), row),
                  pl.BlockSpec((ROW_TILE, d), row), pl.BlockSpec((ROW_TILE, ROUTER_PAD), row),
                  pl.BlockSpec((1, 6, d), lambda i: (jnp.minimum(i // tiles_per_seq, batch), 0, 0)),
                  pl.BlockSpec((1, d), lambda i: (0, 0))],
        out_specs=pl.BlockSpec((ROW_TILE, d), row),
        out_shape=jax.ShapeDtypeStruct((n_tiles * ROW_TILE, d), F32),
        compiler_params=_cparams(("parallel",)),
        name="final_norm",
    )(x_lat, *moe_out, mod, g_final)


def _rope_tables(seq):
    quarter = HEAD_DIM // 4
    freqs = ROPE_THETA ** (-jnp.arange(quarter, dtype=F32) / quarter)
    pos = jnp.arange(seq)
    ang_r = (pos // GRID_W).astype(F32)[:, None] * freqs[None, :]
    ang_c = (pos % GRID_W).astype(F32)[:, None] * freqs[None, :]
    cos = jnp.concatenate([jnp.cos(ang_r)] * 2 + [jnp.cos(ang_c)] * 2, axis=-1)
    sin = jnp.concatenate([-jnp.sin(ang_r), jnp.sin(ang_r), -jnp.sin(ang_c), jnp.sin(ang_c)], axis=-1)
    reps = 128 // HEAD_DIM
    cos = jnp.tile(cos, (1, reps))
    sin = jnp.tile(sin, (1, reps))
    cos = jnp.concatenate([cos, jnp.ones((ROW_TILE, 128), F32)], axis=0)
    sin = jnp.concatenate([sin, jnp.zeros((ROW_TILE, 128), F32)], axis=0)
    return cos, sin


def kernel(x, c, ctx, c_ctx, w_mod, b_mod, g_norm1, g_norm2, w_in, attn_sink, ssm_lam_re, ssm_lam_im,
           ssm_log_dt, ssm_b_re, ssm_b_im, ssm_c_re, ssm_c_im, ssm_d, w_glu, w_br_attn, w_br_ssm, w_out,
           w_router_group, w_router_expert, w_exp_gate, w_exp_up, w_exp_down, g_final):
    batch, seq, d = x.shape
    ctx_len = ctx.shape[1]
    depth = w_mod.shape[0]
    assert d == D_MODEL and batch * SSM_STATE == 128
    assert seq % ROW_TILE == 0 and ctx_len % ROW_TILE == 0 and (batch * seq) % ctx_len == 0
    assert batch + 1 <= MOD_ROWS
    t_lat, t_ctx = batch * seq, batch * ctx_len
    tiles_per_seq = seq // ROW_TILE
    n_lat_tiles = t_lat // ROW_TILE
    n_all_tiles = (t_lat + t_ctx) // ROW_TILE
    nb_per_seq = seq // ATTN_BLOCK
    ncc, ncl = ctx_len // SSM_CHUNK, seq // SSM_CHUNK

    c_rows = jnp.zeros((MOD_ROWS, d), F32).at[:batch].set(c).at[batch].set(c_ctx)
    mod_all = _modulation(c_rows, w_mod, b_mod).reshape(depth, MOD_ROWS, 6, d)
    cos_t, sin_t = _rope_tables(seq)
    x_all = jnp.concatenate([x.reshape(t_lat, d), ctx.reshape(t_ctx, d)], axis=0)
    f_all = None
    for l in range(depth):
        ctx_out = l < depth - 1
        mod = mod_all[l]
        x_all, (q, k, v, u, ga, gs) = _inproj(
            x_all, f_all, mod_all[l - 1] if l > 0 else None, mod, g_norm1[l].reshape(1, d), cos_t, sin_t,
            w_in[l].astype(BF16), n_all_tiles, tiles_per_seq, n_lat_tiles, batch)
        sink = attn_sink[l].astype(F32)
        attn = _attention(sink, q, k, v, t_lat // ATTN_BLOCK, nb_per_seq, t_lat // ctx_len, ctx_len, True, 0)
        if ctx_out:
            attn_c = _attention(sink, q, k, v, t_ctx // ATTN_BLOCK, ctx_len // ATTN_BLOCK, t_lat // ctx_len,
                                ctx_len, False, t_lat // ATTN_BLOCK)
            attn = jnp.concatenate([attn, attn_c], axis=0)
        w1, abar, tmat, wout = _ssm_weights(ssm_lam_re[l], ssm_lam_im[l], ssm_log_dt[l], ssm_b_re[l], ssm_b_im[l],
                                            ssm_c_re[l], ssm_c_im[l], ssm_d[l], batch)
        z = _from_chunks(_ssm_mixer(_to_chunks(u, batch, seq, ctx_len), w1, abar, tmat, wout, ncc, ncl),
                         batch, seq, ctx_len)
        w_r = jnp.zeros((d, ROUTER_PAD), F32)
        w_r = w_r.at[:, :N_EXPERT_GROUPS].set(w_router_group[l])
        w_r = w_r.at[:, N_EXPERT_GROUPS:N_EXPERT_GROUPS + N_EXPERTS].set(w_router_expert[l])
        wr_hi, wr_lo = _split_bf16(w_r)
        n_tiles = n_all_tiles if ctx_out else n_lat_tiles
        x_all, h2, route = _merge(attn, z, ga, gs, x_all, mod, g_norm2[l].reshape(1, d), w_glu[l].astype(BF16),
                                  w_br_attn[l].astype(BF16), w_br_ssm[l].astype(BF16), w_out[l].astype(BF16),
                                  wr_hi, wr_lo, n_tiles, tiles_per_seq, batch)
        y0, y1 = _moe(h2, route, n_tiles, l, w_exp_gate, w_exp_up, w_exp_down)
        f_all = (y0, y1, route)
    out = _final(x_all, f_all, mod_all[depth - 1], g_final.reshape(1, d), n_lat_tiles, tiles_per_seq, batch)
    return out.reshape(batch, seq, d)
```

```python
import functools
import math

import jax
import jax.numpy as jnp
from jax import lax
from jax.experimental import pallas as pl
from jax.experimental.pallas import tpu as pltpu

F32 = jnp.float32
BF16 = jnp.bfloat16

D_MODEL = 1024
GRID_W = 64
N_HEADS = 8
N_KV_HEADS = 2
HEAD_DIM = 64
Q_GROUP = N_HEADS // N_KV_HEADS
ATTN_BLOCK = 128
ROPE_THETA = 10000.0
D_SSM = D_MODEL // 2
SSM_GROUP = 16
N_SSM_GROUPS = D_SSM // SSM_GROUP
SSM_STATE = 64
N_EXPERT_GROUPS = 4
EXPERTS_PER_GROUP = 8
N_EXPERTS = N_EXPERT_GROUPS * EXPERTS_PER_GROUP
TOP_K = 2
D_EXPERT = D_MODEL // 2
Q_W = N_HEADS * HEAD_DIM
KV_W = N_KV_HEADS * HEAD_DIM
O_K = Q_W
O_V = O_K + KV_W
O_U = O_V + KV_W
O_GA = O_U + D_SSM
O_GS = O_GA + D_MODEL
D_IN = O_GS + D_MODEL
EPS = 1e-6
NEG_INF = -1e30

ROW_TILE = 256
SSM_CHUNK = 16
SSM_CW = SSM_CHUNK * SSM_GROUP
SSM_SCAN_GROUPS = 8
SSM_LANE_GROUPS = 128 // SSM_GROUP
SSM_LANE_TILES = D_SSM // 128
MOE_BLOCK = 256
ROUTER_PAD = 128
MOD_ROWS = 8
VMEM_LIMIT = 48 * 1024 * 1024


def _cparams(sem):
    return pltpu.CompilerParams(dimension_semantics=sem, vmem_limit_bytes=VMEM_LIMIT)


def _dot(a, b):
    return jnp.dot(a, b, preferred_element_type=F32)


def _split_bf16(a):
    hi = a.astype(BF16)
    lo = (a - hi.astype(F32)).astype(BF16)
    return hi, lo


def _rms_mod(x, g, shift, scale):
    y = x * lax.rsqrt(jnp.mean(x * x, axis=-1, keepdims=True) + EPS) * g
    return y * (1.0 + scale) + shift


def _mod_kernel(c_ref, w_ref, b_ref, o_ref):
    c = c_ref[...]
    s_hi, s_lo = _split_bf16(c * jax.nn.sigmoid(c))
    w_hi, w_lo = _split_bf16(w_ref[0])
    o_ref[0] = _dot(s_hi, w_hi) + _dot(s_lo, w_hi) + _dot(s_hi, w_lo) + b_ref[0]


def _modulation(c_rows, w_mod, b_mod):
    depth, d, n = w_mod.shape
    nb = n // 4
    return pl.pallas_call(
        _mod_kernel,
        grid=(depth, n // nb),
        in_specs=[
            pl.BlockSpec((MOD_ROWS, d), lambda l, j: (0, 0)),
            pl.BlockSpec((1, d, nb), lambda l, j: (l, 0, j)),
            pl.BlockSpec((1, 1, nb), lambda l, j: (l, 0, j)),
        ],
        out_specs=pl.BlockSpec((1, MOD_ROWS, nb), lambda l, j: (l, 0, j)),
        out_shape=jax.ShapeDtypeStruct((depth, MOD_ROWS, n), F32),
        compiler_params=_cparams(("arbitrary", "arbitrary")),
        name="modulation",
    )(c_rows, w_mod, b_mod.reshape(depth, 1, n))


def _moe_residual(x_ref, y0_ref, y1_ref, rt_ref, mod_ref):
    rt = rt_ref[...]
    f = rt[:, 2:3] * y0_ref[...].astype(F32) + rt[:, 3:4] * y1_ref[...].astype(F32)
    return x_ref[...] + mod_ref[0, 5:6, :] * f


def _inproj_kernel(has_f, *refs):
    if has_f:
        (x_ref, y0_ref, y1_ref, rt_ref, modp_ref, mod_ref, g_ref, cos_ref, sin_ref, w_ref,
         xo_ref, q_ref, k_ref, v_ref, u_ref, ga_ref, gs_ref) = refs
        x = _moe_residual(x_ref, y0_ref, y1_ref, rt_ref, modp_ref)
        xo_ref[...] = x
    else:
        (x_ref, mod_ref, g_ref, cos_ref, sin_ref, w_ref,
         q_ref, k_ref, v_ref, u_ref, ga_ref, gs_ref) = refs
        x = x_ref[...]
    m = mod_ref[0]
    h = _rms_mod(x, g_ref[...], m[0:1], m[1:2]).astype(BF16)
    cos = cos_ref[...]
    sin = sin_ref[...]
    lane = lax.broadcasted_iota(jnp.int32, cos.shape, 1)
    first = (lane % (HEAD_DIM // 2)) < (HEAD_DIM // 4)

    def rope(t):
        sw = jnp.where(first, pltpu.roll(t, 128 - HEAD_DIM // 4, 1), pltpu.roll(t, HEAD_DIM // 4, 1))
        return t * cos + sw * sin

    def proj(lo, hi):
        return _dot(h, w_ref[:, lo:hi])

    q = proj(0, O_K)
    for j in range(Q_W // 128):
        q_ref[:, 128 * j:128 * (j + 1)] = (rope(q[:, 128 * j:128 * (j + 1)]) * HEAD_DIM ** -0.5).astype(BF16)
    k_ref[...] = rope(proj(O_K, O_V)).astype(BF16)
    v_ref[...] = proj(O_V, O_U).astype(BF16)
    u = proj(O_U, O_GA)
    for j in range(SSM_LANE_TILES):
        u_ref[j] = u[:, 128 * j:128 * (j + 1)]
    ga_ref[...] = jax.nn.sigmoid(proj(O_GA, O_GS)).astype(BF16)
    gs_ref[...] = jax.nn.sigmoid(proj(O_GS, D_IN)).astype(BF16)


def _inproj(x_all, moe_out, mod_prev, mod, g1, cos_t, sin_t, w_in, n_tiles, tiles_per_seq, n_lat_tiles, batch):
    t_all, d = x_all.shape
    has_f = moe_out is not None
    row = lambda i: (i, 0)
    modi = lambda i: (jnp.minimum(i // tiles_per_seq, batch), 0, 0)
    const = lambda i: (0, 0)
    ropei = lambda i: (jnp.where(i < n_lat_tiles, i % tiles_per_seq, tiles_per_seq), 0)
    in_specs = [pl.BlockSpec((ROW_TILE, d), row)]
    args = [x_all]
    if has_f:
        in_specs += [pl.BlockSpec((ROW_TILE, d), row), pl.BlockSpec((ROW_TILE, d), row),
                     pl.BlockSpec((ROW_TILE, ROUTER_PAD), row), pl.BlockSpec((1, 6, d), modi)]
        args += [*moe_out, mod_prev]
    in_specs += [
        pl.BlockSpec((1, 6, d), modi),
        pl.BlockSpec((1, d), const),
        pl.BlockSpec((ROW_TILE, 128), ropei),
        pl.BlockSpec((ROW_TILE, 128), ropei),
        pl.BlockSpec((d, D_IN), const),
    ]
    args += [mod, g1, cos_t, sin_t, w_in]
    widths = [Q_W, KV_W, KV_W, D_SSM, D_MODEL, D_MODEL]
    out_specs = [pl.BlockSpec((ROW_TILE, w), row) for w in widths]
    out_shape = [jax.ShapeDtypeStruct((n_tiles * ROW_TILE, w), BF16) for w in widths]
    out_specs[3] = pl.BlockSpec((SSM_LANE_TILES, ROW_TILE, 128), lambda i: (0, i, 0))
    out_shape[3] = jax.ShapeDtypeStruct((SSM_LANE_TILES, n_tiles * ROW_TILE, 128), F32)
    if has_f:
        out_specs = [pl.BlockSpec((ROW_TILE, d), row)] + out_specs
        out_shape = [jax.ShapeDtypeStruct((n_tiles * ROW_TILE, d), F32)] + out_shape
    outs = pl.pallas_call(
        functools.partial(_inproj_kernel, has_f),
        grid=(n_tiles,),
        in_specs=in_specs,
        out_specs=out_specs,
        out_shape=out_shape,
        compiler_params=_cparams(("parallel",)),
        name="inproj",
    )(*args)
    if has_f:
        return outs[0], outs[1:]
    return x_all, outs


def _attn_kernel(band, nb_per_seq, sink_ref, *refs):
    if band:
        q_ref, kp_ref, kc_ref, kn_ref, vp_ref, vc_ref, vn_ref, kx_ref, vx_ref, o_ref = refs
    else:
        q_ref, kx_ref, vx_ref, o_ref = refs
    blk = q_ref.shape[0]
    if band:
        j = pl.program_id(0) % nb_per_seq
        r = lax.broadcasted_iota(jnp.int32, (blk, blk), 0)
        c = lax.broadcasted_iota(jnp.int32, (blk, blk), 1)
        edge_p = jnp.where(j > 0, 0.0, NEG_INF).astype(F32)
        edge_n = jnp.where(j < nb_per_seq - 1, 0.0, NEG_INF).astype(F32)
        bias_p = jnp.where(c >= r, edge_p, NEG_INF).astype(F32)
        bias_n = jnp.where(c <= r, edge_n, NEG_INF).astype(F32)
        bias = jnp.concatenate(
            [bias_p, jnp.zeros((blk, blk), F32), bias_n, jnp.zeros((blk, kx_ref.shape[0]), F32)], axis=1)
    outs = []
    for g in range(N_KV_HEADS):
        gs = slice(g * HEAD_DIM, (g + 1) * HEAD_DIM)
        if band:
            k_all = jnp.concatenate([kp_ref[:, gs], kc_ref[:, gs], kn_ref[:, gs], kx_ref[:, gs]], axis=0)
            v_all = jnp.concatenate([vp_ref[:, gs], vc_ref[:, gs], vn_ref[:, gs], vx_ref[:, gs]], axis=0)
        else:
            k_all = kx_ref[:, gs]
            v_all = vx_ref[:, gs]
        qg = jnp.concatenate(
            [q_ref[:, (g * Q_GROUP + h) * HEAD_DIM:(g * Q_GROUP + h + 1) * HEAD_DIM] for h in range(Q_GROUP)],
            axis=0)
        s_all = lax.dot_general(qg, k_all, (((1,), (1,)), ((), ())), preferred_element_type=F32)
        for h in range(Q_GROUP):
            s = s_all[h * blk:(h + 1) * blk]
            if band:
                s = s + bias
            sink = sink_ref[g * Q_GROUP + h]
            m = jnp.maximum(jnp.max(s, axis=-1, keepdims=True), sink)
            p = jnp.exp(s - m)
            denom = jnp.sum(p, axis=-1, keepdims=True) + jnp.exp(sink - m)
            o = _dot(p.astype(BF16), v_all) * (1.0 / denom)
            outs.append(o.astype(BF16))
    o_ref[...] = jnp.concatenate(outs, axis=1)


def _attention(sink, q, k, v, n_blocks, nb_per_seq, kx_block0, ctx_len, band, q_block0):
    blk = ATTN_BLOCK
    qi = lambda i, s: (q_block0 + i, 0)
    cur = lambda i, s: (i, 0)
    prv = lambda i, s: (jnp.maximum(i - 1, 0), 0)
    nxt = lambda i, s: (jnp.minimum(i + 1, n_blocks - 1), 0)
    kxi = lambda i, s: (kx_block0 + i // nb_per_seq, 0)
    kspec = lambda f: pl.BlockSpec((blk, KV_W), f)
    xspec = pl.BlockSpec((ctx_len, KV_W), kxi)
    if band:
        in_specs = [pl.BlockSpec((blk, Q_W), qi), kspec(prv), kspec(cur), kspec(nxt),
                    kspec(prv), kspec(cur), kspec(nxt), xspec, xspec]
        args = (q, k, k, k, v, v, v, k, v)
    else:
        in_specs = [pl.BlockSpec((blk, Q_W), qi), xspec, xspec]
        args = (q, k, v)
    return pl.pallas_call(
        functools.partial(_attn_kernel, band, nb_per_seq),
        grid_spec=pltpu.PrefetchScalarGridSpec(
            num_scalar_prefetch=1,
            grid=(n_blocks,),
            in_specs=in_specs,
            out_specs=pl.BlockSpec((blk, Q_W), lambda i, s: (i, 0)),
        ),
        out_shape=jax.ShapeDtypeStruct((n_blocks * blk, Q_W), BF16),
        compiler_params=_cparams(("parallel",)),
        name="band_attention" if band else "context_attention",
    )(sink, *args)


def _ssm_weights(lam_re, lam_im, log_dt, b_re, b_im, c_re, c_im, d_skip, batch):
    hp = lax.Precision.HIGHEST
    g, p, mm, lc = N_SSM_GROUPS, SSM_STATE, SSM_GROUP, SSM_CHUNK
    lr, li = lam_re.astype(F32), lam_im.astype(F32)
    dt = jnp.exp(log_dt.astype(F32))[..., None]
    mag = jnp.exp(lr * dt)
    a_re = mag * jnp.cos(li * dt)
    a_im = mag * jnp.sin(li * dt)
    den = lr * lr + li * li
    nr = a_re - 1.0
    f_re = (nr * lr + a_im * li) / den
    f_im = (a_im * lr - nr * li) / den
    br, bi = b_re.astype(F32), b_im.astype(F32)
    bb_re = f_re[..., None] * br - f_im[..., None] * bi
    bb_im = f_re[..., None] * bi + f_im[..., None] * br
    jj = jnp.arange(lc + 1, dtype=F32)[None, None, :, None]
    pmag = jnp.exp(lr[:, :, None, :] * dt[:, :, None, :] * jj)
    pang = li[:, :, None, :] * dt[:, :, None, :] * jj
    pw_re = pmag * jnp.cos(pang)
    pw_im = pmag * jnp.sin(pang)
    cr, ci = c_re.astype(F32), c_im.astype(F32)
    cp_re = cr[:, :, None] * pw_re[:, :, :, None, :] - ci[:, :, None] * pw_im[:, :, :, None, :]
    cp_im = cr[:, :, None] * pw_im[:, :, :, None, :] + ci[:, :, None] * pw_re[:, :, :, None, :]
    kk = (jnp.einsum('dgjmp,dgpn->dgjmn', cp_re, bb_re, precision=hp)
          - jnp.einsum('dgjmp,dgpn->dgjmn', cp_im, bb_im, precision=hp))
    s_i = jnp.arange(lc)[:, None]
    t_i = jnp.arange(lc)[None, :]
    kf = kk[0][:, jnp.clip(t_i - s_i, 0, lc)] * (t_i >= s_i)[None, :, :, None, None].astype(F32)
    kb = kk[1][:, jnp.clip(s_i - t_i, 0, lc)] * (s_i >= t_i)[None, :, :, None, None].astype(F32)
    dg = d_skip.astype(F32).reshape(g, mm)
    eye_t = jnp.eye(lc, dtype=F32)[None, :, :, None, None]
    eye_m = jnp.eye(mm, dtype=F32)[None, None, None, :, :]
    tm = kf + kb + eye_t * eye_m * dg[:, None, None, :, None]
    tmat = tm.transpose(0, 1, 4, 2, 3).reshape(g, lc * mm, lc * mm)

    def state_in(d, pidx):
        pr = pw_re[d][:, pidx]
        pi = pw_im[d][:, pidx]
        w_re = pr[:, :, None, :] * bb_re[d].transpose(0, 2, 1)[:, None] - pi[:, :, None, :] * bb_im[d].transpose(0, 2, 1)[:, None]
        w_im = pr[:, :, None, :] * bb_im[d].transpose(0, 2, 1)[:, None] + pi[:, :, None, :] * bb_re[d].transpose(0, 2, 1)[:, None]
        return w_re, w_im

    wf_re, wf_im = state_in(0, lc - 1 - jnp.arange(lc))
    wb_re, wb_im = state_in(1, jnp.arange(lc))
    base = jnp.stack([wf_re, wf_im, wb_re, wb_im], axis=3).reshape(g, lc * mm, 4, p)
    eye_b = jnp.eye(batch, dtype=F32)
    w1 = base[:, None, :, :, None, :] * eye_b[None, :, None, None, :, None]
    w1 = w1.reshape(g, batch, lc * mm, 4 * batch * p)

    def state_out(d, pidx):
        pr = pw_re[d][:, pidx].transpose(0, 2, 1)[:, :, :, None]
        pi = pw_im[d][:, pidx].transpose(0, 2, 1)[:, :, :, None]
        crd = cr[d].transpose(0, 2, 1)[:, :, None, :]
        cid = ci[d].transpose(0, 2, 1)[:, :, None, :]
        return crd * pr - cid * pi, -crd * pi - cid * pr

    of_re, of_im = state_out(0, jnp.arange(lc) + 1)
    ob_re, ob_im = state_out(1, lc - jnp.arange(lc))
    obase = jnp.stack([of_re, of_im, ob_re, ob_im], axis=1).reshape(g, 4, p, lc * mm)
    wout = obase[:, None, :, None, :, :] * eye_b[None, :, None, :, None, None]
    wout = wout.reshape(g, batch, 4 * batch * p, lc * mm)

    def plane(x):
        return jnp.tile(x, (1, batch))

    abar = jnp.stack([plane(pw_re[0][:, lc]), plane(pw_im[0][:, lc]),
                      plane(pw_re[1][:, lc]), plane(pw_im[1][:, lc])], axis=0)
    return w1.astype(BF16), abar, tmat.astype(BF16), wout.astype(BF16)


def _ssm_p1_kernel(ul_ref, uc_ref, w_ref, v_ref, *s_refs):
    b = pl.program_id(1)
    ncc, ncl = uc_ref.shape[0] // SSM_CHUNK, ul_ref.shape[0] // SSM_CHUNK
    nc = ncc + ncl
    pw = s_refs[0].shape[1]
    xs = [jnp.concatenate([uc_ref[pl.ds(s, ncc, stride=SSM_CHUNK), :], ul_ref[pl.ds(s, ncl, stride=SSM_CHUNK), :]],
                          axis=0) for s in range(SSM_CHUNK)]
    for j in range(SSM_LANE_GROUPS):
        v = jnp.concatenate([x[:, j * SSM_GROUP:(j + 1) * SSM_GROUP] for x in xs], axis=1).astype(BF16)
        v_ref[j] = v
        acc = _dot(v, w_ref[j])
        for k, s_ref in enumerate(s_refs):
            part = acc[:, k * pw:(k + 1) * pw]
            rows = slice(j * nc, (j + 1) * nc)

            @pl.when(b == 0)
            def _():
                s_ref[rows, :] = part

            @pl.when(b > 0)
            def _():
                s_ref[rows, :] = s_ref[rows, :] + part


def _ssm_scan_kernel(ncc, ncl, a_ref, sfr_ref, sfi_ref, sbr_ref, sbi_ref, xfr_ref, xfi_ref, xbr_ref, xbi_ref):
    nc = ncc + ncl
    gb = a_ref.shape[1]
    pw = a_ref.shape[2]
    afr, afi, abr, abi = a_ref[0], a_ref[1], a_ref[2], a_ref[3]

    def rows(r):
        return pl.ds(r, gb, stride=nc)

    def step(rf, rb, carry):
        xfr, xfi, xbr, xbi = carry
        xfr_ref[rows(rf), :] = xfr
        xfi_ref[rows(rf), :] = xfi
        xbr_ref[rows(rb), :] = xbr
        xbi_ref[rows(rb), :] = xbi
        sfr = sfr_ref[rows(rf), :]
        sfi = sfi_ref[rows(rf), :]
        sbr = sbr_ref[rows(rb), :]
        sbi = sbi_ref[rows(rb), :]
        return (afr * xfr - afi * xfi + sfr, afr * xfi + afi * xfr + sfi,
                abr * xbr - abi * xbi + sbr, abr * xbi + abi * xbr + sbi)

    zero = jnp.zeros((gb, pw), F32)
    carry = lax.fori_loop(0, ncc, lambda t, c: step(t, ncc - 1 - t, c), (zero, zero, zero, zero))
    lax.fori_loop(0, ncl, lambda t, c: step(ncc + t, nc - 1 - t, c), carry)


def _ssm_p3_kernel(v_ref, xfr_ref, xfi_ref, xbr_ref, xbi_ref, t_ref, wo_ref, zl_ref, zc_ref):
    ncc, ncl = zc_ref.shape[0] // SSM_CHUNK, zl_ref.shape[0] // SSM_CHUNK
    nc = ncc + ncl
    ys = []
    for j in range(SSM_LANE_GROUPS):
        rows = slice(j * nc, (j + 1) * nc)
        xin = jnp.concatenate([xfr_ref[rows, :], xfi_ref[rows, :], xbr_ref[rows, :], xbi_ref[rows, :]],
                              axis=1).astype(BF16)
        y = _dot(v_ref[j], t_ref[j]) + _dot(xin, wo_ref[j])
        ys.append(jax.nn.gelu(y, approximate=True))
    for t in range(SSM_CHUNK):
        zt = jnp.concatenate([y[:, t * SSM_GROUP:(t + 1) * SSM_GROUP] for y in ys], axis=1)
        zc_ref[pl.ds(t, ncc, stride=SSM_CHUNK), :] = zt[:ncc]
        zl_ref[pl.ds(t, ncl, stride=SSM_CHUNK), :] = zt[ncc:]


def _ssm_mixer(u4, w1, abar, tmat, wout, batch, seq, ctx_len):
    n_lt = u4.shape[0]
    g = N_SSM_GROUPS
    lg = SSM_LANE_GROUPS
    ncc, ncl = ctx_len // SSM_CHUNK, seq // SSM_CHUNK
    nc = ncc + ncl
    cw = SSM_CW
    sw = w1.shape[-1]
    pw = sw // 4
    ctx_blk0 = (batch * seq) // ctx_len
    plane_shape = [jax.ShapeDtypeStruct((g * nc, pw), F32)] * 4
    lat_spec = pl.BlockSpec((None, seq, 128), lambda k, b: (k, b, 0))
    ctx_spec = pl.BlockSpec((None, ctx_len, 128), lambda k, b: (k, ctx_blk0 + b, 0))
    v_spec = pl.BlockSpec((lg, None, nc, cw), lambda k, b: (k, b, 0, 0))
    plane_spec = pl.BlockSpec((lg * nc, pw), lambda k, b: (k, 0))
    v_chunks, *s_planes = pl.pallas_call(
        _ssm_p1_kernel,
        grid=(n_lt, batch),
        in_specs=[lat_spec, ctx_spec, pl.BlockSpec((lg, None, cw, sw), lambda k, b: (k, b, 0, 0))],
        out_specs=[v_spec] + [plane_spec] * 4,
        out_shape=[jax.ShapeDtypeStruct((g, batch, nc, cw), BF16)] + plane_shape,
        compiler_params=_cparams(("parallel", "arbitrary")),
        name="ssm_chunk_states",
    )(u4, u4, w1)
    gb = SSM_SCAN_GROUPS
    x_planes = pl.pallas_call(
        functools.partial(_ssm_scan_kernel, ncc, ncl),
        grid=(g // gb,),
        in_specs=[pl.BlockSpec((4, gb, pw), lambda i: (0, i, 0))]
        + [pl.BlockSpec((gb * nc, pw), lambda i: (i, 0))] * 4,
        out_specs=[pl.BlockSpec((gb * nc, pw), lambda i: (i, 0))] * 4,
        out_shape=plane_shape,
        compiler_params=_cparams(("parallel",)),
        name="ssm_chunk_scan",
    )(abar, *s_planes)
    return pl.pallas_call(
        _ssm_p3_kernel,
        grid=(n_lt, batch),
        in_specs=[v_spec] + [plane_spec] * 4
        + [pl.BlockSpec((lg, cw, cw), lambda k, b: (k, 0, 0)),
           pl.BlockSpec((lg, None, sw, cw), lambda k, b: (k, b, 0, 0))],
        out_specs=[pl.BlockSpec((None, seq, 128), lambda k, b: (k, b, 0)),
                   pl.BlockSpec((None, ctx_len, 128), lambda k, b: (k, b, 0))],
        out_shape=[jax.ShapeDtypeStruct((n_lt, batch * seq, 128), F32),
                   jax.ShapeDtypeStruct((n_lt, batch * ctx_len, 128), F32)],
        compiler_params=_cparams(("parallel", "arbitrary")),
        name="ssm_chunk_outputs",
    )(v_chunks, *x_planes, tmat, wout)


def _merge_kernel(attn_ref, z_ref, ga_ref, gs_ref, x_ref, mod_ref, g2_ref, wglu_ref, wba_ref, wbs_ref,
                  wout_ref, wrh_ref, wrl_ref, xo_ref, h2_ref, rt_ref):
    zf = jnp.concatenate([z_ref[j] for j in range(SSM_LANE_TILES)], axis=1)
    z = zf.astype(BF16)
    glu = (z.astype(F32) * jax.nn.sigmoid(_dot(z, wglu_ref[...]))).astype(BF16)
    mix = (ga_ref[...].astype(F32) * _dot(attn_ref[...], wba_ref[...])
           + gs_ref[...].astype(F32) * _dot(glu, wbs_ref[...])).astype(BF16)
    m = mod_ref[0]
    x = x_ref[...] + m[2:3] * _dot(mix, wout_ref[...])
    xo_ref[...] = x
    h2 = _rms_mod(x, g2_ref[...], m[3:4], m[4:5])
    hi, lo = _split_bf16(h2)
    h2_ref[...] = hi.astype(F32)
    logits = _dot(hi, wrh_ref[...]) + _dot(lo, wrh_ref[...]) + _dot(hi, wrl_ref[...])
    rt_ref[...] = _route(logits)


def _route(lg):
    ng, epg = N_EXPERT_GROUPS, EXPERTS_PER_GROUP
    lane_i = lax.broadcasted_iota(jnp.int32, lg.shape, 1)
    lane = lane_i.astype(F32)
    big = float(ROUTER_PAD)

    def rmax(mask_val):
        return jnp.max(mask_val, axis=-1, keepdims=True)

    def first_lane(mask, val, mx):
        return jnp.min(jnp.where(mask, jnp.where(val == mx, lane, big), big), axis=-1, keepdims=True)

    gmask = lane_i < ng
    lgm = jnp.where(gmask, lg, NEG_INF)
    mg = rmax(lgm)
    g_prob = 1.0 / jnp.sum(jnp.exp(lgm - mg), axis=-1, keepdims=True)
    g_idx = first_lane(gmask, lg, mg)
    egroup = jnp.floor((lane - float(ng)) * (1.0 / epg))
    emask = egroup == g_idx
    l1 = jnp.where(emask, lg, NEG_INF)
    m1 = rmax(l1)
    i1 = first_lane(emask, lg, m1)
    l2 = jnp.where(lane == i1, NEG_INF, l1)
    m2 = rmax(l2)
    i2 = jnp.min(jnp.where(l2 == m2, jnp.where(emask, lane, big), big), axis=-1, keepdims=True)
    r = jnp.exp(m2 - m1)
    w1 = g_prob / (1.0 + r)
    w2 = w1 * r
    return jnp.where(lane_i == 0, i1 - float(ng),
                     jnp.where(lane_i == 1, i2 - float(ng),
                               jnp.where(lane_i == 2, w1, jnp.where(lane_i == 3, w2, 0.0))))


def _merge(attn, z, ga, gs, x_all, mod, g2, w_glu, w_ba, w_bs, w_out, wr_hi, wr_lo, n_tiles, tiles_per_seq, batch):
    d = D_MODEL
    row = lambda i: (i, 0)
    const = lambda i: (0, 0)
    modi = lambda i: (jnp.minimum(i // tiles_per_seq, batch), 0, 0)
    n = n_tiles * ROW_TILE
    return pl.pallas_call(
        _merge_kernel,
        grid=(n_tiles,),
        in_specs=[
            pl.BlockSpec((ROW_TILE, Q_W), row),
            pl.BlockSpec((SSM_LANE_TILES, ROW_TILE, 128), lambda i: (0, i, 0)),
            pl.BlockSpec((ROW_TILE, d), row),
            pl.BlockSpec((ROW_TILE, d), row),
            pl.BlockSpec((ROW_TILE, d), row),
            pl.BlockSpec((1, 6, d), modi),
            pl.BlockSpec((1, d), const),
            pl.BlockSpec((D_SSM, D_SSM), const),
            pl.BlockSpec((Q_W, d), const),
            pl.BlockSpec((D_SSM, d), const),
            pl.BlockSpec((d, d), const),
            pl.BlockSpec((d, ROUTER_PAD), const),
            pl.BlockSpec((d, ROUTER_PAD), const),
        ],
        out_specs=[pl.BlockSpec((ROW_TILE, d), row), pl.BlockSpec((ROW_TILE, d), row),
                   pl.BlockSpec((ROW_TILE, ROUTER_PAD), row)],
        out_shape=[jax.ShapeDtypeStruct((n, d), F32), jax.ShapeDtypeStruct((n, d), F32),
                   jax.ShapeDtypeStruct((n, ROUTER_PAD), F32)],
        compiler_params=_cparams(("parallel",)),
        name="merge_router",
    )(attn, z, ga, gs, x_all, mod, g2, w_glu, w_ba, w_bs, w_out, wr_hi, wr_lo)


def _expert_kernel(be_ref, nu_ref, x_ref, wg_ref, wu_ref, wd_ref, y_ref, wg_s, wu_s, wd_s):
    i = pl.program_id(0)
    changed = jnp.logical_or(i == 0, be_ref[i] != be_ref[jnp.maximum(i - 1, 0)])

    @pl.when(changed)
    def _():
        wg_s[...] = wg_ref[0, 0].astype(BF16)
        wu_s[...] = wu_ref[0, 0].astype(BF16)
        wd_s[...] = wd_ref[0, 0].astype(BF16)

    @pl.when(i < nu_ref[0])
    def _():
        x = x_ref[...].astype(BF16)
        gate = _dot(x, wg_s[...])
        up = _dot(x, wu_s[...])
        hid = (gate * jax.nn.sigmoid(gate) * up).astype(BF16)
        y_ref[...] = _dot(hid, wd_s[...]).astype(y_ref.dtype)

    @pl.when(i >= nu_ref[0])
    def _():
        y_ref[...] = jnp.zeros_like(y_ref)


def _experts(blk_exp, n_used, xs, layer, w_gate, w_up, w_down):
    n_slots, d = xs.shape
    n_blk = n_slots // MOE_BLOCK
    de = w_gate.shape[-1]
    return pl.pallas_call(
        _expert_kernel,
        grid_spec=pltpu.PrefetchScalarGridSpec(
            num_scalar_prefetch=2,
            grid=(n_blk,),
            in_specs=[
                pl.BlockSpec((MOE_BLOCK, d), lambda i, be, nu: (i, 0)),
                pl.BlockSpec((1, 1, d, de), lambda i, be, nu: (layer, be[i], 0, 0)),
                pl.BlockSpec((1, 1, d, de), lambda i, be, nu: (layer, be[i], 0, 0)),
                pl.BlockSpec((1, 1, de, d), lambda i, be, nu: (layer, be[i], 0, 0)),
            ],
            out_specs=pl.BlockSpec((MOE_BLOCK, d), lambda i, be, nu: (i, 0)),
            scratch_shapes=[pltpu.VMEM((d, de), BF16), pltpu.VMEM((d, de), BF16), pltpu.VMEM((de, d), BF16)],
        ),
        out_shape=jax.ShapeDtypeStruct((n_slots, d), BF16),
        compiler_params=_cparams(("arbitrary",)),
        name="expert_mlp",
    )(blk_exp, n_used, xs, w_gate, w_up, w_down)


def _plan_kernel(rt_ref, dest_ref, cnt_ref, run_s):
    p = pl.program_id(0)
    i = pl.program_id(1)
    rows = rt_ref.shape[0]
    lane = lax.broadcasted_iota(jnp.int32, (rows, ROUTER_PAD), 1)
    lane_f = lane.astype(F32)
    rt = rt_ref[...]
    oh0 = jnp.where(lane_f == rt[:, 0:1], 1.0, 0.0)
    oh1 = jnp.where(lane_f == rt[:, 1:2], 1.0, 0.0)
    tot0 = jnp.sum(oh0, axis=0, keepdims=True)
    tot1 = jnp.sum(oh1, axis=0, keepdims=True)

    @pl.when(jnp.logical_and(p == 0, i == 0))
    def _():
        run_s[...] = jnp.zeros_like(run_s)

    @pl.when(p == 0)
    def _():
        run_s[0:1, :] = run_s[0:1, :] + tot0 + tot1

    @pl.when(jnp.logical_and(p == 1, i == 0))
    def _():
        counts = run_s[...]
        padded = jnp.floor((counts + float(MOE_BLOCK - 1)) * (1.0 / MOE_BLOCK)) * float(MOE_BLOCK)
        r = lax.broadcasted_iota(jnp.int32, (ROUTER_PAD, ROUTER_PAD), 0)
        c = lax.broadcasted_iota(jnp.int32, (ROUTER_PAD, ROUTER_PAD), 1)
        incl = jnp.where(r <= c, 1.0, 0.0).astype(BF16)
        hi, lo = _split_bf16(padded)
        pend = _dot(hi, incl) + _dot(lo, incl)
        pstart = pend - padded
        cnt_ref[...] = jnp.where(lax.broadcasted_iota(jnp.int32, counts.shape, 0) == 0, pend, 0.0)
        run_s[...] = pstart

    @pl.when(p == 1)
    def _():
        r = lax.broadcasted_iota(jnp.int32, (rows, rows), 0)
        c = lax.broadcasted_iota(jnp.int32, (rows, rows), 1)
        before = jnp.where(c < r, 1.0, 0.0).astype(BF16)
        run = run_s[0:1, :]
        pos0 = jnp.sum(oh0 * (run + _dot(before, oh0.astype(BF16))), axis=-1, keepdims=True)
        pos1 = jnp.sum(oh1 * (run + tot0 + _dot(before, oh1.astype(BF16))), axis=-1, keepdims=True)
        run_s[0:1, :] = run + tot0 + tot1
        dest_ref[...] = jnp.where(lane == 0, pos0, jnp.where(lane == 1, pos1, 0.0)).astype(jnp.int32)


def _plan(route, n_tiles):
    return pl.pallas_call(
        _plan_kernel,
        grid=(2, n_tiles),
        in_specs=[pl.BlockSpec((ROW_TILE, ROUTER_PAD), lambda p, i: (i, 0))],
        out_specs=[pl.BlockSpec((ROW_TILE, ROUTER_PAD), lambda p, i: (i * p, 0)),
                   pl.BlockSpec((8, ROUTER_PAD), lambda p, i: (0, 0))],
        out_shape=[jax.ShapeDtypeStruct((n_tiles * ROW_TILE, ROUTER_PAD), jnp.int32),
                   jax.ShapeDtypeStruct((8, ROUTER_PAD), F32)],
        scratch_shapes=[pltpu.VMEM((8, ROUTER_PAD), F32)],
        compiler_params=_cparams(("arbitrary", "arbitrary")),
        name="dispatch_plan",
    )(route)


def _dispatch_kernel(dest_ref, h_ref, xs_in_ref, xs_ref, sem):
    del xs_in_ref
    rows = h_ref.shape[0]

    def row_copy(r, d):
        return pltpu.make_async_copy(h_ref.at[pl.ds(r, 1)], xs_ref.at[pl.ds(d, 1)], sem)

    def issue(r, carry):
        for k in range(TOP_K):
            row_copy(r, dest_ref[0, 0, k * rows + r]).start()
        return carry

    lax.fori_loop(0, rows, issue, 0, unroll=8)
    for _ in range(TOP_K * rows):
        row_copy(0, 0).wait()


def _dispatch(dest_tiles, h2, xs_init, n_tiles):
    d = h2.shape[1]
    return pl.pallas_call(
        _dispatch_kernel,
        grid=(n_tiles,),
        in_specs=[pl.BlockSpec((1, 1, TOP_K * ROW_TILE), lambda i: (i, 0, 0), memory_space=pltpu.SMEM),
                  pl.BlockSpec((ROW_TILE, d), lambda i: (i, 0)),
                  pl.BlockSpec(memory_space=pl.ANY)],
        out_specs=pl.BlockSpec(memory_space=pl.ANY),
        out_shape=jax.ShapeDtypeStruct(xs_init.shape, xs_init.dtype),
        scratch_shapes=[pltpu.SemaphoreType.DMA],
        input_output_aliases={2: 0},
        compiler_params=_cparams(("arbitrary",)),
        name="dispatch_rows",
    )(dest_tiles, h2, xs_init)


def _moe(h2, route, n_tiles, layer, w_gate, w_up, w_down):
    t, d = h2.shape
    dest, cnt = _plan(route, n_tiles)
    pend = cnt[0, :N_EXPERTS].astype(jnp.int32)
    n_blk = -(-(t * TOP_K) // MOE_BLOCK) + N_EXPERTS
    blk_start = jnp.arange(n_blk, dtype=jnp.int32) * MOE_BLOCK
    blk_exp = jnp.minimum(jnp.sum((pend[None, :] <= blk_start[:, None]).astype(jnp.int32), axis=1), N_EXPERTS - 1)
    n_used = (pend[N_EXPERTS - 1] // MOE_BLOCK).reshape(1)
    dest2 = dest[:, :TOP_K]
    dest_tiles = dest2.reshape(n_tiles, ROW_TILE, TOP_K).transpose(0, 2, 1).reshape(n_tiles, 1, TOP_K * ROW_TILE)
    xs = _dispatch(dest_tiles, h2, jnp.zeros((n_blk * MOE_BLOCK, d), F32), n_tiles)
    ys = _experts(blk_exp, n_used, xs, layer, w_gate, w_up, w_down)
    return ys[dest2[:, 0]], ys[dest2[:, 1]]


def _final_kernel(x_ref, y0_ref, y1_ref, rt_ref, mod_ref, g_ref, o_ref):
    x = _moe_residual(x_ref, y0_ref, y1_ref, rt_ref, mod_ref)
    o_ref[...] = x * lax.rsqrt(jnp.mean(x * x, axis=-1, keepdims=True) + EPS) * g_ref[...]


def _final(x_lat, moe_out, mod, g_final, n_tiles, tiles_per_seq, batch):
    d = D_MODEL
    row = lambda i: (i, 0)
    return pl.pallas_call(
        _final_kernel,
        grid=(n_tiles,),
        in_specs=[pl.BlockSpec((ROW_TILE, d), row), pl.BlockSpec((ROW_TILE, d), row),
                  pl.BlockSpec((ROW_TILE, d), row), pl.BlockSpec((ROW_TILE, ROUTER_PAD), row),
                  pl.BlockSpec((1, 6, d), lambda i: (jnp.minimum(i // tiles_per_seq, batch), 0, 0)),
                  pl.BlockSpec((1, d), lambda i: (0, 0))],
        out_specs=pl.BlockSpec((ROW_TILE, d), row),
        out_shape=jax.ShapeDtypeStruct((n_tiles * ROW_TILE, d), F32),
        compiler_params=_cparams(("parallel",)),
        name="final_norm",
    )(x_lat, *moe_out, mod, g_final)


def _rope_tables(seq):
    quarter = HEAD_DIM // 4
    freqs = ROPE_THETA ** (-jnp.arange(quarter, dtype=F32) / quarter)
    pos = jnp.arange(seq)
    ang_r = (pos // GRID_W).astype(F32)[:, None] * freqs[None, :]
    ang_c = (pos % GRID_W).astype(F32)[:, None] * freqs[None, :]
    cos = jnp.concatenate([jnp.cos(ang_r)] * 2 + [jnp.cos(ang_c)] * 2, axis=-1)
    sin = jnp.concatenate([-jnp.sin(ang_r), jnp.sin(ang_r), -jnp.sin(ang_c), jnp.sin(ang_c)], axis=-1)
    reps = 128 // HEAD_DIM
    cos = jnp.tile(cos, (1, reps))
    sin = jnp.tile(sin, (1, reps))
    cos = jnp.concatenate([cos, jnp.ones((ROW_TILE, 128), F32)], axis=0)
    sin = jnp.concatenate([sin, jnp.zeros((ROW_TILE, 128), F32)], axis=0)
    return cos, sin


def kernel(x, c, ctx, c_ctx, w_mod, b_mod, g_norm1, g_norm2, w_in, attn_sink, ssm_lam_re, ssm_lam_im,
           ssm_log_dt, ssm_b_re, ssm_b_im, ssm_c_re, ssm_c_im, ssm_d, w_glu, w_br_attn, w_br_ssm, w_out,
           w_router_group, w_router_expert, w_exp_gate, w_exp_up, w_exp_down, g_final):
    batch, seq, d = x.shape
    ctx_len = ctx.shape[1]
    depth = w_mod.shape[0]
    assert d == D_MODEL and batch * SSM_STATE == 128
    assert seq % ROW_TILE == 0 and ctx_len % ROW_TILE == 0 and (batch * seq) % ctx_len == 0
    assert batch + 1 <= MOD_ROWS
    t_lat, t_ctx = batch * seq, batch * ctx_len
    tiles_per_seq = seq // ROW_TILE
    n_lat_tiles = t_lat // ROW_TILE
    n_all_tiles = (t_lat + t_ctx) // ROW_TILE
    nb_per_seq = seq // ATTN_BLOCK
    ncc, ncl = ctx_len // SSM_CHUNK, seq // SSM_CHUNK

    c_rows = jnp.zeros((MOD_ROWS, d), F32).at[:batch].set(c).at[batch].set(c_ctx)
    mod_all = _modulation(c_rows, w_mod, b_mod).reshape(depth, MOD_ROWS, 6, d)
    cos_t, sin_t = _rope_tables(seq)
    x_all = jnp.concatenate([x.reshape(t_lat, d), ctx.reshape(t_ctx, d)], axis=0)
    f_all = None
    for l in range(depth):
        ctx_out = l < depth - 1
        mod = mod_all[l]
        x_all, (q, k, v, u, ga, gs) = _inproj(
            x_all, f_all, mod_all[l - 1] if l > 0 else None, mod, g_norm1[l].reshape(1, d), cos_t, sin_t,
            w_in[l].astype(BF16), n_all_tiles, tiles_per_seq, n_lat_tiles, batch)
        sink = attn_sink[l].astype(F32)
        attn = _attention(sink, q, k, v, t_lat // ATTN_BLOCK, nb_per_seq, t_lat // ctx_len, ctx_len, True, 0)
        if ctx_out:
            attn_c = _attention(sink, q, k, v, t_ctx // ATTN_BLOCK, ctx_len // ATTN_BLOCK, t_lat // ctx_len,
                                ctx_len, False, t_lat // ATTN_BLOCK)
            attn = jnp.concatenate([attn, attn_c], axis=0)
        w1, abar, tmat, wout = _ssm_weights(ssm_lam_re[l], ssm_lam_im[l], ssm_log_dt[l], ssm_b_re[l], ssm_b_im[l],
                                            ssm_c_re[l], ssm_c_im[l], ssm_d[l], batch)
        z, z_ctx = _ssm_mixer(u, w1, abar, tmat, wout, batch, seq, ctx_len)
        if ctx_out:
            z = jnp.concatenate([z, z_ctx], axis=1)
        w_r = jnp.zeros((d, ROUTER_PAD), F32)
        w_r = w_r.at[:, :N_EXPERT_GROUPS].set(w_router_group[l])
        w_r = w_r.at[:, N_EXPERT_GROUPS:N_EXPERT_GROUPS + N_EXPERTS].set(w_router_expert[l])
        wr_hi, wr_lo = _split_bf16(w_r)
        n_tiles = n_all_tiles if ctx_out else n_lat_tiles
        x_all, h2, route = _merge(attn, z, ga, gs, x_all, mod, g_norm2[l].reshape(1, d), w_glu[l].astype(BF16),
                                  w_br_attn[l].astype(BF16), w_br_ssm[l].astype(BF16), w_out[l].astype(BF16),
                                  wr_hi, wr_lo, n_tiles, tiles_per_seq, batch)
        y0, y1 = _moe(h2, route, n_tiles, l, w_exp_gate, w_exp_up, w_exp_down)
        f_all = (y0, y1, route)
    out = _final(x_all, f_all, mod_all[depth - 1], g_final.reshape(1, d), n_lat_tiles, tiles_per_seq, batch)
    return out.reshape(batch, seq, d)
```

```python
import functools
import math

import jax
import jax.numpy as jnp
from jax import lax
from jax.experimental import pallas as pl
from jax.experimental.pallas import tpu as pltpu

F32 = jnp.float32
BF16 = jnp.bfloat16

D_MODEL = 1024
GRID_W = 64
N_HEADS = 8
N_KV_HEADS = 2
HEAD_DIM = 64
Q_GROUP = N_HEADS // N_KV_HEADS
ATTN_BLOCK = 128
ROPE_THETA = 10000.0
D_SSM = D_MODEL // 2
SSM_GROUP = 16
N_SSM_GROUPS = D_SSM // SSM_GROUP
SSM_STATE = 64
N_EXPERT_GROUPS = 4
EXPERTS_PER_GROUP = 8
N_EXPERTS = N_EXPERT_GROUPS * EXPERTS_PER_GROUP
TOP_K = 2
D_EXPERT = D_MODEL // 2
Q_W = N_HEADS * HEAD_DIM
KV_W = N_KV_HEADS * HEAD_DIM
O_K = Q_W
O_V = O_K + KV_W
O_U = O_V + KV_W
O_GA = O_U + D_SSM
O_GS = O_GA + D_MODEL
D_IN = O_GS + D_MODEL
EPS = 1e-6
NEG_INF = -1e30

ROW_TILE = 256
SSM_CHUNK = 16
SSM_CW = SSM_CHUNK * SSM_GROUP
SSM_SCAN_GROUPS = 8
SSM_LANE_GROUPS = 128 // SSM_GROUP
SSM_LANE_TILES = D_SSM // 128
MOE_BLOCK = 256
PLAN_TILE = 512
ROUTER_PAD = 128
MOD_ROWS = 8
VMEM_LIMIT = 48 * 1024 * 1024


def _cparams(sem):
    return pltpu.CompilerParams(dimension_semantics=sem, vmem_limit_bytes=VMEM_LIMIT)


def _dot(a, b):
    return jnp.dot(a, b, preferred_element_type=F32)


def _split_bf16(a):
    hi = a.astype(BF16)
    lo = (a - hi.astype(F32)).astype(BF16)
    return hi, lo


def _rms_mod(x, g, shift, scale):
    y = x * lax.rsqrt(jnp.mean(x * x, axis=-1, keepdims=True) + EPS) * g
    return y * (1.0 + scale) + shift


def _mod_kernel(c_ref, w_ref, b_ref, o_ref):
    c = c_ref[...]
    s_hi, s_lo = _split_bf16(c * jax.nn.sigmoid(c))
    w_hi, w_lo = _split_bf16(w_ref[0])
    o_ref[0] = _dot(s_hi, w_hi) + _dot(s_lo, w_hi) + _dot(s_hi, w_lo) + b_ref[0]


def _modulation(c_rows, w_mod, b_mod):
    depth, d, n = w_mod.shape
    nb = n // 4
    return pl.pallas_call(
        _mod_kernel,
        grid=(depth, n // nb),
        in_specs=[
            pl.BlockSpec((MOD_ROWS, d), lambda l, j: (0, 0)),
            pl.BlockSpec((1, d, nb), lambda l, j: (l, 0, j)),
            pl.BlockSpec((1, 1, nb), lambda l, j: (l, 0, j)),
        ],
        out_specs=pl.BlockSpec((1, MOD_ROWS, nb), lambda l, j: (l, 0, j)),
        out_shape=jax.ShapeDtypeStruct((depth, MOD_ROWS, n), F32),
        compiler_params=_cparams(("arbitrary", "arbitrary")),
        name="modulation",
    )(c_rows, w_mod, b_mod.reshape(depth, 1, n))


def _moe_residual(x_ref, y0_ref, y1_ref, rt_ref, mod_ref):
    rt = rt_ref[...]
    f = rt[:, 2:3] * y0_ref[...].astype(F32) + rt[:, 3:4] * y1_ref[...].astype(F32)
    return x_ref[...] + mod_ref[0, 5:6, :] * f


def _inproj_kernel(has_f, *refs):
    if has_f:
        (x_ref, y0_ref, y1_ref, rt_ref, modp_ref, mod_ref, g_ref, cos_ref, sin_ref, w_ref,
         xo_ref, q_ref, k_ref, v_ref, u_ref, ga_ref, gs_ref) = refs
        x = _moe_residual(x_ref, y0_ref, y1_ref, rt_ref, modp_ref)
        xo_ref[...] = x
    else:
        (x_ref, mod_ref, g_ref, cos_ref, sin_ref, w_ref,
         q_ref, k_ref, v_ref, u_ref, ga_ref, gs_ref) = refs
        x = x_ref[...]
    m = mod_ref[0]
    h = _rms_mod(x, g_ref[...], m[0:1], m[1:2]).astype(BF16)
    cos = cos_ref[...]
    sin = sin_ref[...]
    lane = lax.broadcasted_iota(jnp.int32, cos.shape, 1)
    first = (lane % (HEAD_DIM // 2)) < (HEAD_DIM // 4)

    def rope(t):
        sw = jnp.where(first, pltpu.roll(t, 128 - HEAD_DIM // 4, 1), pltpu.roll(t, HEAD_DIM // 4, 1))
        return t * cos + sw * sin

    def proj(lo, hi):
        return _dot(h, w_ref[:, lo:hi])

    q = proj(0, O_K)
    for j in range(Q_W // 128):
        q_ref[:, 128 * j:128 * (j + 1)] = (rope(q[:, 128 * j:128 * (j + 1)]) * HEAD_DIM ** -0.5).astype(BF16)
    k_ref[...] = rope(proj(O_K, O_V)).astype(BF16)
    v_ref[...] = proj(O_V, O_U).astype(BF16)
    u = proj(O_U, O_GA)
    for j in range(SSM_LANE_TILES):
        u_ref[j] = u[:, 128 * j:128 * (j + 1)]
    ga_ref[...] = jax.nn.sigmoid(proj(O_GA, O_GS)).astype(BF16)
    gs_ref[...] = jax.nn.sigmoid(proj(O_GS, D_IN)).astype(BF16)


def _inproj(x_all, moe_out, mod_prev, mod, g1, cos_t, sin_t, w_in, n_tiles, tiles_per_seq, n_lat_tiles, batch):
    t_all, d = x_all.shape
    has_f = moe_out is not None
    row = lambda i: (i, 0)
    modi = lambda i: (jnp.minimum(i // tiles_per_seq, batch), 0, 0)
    const = lambda i: (0, 0)
    ropei = lambda i: (jnp.where(i < n_lat_tiles, i % tiles_per_seq, tiles_per_seq), 0)
    in_specs = [pl.BlockSpec((ROW_TILE, d), row)]
    args = [x_all]
    if has_f:
        in_specs += [pl.BlockSpec((ROW_TILE, d), row), pl.BlockSpec((ROW_TILE, d), row),
                     pl.BlockSpec((ROW_TILE, ROUTER_PAD), row), pl.BlockSpec((1, 6, d), modi)]
        args += [*moe_out, mod_prev]
    in_specs += [
        pl.BlockSpec((1, 6, d), modi),
        pl.BlockSpec((1, d), const),
        pl.BlockSpec((ROW_TILE, 128), ropei),
        pl.BlockSpec((ROW_TILE, 128), ropei),
        pl.BlockSpec((d, D_IN), const),
    ]
    args += [mod, g1, cos_t, sin_t, w_in]
    widths = [Q_W, KV_W, KV_W, D_SSM, D_MODEL, D_MODEL]
    out_specs = [pl.BlockSpec((ROW_TILE, w), row) for w in widths]
    out_shape = [jax.ShapeDtypeStruct((n_tiles * ROW_TILE, w), BF16) for w in widths]
    out_specs[3] = pl.BlockSpec((SSM_LANE_TILES, ROW_TILE, 128), lambda i: (0, i, 0))
    out_shape[3] = jax.ShapeDtypeStruct((SSM_LANE_TILES, n_tiles * ROW_TILE, 128), F32)
    if has_f:
        out_specs = [pl.BlockSpec((ROW_TILE, d), row)] + out_specs
        out_shape = [jax.ShapeDtypeStruct((n_tiles * ROW_TILE, d), F32)] + out_shape
    outs = pl.pallas_call(
        functools.partial(_inproj_kernel, has_f),
        grid=(n_tiles,),
        in_specs=in_specs,
        out_specs=out_specs,
        out_shape=out_shape,
        compiler_params=_cparams(("parallel",)),
        name="inproj",
    )(*args)
    if has_f:
        return outs[0], outs[1:]
    return x_all, outs


def _attn_kernel(band, nb_per_seq, sink_ref, *refs):
    if band:
        q_ref, kp_ref, kc_ref, kn_ref, vp_ref, vc_ref, vn_ref, kx_ref, vx_ref, o_ref = refs
    else:
        q_ref, kx_ref, vx_ref, o_ref = refs
    blk = q_ref.shape[0]
    if band:
        j = pl.program_id(0) % nb_per_seq
        r = lax.broadcasted_iota(jnp.int32, (blk, blk), 0)
        c = lax.broadcasted_iota(jnp.int32, (blk, blk), 1)
        edge_p = jnp.where(j > 0, 0.0, NEG_INF).astype(F32)
        edge_n = jnp.where(j < nb_per_seq - 1, 0.0, NEG_INF).astype(F32)
        bias_p = jnp.where(c >= r, edge_p, NEG_INF).astype(F32)
        bias_n = jnp.where(c <= r, edge_n, NEG_INF).astype(F32)
        bias = jnp.concatenate(
            [bias_p, jnp.zeros((blk, blk), F32), bias_n, jnp.zeros((blk, kx_ref.shape[0]), F32)], axis=1)
    outs = []
    for g in range(N_KV_HEADS):
        gs = slice(g * HEAD_DIM, (g + 1) * HEAD_DIM)
        if band:
            k_all = jnp.concatenate([kp_ref[:, gs], kc_ref[:, gs], kn_ref[:, gs], kx_ref[:, gs]], axis=0)
            v_all = jnp.concatenate([vp_ref[:, gs], vc_ref[:, gs], vn_ref[:, gs], vx_ref[:, gs]], axis=0)
        else:
            k_all = kx_ref[:, gs]
            v_all = vx_ref[:, gs]
        qg = jnp.concatenate(
            [q_ref[:, (g * Q_GROUP + h) * HEAD_DIM:(g * Q_GROUP + h + 1) * HEAD_DIM] for h in range(Q_GROUP)],
            axis=0)
        s_all = lax.dot_general(qg, k_all, (((1,), (1,)), ((), ())), preferred_element_type=F32)
        for h in range(Q_GROUP):
            s = s_all[h * blk:(h + 1) * blk]
            if band:
                s = s + bias
            sink = sink_ref[g * Q_GROUP + h]
            m = jnp.maximum(jnp.max(s, axis=-1, keepdims=True), sink)
            p = jnp.exp(s - m)
            denom = jnp.sum(p, axis=-1, keepdims=True) + jnp.exp(sink - m)
            o = _dot(p.astype(BF16), v_all) * (1.0 / denom)
            outs.append(o.astype(BF16))
    o_ref[...] = jnp.concatenate(outs, axis=1)


def _attention(sink, q, k, v, n_blocks, nb_per_seq, kx_block0, ctx_len, band, q_block0):
    blk = ATTN_BLOCK
    qi = lambda i, s: (q_block0 + i, 0)
    cur = lambda i, s: (i, 0)
    prv = lambda i, s: (jnp.maximum(i - 1, 0), 0)
    nxt = lambda i, s: (jnp.minimum(i + 1, n_blocks - 1), 0)
    kxi = lambda i, s: (kx_block0 + i // nb_per_seq, 0)
    kspec = lambda f: pl.BlockSpec((blk, KV_W), f)
    xspec = pl.BlockSpec((ctx_len, KV_W), kxi)
    if band:
        in_specs = [pl.BlockSpec((blk, Q_W), qi), kspec(prv), kspec(cur), kspec(nxt),
                    kspec(prv), kspec(cur), kspec(nxt), xspec, xspec]
        args = (q, k, k, k, v, v, v, k, v)
    else:
        in_specs = [pl.BlockSpec((blk, Q_W), qi), xspec, xspec]
        args = (q, k, v)
    return pl.pallas_call(
        functools.partial(_attn_kernel, band, nb_per_seq),
        grid_spec=pltpu.PrefetchScalarGridSpec(
            num_scalar_prefetch=1,
            grid=(n_blocks,),
            in_specs=in_specs,
            out_specs=pl.BlockSpec((blk, Q_W), lambda i, s: (i, 0)),
        ),
        out_shape=jax.ShapeDtypeStruct((n_blocks * blk, Q_W), BF16),
        compiler_params=_cparams(("parallel",)),
        name="band_attention" if band else "context_attention",
    )(sink, *args)


def _ssm_weights(lam_re, lam_im, log_dt, b_re, b_im, c_re, c_im, d_skip, batch):
    hp = lax.Precision.HIGHEST
    g, p, mm, lc = N_SSM_GROUPS, SSM_STATE, SSM_GROUP, SSM_CHUNK
    lr, li = lam_re.astype(F32), lam_im.astype(F32)
    dt = jnp.exp(log_dt.astype(F32))[..., None]
    mag = jnp.exp(lr * dt)
    a_re = mag * jnp.cos(li * dt)
    a_im = mag * jnp.sin(li * dt)
    den = lr * lr + li * li
    nr = a_re - 1.0
    f_re = (nr * lr + a_im * li) / den
    f_im = (a_im * lr - nr * li) / den
    br, bi = b_re.astype(F32), b_im.astype(F32)
    bb_re = f_re[..., None] * br - f_im[..., None] * bi
    bb_im = f_re[..., None] * bi + f_im[..., None] * br
    jj = jnp.arange(lc + 1, dtype=F32)[None, None, :, None]
    pmag = jnp.exp(lr[:, :, None, :] * dt[:, :, None, :] * jj)
    pang = li[:, :, None, :] * dt[:, :, None, :] * jj
    pw_re = pmag * jnp.cos(pang)
    pw_im = pmag * jnp.sin(pang)
    cr, ci = c_re.astype(F32), c_im.astype(F32)
    cp_re = cr[:, :, None] * pw_re[:, :, :, None, :] - ci[:, :, None] * pw_im[:, :, :, None, :]
    cp_im = cr[:, :, None] * pw_im[:, :, :, None, :] + ci[:, :, None] * pw_re[:, :, :, None, :]
    kk = jnp.einsum('dgxk,dgkn->dgxn',
                    jnp.concatenate([cp_re, -cp_im], axis=-1).reshape(2, g, (lc + 1) * mm, 2 * p),
                    jnp.concatenate([bb_re, bb_im], axis=-2), precision=hp).reshape(2, g, lc + 1, mm, mm)
    s_i = jnp.arange(lc)[:, None]
    t_i = jnp.arange(lc)[None, :]
    kf = kk[0][:, jnp.clip(t_i - s_i, 0, lc)] * (t_i >= s_i)[None, :, :, None, None].astype(F32)
    kb = kk[1][:, jnp.clip(s_i - t_i, 0, lc)] * (s_i >= t_i)[None, :, :, None, None].astype(F32)
    dg = d_skip.astype(F32).reshape(g, mm)
    eye_t = jnp.eye(lc, dtype=F32)[None, :, :, None, None]
    eye_m = jnp.eye(mm, dtype=F32)[None, None, None, :, :]
    tm = kf + kb + eye_t * eye_m * dg[:, None, None, :, None]
    tmat = tm.transpose(0, 1, 4, 2, 3).reshape(g, lc * mm, lc * mm)

    def state_in(d, pidx):
        pr = pw_re[d][:, pidx]
        pi = pw_im[d][:, pidx]
        w_re = pr[:, :, None, :] * bb_re[d].transpose(0, 2, 1)[:, None] - pi[:, :, None, :] * bb_im[d].transpose(0, 2, 1)[:, None]
        w_im = pr[:, :, None, :] * bb_im[d].transpose(0, 2, 1)[:, None] + pi[:, :, None, :] * bb_re[d].transpose(0, 2, 1)[:, None]
        return w_re, w_im

    wf_re, wf_im = state_in(0, lc - 1 - jnp.arange(lc))
    wb_re, wb_im = state_in(1, jnp.arange(lc))
    base = jnp.stack([wf_re, wf_im, wb_re, wb_im], axis=3).reshape(g, lc * mm, 4, p)
    eye_b = jnp.eye(batch, dtype=F32)
    w1 = base[:, None, :, :, None, :] * eye_b[None, :, None, None, :, None]
    w1 = w1.reshape(g, batch, lc * mm, 4 * batch * p)

    def state_out(d, pidx):
        pr = pw_re[d][:, pidx].transpose(0, 2, 1)[:, :, :, None]
        pi = pw_im[d][:, pidx].transpose(0, 2, 1)[:, :, :, None]
        crd = cr[d].transpose(0, 2, 1)[:, :, None, :]
        cid = ci[d].transpose(0, 2, 1)[:, :, None, :]
        return crd * pr - cid * pi, -crd * pi - cid * pr

    of_re, of_im = state_out(0, jnp.arange(lc) + 1)
    ob_re, ob_im = state_out(1, lc - jnp.arange(lc))
    obase = jnp.stack([of_re, of_im, ob_re, ob_im], axis=1).reshape(g, 4, p, lc * mm)
    wout = obase[:, None, :, None, :, :] * eye_b[None, :, None, :, None, None]
    wout = wout.reshape(g, batch, 4 * batch * p, lc * mm)

    def plane(x):
        return jnp.tile(x, (1, batch))

    abar = jnp.stack([plane(pw_re[0][:, lc]), plane(pw_im[0][:, lc]),
                      plane(pw_re[1][:, lc]), plane(pw_im[1][:, lc])], axis=0)
    return w1.astype(BF16), abar, tmat.astype(BF16), wout.astype(BF16)


def _ssm_p1_kernel(ul_ref, uc_ref, w_ref, v_ref, *s_refs):
    b = pl.program_id(1)
    ncc, ncl = uc_ref.shape[0] // SSM_CHUNK, ul_ref.shape[0] // SSM_CHUNK
    nc = ncc + ncl
    pw = s_refs[0].shape[1]
    xs = [jnp.concatenate([uc_ref[pl.ds(s, ncc, stride=SSM_CHUNK), :], ul_ref[pl.ds(s, ncl, stride=SSM_CHUNK), :]],
                          axis=0) for s in range(SSM_CHUNK)]
    for j in range(SSM_LANE_GROUPS):
        v = jnp.concatenate([x[:, j * SSM_GROUP:(j + 1) * SSM_GROUP] for x in xs], axis=1).astype(BF16)
        v_ref[j] = v
        acc = _dot(v, w_ref[j])
        for k, s_ref in enumerate(s_refs):
            part = acc[:, k * pw:(k + 1) * pw]
            rows = slice(j * nc, (j + 1) * nc)

            @pl.when(b == 0)
            def _():
                s_ref[rows, :] = part

            @pl.when(b > 0)
            def _():
                s_ref[rows, :] = s_ref[rows, :] + part


def _ssm_scan_kernel(ncc, ncl, a_ref, sfr_ref, sfi_ref, sbr_ref, sbi_ref, xfr_ref, xfi_ref, xbr_ref, xbi_ref):
    nc = ncc + ncl
    gb = a_ref.shape[1]
    pw = a_ref.shape[2]
    afr, afi, abr, abi = a_ref[0], a_ref[1], a_ref[2], a_ref[3]

    def rows(r):
        return pl.ds(r, gb, stride=nc)

    def step(rf, rb, carry):
        xfr, xfi, xbr, xbi = carry
        xfr_ref[rows(rf), :] = xfr
        xfi_ref[rows(rf), :] = xfi
        xbr_ref[rows(rb), :] = xbr
        xbi_ref[rows(rb), :] = xbi
        sfr = sfr_ref[rows(rf), :]
        sfi = sfi_ref[rows(rf), :]
        sbr = sbr_ref[rows(rb), :]
        sbi = sbi_ref[rows(rb), :]
        return (afr * xfr - afi * xfi + sfr, afr * xfi + afi * xfr + sfi,
                abr * xbr - abi * xbi + sbr, abr * xbi + abi * xbr + sbi)

    zero = jnp.zeros((gb, pw), F32)
    carry = lax.fori_loop(0, ncc, lambda t, c: step(t, ncc - 1 - t, c), (zero, zero, zero, zero))
    lax.fori_loop(0, ncl, lambda t, c: step(ncc + t, nc - 1 - t, c), carry)


def _ssm_p3_kernel(v_ref, xfr_ref, xfi_ref, xbr_ref, xbi_ref, t_ref, wo_ref, zl_ref, zc_ref):
    ncc, ncl = zc_ref.shape[0] // SSM_CHUNK, zl_ref.shape[0] // SSM_CHUNK
    nc = ncc + ncl
    ys = []
    for j in range(SSM_LANE_GROUPS):
        rows = slice(j * nc, (j + 1) * nc)
        xin = jnp.concatenate([xfr_ref[rows, :], xfi_ref[rows, :], xbr_ref[rows, :], xbi_ref[rows, :]],
                              axis=1).astype(BF16)
        y = _dot(v_ref[j], t_ref[j]) + _dot(xin, wo_ref[j])
        ys.append(jax.nn.gelu(y, approximate=True))
    for t in range(SSM_CHUNK):
        zt = jnp.concatenate([y[:, t * SSM_GROUP:(t + 1) * SSM_GROUP] for y in ys], axis=1)
        zc_ref[pl.ds(t, ncc, stride=SSM_CHUNK), :] = zt[:ncc]
        zl_ref[pl.ds(t, ncl, stride=SSM_CHUNK), :] = zt[ncc:]


def _ssm_mixer(u4, w1, abar, tmat, wout, batch, seq, ctx_len):
    n_lt = u4.shape[0]
    g = N_SSM_GROUPS
    lg = SSM_LANE_GROUPS
    ncc, ncl = ctx_len // SSM_CHUNK, seq // SSM_CHUNK
    nc = ncc + ncl
    cw = SSM_CW
    sw = w1.shape[-1]
    pw = sw // 4
    ctx_blk0 = (batch * seq) // ctx_len
    plane_shape = [jax.ShapeDtypeStruct((g * nc, pw), F32)] * 4
    lat_spec = pl.BlockSpec((None, seq, 128), lambda k, b: (k, b, 0))
    ctx_spec = pl.BlockSpec((None, ctx_len, 128), lambda k, b: (k, ctx_blk0 + b, 0))
    v_spec = pl.BlockSpec((lg, None, nc, cw), lambda k, b: (k, b, 0, 0))
    plane_spec = pl.BlockSpec((lg * nc, pw), lambda k, b: (k, 0))
    v_chunks, *s_planes = pl.pallas_call(
        _ssm_p1_kernel,
        grid=(n_lt, batch),
        in_specs=[lat_spec, ctx_spec, pl.BlockSpec((lg, None, cw, sw), lambda k, b: (k, b, 0, 0))],
        out_specs=[v_spec] + [plane_spec] * 4,
        out_shape=[jax.ShapeDtypeStruct((g, batch, nc, cw), BF16)] + plane_shape,
        compiler_params=_cparams(("parallel", "arbitrary")),
        name="ssm_chunk_states",
    )(u4, u4, w1)
    gb = SSM_SCAN_GROUPS
    x_planes = pl.pallas_call(
        functools.partial(_ssm_scan_kernel, ncc, ncl),
        grid=(g // gb,),
        in_specs=[pl.BlockSpec((4, gb, pw), lambda i: (0, i, 0))]
        + [pl.BlockSpec((gb * nc, pw), lambda i: (i, 0))] * 4,
        out_specs=[pl.BlockSpec((gb * nc, pw), lambda i: (i, 0))] * 4,
        out_shape=plane_shape,
        compiler_params=_cparams(("parallel",)),
        name="ssm_chunk_scan",
    )(abar, *s_planes)
    return pl.pallas_call(
        _ssm_p3_kernel,
        grid=(n_lt, batch),
        in_specs=[v_spec] + [plane_spec] * 4
        + [pl.BlockSpec((lg, cw, cw), lambda k, b: (k, 0, 0)),
           pl.BlockSpec((lg, None, sw, cw), lambda k, b: (k, b, 0, 0))],
        out_specs=[pl.BlockSpec((None, seq, 128), lambda k, b: (k, b, 0)),
                   pl.BlockSpec((None, ctx_len, 128), lambda k, b: (k, b, 0))],
        out_shape=[jax.ShapeDtypeStruct((n_lt, batch * seq, 128), F32),
                   jax.ShapeDtypeStruct((n_lt, batch * ctx_len, 128), F32)],
        compiler_params=_cparams(("parallel", "arbitrary")),
        name="ssm_chunk_outputs",
    )(v_chunks, *x_planes, tmat, wout)


def _merge_kernel(attn_ref, z_ref, ga_ref, gs_ref, x_ref, mod_ref, g2_ref, wglu_ref, wba_ref, wbs_ref,
                  wout_ref, wrh_ref, wrl_ref, xo_ref, h2_ref, rt_ref, cnt_ref):
    zf = jnp.concatenate([z_ref[j] for j in range(SSM_LANE_TILES)], axis=1)
    z = zf.astype(BF16)
    glu = (z.astype(F32) * jax.nn.sigmoid(_dot(z, wglu_ref[...]))).astype(BF16)
    mix = (ga_ref[...].astype(F32) * _dot(attn_ref[...], wba_ref[...])
           + gs_ref[...].astype(F32) * _dot(glu, wbs_ref[...])).astype(BF16)
    m = mod_ref[0]
    x = x_ref[...] + m[2:3] * _dot(mix, wout_ref[...])
    xo_ref[...] = x
    h2 = _rms_mod(x, g2_ref[...], m[3:4], m[4:5])
    hi, lo = _split_bf16(h2)
    h2_ref[...] = hi.astype(F32)
    logits = _dot(hi, wrh_ref[...]) + _dot(lo, wrh_ref[...]) + _dot(hi, wrl_ref[...])
    rt = _route(logits)
    rt_ref[...] = rt
    oh0, oh1 = _expert_onehots(rt)
    tile_counts = jnp.sum(oh0 + oh1, axis=0, keepdims=True)

    @pl.when(pl.program_id(0) == 0)
    def _():
        cnt_ref[...] = jnp.zeros_like(cnt_ref)

    cnt_ref[0:1, :] = cnt_ref[0:1, :] + tile_counts


def _route(lg):
    ng, epg = N_EXPERT_GROUPS, EXPERTS_PER_GROUP
    lane_i = lax.broadcasted_iota(jnp.int32, lg.shape, 1)
    lane = lane_i.astype(F32)
    big = float(ROUTER_PAD)

    def rmax(mask_val):
        return jnp.max(mask_val, axis=-1, keepdims=True)

    def first_lane(mask, val, mx):
        return jnp.min(jnp.where(mask, jnp.where(val == mx, lane, big), big), axis=-1, keepdims=True)

    gmask = lane_i < ng
    lgm = jnp.where(gmask, lg, NEG_INF)
    mg = rmax(lgm)
    g_prob = 1.0 / jnp.sum(jnp.exp(lgm - mg), axis=-1, keepdims=True)
    g_idx = first_lane(gmask, lg, mg)
    egroup = jnp.floor((lane - float(ng)) * (1.0 / epg))
    emask = egroup == g_idx
    l1 = jnp.where(emask, lg, NEG_INF)
    m1 = rmax(l1)
    i1 = first_lane(emask, lg, m1)
    l2 = jnp.where(lane == i1, NEG_INF, l1)
    m2 = rmax(l2)
    i2 = jnp.min(jnp.where(l2 == m2, jnp.where(emask, lane, big), big), axis=-1, keepdims=True)
    r = jnp.exp(m2 - m1)
    w1 = g_prob / (1.0 + r)
    w2 = w1 * r
    return jnp.where(lane_i == 0, i1 - float(ng),
                     jnp.where(lane_i == 1, i2 - float(ng),
                               jnp.where(lane_i == 2, w1, jnp.where(lane_i == 3, w2, 0.0))))


def _merge(attn, z, ga, gs, x_all, mod, g2, w_glu, w_ba, w_bs, w_out, wr_hi, wr_lo, n_tiles, tiles_per_seq, batch):
    d = D_MODEL
    row = lambda i: (i, 0)
    const = lambda i: (0, 0)
    modi = lambda i: (jnp.minimum(i // tiles_per_seq, batch), 0, 0)
    n = n_tiles * ROW_TILE
    return pl.pallas_call(
        _merge_kernel,
        grid=(n_tiles,),
        in_specs=[
            pl.BlockSpec((ROW_TILE, Q_W), row),
            pl.BlockSpec((SSM_LANE_TILES, ROW_TILE, 128), lambda i: (0, i, 0)),
            pl.BlockSpec((ROW_TILE, d), row),
            pl.BlockSpec((ROW_TILE, d), row),
            pl.BlockSpec((ROW_TILE, d), row),
            pl.BlockSpec((1, 6, d), modi),
            pl.BlockSpec((1, d), const),
            pl.BlockSpec((D_SSM, D_SSM), const),
            pl.BlockSpec((Q_W, d), const),
            pl.BlockSpec((D_SSM, d), const),
            pl.BlockSpec((d, d), const),
            pl.BlockSpec((d, ROUTER_PAD), const),
            pl.BlockSpec((d, ROUTER_PAD), const),
        ],
        out_specs=[pl.BlockSpec((ROW_TILE, d), row), pl.BlockSpec((ROW_TILE, d), row),
                   pl.BlockSpec((ROW_TILE, ROUTER_PAD), row), pl.BlockSpec((8, ROUTER_PAD), const)],
        out_shape=[jax.ShapeDtypeStruct((n, d), F32), jax.ShapeDtypeStruct((n, d), F32),
                   jax.ShapeDtypeStruct((n, ROUTER_PAD), F32), jax.ShapeDtypeStruct((8, ROUTER_PAD), F32)],
        compiler_params=_cparams(("arbitrary",)),
        name="merge_router",
    )(attn, z, ga, gs, x_all, mod, g2, w_glu, w_ba, w_bs, w_out, wr_hi, wr_lo)


def _expert_kernel(be_ref, nu_ref, x_ref, wg_ref, wu_ref, wd_ref, y_ref, wg_s, wu_s, wd_s):
    i = pl.program_id(0)
    changed = jnp.logical_or(i == 0, be_ref[i] != be_ref[jnp.maximum(i - 1, 0)])

    @pl.when(changed)
    def _():
        wg_s[...] = wg_ref[0, 0].astype(BF16)
        wu_s[...] = wu_ref[0, 0].astype(BF16)
        wd_s[...] = wd_ref[0, 0].astype(BF16)

    @pl.when(i < nu_ref[0])
    def _():
        x = x_ref[...].astype(BF16)
        gate = _dot(x, wg_s[...])
        up = _dot(x, wu_s[...])
        hid = (gate * jax.nn.sigmoid(gate) * up).astype(BF16)
        y_ref[...] = _dot(hid, wd_s[...]).astype(y_ref.dtype)

    @pl.when(i >= nu_ref[0])
    def _():
        y_ref[...] = jnp.zeros_like(y_ref)


def _experts(blk_exp, n_used, xs, layer, w_gate, w_up, w_down):
    n_slots, d = xs.shape
    n_blk = n_slots // MOE_BLOCK
    de = w_gate.shape[-1]
    return pl.pallas_call(
        _expert_kernel,
        grid_spec=pltpu.PrefetchScalarGridSpec(
            num_scalar_prefetch=2,
            grid=(n_blk,),
            in_specs=[
                pl.BlockSpec((MOE_BLOCK, d), lambda i, be, nu: (jnp.minimum(i, nu[0] - 1), 0)),
                pl.BlockSpec((1, 1, d, de), lambda i, be, nu: (layer, be[i], 0, 0)),
                pl.BlockSpec((1, 1, d, de), lambda i, be, nu: (layer, be[i], 0, 0)),
                pl.BlockSpec((1, 1, de, d), lambda i, be, nu: (layer, be[i], 0, 0)),
            ],
            out_specs=pl.BlockSpec((MOE_BLOCK, d), lambda i, be, nu: (i, 0)),
            scratch_shapes=[pltpu.VMEM((d, de), BF16), pltpu.VMEM((d, de), BF16), pltpu.VMEM((de, d), BF16)],
        ),
        out_shape=jax.ShapeDtypeStruct((n_slots, d), BF16),
        compiler_params=_cparams(("arbitrary",)),
        name="expert_mlp",
    )(blk_exp, n_used, xs, w_gate, w_up, w_down)


def _expert_onehots(rt):
    lane_f = lax.broadcasted_iota(jnp.int32, rt.shape, 1).astype(F32)
    return jnp.where(lane_f == rt[:, 0:1], 1.0, 0.0), jnp.where(lane_f == rt[:, 1:2], 1.0, 0.0)


def _plan_kernel(rt_ref, cnt_ref, dest_ref, pend_ref, run_s):
    i = pl.program_id(0)
    rows = rt_ref.shape[0]
    lane = lax.broadcasted_iota(jnp.int32, (rows, ROUTER_PAD), 1)
    oh0, oh1 = _expert_onehots(rt_ref[...])
    tot0 = jnp.sum(oh0, axis=0, keepdims=True)
    tot1 = jnp.sum(oh1, axis=0, keepdims=True)

    @pl.when(i == 0)
    def _():
        counts = cnt_ref[...]
        padded = jnp.floor((counts + float(MOE_BLOCK - 1)) * (1.0 / MOE_BLOCK)) * float(MOE_BLOCK)
        r = lax.broadcasted_iota(jnp.int32, (ROUTER_PAD, ROUTER_PAD), 0)
        c = lax.broadcasted_iota(jnp.int32, (ROUTER_PAD, ROUTER_PAD), 1)
        incl = jnp.where(r <= c, 1.0, 0.0).astype(BF16)
        hi, lo = _split_bf16(padded)
        pend = _dot(hi, incl) + _dot(lo, incl)
        pend_ref[...] = pend
        run_s[...] = pend - padded

    r = lax.broadcasted_iota(jnp.int32, (rows, rows), 0)
    c = lax.broadcasted_iota(jnp.int32, (rows, rows), 1)
    before = jnp.where(c < r, 1.0, 0.0).astype(BF16)
    run = run_s[0:1, :]
    pos0 = jnp.sum(oh0 * (run + _dot(before, oh0.astype(BF16))), axis=-1, keepdims=True)
    pos1 = jnp.sum(oh1 * (run + tot0 + _dot(before, oh1.astype(BF16))), axis=-1, keepdims=True)
    run_s[0:1, :] = run + tot0 + tot1
    dest_ref[...] = jnp.where(lane == 0, pos0, jnp.where(lane == 1, pos1, 0.0)).astype(jnp.int32)


def _plan(route, counts, n_rows):
    rows = PLAN_TILE
    return pl.pallas_call(
        _plan_kernel,
        grid=(n_rows // rows,),
        in_specs=[pl.BlockSpec((rows, ROUTER_PAD), lambda i: (i, 0)),
                  pl.BlockSpec((8, ROUTER_PAD), lambda i: (0, 0))],
        out_specs=[pl.BlockSpec((rows, ROUTER_PAD), lambda i: (i, 0)),
                   pl.BlockSpec((8, ROUTER_PAD), lambda i: (0, 0))],
        out_shape=[jax.ShapeDtypeStruct((n_rows, ROUTER_PAD), jnp.int32),
                   jax.ShapeDtypeStruct((8, ROUTER_PAD), F32)],
        scratch_shapes=[pltpu.VMEM((8, ROUTER_PAD), F32)],
        compiler_params=_cparams(("arbitrary",)),
        name="dispatch_plan",
    )(route, counts)


def _dispatch_kernel(pend_ref, dest_ref, h_ref, xs_ref, zero_s, sem, zsem):
    rows = h_ref.shape[0]

    n_blk = xs_ref.shape[0] // MOE_BLOCK
    n_used = pend_ref[N_EXPERTS - 1] // MOE_BLOCK

    def zero_block(start):
        return pltpu.make_async_copy(zero_s, xs_ref.at[pl.ds(pl.multiple_of(start, MOE_BLOCK), MOE_BLOCK)], zsem)

    @pl.when(pl.program_id(0) == 0)
    def _():
        zero_s[...] = jnp.zeros_like(zero_s)
        for e in range(N_EXPERTS):
            @pl.when(pend_ref[e] > 0)
            def _():
                zero_block(pend_ref[e] - MOE_BLOCK).start()
        lax.fori_loop(n_used, n_blk, lambda j, c: (zero_block(j * MOE_BLOCK).start(), c)[1], 0)
        for e in range(N_EXPERTS):
            @pl.when(pend_ref[e] > 0)
            def _():
                zero_block(pend_ref[e] - MOE_BLOCK).wait()
        lax.fori_loop(n_used, n_blk, lambda j, c: (zero_block(j * MOE_BLOCK).wait(), c)[1], 0)

    def row_copy(r, d):
        return pltpu.make_async_copy(h_ref.at[pl.ds(r, 1)], xs_ref.at[pl.ds(d, 1)], sem)

    def issue(r, carry):
        for k in range(TOP_K):
            row_copy(r, dest_ref[0, 0, k * rows + r]).start(priority=k % 2)
        return carry

    lax.fori_loop(0, rows, issue, 0, unroll=8)
    for _ in range(TOP_K * rows):
        row_copy(0, 0).wait()


def _dispatch(pend, dest_tiles, h2, n_slots, n_tiles):
    d = h2.shape[1]
    return pl.pallas_call(
        _dispatch_kernel,
        grid_spec=pltpu.PrefetchScalarGridSpec(
            num_scalar_prefetch=1,
            grid=(n_tiles,),
            in_specs=[pl.BlockSpec((1, 1, TOP_K * ROW_TILE), lambda i, pe: (i, 0, 0), memory_space=pltpu.SMEM),
                      pl.BlockSpec((ROW_TILE, d), lambda i, pe: (i, 0))],
            out_specs=pl.BlockSpec(memory_space=pl.ANY),
            scratch_shapes=[pltpu.VMEM((MOE_BLOCK, d), F32), pltpu.SemaphoreType.DMA, pltpu.SemaphoreType.DMA],
        ),
        out_shape=jax.ShapeDtypeStruct((n_slots, d), F32),
        compiler_params=_cparams(("arbitrary",)),
        name="dispatch_rows",
    )(pend, dest_tiles, h2)


def _moe(h2, route, counts, n_tiles, layer, w_gate, w_up, w_down):
    t, d = h2.shape
    dest, pend_f = _plan(route, counts, t)
    pend = pend_f[0, :N_EXPERTS].astype(jnp.int32)
    n_blk = -(-(t * TOP_K) // MOE_BLOCK) + N_EXPERTS
    blk_start = jnp.arange(n_blk, dtype=jnp.int32) * MOE_BLOCK
    blk_exp = jnp.minimum(jnp.sum((pend[None, :] <= blk_start[:, None]).astype(jnp.int32), axis=1), N_EXPERTS - 1)
    n_used = (pend[N_EXPERTS - 1] // MOE_BLOCK).reshape(1)
    dest2 = dest[:, :TOP_K]
    dest_tiles = dest2.reshape(n_tiles, ROW_TILE, TOP_K).transpose(0, 2, 1).reshape(n_tiles, 1, TOP_K * ROW_TILE)
    xs = _dispatch(pend, dest_tiles, h2, n_blk * MOE_BLOCK, n_tiles)
    ys = _experts(blk_exp, n_used, xs, layer, w_gate, w_up, w_down)
    return ys[dest2[:, 0]], ys[dest2[:, 1]]


def _final_kernel(x_ref, y0_ref, y1_ref, rt_ref, mod_ref, g_ref, o_ref):
    x = _moe_residual(x_ref, y0_ref, y1_ref, rt_ref, mod_ref)
    o_ref[...] = x * lax.rsqrt(jnp.mean(x * x, axis=-1, keepdims=True) + EPS) * g_ref[...]


def _final(x_lat, moe_out, mod, g_final, n_tiles, tiles_per_seq, batch):
    d = D_MODEL
    row = lambda i: (i, 0)
    return pl.pallas_call(
        _final_kernel,
        grid=(n_tiles,),
        in_specs=[pl.BlockSpec((ROW_TILE, d), row), pl.BlockSpec((ROW_TILE, d), row),
                  pl.BlockSpec((ROW_TILE, d), row), pl.BlockSpec((ROW_TILE, ROUTER_PAD), row),
                  pl.BlockSpec((1, 6, d), lambda i: (jnp.minimum(i // tiles_per_seq, batch), 0, 0)),
                  pl.BlockSpec((1, d), lambda i: (0, 0))],
        out_specs=pl.BlockSpec((ROW_TILE, d), row),
        out_shape=jax.ShapeDtypeStruct((n_tiles * ROW_TILE, d), F32),
        compiler_params=_cparams(("parallel",)),
        name="final_norm",
    )(x_lat, *moe_out, mod, g_final)


def _rope_tables(seq):
    quarter = HEAD_DIM // 4
    freqs = ROPE_THETA ** (-jnp.arange(quarter, dtype=F32) / quarter)
    pos = jnp.arange(seq)
    ang_r = (pos // GRID_W).astype(F32)[:, None] * freqs[None, :]
    ang_c = (pos % GRID_W).astype(F32)[:, None] * freqs[None, :]
    cos = jnp.concatenate([jnp.cos(ang_r)] * 2 + [jnp.cos(ang_c)] * 2, axis=-1)
    sin = jnp.concatenate([-jnp.sin(ang_r), jnp.sin(ang_r), -jnp.sin(ang_c), jnp.sin(ang_c)], axis=-1)
    reps = 128 // HEAD_DIM
    cos = jnp.tile(cos, (1, reps))
    sin = jnp.tile(sin, (1, reps))
    cos = jnp.concatenate([cos, jnp.ones((ROW_TILE, 128), F32)], axis=0)
    sin = jnp.concatenate([sin, jnp.zeros((ROW_TILE, 128), F32)], axis=0)
    return cos, sin


def kernel(x, c, ctx, c_ctx, w_mod, b_mod, g_norm1, g_norm2, w_in, attn_sink, ssm_lam_re, ssm_lam_im,
           ssm_log_dt, ssm_b_re, ssm_b_im, ssm_c_re, ssm_c_im, ssm_d, w_glu, w_br_attn, w_br_ssm, w_out,
           w_router_group, w_router_expert, w_exp_gate, w_exp_up, w_exp_down, g_final):
    batch, seq, d = x.shape
    ctx_len = ctx.shape[1]
    depth = w_mod.shape[0]
    assert d == D_MODEL and batch * SSM_STATE == 128
    assert seq % ROW_TILE == 0 and ctx_len % ROW_TILE == 0 and (batch * seq) % ctx_len == 0
    assert batch + 1 <= MOD_ROWS
    t_lat, t_ctx = batch * seq, batch * ctx_len
    tiles_per_seq = seq // ROW_TILE
    n_lat_tiles = t_lat // ROW_TILE
    n_all_tiles = (t_lat + t_ctx) // ROW_TILE
    nb_per_seq = seq // ATTN_BLOCK
    ncc, ncl = ctx_len // SSM_CHUNK, seq // SSM_CHUNK

    c_rows = jnp.zeros((MOD_ROWS, d), F32).at[:batch].set(c).at[batch].set(c_ctx)
    mod_all = _modulation(c_rows, w_mod, b_mod).reshape(depth, MOD_ROWS, 6, d)
    cos_t, sin_t = _rope_tables(seq)
    ssm_w = jax.vmap(functools.partial(_ssm_weights, batch=batch))(
        ssm_lam_re, ssm_lam_im, ssm_log_dt, ssm_b_re, ssm_b_im, ssm_c_re, ssm_c_im, ssm_d)
    x_all = jnp.concatenate([x.reshape(t_lat, d), ctx.reshape(t_ctx, d)], axis=0)
    f_all = None
    for l in range(depth):
        ctx_out = l < depth - 1
        mod = mod_all[l]
        x_all, (q, k, v, u, ga, gs) = _inproj(
            x_all, f_all, mod_all[l - 1] if l > 0 else None, mod, g_norm1[l].reshape(1, d), cos_t, sin_t,
            w_in[l].astype(BF16), n_all_tiles, tiles_per_seq, n_lat_tiles, batch)
        sink = attn_sink[l].astype(F32)
        attn = _attention(sink, q, k, v, t_lat // ATTN_BLOCK, nb_per_seq, t_lat // ctx_len, ctx_len, True, 0)
        if ctx_out:
            attn_c = _attention(sink, q, k, v, t_ctx // ATTN_BLOCK, ctx_len // ATTN_BLOCK, t_lat // ctx_len,
                                ctx_len, False, t_lat // ATTN_BLOCK)
            attn = jnp.concatenate([attn, attn_c], axis=0)
        w1, abar, tmat, wout = (a[l] for a in ssm_w)
        z, z_ctx = _ssm_mixer(u, w1, abar, tmat, wout, batch, seq, ctx_len)
        if ctx_out:
            z = jnp.concatenate([z, z_ctx], axis=1)
        w_r = jnp.zeros((d, ROUTER_PAD), F32)
        w_r = w_r.at[:, :N_EXPERT_GROUPS].set(w_router_group[l])
        w_r = w_r.at[:, N_EXPERT_GROUPS:N_EXPERT_GROUPS + N_EXPERTS].set(w_router_expert[l])
        wr_hi, wr_lo = _split_bf16(w_r)
        n_tiles = n_all_tiles if ctx_out else n_lat_tiles
        x_all, h2, route, counts = _merge(
            attn, z, ga, gs, x_all, mod, g_norm2[l].reshape(1, d), w_glu[l].astype(BF16),
            w_br_attn[l].astype(BF16), w_br_ssm[l].astype(BF16), w_out[l].astype(BF16),
            wr_hi, wr_lo, n_tiles, tiles_per_seq, batch)
        y0, y1 = _moe(h2, route, counts, n_tiles, l, w_exp_gate, w_exp_up, w_exp_down)
        f_all = (y0, y1, route)
    out = _final(x_all, f_all, mod_all[depth - 1], g_final.reshape(1, d), n_lat_tiles, tiles_per_seq, batch)
    return out.reshape(batch, seq, d)
```

```python
import functools
import math

import jax
import jax.numpy as jnp
from jax import lax
from jax.experimental import pallas as pl
from jax.experimental.pallas import tpu as pltpu

F32 = jnp.float32
BF16 = jnp.bfloat16

D_MODEL = 1024
GRID_W = 64
N_HEADS = 8
N_KV_HEADS = 2
HEAD_DIM = 64
Q_GROUP = N_HEADS // N_KV_HEADS
ATTN_BLOCK = 128
ATTN_STEP_BLOCKS = 2
ROPE_THETA = 10000.0
D_SSM = D_MODEL // 2
SSM_GROUP = 16
N_SSM_GROUPS = D_SSM // SSM_GROUP
SSM_STATE = 64
N_EXPERT_GROUPS = 4
EXPERTS_PER_GROUP = 8
N_EXPERTS = N_EXPERT_GROUPS * EXPERTS_PER_GROUP
TOP_K = 2
D_EXPERT = D_MODEL // 2
Q_W = N_HEADS * HEAD_DIM
KV_W = N_KV_HEADS * HEAD_DIM
O_K = Q_W
O_V = O_K + KV_W
O_U = O_V + KV_W
O_GA = O_U + D_SSM
O_GS = O_GA + D_MODEL
D_IN = O_GS + D_MODEL
EPS = 1e-6
NEG_INF = -1e30

ROW_TILE = 256
MERGE_TILE = 512
MERGE_SUB = 256
SSM_CHUNK = 16
SSM_CW = SSM_CHUNK * SSM_GROUP
SSM_SCAN_GROUPS = 8
SSM_LANE_GROUPS = 128 // SSM_GROUP
SSM_LANE_TILES = D_SSM // 128
MOE_BLOCK = 256
PLAN_TILE = 512
ROUTER_PAD = 128
MOD_ROWS = 8
VMEM_LIMIT = 48 * 1024 * 1024


def _cparams(sem):
    return pltpu.CompilerParams(dimension_semantics=sem, vmem_limit_bytes=VMEM_LIMIT)


def _dot(a, b):
    return jnp.dot(a, b, preferred_element_type=F32)


def _split_bf16(a):
    hi = a.astype(BF16)
    lo = (a - hi.astype(F32)).astype(BF16)
    return hi, lo


def _rms_mod(x, g, shift, scale):
    y = x * lax.rsqrt(jnp.mean(x * x, axis=-1, keepdims=True) + EPS) * g
    return y * (1.0 + scale) + shift


def _mod_kernel(c_ref, w_ref, b_ref, o_ref):
    c = c_ref[...]
    s_hi, s_lo = _split_bf16(c * jax.nn.sigmoid(c))
    w_hi, w_lo = _split_bf16(w_ref[0])
    o_ref[0] = _dot(s_hi, w_hi) + _dot(s_lo, w_hi) + _dot(s_hi, w_lo) + b_ref[0]


def _modulation(c_rows, w_mod, b_mod):
    depth, d, n = w_mod.shape
    nb = n // 4
    return pl.pallas_call(
        _mod_kernel,
        grid=(depth, n // nb),
        in_specs=[
            pl.BlockSpec((MOD_ROWS, d), lambda l, j: (0, 0)),
            pl.BlockSpec((1, d, nb), lambda l, j: (l, 0, j)),
            pl.BlockSpec((1, 1, nb), lambda l, j: (l, 0, j)),
        ],
        out_specs=pl.BlockSpec((1, MOD_ROWS, nb), lambda l, j: (l, 0, j)),
        out_shape=jax.ShapeDtypeStruct((depth, MOD_ROWS, n), F32),
        compiler_params=_cparams(("arbitrary", "arbitrary")),
        name="modulation",
    )(c_rows, w_mod, b_mod.reshape(depth, 1, n))


def _moe_residual(x_ref, y0_ref, y1_ref, rt_ref, mod_ref):
    rt = rt_ref[...]
    f = rt[:, 2:3] * y0_ref[...].astype(F32) + rt[:, 3:4] * y1_ref[...].astype(F32)
    return x_ref[...] + mod_ref[0, 5:6, :] * f


def _inproj_kernel(has_f, n_lat_tiles, *refs):
    if has_f:
        (x_ref, y0_ref, y1_ref, rt_ref, modp_ref, mod_ref, g_ref, cos_ref, sin_ref, w_ref,
         xo_ref, q_ref, k_ref, v_ref, u_ref, ga_ref, gs_ref) = refs
        x = _moe_residual(x_ref, y0_ref, y1_ref, rt_ref, modp_ref)
        xo_ref[...] = x
    else:
        (xl_ref, xc_ref, mod_ref, g_ref, cos_ref, sin_ref, w_ref,
         q_ref, k_ref, v_ref, u_ref, ga_ref, gs_ref) = refs
        x = jnp.where(pl.program_id(0) >= n_lat_tiles, xc_ref[...], xl_ref[...])
    m = mod_ref[0]
    h = _rms_mod(x, g_ref[...], m[0:1], m[1:2]).astype(BF16)
    cos = cos_ref[...]
    sin = sin_ref[...]
    lane = lax.broadcasted_iota(jnp.int32, cos.shape, 1)
    first = (lane % (HEAD_DIM // 2)) < (HEAD_DIM // 4)

    def rope(t):
        sw = jnp.where(first, pltpu.roll(t, 128 - HEAD_DIM // 4, 1), pltpu.roll(t, HEAD_DIM // 4, 1))
        return t * cos + sw * sin

    def proj(lo, hi):
        return _dot(h, w_ref[:, lo:hi])

    q = proj(0, O_K)
    for j in range(Q_W // 128):
        q_ref[:, 128 * j:128 * (j + 1)] = (rope(q[:, 128 * j:128 * (j + 1)]) * HEAD_DIM ** -0.5).astype(BF16)
    k_ref[...] = rope(proj(O_K, O_V)).astype(BF16)
    v_ref[...] = proj(O_V, O_U).astype(BF16)
    u = proj(O_U, O_GA)
    for j in range(SSM_LANE_TILES):
        u_ref[j] = u[:, 128 * j:128 * (j + 1)]
    ga_ref[...] = jax.nn.sigmoid(proj(O_GA, O_GS)).astype(BF16)
    gs_ref[...] = jax.nn.sigmoid(proj(O_GS, D_IN)).astype(BF16)


def _inproj(x_parts, moe_out, mod_prev, mod, g1, cos_t, sin_t, w_in, n_tiles, tiles_per_seq, n_lat_tiles, batch):
    d = D_MODEL
    has_f = moe_out is not None
    row = lambda i: (i, 0)
    modi = lambda i: (jnp.minimum(i // tiles_per_seq, batch), 0, 0)
    const = lambda i: (0, 0)
    ropei = lambda i: (jnp.where(i < n_lat_tiles, i % tiles_per_seq, tiles_per_seq), 0)
    if has_f:
        in_specs = [pl.BlockSpec((ROW_TILE, d), row), pl.BlockSpec((ROW_TILE, d), row),
                    pl.BlockSpec((ROW_TILE, d), row), pl.BlockSpec((ROW_TILE, ROUTER_PAD), row),
                    pl.BlockSpec((1, 6, d), modi)]
        args = [*x_parts, *moe_out, mod_prev]
    else:
        in_specs = [pl.BlockSpec((ROW_TILE, d), lambda i: (jnp.minimum(i, n_lat_tiles - 1), 0)),
                    pl.BlockSpec((ROW_TILE, d), lambda i: (jnp.maximum(i - n_lat_tiles, 0), 0))]
        args = list(x_parts)
    in_specs += [
        pl.BlockSpec((1, 6, d), modi),
        pl.BlockSpec((1, d), const),
        pl.BlockSpec((ROW_TILE, 128), ropei),
        pl.BlockSpec((ROW_TILE, 128), ropei),
        pl.BlockSpec((d, D_IN), const),
    ]
    args += [mod, g1, cos_t, sin_t, w_in]
    widths = [Q_W, KV_W, KV_W, D_SSM, D_MODEL, D_MODEL]
    out_specs = [pl.BlockSpec((ROW_TILE, w), row) for w in widths]
    out_shape = [jax.ShapeDtypeStruct((n_tiles * ROW_TILE, w), BF16) for w in widths]
    out_specs[3] = pl.BlockSpec((SSM_LANE_TILES, ROW_TILE, 128), lambda i: (0, i, 0))
    out_shape[3] = jax.ShapeDtypeStruct((SSM_LANE_TILES, n_tiles * ROW_TILE, 128), F32)
    if has_f:
        out_specs = [pl.BlockSpec((ROW_TILE, d), row)] + out_specs
        out_shape = [jax.ShapeDtypeStruct((n_tiles * ROW_TILE, d), F32)] + out_shape
    outs = pl.pallas_call(
        functools.partial(_inproj_kernel, has_f, n_lat_tiles),
        grid=(n_tiles,),
        in_specs=in_specs,
        out_specs=out_specs,
        out_shape=out_shape,
        compiler_params=_cparams(("parallel",)),
        name="inproj",
    )(*args)
    if has_f:
        return (outs[0],), outs[1:]
    return x_parts, outs


def _attn_block(sink_ref, q, k_tiles, v_tiles, biases):
    blk = q.shape[0]
    outs = []
    for g in range(N_KV_HEADS):
        gs = slice(g * HEAD_DIM, (g + 1) * HEAD_DIM)
        k_all = jnp.concatenate([t[:, gs] for t in k_tiles], axis=0)
        v_all = jnp.concatenate([t[:, gs] for t in v_tiles], axis=0)
        v_ext = jnp.concatenate([v_all, jnp.ones_like(v_all)], axis=1)
        qg = jnp.concatenate(
            [q[:, (g * Q_GROUP + h) * HEAD_DIM:(g * Q_GROUP + h + 1) * HEAD_DIM] for h in range(Q_GROUP)], axis=0)
        s_all = lax.dot_general(qg, k_all, (((1,), (1,)), ((), ())), preferred_element_type=F32)
        for h in range(Q_GROUP):
            s = s_all[h * blk:(h + 1) * blk]
            tiles, col = [], 0
            for kt, bias in zip(k_tiles, biases):
                t = s[:, col:col + kt.shape[0]]
                tiles.append(t if bias is None else t + bias)
                col += kt.shape[0]
            mx = tiles[0]
            for t in tiles[1:]:
                for c0 in range(0, t.shape[1], blk):
                    mx = jnp.maximum(mx, t[:, c0:c0 + blk])
            sink = sink_ref[g * Q_GROUP + h]
            m = jnp.maximum(jnp.max(mx, axis=-1, keepdims=True), sink)
            p = jnp.exp(jnp.concatenate([(t - m).astype(BF16) for t in tiles], axis=1))
            o_ext = _dot(p, v_ext)
            denom = o_ext[:, HEAD_DIM:HEAD_DIM + 1] + jnp.exp(sink - m)
            outs.append((o_ext[:, :HEAD_DIM] * (1.0 / denom)).astype(BF16))
    return jnp.concatenate(outs, axis=1)


def _attn_kernel(band, nb_per_seq, sink_ref, *refs):
    blk = ATTN_BLOCK
    if not band:
        q_ref, kx_ref, vx_ref, o_ref = refs
        for a in range(q_ref.shape[0] // blk):
            rows = slice(a * blk, (a + 1) * blk)
            o_ref[rows, :] = _attn_block(sink_ref, q_ref[rows, :], [kx_ref[...]], [vx_ref[...]], [None])
        return
    q_ref, kp_ref, kc_ref, kn_ref, vp_ref, vc_ref, vn_ref, kx_ref, vx_ref, o_ref = refs
    n_sub = q_ref.shape[0] // blk
    j0 = (pl.program_id(0) * n_sub) % nb_per_seq
    r = lax.broadcasted_iota(jnp.int32, (blk, blk), 0)
    c = lax.broadcasted_iota(jnp.int32, (blk, blk), 1)
    k_blocks = [kp_ref[...]] + [kc_ref[a * blk:(a + 1) * blk, :] for a in range(n_sub)] + [kn_ref[...]]
    v_blocks = [vp_ref[...]] + [vc_ref[a * blk:(a + 1) * blk, :] for a in range(n_sub)] + [vn_ref[...]]
    for a in range(n_sub):
        edge_p = jnp.where(j0 + a > 0, 0.0, NEG_INF).astype(F32)
        edge_n = jnp.where(j0 + a < nb_per_seq - 1, 0.0, NEG_INF).astype(F32)
        bias_p = jnp.where(c >= r, edge_p, NEG_INF).astype(F32)
        bias_n = jnp.where(c <= r, edge_n, NEG_INF).astype(F32)
        rows = slice(a * blk, (a + 1) * blk)
        o_ref[rows, :] = _attn_block(sink_ref, q_ref[rows, :], k_blocks[a:a + 3] + [kx_ref[...]],
                                     v_blocks[a:a + 3] + [vx_ref[...]], [bias_p, None, bias_n, None])


def _attention(sink, q, k, v, n_blocks, nb_per_seq, kx_block0, ctx_len, band, q_block0):
    blk = ATTN_BLOCK
    n_sub = ATTN_STEP_BLOCKS
    assert n_blocks % n_sub == 0 and nb_per_seq % n_sub == 0 and q_block0 % n_sub == 0
    step = n_sub * blk
    n_steps = n_blocks // n_sub
    qi = lambda i, s: (q_block0 // n_sub + i, 0)
    cur = lambda i, s: (i, 0)
    prv = lambda i, s: (jnp.maximum(i * n_sub - 1, 0), 0)
    nxt = lambda i, s: (jnp.minimum((i + 1) * n_sub, n_blocks - 1), 0)
    kxi = lambda i, s: (kx_block0 + (i * n_sub) // nb_per_seq, 0)
    kspec = lambda f: pl.BlockSpec((blk, KV_W), f)
    cspec = pl.BlockSpec((step, KV_W), cur)
    xspec = pl.BlockSpec((ctx_len, KV_W), kxi)
    if band:
        in_specs = [pl.BlockSpec((step, Q_W), qi), kspec(prv), cspec, kspec(nxt),
                    kspec(prv), cspec, kspec(nxt), xspec, xspec]
        args = (q, k, k, k, v, v, v, k, v)
    else:
        in_specs = [pl.BlockSpec((step, Q_W), qi), xspec, xspec]
        args = (q, k, v)
    return pl.pallas_call(
        functools.partial(_attn_kernel, band, nb_per_seq),
        grid_spec=pltpu.PrefetchScalarGridSpec(
            num_scalar_prefetch=1,
            grid=(n_steps,),
            in_specs=in_specs,
            out_specs=pl.BlockSpec((step, Q_W), lambda i, s: (i, 0)),
        ),
        out_shape=jax.ShapeDtypeStruct((n_blocks * blk, Q_W), BF16),
        compiler_params=_cparams(("parallel",)),
        name="band_attention" if band else "context_attention",
    )(sink, *args)


def _ssm_weights(lam_re, lam_im, log_dt, b_re, b_im, c_re, c_im, d_skip, batch):
    hp = lax.Precision.HIGHEST
    g, p, mm, lc = N_SSM_GROUPS, SSM_STATE, SSM_GROUP, SSM_CHUNK
    lr, li = lam_re.astype(F32), lam_im.astype(F32)
    dt = jnp.exp(log_dt.astype(F32))[..., None]
    mag = jnp.exp(lr * dt)
    a_re = mag * jnp.cos(li * dt)
    a_im = mag * jnp.sin(li * dt)
    den = lr * lr + li * li
    nr = a_re - 1.0
    f_re = (nr * lr + a_im * li) / den
    f_im = (a_im * lr - nr * li) / den
    br, bi = b_re.astype(F32), b_im.astype(F32)
    bb_re = f_re[..., None] * br - f_im[..., None] * bi
    bb_im = f_re[..., None] * bi + f_im[..., None] * br
    jj = jnp.arange(lc + 1, dtype=F32)[None, None, :, None]
    pmag = jnp.exp(lr[:, :, None, :] * dt[:, :, None, :] * jj)
    pang = li[:, :, None, :] * dt[:, :, None, :] * jj
    pw_re = pmag * jnp.cos(pang)
    pw_im = pmag * jnp.sin(pang)
    cr, ci = c_re.astype(F32), c_im.astype(F32)
    cp_re = cr[:, :, None] * pw_re[:, :, :, None, :] - ci[:, :, None] * pw_im[:, :, :, None, :]
    cp_im = cr[:, :, None] * pw_im[:, :, :, None, :] + ci[:, :, None] * pw_re[:, :, :, None, :]
    kk = jnp.einsum('dgxk,dgkn->dgxn',
                    jnp.concatenate([cp_re, -cp_im], axis=-1).reshape(2, g, (lc + 1) * mm, 2 * p),
                    jnp.concatenate([bb_re, bb_im], axis=-2), precision=hp).reshape(2, g, lc + 1, mm, mm)
    s_i = jnp.arange(lc)[:, None]
    t_i = jnp.arange(lc)[None, :]
    kf = kk[0][:, jnp.clip(t_i - s_i, 0, lc)] * (t_i >= s_i)[None, :, :, None, None].astype(F32)
    kb = kk[1][:, jnp.clip(s_i - t_i, 0, lc)] * (s_i >= t_i)[None, :, :, None, None].astype(F32)
    dg = d_skip.astype(F32).reshape(g, mm)
    eye_t = jnp.eye(lc, dtype=F32)[None, :, :, None, None]
    eye_m = jnp.eye(mm, dtype=F32)[None, None, None, :, :]
    tm = kf + kb + eye_t * eye_m * dg[:, None, None, :, None]
    tmat = tm.transpose(0, 1, 4, 2, 3).reshape(g, lc * mm, lc * mm)

    def state_in(d, pidx):
        pr = pw_re[d][:, pidx]
        pi = pw_im[d][:, pidx]
        w_re = pr[:, :, None, :] * bb_re[d].transpose(0, 2, 1)[:, None] - pi[:, :, None, :] * bb_im[d].transpose(0, 2, 1)[:, None]
        w_im = pr[:, :, None, :] * bb_im[d].transpose(0, 2, 1)[:, None] + pi[:, :, None, :] * bb_re[d].transpose(0, 2, 1)[:, None]
        return w_re, w_im

    wf_re, wf_im = state_in(0, lc - 1 - jnp.arange(lc))
    wb_re, wb_im = state_in(1, jnp.arange(lc))
    base = jnp.stack([wf_re, wf_im, wb_re, wb_im], axis=3).reshape(g, lc * mm, 4, p)
    eye_b = jnp.eye(batch, dtype=F32)
    w1 = base[:, None, :, :, None, :] * eye_b[None, :, None, None, :, None]
    w1 = w1.reshape(g, batch, lc * mm, 4 * batch * p)

    def state_out(d, pidx):
        pr = pw_re[d][:, pidx].transpose(0, 2, 1)[:, :, :, None]
        pi = pw_im[d][:, pidx].transpose(0, 2, 1)[:, :, :, None]
        crd = cr[d].transpose(0, 2, 1)[:, :, None, :]
        cid = ci[d].transpose(0, 2, 1)[:, :, None, :]
        return crd * pr - cid * pi, -crd * pi - cid * pr

    of_re, of_im = state_out(0, jnp.arange(lc) + 1)
    ob_re, ob_im = state_out(1, lc - jnp.arange(lc))
    obase = jnp.stack([of_re, of_im, ob_re, ob_im], axis=1).reshape(g, 4, p, lc * mm)
    wout = obase[:, None, :, None, :, :] * eye_b[None, :, None, :, None, None]
    wout = wout.reshape(g, batch, 4 * batch * p, lc * mm)

    def plane(x):
        return jnp.tile(x, (1, batch))

    abar = jnp.stack([plane(pw_re[0][:, lc]), plane(pw_im[0][:, lc]),
                      plane(pw_re[1][:, lc]), plane(pw_im[1][:, lc])], axis=0)
    return w1.astype(BF16), abar, tmat.astype(BF16), wout.astype(BF16)


def _ssm_p1_kernel(ul_ref, uc_ref, w_ref, v_ref, *s_refs):
    b = pl.program_id(1)
    ncc, ncl = uc_ref.shape[0] // SSM_CHUNK, ul_ref.shape[0] // SSM_CHUNK
    nc = ncc + ncl
    pw = s_refs[0].shape[1]
    xs = [jnp.concatenate([uc_ref[pl.ds(s, ncc, stride=SSM_CHUNK), :], ul_ref[pl.ds(s, ncl, stride=SSM_CHUNK), :]],
                          axis=0) for s in range(SSM_CHUNK)]
    for j in range(SSM_LANE_GROUPS):
        v = jnp.concatenate([x[:, j * SSM_GROUP:(j + 1) * SSM_GROUP] for x in xs], axis=1).astype(BF16)
        v_ref[j] = v
        acc = _dot(v, w_ref[j])
        for k, s_ref in enumerate(s_refs):
            part = acc[:, k * pw:(k + 1) * pw]
            rows = slice(j * nc, (j + 1) * nc)

            @pl.when(b == 0)
            def _():
                s_ref[rows, :] = part

            @pl.when(b > 0)
            def _():
                s_ref[rows, :] = s_ref[rows, :] + part


def _ssm_scan_kernel(ncc, ncl, a_ref, sfr_ref, sfi_ref, sbr_ref, sbi_ref, xfr_ref, xfi_ref, xbr_ref, xbi_ref):
    nc = ncc + ncl
    gb = a_ref.shape[1]
    pw = a_ref.shape[2]
    afr, afi, abr, abi = a_ref[0], a_ref[1], a_ref[2], a_ref[3]

    def rows(r):
        return pl.ds(r, gb, stride=nc)

    def step(rf, rb, carry):
        xfr, xfi, xbr, xbi = carry
        xfr_ref[rows(rf), :] = xfr
        xfi_ref[rows(rf), :] = xfi
        xbr_ref[rows(rb), :] = xbr
        xbi_ref[rows(rb), :] = xbi
        sfr = sfr_ref[rows(rf), :]
        sfi = sfi_ref[rows(rf), :]
        sbr = sbr_ref[rows(rb), :]
        sbi = sbi_ref[rows(rb), :]
        return (afr * xfr - afi * xfi + sfr, afr * xfi + afi * xfr + sfi,
                abr * xbr - abi * xbi + sbr, abr * xbi + abi * xbr + sbi)

    zero = jnp.zeros((gb, pw), F32)
    carry = lax.fori_loop(0, ncc, lambda t, c: step(t, ncc - 1 - t, c), (zero, zero, zero, zero))
    lax.fori_loop(0, ncl, lambda t, c: step(ncc + t, nc - 1 - t, c), carry)


def _ssm_p3_kernel(v_ref, xfr_ref, xfi_ref, xbr_ref, xbi_ref, t_ref, wo_ref, zl_ref, zc_ref):
    ncc, ncl = zc_ref.shape[0] // SSM_CHUNK, zl_ref.shape[0] // SSM_CHUNK
    nc = ncc + ncl
    ys = []
    for j in range(SSM_LANE_GROUPS):
        rows = slice(j * nc, (j + 1) * nc)
        xin = jnp.concatenate([xfr_ref[rows, :], xfi_ref[rows, :], xbr_ref[rows, :], xbi_ref[rows, :]],
                              axis=1).astype(BF16)
        y = _dot(v_ref[j], t_ref[j]) + _dot(xin, wo_ref[j])
        ys.append(jax.nn.gelu(y, approximate=True))
    for t in range(SSM_CHUNK):
        zt = jnp.concatenate([y[:, t * SSM_GROUP:(t + 1) * SSM_GROUP] for y in ys], axis=1)
        zc_ref[pl.ds(t, ncc, stride=SSM_CHUNK), :] = zt[:ncc]
        zl_ref[pl.ds(t, ncl, stride=SSM_CHUNK), :] = zt[ncc:]


def _ssm_mixer(u4, w1, abar, tmat, wout, batch, seq, ctx_len):
    n_lt = u4.shape[0]
    g = N_SSM_GROUPS
    lg = SSM_LANE_GROUPS
    ncc, ncl = ctx_len // SSM_CHUNK, seq // SSM_CHUNK
    nc = ncc + ncl
    cw = SSM_CW
    sw = w1.shape[-1]
    pw = sw // 4
    ctx_blk0 = (batch * seq) // ctx_len
    plane_shape = [jax.ShapeDtypeStruct((g * nc, pw), F32)] * 4
    lat_spec = pl.BlockSpec((None, seq, 128), lambda k, b: (k, b, 0))
    ctx_spec = pl.BlockSpec((None, ctx_len, 128), lambda k, b: (k, ctx_blk0 + b, 0))
    v_spec = pl.BlockSpec((lg, None, nc, cw), lambda k, b: (k, b, 0, 0))
    plane_spec = pl.BlockSpec((lg * nc, pw), lambda k, b: (k, 0))
    v_chunks, *s_planes = pl.pallas_call(
        _ssm_p1_kernel,
        grid=(n_lt, batch),
        in_specs=[lat_spec, ctx_spec, pl.BlockSpec((lg, None, cw, sw), lambda k, b: (k, b, 0, 0))],
        out_specs=[v_spec] + [plane_spec] * 4,
        out_shape=[jax.ShapeDtypeStruct((g, batch, nc, cw), BF16)] + plane_shape,
        compiler_params=_cparams(("parallel", "arbitrary")),
        name="ssm_chunk_states",
    )(u4, u4, w1)
    gb = SSM_SCAN_GROUPS
    x_planes = pl.pallas_call(
        functools.partial(_ssm_scan_kernel, ncc, ncl),
        grid=(g // gb,),
        in_specs=[pl.BlockSpec((4, gb, pw), lambda i: (0, i, 0))]
        + [pl.BlockSpec((gb * nc, pw), lambda i: (i, 0))] * 4,
        out_specs=[pl.BlockSpec((gb * nc, pw), lambda i: (i, 0))] * 4,
        out_shape=plane_shape,
        compiler_params=_cparams(("parallel",)),
        name="ssm_chunk_scan",
    )(abar, *s_planes)
    return pl.pallas_call(
        _ssm_p3_kernel,
        grid=(n_lt, batch),
        in_specs=[v_spec] + [plane_spec] * 4
        + [pl.BlockSpec((lg, cw, cw), lambda k, b: (k, 0, 0)),
           pl.BlockSpec((lg, None, sw, cw), lambda k, b: (k, b, 0, 0))],
        out_specs=[pl.BlockSpec((None, seq, 128), lambda k, b: (k, b, 0)),
                   pl.BlockSpec((None, ctx_len, 128), lambda k, b: (k, b, 0))],
        out_shape=[jax.ShapeDtypeStruct((n_lt, batch * seq, 128), F32),
                   jax.ShapeDtypeStruct((n_lt, batch * ctx_len, 128), F32)],
        compiler_params=_cparams(("parallel", "arbitrary")),
        name="ssm_chunk_outputs",
    )(v_chunks, *x_planes, tmat, wout)


def _merge_kernel(n_lat_tiles, *refs):
    if n_lat_tiles is None:
        attn_ref, z_ref, x_ref = refs[:3]
        refs = refs[3:]
    else:
        attn_ref, z_ref, x_ref, attn_c_ref, z_c_ref, x_c_ref = refs[:6]
        refs = refs[6:]
        is_ctx = pl.program_id(0) >= n_lat_tiles
    (ga_ref, gs_ref, mod_ref, g2_ref, wglu_ref, wba_ref, wbs_ref, wout_ref, wrh_ref, wrl_ref,
     xo_ref, h2_ref, rt_ref, cnt_ref) = refs
    m = mod_ref[0]
    tile_counts = None
    for r in range(MERGE_TILE // MERGE_SUB):
        rows = slice(r * MERGE_SUB, (r + 1) * MERGE_SUB)
        zf = jnp.concatenate([z_ref[j, rows, :] for j in range(SSM_LANE_TILES)], axis=1)
        attn = attn_ref[rows, :]
        x_in = x_ref[rows, :]
        if n_lat_tiles is not None:
            zf = jnp.where(is_ctx, jnp.concatenate([z_c_ref[j, rows, :] for j in range(SSM_LANE_TILES)], axis=1), zf)
            attn = jnp.where(is_ctx, attn_c_ref[rows, :], attn)
            x_in = jnp.where(is_ctx, x_c_ref[rows, :], x_in)
        z = zf.astype(BF16)
        glu = (z.astype(F32) * jax.nn.sigmoid(_dot(z, wglu_ref[...]))).astype(BF16)
        mix = (ga_ref[rows, :].astype(F32) * _dot(attn, wba_ref[...])
               + gs_ref[rows, :].astype(F32) * _dot(glu, wbs_ref[...])).astype(BF16)
        x = x_in + m[2:3] * _dot(mix, wout_ref[...])
        xo_ref[rows, :] = x
        h2 = _rms_mod(x, g2_ref[...], m[3:4], m[4:5])
        hi, lo = _split_bf16(h2)
        h2_ref[rows, :] = hi.astype(F32)
        logits = _dot(hi, wrh_ref[...]) + _dot(lo, wrh_ref[...]) + _dot(hi, wrl_ref[...])
        rt = _route(logits)
        rt_ref[rows, :] = rt
        oh0, oh1 = _expert_onehots(rt)
        part = jnp.sum(oh0 + oh1, axis=0, keepdims=True)
        tile_counts = part if tile_counts is None else tile_counts + part

    @pl.when(pl.program_id(0) == 0)
    def _():
        cnt_ref[...] = jnp.zeros_like(cnt_ref)

    cnt_ref[0:1, :] = cnt_ref[0:1, :] + tile_counts


def _route(lg):
    ng, epg = N_EXPERT_GROUPS, EXPERTS_PER_GROUP
    lane_i = lax.broadcasted_iota(jnp.int32, lg.shape, 1)
    lane = lane_i.astype(F32)
    big = float(ROUTER_PAD)

    def rmax(mask_val):
        return jnp.max(mask_val, axis=-1, keepdims=True)

    def first_lane(mask, val, mx):
        return jnp.min(jnp.where(mask, jnp.where(val == mx, lane, big), big), axis=-1, keepdims=True)

    gmask = lane_i < ng
    lgm = jnp.where(gmask, lg, NEG_INF)
    mg = rmax(lgm)
    g_prob = 1.0 / jnp.sum(jnp.exp(lgm - mg), axis=-1, keepdims=True)
    g_idx = first_lane(gmask, lg, mg)
    egroup = jnp.floor((lane - float(ng)) * (1.0 / epg))
    emask = egroup == g_idx
    l1 = jnp.where(emask, lg, NEG_INF)
    m1 = rmax(l1)
    i1 = first_lane(emask, lg, m1)
    l2 = jnp.where(lane == i1, NEG_INF, l1)
    m2 = rmax(l2)
    i2 = jnp.min(jnp.where(l2 == m2, jnp.where(emask, lane, big), big), axis=-1, keepdims=True)
    r = jnp.exp(m2 - m1)
    w1 = g_prob / (1.0 + r)
    w2 = w1 * r
    return jnp.where(lane_i == 0, i1 - float(ng),
                     jnp.where(lane_i == 1, i2 - float(ng),
                               jnp.where(lane_i == 2, w1, jnp.where(lane_i == 3, w2, 0.0))))


def _merge(attn, z, x, ctx_parts, ga, gs, mod, g2, w_glu, w_ba, w_bs, w_out, wr_hi, wr_lo, n_rows, t_lat, seq, batch):
    d = D_MODEL
    tile = MERGE_TILE
    assert n_rows % tile == 0 and t_lat % tile == 0 and seq % tile == 0
    n_tiles = n_rows // tile
    n_lat = t_lat // tile
    row = lambda i: (i, 0)
    lat = lambda i: (jnp.minimum(i, n_lat - 1), 0)
    cxt = lambda i: (jnp.maximum(i - n_lat, 0), 0)
    const = lambda i: (0, 0)
    modi = lambda i: (jnp.minimum(i // (seq // tile), batch), 0, 0)
    in_specs = [pl.BlockSpec((tile, Q_W), lat),
                pl.BlockSpec((SSM_LANE_TILES, tile, 128), lambda i: (0,) + lat(i)),
                pl.BlockSpec((tile, d), lat)]
    args = [attn, z, x]
    if ctx_parts is not None:
        in_specs += [pl.BlockSpec((tile, Q_W), cxt),
                     pl.BlockSpec((SSM_LANE_TILES, tile, 128), lambda i: (0,) + cxt(i)),
                     pl.BlockSpec((tile, d), cxt)]
        args += list(ctx_parts)
    in_specs += [
        pl.BlockSpec((tile, d), row),
        pl.BlockSpec((tile, d), row),
        pl.BlockSpec((1, 6, d), modi),
        pl.BlockSpec((1, d), const),
        pl.BlockSpec((D_SSM, D_SSM), const),
        pl.BlockSpec((Q_W, d), const),
        pl.BlockSpec((D_SSM, d), const),
        pl.BlockSpec((d, d), const),
        pl.BlockSpec((d, ROUTER_PAD), const),
        pl.BlockSpec((d, ROUTER_PAD), const),
    ]
    args += [ga, gs, mod, g2, w_glu, w_ba, w_bs, w_out, wr_hi, wr_lo]
    return pl.pallas_call(
        functools.partial(_merge_kernel, n_lat if ctx_parts is not None else None),
        grid=(n_tiles,),
        in_specs=in_specs,
        out_specs=[pl.BlockSpec((tile, d), row), pl.BlockSpec((tile, d), row),
                   pl.BlockSpec((tile, ROUTER_PAD), row), pl.BlockSpec((8, ROUTER_PAD), const)],
        out_shape=[jax.ShapeDtypeStruct((n_rows, d), F32), jax.ShapeDtypeStruct((n_rows, d), F32),
                   jax.ShapeDtypeStruct((n_rows, ROUTER_PAD), F32), jax.ShapeDtypeStruct((8, ROUTER_PAD), F32)],
        compiler_params=_cparams(("arbitrary",)),
        name="merge_router",
    )(*args)


def _expert_kernel(be_ref, nu_ref, x_ref, wg_ref, wu_ref, wd_ref, y_ref, wg_s, wu_s, wd_s):
    i = pl.program_id(0)
    changed = jnp.logical_or(i == 0, be_ref[i] != be_ref[jnp.maximum(i - 1, 0)])

    @pl.when(changed)
    def _():
        wg_s[...] = wg_ref[0, 0].astype(BF16)
        wu_s[...] = wu_ref[0, 0].astype(BF16)
        wd_s[...] = wd_ref[0, 0].astype(BF16)

    @pl.when(i < nu_ref[0])
    def _():
        x = x_ref[...].astype(BF16)
        gate = _dot(x, wg_s[...])
        up = _dot(x, wu_s[...])
        hid = (gate * jax.nn.sigmoid(gate) * up).astype(BF16)
        y_ref[...] = _dot(hid, wd_s[...]).astype(y_ref.dtype)

    @pl.when(i >= nu_ref[0])
    def _():
        y_ref[...] = jnp.zeros_like(y_ref)


def _experts(blk_exp, n_used, xs, layer, w_gate, w_up, w_down):
    n_slots, d = xs.shape
    n_blk = n_slots // MOE_BLOCK
    de = w_gate.shape[-1]
    return pl.pallas_call(
        _expert_kernel,
        grid_spec=pltpu.PrefetchScalarGridSpec(
            num_scalar_prefetch=2,
            grid=(n_blk,),
            in_specs=[
                pl.BlockSpec((MOE_BLOCK, d), lambda i, be, nu: (jnp.minimum(i, nu[0] - 1), 0)),
                pl.BlockSpec((1, 1, d, de), lambda i, be, nu: (layer, be[i], 0, 0)),
                pl.BlockSpec((1, 1, d, de), lambda i, be, nu: (layer, be[i], 0, 0)),
                pl.BlockSpec((1, 1, de, d), lambda i, be, nu: (layer, be[i], 0, 0)),
            ],
            out_specs=pl.BlockSpec((MOE_BLOCK, d), lambda i, be, nu: (i, 0)),
            scratch_shapes=[pltpu.VMEM((d, de), BF16), pltpu.VMEM((d, de), BF16), pltpu.VMEM((de, d), BF16)],
        ),
        out_shape=jax.ShapeDtypeStruct((n_slots, d), BF16),
        compiler_params=_cparams(("arbitrary",)),
        name="expert_mlp",
    )(blk_exp, n_used, xs, w_gate, w_up, w_down)


def _expert_onehots(rt):
    lane_f = lax.broadcasted_iota(jnp.int32, rt.shape, 1).astype(F32)
    return jnp.where(lane_f == rt[:, 0:1], 1.0, 0.0), jnp.where(lane_f == rt[:, 1:2], 1.0, 0.0)


def _plan_kernel(rt_ref, cnt_ref, dest_ref, pend_ref, run_s):
    i = pl.program_id(0)
    rows = rt_ref.shape[0]
    lane = lax.broadcasted_iota(jnp.int32, (rows, ROUTER_PAD), 1)
    oh0, oh1 = _expert_onehots(rt_ref[...])
    tot0 = jnp.sum(oh0, axis=0, keepdims=True)
    tot1 = jnp.sum(oh1, axis=0, keepdims=True)

    @pl.when(i == 0)
    def _():
        counts = cnt_ref[...]
        padded = jnp.floor((counts + float(MOE_BLOCK - 1)) * (1.0 / MOE_BLOCK)) * float(MOE_BLOCK)
        r = lax.broadcasted_iota(jnp.int32, (ROUTER_PAD, ROUTER_PAD), 0)
        c = lax.broadcasted_iota(jnp.int32, (ROUTER_PAD, ROUTER_PAD), 1)
        incl = jnp.where(r <= c, 1.0, 0.0).astype(BF16)
        hi, lo = _split_bf16(padded)
        pend = _dot(hi, incl) + _dot(lo, incl)
        pend_ref[...] = pend
        run_s[...] = pend - padded

    r = lax.broadcasted_iota(jnp.int32, (rows, rows), 0)
    c = lax.broadcasted_iota(jnp.int32, (rows, rows), 1)
    before = jnp.where(c < r, 1.0, 0.0).astype(BF16)
    run = run_s[0:1, :]
    pos0 = jnp.sum(oh0 * (run + _dot(before, oh0.astype(BF16))), axis=-1, keepdims=True)
    pos1 = jnp.sum(oh1 * (run + tot0 + _dot(before, oh1.astype(BF16))), axis=-1, keepdims=True)
    run_s[0:1, :] = run + tot0 + tot1
    dest_ref[...] = jnp.where(lane == 0, pos0, jnp.where(lane == 1, pos1, 0.0)).astype(jnp.int32)


def _plan(route, counts, n_rows):
    rows = PLAN_TILE
    return pl.pallas_call(
        _plan_kernel,
        grid=(n_rows // rows,),
        in_specs=[pl.BlockSpec((rows, ROUTER_PAD), lambda i: (i, 0)),
                  pl.BlockSpec((8, ROUTER_PAD), lambda i: (0, 0))],
        out_specs=[pl.BlockSpec((rows, ROUTER_PAD), lambda i: (i, 0)),
                   pl.BlockSpec((8, ROUTER_PAD), lambda i: (0, 0))],
        out_shape=[jax.ShapeDtypeStruct((n_rows, ROUTER_PAD), jnp.int32),
                   jax.ShapeDtypeStruct((8, ROUTER_PAD), F32)],
        scratch_shapes=[pltpu.VMEM((8, ROUTER_PAD), F32)],
        compiler_params=_cparams(("arbitrary",)),
        name="dispatch_plan",
    )(route, counts)


def _dispatch_kernel(pend_ref, dest_ref, h_ref, xs_ref, zero_s, sem, zsem):
    rows = h_ref.shape[0]

    n_blk = xs_ref.shape[0] // MOE_BLOCK
    n_used = pend_ref[N_EXPERTS - 1] // MOE_BLOCK

    def zero_block(start):
        return pltpu.make_async_copy(zero_s, xs_ref.at[pl.ds(pl.multiple_of(start, MOE_BLOCK), MOE_BLOCK)], zsem)

    @pl.when(pl.program_id(0) == 0)
    def _():
        zero_s[...] = jnp.zeros_like(zero_s)
        for e in range(N_EXPERTS):
            @pl.when(pend_ref[e] > 0)
            def _():
                zero_block(pend_ref[e] - MOE_BLOCK).start()
        lax.fori_loop(n_used, n_blk, lambda j, c: (zero_block(j * MOE_BLOCK).start(), c)[1], 0)
        for e in range(N_EXPERTS):
            @pl.when(pend_ref[e] > 0)
            def _():
                zero_block(pend_ref[e] - MOE_BLOCK).wait()
        lax.fori_loop(n_used, n_blk, lambda j, c: (zero_block(j * MOE_BLOCK).wait(), c)[1], 0)

    def row_copy(r, d):
        return pltpu.make_async_copy(h_ref.at[pl.ds(r, 1)], xs_ref.at[pl.ds(d, 1)], sem)

    def issue(r, carry):
        for k in range(TOP_K):
            row_copy(r, dest_ref[0, 0, k * rows + r]).start(priority=k % 2)
        return carry

    lax.fori_loop(0, rows, issue, 0, unroll=8)
    for _ in range(TOP_K * rows):
        row_copy(0, 0).wait()


def _dispatch(pend, dest_tiles, h2, n_slots, n_tiles):
    d = h2.shape[1]
    return pl.pallas_call(
        _dispatch_kernel,
        grid_spec=pltpu.PrefetchScalarGridSpec(
            num_scalar_prefetch=1,
            grid=(n_tiles,),
            in_specs=[pl.BlockSpec((1, 1, TOP_K * ROW_TILE), lambda i, pe: (i, 0, 0), memory_space=pltpu.SMEM),
                      pl.BlockSpec((ROW_TILE, d), lambda i, pe: (i, 0))],
            out_specs=pl.BlockSpec(memory_space=pl.ANY),
            scratch_shapes=[pltpu.VMEM((MOE_BLOCK, d), F32), pltpu.SemaphoreType.DMA, pltpu.SemaphoreType.DMA],
        ),
        out_shape=jax.ShapeDtypeStruct((n_slots, d), F32),
        compiler_params=_cparams(("arbitrary",)),
        name="dispatch_rows",
    )(pend, dest_tiles, h2)


def _moe(h2, route, counts, n_tiles, layer, w_gate, w_up, w_down):
    t, d = h2.shape
    dest, pend_f = _plan(route, counts, t)
    pend = pend_f[0, :N_EXPERTS].astype(jnp.int32)
    n_blk = -(-(t * TOP_K) // MOE_BLOCK) + N_EXPERTS
    blk_start = jnp.arange(n_blk, dtype=jnp.int32) * MOE_BLOCK
    blk_exp = jnp.minimum(jnp.sum((pend[None, :] <= blk_start[:, None]).astype(jnp.int32), axis=1), N_EXPERTS - 1)
    n_used = (pend[N_EXPERTS - 1] // MOE_BLOCK).reshape(1)
    dest2 = dest[:, :TOP_K]
    dest_tiles = dest2.reshape(n_tiles, ROW_TILE, TOP_K).transpose(0, 2, 1).reshape(n_tiles, 1, TOP_K * ROW_TILE)
    xs = _dispatch(pend, dest_tiles, h2, n_blk * MOE_BLOCK, n_tiles)
    ys = _experts(blk_exp, n_used, xs, layer, w_gate, w_up, w_down)
    return ys[dest2[:, 0]], ys[dest2[:, 1]]


def _final_kernel(x_ref, y0_ref, y1_ref, rt_ref, mod_ref, g_ref, o_ref):
    x = _moe_residual(x_ref, y0_ref, y1_ref, rt_ref, mod_ref)
    o_ref[...] = x * lax.rsqrt(jnp.mean(x * x, axis=-1, keepdims=True) + EPS) * g_ref[...]


def _final(x_lat, moe_out, mod, g_final, n_tiles, tiles_per_seq, batch):
    d = D_MODEL
    row = lambda i: (i, 0)
    return pl.pallas_call(
        _final_kernel,
        grid=(n_tiles,),
        in_specs=[pl.BlockSpec((ROW_TILE, d), row), pl.BlockSpec((ROW_TILE, d), row),
                  pl.BlockSpec((ROW_TILE, d), row), pl.BlockSpec((ROW_TILE, ROUTER_PAD), row),
                  pl.BlockSpec((1, 6, d), lambda i: (jnp.minimum(i // tiles_per_seq, batch), 0, 0)),
                  pl.BlockSpec((1, d), lambda i: (0, 0))],
        out_specs=pl.BlockSpec((ROW_TILE, d), row),
        out_shape=jax.ShapeDtypeStruct((n_tiles * ROW_TILE, d), F32),
        compiler_params=_cparams(("parallel",)),
        name="final_norm",
    )(x_lat, *moe_out, mod, g_final)


def _rope_tables(seq):
    quarter = HEAD_DIM // 4
    freqs = ROPE_THETA ** (-jnp.arange(quarter, dtype=F32) / quarter)
    pos = jnp.arange(seq)
    ang_r = (pos // GRID_W).astype(F32)[:, None] * freqs[None, :]
    ang_c = (pos % GRID_W).astype(F32)[:, None] * freqs[None, :]
    cos = jnp.concatenate([jnp.cos(ang_r)] * 2 + [jnp.cos(ang_c)] * 2, axis=-1)
    sin = jnp.concatenate([-jnp.sin(ang_r), jnp.sin(ang_r), -jnp.sin(ang_c), jnp.sin(ang_c)], axis=-1)
    reps = 128 // HEAD_DIM
    cos = jnp.tile(cos, (1, reps))
    sin = jnp.tile(sin, (1, reps))
    cos = jnp.concatenate([cos, jnp.ones((ROW_TILE, 128), F32)], axis=0)
    sin = jnp.concatenate([sin, jnp.zeros((ROW_TILE, 128), F32)], axis=0)
    return cos, sin


def kernel(x, c, ctx, c_ctx, w_mod, b_mod, g_norm1, g_norm2, w_in, attn_sink, ssm_lam_re, ssm_lam_im,
           ssm_log_dt, ssm_b_re, ssm_b_im, ssm_c_re, ssm_c_im, ssm_d, w_glu, w_br_attn, w_br_ssm, w_out,
           w_router_group, w_router_expert, w_exp_gate, w_exp_up, w_exp_down, g_final):
    batch, seq, d = x.shape
    ctx_len = ctx.shape[1]
    depth = w_mod.shape[0]
    assert d == D_MODEL and batch * SSM_STATE == 128
    assert seq % ROW_TILE == 0 and ctx_len % ROW_TILE == 0 and (batch * seq) % ctx_len == 0
    assert batch + 1 <= MOD_ROWS
    t_lat, t_ctx = batch * seq, batch * ctx_len
    tiles_per_seq = seq // ROW_TILE
    n_lat_tiles = t_lat // ROW_TILE
    n_all_tiles = (t_lat + t_ctx) // ROW_TILE
    nb_per_seq = seq // ATTN_BLOCK
    ncc, ncl = ctx_len // SSM_CHUNK, seq // SSM_CHUNK

    c_rows = jnp.zeros((MOD_ROWS, d), F32).at[:batch].set(c).at[batch].set(c_ctx)
    mod_all = _modulation(c_rows, w_mod, b_mod).reshape(depth, MOD_ROWS, 6, d)
    cos_t, sin_t = _rope_tables(seq)
    ssm_w = jax.vmap(functools.partial(_ssm_weights, batch=batch))(
        ssm_lam_re, ssm_lam_im, ssm_log_dt, ssm_b_re, ssm_b_im, ssm_c_re, ssm_c_im, ssm_d)
    x_parts = (x.reshape(t_lat, d), ctx.reshape(t_ctx, d))
    f_all = None
    for l in range(depth):
        ctx_out = l < depth - 1
        mod = mod_all[l]
        x_parts, (q, k, v, u, ga, gs) = _inproj(
            x_parts, f_all, mod_all[l - 1] if l > 0 else None, mod, g_norm1[l].reshape(1, d), cos_t, sin_t,
            w_in[l].astype(BF16), n_all_tiles, tiles_per_seq, n_lat_tiles, batch)
        sink = attn_sink[l].astype(F32)
        attn = _attention(sink, q, k, v, t_lat // ATTN_BLOCK, nb_per_seq, t_lat // ctx_len, ctx_len, True, 0)
        w1, abar, tmat, wout = (a[l] for a in ssm_w)
        z, z_ctx = _ssm_mixer(u, w1, abar, tmat, wout, batch, seq, ctx_len)
        ctx_parts = None
        if ctx_out:
            attn_c = _attention(sink, q, k, v, t_ctx // ATTN_BLOCK, ctx_len // ATTN_BLOCK, t_lat // ctx_len,
                                ctx_len, False, t_lat // ATTN_BLOCK)
            ctx_parts = (attn_c, z_ctx, x_parts[1])
        w_r = jnp.zeros((d, ROUTER_PAD), F32)
        w_r = w_r.at[:, :N_EXPERT_GROUPS].set(w_router_group[l])
        w_r = w_r.at[:, N_EXPERT_GROUPS:N_EXPERT_GROUPS + N_EXPERTS].set(w_router_expert[l])
        wr_hi, wr_lo = _split_bf16(w_r)
        n_rows = t_lat + t_ctx if ctx_out else t_lat
        x_all, h2, route, counts = _merge(
            attn, z, x_parts[0], ctx_parts, ga, gs, mod, g_norm2[l].reshape(1, d), w_glu[l].astype(BF16),
            w_br_attn[l].astype(BF16), w_br_ssm[l].astype(BF16), w_out[l].astype(BF16),
            wr_hi, wr_lo, n_rows, t_lat, seq, batch)
        y0, y1 = _moe(h2, route, counts, n_rows // ROW_TILE, l, w_exp_gate, w_exp_up, w_exp_down)
        f_all = (y0, y1, route)
        x_parts = (x_all,)
    out = _final(x_all, f_all, mod_all[depth - 1], g_final.reshape(1, d), n_lat_tiles, tiles_per_seq, batch)
    return out.reshape(batch, seq, d)
```

```python
import functools
import math

import jax
import jax.numpy as jnp
from jax import lax
from jax.experimental import pallas as pl
from jax.experimental.pallas import tpu as pltpu

F32 = jnp.float32
BF16 = jnp.bfloat16

D_MODEL = 1024
GRID_W = 64
N_HEADS = 8
N_KV_HEADS = 2
HEAD_DIM = 64
Q_GROUP = N_HEADS // N_KV_HEADS
ATTN_BLOCK = 128
ATTN_STEP_BLOCKS = 2
ROPE_THETA = 10000.0
D_SSM = D_MODEL // 2
SSM_GROUP = 16
N_SSM_GROUPS = D_SSM // SSM_GROUP
SSM_STATE = 64
N_EXPERT_GROUPS = 4
EXPERTS_PER_GROUP = 8
N_EXPERTS = N_EXPERT_GROUPS * EXPERTS_PER_GROUP
TOP_K = 2
D_EXPERT = D_MODEL // 2
Q_W = N_HEADS * HEAD_DIM
KV_W = N_KV_HEADS * HEAD_DIM
O_K = Q_W
O_V = O_K + KV_W
O_U = O_V + KV_W
O_GA = O_U + D_SSM
O_GS = O_GA + D_MODEL
D_IN = O_GS + D_MODEL
EPS = 1e-6
NEG_INF = -1e30

ROW_TILE = 256
MERGE_TILE = 512
MERGE_SUB = 256
SSM_CHUNK = 16
SSM_CW = SSM_CHUNK * SSM_GROUP
SSM_SCAN_GROUPS = 8
SSM_LANE_GROUPS = 128 // SSM_GROUP
SSM_LANE_TILES = D_SSM // 128
MOE_BLOCK = 256
PLAN_TILE = 512
ROUTER_PAD = 128
MOD_ROWS = 8
VMEM_LIMIT = 48 * 1024 * 1024


def _cparams(sem):
    return pltpu.CompilerParams(dimension_semantics=sem, vmem_limit_bytes=VMEM_LIMIT)


def _dot(a, b):
    return jnp.dot(a, b, preferred_element_type=F32)


def _split_bf16(a):
    hi = a.astype(BF16)
    lo = (a - hi.astype(F32)).astype(BF16)
    return hi, lo


def _rms_mod(x, g, shift, scale):
    y = x * lax.rsqrt(jnp.mean(x * x, axis=-1, keepdims=True) + EPS) * g
    return y * (1.0 + scale) + shift


def _mod_kernel(c_ref, w_ref, b_ref, o_ref):
    c = c_ref[...]
    s_hi, s_lo = _split_bf16(c * jax.nn.sigmoid(c))
    w_hi, w_lo = _split_bf16(w_ref[0])
    o_ref[0] = _dot(s_hi, w_hi) + _dot(s_lo, w_hi) + _dot(s_hi, w_lo) + b_ref[0]


def _modulation(c_rows, w_mod, b_mod):
    depth, d, n = w_mod.shape
    nb = n // 4
    return pl.pallas_call(
        _mod_kernel,
        grid=(depth, n // nb),
        in_specs=[
            pl.BlockSpec((MOD_ROWS, d), lambda l, j: (0, 0)),
            pl.BlockSpec((1, d, nb), lambda l, j: (l, 0, j)),
            pl.BlockSpec((1, 1, nb), lambda l, j: (l, 0, j)),
        ],
        out_specs=pl.BlockSpec((1, MOD_ROWS, nb), lambda l, j: (l, 0, j)),
        out_shape=jax.ShapeDtypeStruct((depth, MOD_ROWS, n), F32),
        compiler_params=_cparams(("arbitrary", "arbitrary")),
        name="modulation",
    )(c_rows, w_mod, b_mod.reshape(depth, 1, n))


def _moe_residual(x_ref, y0_ref, y1_ref, rt_ref, mod_ref):
    rt = rt_ref[...]
    f = rt[:, 2:3] * y0_ref[...].astype(F32) + rt[:, 3:4] * y1_ref[...].astype(F32)
    return x_ref[...] + mod_ref[0, 5:6, :] * f


def _inproj_kernel(has_f, n_lat_tiles, *refs):
    if has_f:
        (x_ref, y0_ref, y1_ref, rt_ref, modp_ref, mod_ref, g_ref, cos_ref, sin_ref, w_ref,
         xo_ref, q_ref, k_ref, v_ref, u_ref, ga_ref, gs_ref) = refs
        x = _moe_residual(x_ref, y0_ref, y1_ref, rt_ref, modp_ref)
        xo_ref[...] = x
    else:
        (xl_ref, xc_ref, mod_ref, g_ref, cos_ref, sin_ref, w_ref,
         q_ref, k_ref, v_ref, u_ref, ga_ref, gs_ref) = refs
        x = jnp.where(pl.program_id(0) >= n_lat_tiles, xc_ref[...], xl_ref[...])
    m = mod_ref[0]
    h = _rms_mod(x, g_ref[...], m[0:1], m[1:2]).astype(BF16)
    cos = cos_ref[...]
    sin = sin_ref[...]
    lane = lax.broadcasted_iota(jnp.int32, cos.shape, 1)
    first = (lane % (HEAD_DIM // 2)) < (HEAD_DIM // 4)

    def rope(t):
        sw = jnp.where(first, pltpu.roll(t, 128 - HEAD_DIM // 4, 1), pltpu.roll(t, HEAD_DIM // 4, 1))
        return t * cos + sw * sin

    def proj(lo, hi):
        return _dot(h, w_ref[:, lo:hi])

    q = proj(0, O_K)
    for j in range(Q_W // 128):
        q_ref[:, 128 * j:128 * (j + 1)] = (rope(q[:, 128 * j:128 * (j + 1)]) * HEAD_DIM ** -0.5).astype(BF16)
    k_ref[...] = rope(proj(O_K, O_V)).astype(BF16)
    v_ref[...] = proj(O_V, O_U).astype(BF16)
    u = proj(O_U, O_GA)
    for j in range(SSM_LANE_TILES):
        u_ref[j] = u[:, 128 * j:128 * (j + 1)]
    ga_ref[...] = jax.nn.sigmoid(proj(O_GA, O_GS)).astype(BF16)
    gs_ref[...] = jax.nn.sigmoid(proj(O_GS, D_IN)).astype(BF16)


def _inproj(x_parts, moe_out, mod_prev, mod, g1, cos_t, sin_t, w_in, n_tiles, tiles_per_seq, n_lat_tiles, batch):
    d = D_MODEL
    has_f = moe_out is not None
    row = lambda i: (i, 0)
    modi = lambda i: (jnp.minimum(i // tiles_per_seq, batch), 0, 0)
    const = lambda i: (0, 0)
    ropei = lambda i: (jnp.where(i < n_lat_tiles, i % tiles_per_seq, tiles_per_seq), 0)
    if has_f:
        in_specs = [pl.BlockSpec((ROW_TILE, d), row), pl.BlockSpec((ROW_TILE, d), row),
                    pl.BlockSpec((ROW_TILE, d), row), pl.BlockSpec((ROW_TILE, ROUTER_PAD), row),
                    pl.BlockSpec((1, 6, d), modi)]
        args = [*x_parts, *moe_out, mod_prev]
    else:
        in_specs = [pl.BlockSpec((ROW_TILE, d), lambda i: (jnp.minimum(i, n_lat_tiles - 1), 0)),
                    pl.BlockSpec((ROW_TILE, d), lambda i: (jnp.maximum(i - n_lat_tiles, 0), 0))]
        args = list(x_parts)
    in_specs += [
        pl.BlockSpec((1, 6, d), modi),
        pl.BlockSpec((1, d), const),
        pl.BlockSpec((ROW_TILE, 128), ropei),
        pl.BlockSpec((ROW_TILE, 128), ropei),
        pl.BlockSpec((d, D_IN), const),
    ]
    args += [mod, g1, cos_t, sin_t, w_in]
    widths = [Q_W, KV_W, KV_W, D_SSM, D_MODEL, D_MODEL]
    out_specs = [pl.BlockSpec((ROW_TILE, w), row) for w in widths]
    out_shape = [jax.ShapeDtypeStruct((n_tiles * ROW_TILE, w), BF16) for w in widths]
    out_specs[3] = pl.BlockSpec((SSM_LANE_TILES, ROW_TILE, 128), lambda i: (0, i, 0))
    out_shape[3] = jax.ShapeDtypeStruct((SSM_LANE_TILES, n_tiles * ROW_TILE, 128), F32)
    if has_f:
        out_specs = [pl.BlockSpec((ROW_TILE, d), row)] + out_specs
        out_shape = [jax.ShapeDtypeStruct((n_tiles * ROW_TILE, d), F32)] + out_shape
    outs = pl.pallas_call(
        functools.partial(_inproj_kernel, has_f, n_lat_tiles),
        grid=(n_tiles,),
        in_specs=in_specs,
        out_specs=out_specs,
        out_shape=out_shape,
        compiler_params=_cparams(("parallel",)),
        name="inproj",
    )(*args)
    if has_f:
        return (outs[0],), outs[1:]
    return x_parts, outs


def _attn_block(sink_ref, q, k_tiles, v_tiles, biases):
    blk = q.shape[0]
    outs = []
    for g in range(N_KV_HEADS):
        gs = slice(g * HEAD_DIM, (g + 1) * HEAD_DIM)
        k_all = jnp.concatenate([t[:, gs] for t in k_tiles], axis=0)
        v_all = jnp.concatenate([t[:, gs] for t in v_tiles], axis=0)
        v_ext = jnp.concatenate([v_all, jnp.ones_like(v_all)], axis=1)
        qg = jnp.concatenate(
            [q[:, (g * Q_GROUP + h) * HEAD_DIM:(g * Q_GROUP + h + 1) * HEAD_DIM] for h in range(Q_GROUP)], axis=0)
        s_all = lax.dot_general(qg, k_all, (((1,), (1,)), ((), ())), preferred_element_type=F32)
        for h in range(Q_GROUP):
            s = s_all[h * blk:(h + 1) * blk]
            tiles, col = [], 0
            for kt, bias in zip(k_tiles, biases):
                t = s[:, col:col + kt.shape[0]]
                tiles.append(t if bias is None else t + bias)
                col += kt.shape[0]
            mx = tiles[0]
            for t in tiles[1:]:
                for c0 in range(0, t.shape[1], blk):
                    mx = jnp.maximum(mx, t[:, c0:c0 + blk])
            sink = sink_ref[g * Q_GROUP + h]
            m = jnp.maximum(jnp.max(mx, axis=-1, keepdims=True), sink)
            p = jnp.exp(jnp.concatenate([(t - m).astype(BF16) for t in tiles], axis=1))
            o_ext = _dot(p, v_ext)
            denom = o_ext[:, HEAD_DIM:HEAD_DIM + 1] + jnp.exp(sink - m)
            outs.append((o_ext[:, :HEAD_DIM] * (1.0 / denom)).astype(BF16))
    return jnp.concatenate(outs, axis=1)


def _attn_kernel(band, nb_per_seq, sink_ref, *refs):
    blk = ATTN_BLOCK
    if not band:
        q_ref, kx_ref, vx_ref, o_ref = refs
        for a in range(q_ref.shape[0] // blk):
            rows = slice(a * blk, (a + 1) * blk)
            o_ref[rows, :] = _attn_block(sink_ref, q_ref[rows, :], [kx_ref[...]], [vx_ref[...]], [None])
        return
    q_ref, kp_ref, kc_ref, kn_ref, vp_ref, vc_ref, vn_ref, kx_ref, vx_ref, o_ref = refs
    n_sub = q_ref.shape[0] // blk
    j0 = (pl.program_id(0) * n_sub) % nb_per_seq
    r = lax.broadcasted_iota(jnp.int32, (blk, blk), 0)
    c = lax.broadcasted_iota(jnp.int32, (blk, blk), 1)
    k_blocks = [kp_ref[...]] + [kc_ref[a * blk:(a + 1) * blk, :] for a in range(n_sub)] + [kn_ref[...]]
    v_blocks = [vp_ref[...]] + [vc_ref[a * blk:(a + 1) * blk, :] for a in range(n_sub)] + [vn_ref[...]]
    for a in range(n_sub):
        edge_p = jnp.where(j0 + a > 0, 0.0, NEG_INF).astype(F32)
        edge_n = jnp.where(j0 + a < nb_per_seq - 1, 0.0, NEG_INF).astype(F32)
        bias_p = jnp.where(c >= r, edge_p, NEG_INF).astype(F32)
        bias_n = jnp.where(c <= r, edge_n, NEG_INF).astype(F32)
        rows = slice(a * blk, (a + 1) * blk)
        o_ref[rows, :] = _attn_block(sink_ref, q_ref[rows, :], k_blocks[a:a + 3] + [kx_ref[...]],
                                     v_blocks[a:a + 3] + [vx_ref[...]], [bias_p, None, bias_n, None])


def _attention(sink, q, k, v, n_blocks, nb_per_seq, kx_block0, ctx_len, band, q_block0):
    blk = ATTN_BLOCK
    n_sub = ATTN_STEP_BLOCKS
    assert n_blocks % n_sub == 0 and nb_per_seq % n_sub == 0 and q_block0 % n_sub == 0
    step = n_sub * blk
    n_steps = n_blocks // n_sub
    qi = lambda i, s: (q_block0 // n_sub + i, 0)
    cur = lambda i, s: (i, 0)
    prv = lambda i, s: (jnp.maximum(i * n_sub - 1, 0), 0)
    nxt = lambda i, s: (jnp.minimum((i + 1) * n_sub, n_blocks - 1), 0)
    kxi = lambda i, s: (kx_block0 + (i * n_sub) // nb_per_seq, 0)
    kspec = lambda f: pl.BlockSpec((blk, KV_W), f)
    cspec = pl.BlockSpec((step, KV_W), cur)
    xspec = pl.BlockSpec((ctx_len, KV_W), kxi)
    if band:
        in_specs = [pl.BlockSpec((step, Q_W), qi), kspec(prv), cspec, kspec(nxt),
                    kspec(prv), cspec, kspec(nxt), xspec, xspec]
        args = (q, k, k, k, v, v, v, k, v)
    else:
        in_specs = [pl.BlockSpec((step, Q_W), qi), xspec, xspec]
        args = (q, k, v)
    return pl.pallas_call(
        functools.partial(_attn_kernel, band, nb_per_seq),
        grid_spec=pltpu.PrefetchScalarGridSpec(
            num_scalar_prefetch=1,
            grid=(n_steps,),
            in_specs=in_specs,
            out_specs=pl.BlockSpec((step, Q_W), lambda i, s: (i, 0)),
        ),
        out_shape=jax.ShapeDtypeStruct((n_blocks * blk, Q_W), BF16),
        compiler_params=_cparams(("parallel",)),
        name="band_attention" if band else "context_attention",
    )(sink, *args)


def _dot_f32(a, b_t):
    a_hi, a_lo = _split_bf16(a)
    b_hi, b_lo = _split_bf16(b_t)
    dn = (((1,), (1,)), ((), ()))
    dg = functools.partial(lax.dot_general, dimension_numbers=dn, preferred_element_type=F32)
    return dg(a_hi, b_hi) + dg(a_lo, b_hi) + dg(a_hi, b_lo)


def _ssm_weight_kernel(batch, lam_r_ref, lam_c_ref, bt_ref, c_ref, ct_ref, d_ref, w1_ref, abar_ref, t_ref, wo_ref):
    lc, mm, p = SSM_CHUNK, SSM_GROUP, SSM_STATE
    up_r = lax.broadcasted_iota(jnp.int32, (lc, 1), 0).astype(F32)
    up_c = lax.broadcasted_iota(jnp.int32, (1, lc), 1).astype(F32)
    lane = lax.broadcasted_iota(jnp.int32, (mm, lc * mm), 1)
    row = lax.broadcasted_iota(jnp.int32, (mm, lc * mm), 0)
    planes_in, planes_out, abar_rows, kt = [], [], [], []
    for d in range(2):
        lr, li, dt = lam_r_ref[0, d, 0:1, :], lam_r_ref[0, d, 1:2, :], jnp.exp(lam_r_ref[0, d, 2:3, :])
        lr_c, li_c, dt_c = lam_c_ref[0, d, :, 0:1], lam_c_ref[0, d, :, 1:2], jnp.exp(lam_c_ref[0, d, :, 2:3])

        def powers(expo):
            mag = jnp.exp(lr * dt * expo)
            return mag * jnp.cos(li * dt * expo), mag * jnp.sin(li * dt * expo)

        def powers_t(expo):
            mag = jnp.exp(lr_c * dt_c * expo)
            return mag * jnp.cos(li_c * dt_c * expo), mag * jnp.sin(li_c * dt_c * expo)

        a_re, a_im = powers(jnp.ones((1, 1), F32))
        den = lr * lr + li * li
        nr = a_re - 1.0
        f_re = (nr * lr + a_im * li) / den
        f_im = (a_im * lr - nr * li) / den
        bt_re, bt_im = bt_ref[0, d, 0], bt_ref[0, d, 1]
        bbt_re = f_re * bt_re - f_im * bt_im
        bbt_im = f_re * bt_im + f_im * bt_re
        c_re, c_im = c_ref[0, d, 0], c_ref[0, d, 1]
        ct_re, ct_im = ct_ref[0, d, 0], ct_ref[0, d, 1]

        pr, pi = powers(lc - 1.0 - up_r if d == 0 else up_r)
        planes_in.append(jnp.concatenate(
            [pr[s:s + 1] * bbt_re - pi[s:s + 1] * bbt_im for s in range(lc)], axis=0))
        planes_in.append(jnp.concatenate(
            [pr[s:s + 1] * bbt_im + pi[s:s + 1] * bbt_re for s in range(lc)], axis=0))

        qr, qi = powers_t(up_c + 1.0 if d == 0 else lc - up_c)
        planes_out.append(jnp.concatenate(
            [ct_re * qr[:, t:t + 1] - ct_im * qi[:, t:t + 1] for t in range(lc)], axis=1))
        planes_out.append(jnp.concatenate(
            [-ct_re * qi[:, t:t + 1] - ct_im * qr[:, t:t + 1] for t in range(lc)], axis=1))

        kr, ki = powers(up_r if d == 0 else lc - 1.0 - up_r)
        cp_re = jnp.concatenate([c_re * kr[j:j + 1] - c_im * ki[j:j + 1] for j in range(lc)], axis=0)
        cp_im = jnp.concatenate([c_re * ki[j:j + 1] + c_im * kr[j:j + 1] for j in range(lc)], axis=0)
        kt.append(_dot_f32(jnp.concatenate([bbt_re, bbt_im], axis=1), jnp.concatenate([cp_re, -cp_im], axis=1)))

        e_re, e_im = powers(jnp.full((1, 1), float(lc), F32))
        abar_rows += [jnp.concatenate([e_re] * batch, axis=1), jnp.concatenate([e_im] * batch, axis=1)]

    dvec = d_ref[0]
    blocks = []
    for s in range(lc):
        fwd = kt[0] if s == 0 else pltpu.roll(kt[0], mm * s, 1)
        bwd = kt[1] if s == lc - 1 else pltpu.roll(kt[1], mm * (s + 1), 1)
        blk = jnp.where(lane >= mm * s, fwd, 0.0) + jnp.where(lane < mm * (s + 1), bwd, 0.0)
        blocks.append(blk + jnp.where(lane == mm * s + row, dvec, 0.0))
    t_ref[0] = jnp.concatenate(blocks, axis=0).astype(BF16)
    abar_ref[0] = jnp.concatenate(abar_rows, axis=0)
    zero_in = jnp.zeros((lc * mm, p), F32)
    zero_out = jnp.zeros((p, lc * mm), F32)
    for b in range(batch):
        w1_ref[0, b] = jnp.concatenate(
            [pl_ if bb == b else zero_in for pl_ in planes_in for bb in range(batch)], axis=1).astype(BF16)
        wo_ref[0, b] = jnp.concatenate(
            [pl_ if bb == b else zero_out for pl_ in planes_out for bb in range(batch)], axis=0).astype(BF16)


def _ssm_weights_all(lam_re, lam_im, log_dt, b_re, b_im, c_re, c_im, d_skip, batch):
    depth = lam_re.shape[0]
    g, p, mm, lc = N_SSM_GROUPS, SSM_STATE, SSM_GROUP, SSM_CHUNK
    n = depth * g
    cw, sw = lc * mm, 4 * batch * p

    def per_group(a):
        return jnp.moveaxis(a.astype(F32), 2, 1).reshape((n, 2) + a.shape[3:])

    lam = jnp.stack([per_group(lam_re), per_group(lam_im),
                     jnp.broadcast_to(per_group(log_dt)[..., None], (n, 2, p))], axis=2)
    bt = jnp.stack([per_group(b_re), per_group(b_im)], axis=2).swapaxes(-1, -2)
    c = jnp.stack([per_group(c_re), per_group(c_im)], axis=2)
    d_t = jnp.tile(d_skip.astype(F32).reshape(n, 1, mm), (1, 1, lc))
    full = lambda *shape: pl.BlockSpec((1,) + shape, lambda i: (i,) + (0,) * len(shape))
    w1, abar, tmat, wout = pl.pallas_call(
        functools.partial(_ssm_weight_kernel, batch),
        grid=(n,),
        in_specs=[full(2, 3, p), full(2, p, 3), full(2, 2, mm, p), full(2, 2, mm, p), full(2, 2, p, mm),
                  full(1, cw)],
        out_specs=[full(batch, cw, sw), full(4, batch * p), full(cw, cw), full(batch, sw, cw)],
        out_shape=[jax.ShapeDtypeStruct((n, batch, cw, sw), BF16), jax.ShapeDtypeStruct((n, 4, batch * p), F32),
                   jax.ShapeDtypeStruct((n, cw, cw), BF16), jax.ShapeDtypeStruct((n, batch, sw, cw), BF16)],
        compiler_params=_cparams(("parallel",)),
        name="ssm_weights",
    )(lam, lam.swapaxes(-1, -2), bt, c, c.swapaxes(-1, -2), d_t)
    abar = abar.reshape(depth, g, 4, batch * p).transpose(0, 2, 1, 3)
    return (w1.reshape(depth, g, batch, cw, sw), abar, tmat.reshape(depth, g, cw, cw),
            wout.reshape(depth, g, batch, sw, cw))


def _ssm_p1_kernel(ul_ref, uc_ref, w_ref, v_ref, *s_refs):
    b = pl.program_id(1)
    ncc, ncl = uc_ref.shape[0] // SSM_CHUNK, ul_ref.shape[0] // SSM_CHUNK
    nc = ncc + ncl
    pw = s_refs[0].shape[1]
    xs = [jnp.concatenate([uc_ref[pl.ds(s, ncc, stride=SSM_CHUNK), :], ul_ref[pl.ds(s, ncl, stride=SSM_CHUNK), :]],
                          axis=0) for s in range(SSM_CHUNK)]
    for j in range(SSM_LANE_GROUPS):
        v = jnp.concatenate([x[:, j * SSM_GROUP:(j + 1) * SSM_GROUP] for x in xs], axis=1).astype(BF16)
        v_ref[j] = v
        acc = _dot(v, w_ref[j])
        for k, s_ref in enumerate(s_refs):
            part = acc[:, k * pw:(k + 1) * pw]
            rows = slice(j * nc, (j + 1) * nc)

            @pl.when(b == 0)
            def _():
                s_ref[rows, :] = part

            @pl.when(b > 0)
            def _():
                s_ref[rows, :] = s_ref[rows, :] + part


def _ssm_scan_kernel(ncc, ncl, a_ref, sfr_ref, sfi_ref, sbr_ref, sbi_ref, xfr_ref, xfi_ref, xbr_ref, xbi_ref):
    nc = ncc + ncl
    gb = a_ref.shape[1]
    pw = a_ref.shape[2]
    afr, afi, abr, abi = a_ref[0], a_ref[1], a_ref[2], a_ref[3]

    def rows(r):
        return pl.ds(r, gb, stride=nc)

    def step(rf, rb, carry):
        xfr, xfi, xbr, xbi = carry
        xfr_ref[rows(rf), :] = xfr
        xfi_ref[rows(rf), :] = xfi
        xbr_ref[rows(rb), :] = xbr
        xbi_ref[rows(rb), :] = xbi
        sfr = sfr_ref[rows(rf), :]
        sfi = sfi_ref[rows(rf), :]
        sbr = sbr_ref[rows(rb), :]
        sbi = sbi_ref[rows(rb), :]
        return (afr * xfr - afi * xfi + sfr, afr * xfi + afi * xfr + sfi,
                abr * xbr - abi * xbi + sbr, abr * xbi + abi * xbr + sbi)

    zero = jnp.zeros((gb, pw), F32)
    carry = lax.fori_loop(0, ncc, lambda t, c: step(t, ncc - 1 - t, c), (zero, zero, zero, zero))
    lax.fori_loop(0, ncl, lambda t, c: step(ncc + t, nc - 1 - t, c), carry)


def _ssm_p3_kernel(v_ref, xfr_ref, xfi_ref, xbr_ref, xbi_ref, t_ref, wo_ref, zl_ref, zc_ref):
    ncc, ncl = zc_ref.shape[0] // SSM_CHUNK, zl_ref.shape[0] // SSM_CHUNK
    nc = ncc + ncl
    ys = []
    for j in range(SSM_LANE_GROUPS):
        rows = slice(j * nc, (j + 1) * nc)
        xin = jnp.concatenate([xfr_ref[rows, :], xfi_ref[rows, :], xbr_ref[rows, :], xbi_ref[rows, :]],
                              axis=1).astype(BF16)
        y = _dot(v_ref[j], t_ref[j]) + _dot(xin, wo_ref[j])
        ys.append(jax.nn.gelu(y, approximate=True))
    for t in range(SSM_CHUNK):
        zt = jnp.concatenate([y[:, t * SSM_GROUP:(t + 1) * SSM_GROUP] for y in ys], axis=1)
        zc_ref[pl.ds(t, ncc, stride=SSM_CHUNK), :] = zt[:ncc]
        zl_ref[pl.ds(t, ncl, stride=SSM_CHUNK), :] = zt[ncc:]


def _ssm_mixer(u4, layer, w1, abar, tmat, wout, batch, seq, ctx_len):
    n_lt = u4.shape[0]
    g = N_SSM_GROUPS
    lg = SSM_LANE_GROUPS
    ncc, ncl = ctx_len // SSM_CHUNK, seq // SSM_CHUNK
    nc = ncc + ncl
    cw = SSM_CW
    sw = w1.shape[-1]
    pw = sw // 4
    ctx_blk0 = (batch * seq) // ctx_len
    plane_shape = [jax.ShapeDtypeStruct((g * nc, pw), F32)] * 4
    lat_spec = pl.BlockSpec((None, seq, 128), lambda k, b: (k, b, 0))
    ctx_spec = pl.BlockSpec((None, ctx_len, 128), lambda k, b: (k, ctx_blk0 + b, 0))
    v_spec = pl.BlockSpec((lg, None, nc, cw), lambda k, b: (k, b, 0, 0))
    plane_spec = pl.BlockSpec((lg * nc, pw), lambda k, b: (k, 0))
    v_chunks, *s_planes = pl.pallas_call(
        _ssm_p1_kernel,
        grid=(n_lt, batch),
        in_specs=[lat_spec, ctx_spec,
                  pl.BlockSpec((None, lg, None, cw, sw), lambda k, b: (layer, k, b, 0, 0))],
        out_specs=[v_spec] + [plane_spec] * 4,
        out_shape=[jax.ShapeDtypeStruct((g, batch, nc, cw), BF16)] + plane_shape,
        compiler_params=_cparams(("parallel", "arbitrary")),
        name="ssm_chunk_states",
    )(u4, u4, w1)
    gb = SSM_SCAN_GROUPS
    x_planes = pl.pallas_call(
        functools.partial(_ssm_scan_kernel, ncc, ncl),
        grid=(g // gb,),
        in_specs=[pl.BlockSpec((None, 4, gb, pw), lambda i: (layer, 0, i, 0))]
        + [pl.BlockSpec((gb * nc, pw), lambda i: (i, 0))] * 4,
        out_specs=[pl.BlockSpec((gb * nc, pw), lambda i: (i, 0))] * 4,
        out_shape=plane_shape,
        compiler_params=_cparams(("parallel",)),
        name="ssm_chunk_scan",
    )(abar, *s_planes)
    return pl.pallas_call(
        _ssm_p3_kernel,
        grid=(n_lt, batch),
        in_specs=[v_spec] + [plane_spec] * 4
        + [pl.BlockSpec((None, lg, cw, cw), lambda k, b: (layer, k, 0, 0)),
           pl.BlockSpec((None, lg, None, sw, cw), lambda k, b: (layer, k, b, 0, 0))],
        out_specs=[pl.BlockSpec((None, seq, 128), lambda k, b: (k, b, 0)),
                   pl.BlockSpec((None, ctx_len, 128), lambda k, b: (k, b, 0))],
        out_shape=[jax.ShapeDtypeStruct((n_lt, batch * seq, 128), F32),
                   jax.ShapeDtypeStruct((n_lt, batch * ctx_len, 128), F32)],
        compiler_params=_cparams(("parallel", "arbitrary")),
        name="ssm_chunk_outputs",
    )(v_chunks, *x_planes, tmat, wout)


def _merge_kernel(n_lat_tiles, *refs):
    if n_lat_tiles is None:
        attn_ref, z_ref, x_ref = refs[:3]
        refs = refs[3:]
    else:
        attn_ref, z_ref, x_ref, attn_c_ref, z_c_ref, x_c_ref = refs[:6]
        refs = refs[6:]
        is_ctx = pl.program_id(0) >= n_lat_tiles
    (ga_ref, gs_ref, mod_ref, g2_ref, wglu_ref, wba_ref, wbs_ref, wout_ref, wrh_ref, wrl_ref,
     xo_ref, h2_ref, rt_ref, cnt_ref) = refs
    m = mod_ref[0]
    tile_counts = None
    for r in range(MERGE_TILE // MERGE_SUB):
        rows = slice(r * MERGE_SUB, (r + 1) * MERGE_SUB)
        zf = jnp.concatenate([z_ref[j, rows, :] for j in range(SSM_LANE_TILES)], axis=1)
        attn = attn_ref[rows, :]
        x_in = x_ref[rows, :]
        if n_lat_tiles is not None:
            zf = jnp.where(is_ctx, jnp.concatenate([z_c_ref[j, rows, :] for j in range(SSM_LANE_TILES)], axis=1), zf)
            attn = jnp.where(is_ctx, attn_c_ref[rows, :], attn)
            x_in = jnp.where(is_ctx, x_c_ref[rows, :], x_in)
        z = zf.astype(BF16)
        glu = (z.astype(F32) * jax.nn.sigmoid(_dot(z, wglu_ref[...]))).astype(BF16)
        mix = (ga_ref[rows, :].astype(F32) * _dot(attn, wba_ref[...])
               + gs_ref[rows, :].astype(F32) * _dot(glu, wbs_ref[...])).astype(BF16)
        x = x_in + m[2:3] * _dot(mix, wout_ref[...])
        xo_ref[rows, :] = x
        h2 = _rms_mod(x, g2_ref[...], m[3:4], m[4:5])
        hi, lo = _split_bf16(h2)
        h2_ref[rows, :] = hi.astype(F32)
        logits = _dot(hi, wrh_ref[...]) + _dot(lo, wrh_ref[...]) + _dot(hi, wrl_ref[...])
        rt = _route(logits)
        rt_ref[rows, :] = rt
        oh0, oh1 = _expert_onehots(rt)
        part = jnp.sum(oh0 + oh1, axis=0, keepdims=True)
        tile_counts = part if tile_counts is None else tile_counts + part

    @pl.when(pl.program_id(0) == 0)
    def _():
        cnt_ref[...] = jnp.zeros_like(cnt_ref)

    cnt_ref[0:1, :] = cnt_ref[0:1, :] + tile_counts


def _route(lg):
    ng, epg = N_EXPERT_GROUPS, EXPERTS_PER_GROUP
    lane_i = lax.broadcasted_iota(jnp.int32, lg.shape, 1)
    lane = lane_i.astype(F32)
    big = float(ROUTER_PAD)

    def rmax(mask_val):
        return jnp.max(mask_val, axis=-1, keepdims=True)

    def first_lane(mask, val, mx):
        return jnp.min(jnp.where(mask, jnp.where(val == mx, lane, big), big), axis=-1, keepdims=True)

    gmask = lane_i < ng
    lgm = jnp.where(gmask, lg, NEG_INF)
    mg = rmax(lgm)
    g_prob = 1.0 / jnp.sum(jnp.exp(lgm - mg), axis=-1, keepdims=True)
    g_idx = first_lane(gmask, lg, mg)
    egroup = jnp.floor((lane - float(ng)) * (1.0 / epg))
    emask = egroup == g_idx
    l1 = jnp.where(emask, lg, NEG_INF)
    m1 = rmax(l1)
    i1 = first_lane(emask, lg, m1)
    l2 = jnp.where(lane == i1, NEG_INF, l1)
    m2 = rmax(l2)
    i2 = jnp.min(jnp.where(l2 == m2, jnp.where(emask, lane, big), big), axis=-1, keepdims=True)
    r = jnp.exp(m2 - m1)
    w1 = g_prob / (1.0 + r)
    w2 = w1 * r
    return jnp.where(lane_i == 0, i1 - float(ng),
                     jnp.where(lane_i == 1, i2 - float(ng),
                               jnp.where(lane_i == 2, w1, jnp.where(lane_i == 3, w2, 0.0))))


def _merge(attn, z, x, ctx_parts, ga, gs, mod, g2, w_glu, w_ba, w_bs, w_out, wr_hi, wr_lo, n_rows, t_lat, seq, batch):
    d = D_MODEL
    tile = MERGE_TILE
    assert n_rows % tile == 0 and t_lat % tile == 0 and seq % tile == 0
    n_tiles = n_rows // tile
    n_lat = t_lat // tile
    row = lambda i: (i, 0)
    lat = lambda i: (jnp.minimum(i, n_lat - 1), 0)
    cxt = lambda i: (jnp.maximum(i - n_lat, 0), 0)
    const = lambda i: (0, 0)
    modi = lambda i: (jnp.minimum(i // (seq // tile), batch), 0, 0)
    in_specs = [pl.BlockSpec((tile, Q_W), lat),
                pl.BlockSpec((SSM_LANE_TILES, tile, 128), lambda i: (0,) + lat(i)),
                pl.BlockSpec((tile, d), lat)]
    args = [attn, z, x]
    if ctx_parts is not None:
        in_specs += [pl.BlockSpec((tile, Q_W), cxt),
                     pl.BlockSpec((SSM_LANE_TILES, tile, 128), lambda i: (0,) + cxt(i)),
                     pl.BlockSpec((tile, d), cxt)]
        args += list(ctx_parts)
    in_specs += [
        pl.BlockSpec((tile, d), row),
        pl.BlockSpec((tile, d), row),
        pl.BlockSpec((1, 6, d), modi),
        pl.BlockSpec((1, d), const),
        pl.BlockSpec((D_SSM, D_SSM), const),
        pl.BlockSpec((Q_W, d), const),
        pl.BlockSpec((D_SSM, d), const),
        pl.BlockSpec((d, d), const),
        pl.BlockSpec((d, ROUTER_PAD), const),
        pl.BlockSpec((d, ROUTER_PAD), const),
    ]
    args += [ga, gs, mod, g2, w_glu, w_ba, w_bs, w_out, wr_hi, wr_lo]
    return pl.pallas_call(
        functools.partial(_merge_kernel, n_lat if ctx_parts is not None else None),
        grid=(n_tiles,),
        in_specs=in_specs,
        out_specs=[pl.BlockSpec((tile, d), row), pl.BlockSpec((tile, d), row),
                   pl.BlockSpec((tile, ROUTER_PAD), row), pl.BlockSpec((8, ROUTER_PAD), const)],
        out_shape=[jax.ShapeDtypeStruct((n_rows, d), F32), jax.ShapeDtypeStruct((n_rows, d), F32),
                   jax.ShapeDtypeStruct((n_rows, ROUTER_PAD), F32), jax.ShapeDtypeStruct((8, ROUTER_PAD), F32)],
        compiler_params=_cparams(("arbitrary",)),
        name="merge_router",
    )(*args)


def _expert_kernel(be_ref, nu_ref, x_ref, wg_ref, wu_ref, wd_ref, y_ref, wg_s, wu_s, wd_s):
    i = pl.program_id(0)
    changed = jnp.logical_or(i == 0, be_ref[i] != be_ref[jnp.maximum(i - 1, 0)])

    @pl.when(changed)
    def _():
        wg_s[...] = wg_ref[0, 0].astype(BF16)
        wu_s[...] = wu_ref[0, 0].astype(BF16)
        wd_s[...] = wd_ref[0, 0].astype(BF16)

    @pl.when(i < nu_ref[0])
    def _():
        x = x_ref[...].astype(BF16)
        gate = _dot(x, wg_s[...])
        up = _dot(x, wu_s[...])
        hid = (gate * jax.nn.sigmoid(gate) * up).astype(BF16)
        y_ref[...] = _dot(hid, wd_s[...]).astype(y_ref.dtype)

    @pl.when(i >= nu_ref[0])
    def _():
        y_ref[...] = jnp.zeros_like(y_ref)


def _experts(blk_exp, n_used, xs, layer, w_gate, w_up, w_down):
    n_slots, d = xs.shape
    n_blk = n_slots // MOE_BLOCK
    de = w_gate.shape[-1]
    return pl.pallas_call(
        _expert_kernel,
        grid_spec=pltpu.PrefetchScalarGridSpec(
            num_scalar_prefetch=2,
            grid=(n_blk,),
            in_specs=[
                pl.BlockSpec((MOE_BLOCK, d), lambda i, be, nu: (jnp.minimum(i, nu[0] - 1), 0)),
                pl.BlockSpec((1, 1, d, de), lambda i, be, nu: (layer, be[i], 0, 0)),
                pl.BlockSpec((1, 1, d, de), lambda i, be, nu: (layer, be[i], 0, 0)),
                pl.BlockSpec((1, 1, de, d), lambda i, be, nu: (layer, be[i], 0, 0)),
            ],
            out_specs=pl.BlockSpec((MOE_BLOCK, d), lambda i, be, nu: (i, 0)),
            scratch_shapes=[pltpu.VMEM((d, de), BF16), pltpu.VMEM((d, de), BF16), pltpu.VMEM((de, d), BF16)],
        ),
        out_shape=jax.ShapeDtypeStruct((n_slots, d), BF16),
        compiler_params=_cparams(("arbitrary",)),
        name="expert_mlp",
    )(blk_exp, n_used, xs, w_gate, w_up, w_down)


def _expert_onehots(rt):
    lane_f = lax.broadcasted_iota(jnp.int32, rt.shape, 1).astype(F32)
    return jnp.where(lane_f == rt[:, 0:1], 1.0, 0.0), jnp.where(lane_f == rt[:, 1:2], 1.0, 0.0)


def _plan_kernel(rt_ref, cnt_ref, dest_ref, pend_ref, run_s):
    i = pl.program_id(0)
    rows = rt_ref.shape[0]
    lane = lax.broadcasted_iota(jnp.int32, (rows, ROUTER_PAD), 1)
    oh0, oh1 = _expert_onehots(rt_ref[...])
    tot0 = jnp.sum(oh0, axis=0, keepdims=True)
    tot1 = jnp.sum(oh1, axis=0, keepdims=True)

    @pl.when(i == 0)
    def _():
        counts = cnt_ref[...]
        padded = jnp.floor((counts + float(MOE_BLOCK - 1)) * (1.0 / MOE_BLOCK)) * float(MOE_BLOCK)
        r = lax.broadcasted_iota(jnp.int32, (ROUTER_PAD, ROUTER_PAD), 0)
        c = lax.broadcasted_iota(jnp.int32, (ROUTER_PAD, ROUTER_PAD), 1)
        incl = jnp.where(r <= c, 1.0, 0.0).astype(BF16)
        hi, lo = _split_bf16(padded)
        pend = _dot(hi, incl) + _dot(lo, incl)
        pend_ref[...] = pend
        run_s[...] = pend - padded

    r = lax.broadcasted_iota(jnp.int32, (rows, rows), 0)
    c = lax.broadcasted_iota(jnp.int32, (rows, rows), 1)
    before = jnp.where(c < r, 1.0, 0.0).astype(BF16)
    run = run_s[0:1, :]
    pos0 = jnp.sum(oh0 * (run + _dot(before, oh0.astype(BF16))), axis=-1, keepdims=True)
    pos1 = jnp.sum(oh1 * (run + tot0 + _dot(before, oh1.astype(BF16))), axis=-1, keepdims=True)
    run_s[0:1, :] = run + tot0 + tot1
    dest_ref[...] = jnp.where(lane == 0, pos0, jnp.where(lane == 1, pos1, 0.0)).astype(jnp.int32)


def _plan(route, counts, n_rows):
    rows = PLAN_TILE
    return pl.pallas_call(
        _plan_kernel,
        grid=(n_rows // rows,),
        in_specs=[pl.BlockSpec((rows, ROUTER_PAD), lambda i: (i, 0)),
                  pl.BlockSpec((8, ROUTER_PAD), lambda i: (0, 0))],
        out_specs=[pl.BlockSpec((rows, ROUTER_PAD), lambda i: (i, 0)),
                   pl.BlockSpec((8, ROUTER_PAD), lambda i: (0, 0))],
        out_shape=[jax.ShapeDtypeStruct((n_rows, ROUTER_PAD), jnp.int32),
                   jax.ShapeDtypeStruct((8, ROUTER_PAD), F32)],
        scratch_shapes=[pltpu.VMEM((8, ROUTER_PAD), F32)],
        compiler_params=_cparams(("arbitrary",)),
        name="dispatch_plan",
    )(route, counts)


def _dispatch_kernel(pend_ref, dest_ref, h_ref, xs_ref, zero_s, sem, zsem):
    rows = h_ref.shape[0]

    n_blk = xs_ref.shape[0] // MOE_BLOCK
    n_used = pend_ref[N_EXPERTS - 1] // MOE_BLOCK

    def zero_block(start):
        return pltpu.make_async_copy(zero_s, xs_ref.at[pl.ds(pl.multiple_of(start, MOE_BLOCK), MOE_BLOCK)], zsem)

    @pl.when(pl.program_id(0) == 0)
    def _():
        zero_s[...] = jnp.zeros_like(zero_s)
        for e in range(N_EXPERTS):
            @pl.when(pend_ref[e] > 0)
            def _():
                zero_block(pend_ref[e] - MOE_BLOCK).start()
        lax.fori_loop(n_used, n_blk, lambda j, c: (zero_block(j * MOE_BLOCK).start(), c)[1], 0)
        for e in range(N_EXPERTS):
            @pl.when(pend_ref[e] > 0)
            def _():
                zero_block(pend_ref[e] - MOE_BLOCK).wait()
        lax.fori_loop(n_used, n_blk, lambda j, c: (zero_block(j * MOE_BLOCK).wait(), c)[1], 0)

    def row_copy(r, d):
        return pltpu.make_async_copy(h_ref.at[pl.ds(r, 1)], xs_ref.at[pl.ds(d, 1)], sem)

    def issue(r, carry):
        for k in range(TOP_K):
            row_copy(r, dest_ref[0, 0, k * rows + r]).start(priority=k % 2)
        return carry

    lax.fori_loop(0, rows, issue, 0, unroll=8)
    for _ in range(TOP_K * rows):
        row_copy(0, 0).wait()


def _dispatch(pend, dest_tiles, h2, n_slots, n_tiles):
    d = h2.shape[1]
    return pl.pallas_call(
        _dispatch_kernel,
        grid_spec=pltpu.PrefetchScalarGridSpec(
            num_scalar_prefetch=1,
            grid=(n_tiles,),
            in_specs=[pl.BlockSpec((1, 1, TOP_K * ROW_TILE), lambda i, pe: (i, 0, 0), memory_space=pltpu.SMEM),
                      pl.BlockSpec((ROW_TILE, d), lambda i, pe: (i, 0))],
            out_specs=pl.BlockSpec(memory_space=pl.ANY),
            scratch_shapes=[pltpu.VMEM((MOE_BLOCK, d), F32), pltpu.SemaphoreType.DMA, pltpu.SemaphoreType.DMA],
        ),
        out_shape=jax.ShapeDtypeStruct((n_slots, d), F32),
        compiler_params=_cparams(("arbitrary",)),
        name="dispatch_rows",
    )(pend, dest_tiles, h2)


def _moe(h2, route, counts, n_tiles, layer, w_gate, w_up, w_down):
    t, d = h2.shape
    dest, pend_f = _plan(route, counts, t)
    pend = pend_f[0, :N_EXPERTS].astype(jnp.int32)
    n_blk = -(-(t * TOP_K) // MOE_BLOCK) + N_EXPERTS
    blk_start = jnp.arange(n_blk, dtype=jnp.int32) * MOE_BLOCK
    blk_exp = jnp.minimum(jnp.sum((pend[None, :] <= blk_start[:, None]).astype(jnp.int32), axis=1), N_EXPERTS - 1)
    n_used = (pend[N_EXPERTS - 1] // MOE_BLOCK).reshape(1)
    dest2 = dest[:, :TOP_K]
    dest_tiles = dest2.reshape(n_tiles, ROW_TILE, TOP_K).transpose(0, 2, 1).reshape(n_tiles, 1, TOP_K * ROW_TILE)
    xs = _dispatch(pend, dest_tiles, h2, n_blk * MOE_BLOCK, n_tiles)
    ys = _experts(blk_exp, n_used, xs, layer, w_gate, w_up, w_down)
    return ys[dest2[:, 0]], ys[dest2[:, 1]]


def _final_kernel(x_ref, y0_ref, y1_ref, rt_ref, mod_ref, g_ref, o_ref):
    x = _moe_residual(x_ref, y0_ref, y1_ref, rt_ref, mod_ref)
    o_ref[...] = x * lax.rsqrt(jnp.mean(x * x, axis=-1, keepdims=True) + EPS) * g_ref[...]


def _final(x_lat, moe_out, mod, g_final, n_tiles, tiles_per_seq, batch):
    d = D_MODEL
    row = lambda i: (i, 0)
    return pl.pallas_call(
        _final_kernel,
        grid=(n_tiles,),
        in_specs=[pl.BlockSpec((ROW_TILE, d), row), pl.BlockSpec((ROW_TILE, d), row),
                  pl.BlockSpec((ROW_TILE, d), row), pl.BlockSpec((ROW_TILE, ROUTER_PAD), row),
                  pl.BlockSpec((1, 6, d), lambda i: (jnp.minimum(i // tiles_per_seq, batch), 0, 0)),
                  pl.BlockSpec((1, d), lambda i: (0, 0))],
        out_specs=pl.BlockSpec((ROW_TILE, d), row),
        out_shape=jax.ShapeDtypeStruct((n_tiles * ROW_TILE, d), F32),
        compiler_params=_cparams(("parallel",)),
        name="final_norm",
    )(x_lat, *moe_out, mod, g_final)


def _rope_tables(seq):
    quarter = HEAD_DIM // 4
    freqs = ROPE_THETA ** (-jnp.arange(quarter, dtype=F32) / quarter)
    pos = jnp.arange(seq)
    ang_r = (pos // GRID_W).astype(F32)[:, None] * freqs[None, :]
    ang_c = (pos % GRID_W).astype(F32)[:, None] * freqs[None, :]
    cos = jnp.concatenate([jnp.cos(ang_r)] * 2 + [jnp.cos(ang_c)] * 2, axis=-1)
    sin = jnp.concatenate([-jnp.sin(ang_r), jnp.sin(ang_r), -jnp.sin(ang_c), jnp.sin(ang_c)], axis=-1)
    reps = 128 // HEAD_DIM
    cos = jnp.tile(cos, (1, reps))
    sin = jnp.tile(sin, (1, reps))
    cos = jnp.concatenate([cos, jnp.ones((ROW_TILE, 128), F32)], axis=0)
    sin = jnp.concatenate([sin, jnp.zeros((ROW_TILE, 128), F32)], axis=0)
    return cos, sin


def kernel(x, c, ctx, c_ctx, w_mod, b_mod, g_norm1, g_norm2, w_in, attn_sink, ssm_lam_re, ssm_lam_im,
           ssm_log_dt, ssm_b_re, ssm_b_im, ssm_c_re, ssm_c_im, ssm_d, w_glu, w_br_attn, w_br_ssm, w_out,
           w_router_group, w_router_expert, w_exp_gate, w_exp_up, w_exp_down, g_final):
    batch, seq, d = x.shape
    ctx_len = ctx.shape[1]
    depth = w_mod.shape[0]
    assert d == D_MODEL and batch * SSM_STATE == 128
    assert seq % ROW_TILE == 0 and ctx_len % ROW_TILE == 0 and (batch * seq) % ctx_len == 0
    assert batch + 1 <= MOD_ROWS
    t_lat, t_ctx = batch * seq, batch * ctx_len
    tiles_per_seq = seq // ROW_TILE
    n_lat_tiles = t_lat // ROW_TILE
    n_all_tiles = (t_lat + t_ctx) // ROW_TILE
    nb_per_seq = seq // ATTN_BLOCK
    ncc, ncl = ctx_len // SSM_CHUNK, seq // SSM_CHUNK

    c_rows = jnp.zeros((MOD_ROWS, d), F32).at[:batch].set(c).at[batch].set(c_ctx)
    mod_all = _modulation(c_rows, w_mod, b_mod).reshape(depth, MOD_ROWS, 6, d)
    cos_t, sin_t = _rope_tables(seq)
    ssm_w = _ssm_weights_all(ssm_lam_re, ssm_lam_im, ssm_log_dt, ssm_b_re, ssm_b_im, ssm_c_re, ssm_c_im, ssm_d, batch)
    x_parts = (x.reshape(t_lat, d), ctx.reshape(t_ctx, d))
    f_all = None
    for l in range(depth):
        ctx_out = l < depth - 1
        mod = mod_all[l]
        x_parts, (q, k, v, u, ga, gs) = _inproj(
            x_parts, f_all, mod_all[l - 1] if l > 0 else None, mod, g_norm1[l].reshape(1, d), cos_t, sin_t,
            w_in[l].astype(BF16), n_all_tiles, tiles_per_seq, n_lat_tiles, batch)
        sink = attn_sink[l].astype(F32)
        attn = _attention(sink, q, k, v, t_lat // ATTN_BLOCK, nb_per_seq, t_lat // ctx_len, ctx_len, True, 0)
        z, z_ctx = _ssm_mixer(u, l, *ssm_w, batch, seq, ctx_len)
        ctx_parts = None
        if ctx_out:
            attn_c = _attention(sink, q, k, v, t_ctx // ATTN_BLOCK, ctx_len // ATTN_BLOCK, t_lat // ctx_len,
                                ctx_len, False, t_lat // ATTN_BLOCK)
            ctx_parts = (attn_c, z_ctx, x_parts[1])
        w_r = jnp.zeros((d, ROUTER_PAD), F32)
        w_r = w_r.at[:, :N_EXPERT_GROUPS].set(w_router_group[l])
        w_r = w_r.at[:, N_EXPERT_GROUPS:N_EXPERT_GROUPS + N_EXPERTS].set(w_router_expert[l])
        wr_hi, wr_lo = _split_bf16(w_r)
        n_rows = t_lat + t_ctx if ctx_out else t_lat
        x_all, h2, route, counts = _merge(
            attn, z, x_parts[0], ctx_parts, ga, gs, mod, g_norm2[l].reshape(1, d), w_glu[l].astype(BF16),
            w_br_attn[l].astype(BF16), w_br_ssm[l].astype(BF16), w_out[l].astype(BF16),
            wr_hi, wr_lo, n_rows, t_lat, seq, batch)
        y0, y1 = _moe(h2, route, counts, n_rows // ROW_TILE, l, w_exp_gate, w_exp_up, w_exp_down)
        f_all = (y0, y1, route)
        x_parts = (x_all,)
    out = _final(x_all, f_all, mod_all[depth - 1], g_final.reshape(1, d), n_lat_tiles, tiles_per_seq, batch)
    return out.reshape(batch, seq, d)
```

```python
import functools
import math

import jax
import jax.numpy as jnp
from jax import lax
from jax.experimental import pallas as pl
from jax.experimental.pallas import tpu as pltpu

F32 = jnp.float32
BF16 = jnp.bfloat16

D_MODEL = 1024
GRID_W = 64
N_HEADS = 8
N_KV_HEADS = 2
HEAD_DIM = 64
Q_GROUP = N_HEADS // N_KV_HEADS
ATTN_BLOCK = 128
ATTN_STEP_BLOCKS = 2
ROPE_THETA = 10000.0
D_SSM = D_MODEL // 2
SSM_GROUP = 16
N_SSM_GROUPS = D_SSM // SSM_GROUP
SSM_STATE = 64
N_EXPERT_GROUPS = 4
EXPERTS_PER_GROUP = 8
N_EXPERTS = N_EXPERT_GROUPS * EXPERTS_PER_GROUP
TOP_K = 2
D_EXPERT = D_MODEL // 2
Q_W = N_HEADS * HEAD_DIM
KV_W = N_KV_HEADS * HEAD_DIM
O_K = Q_W
O_V = O_K + KV_W
O_U = O_V + KV_W
O_GA = O_U + D_SSM
O_GS = O_GA + D_MODEL
D_IN = O_GS + D_MODEL
EPS = 1e-6
NEG_INF = -1e30

ROW_TILE = 256
MERGE_TILE = 512
MERGE_SUB = 256
SSM_CHUNK = 16
SSM_CW = SSM_CHUNK * SSM_GROUP
SSM_SCAN_GROUPS = 8
SSM_LANE_GROUPS = 128 // SSM_GROUP
SSM_LANE_TILES = D_SSM // 128
MOE_BLOCK = 256
PLAN_TILE = 512
ROUTER_PAD = 128
MOD_ROWS = 8
VMEM_LIMIT = 48 * 1024 * 1024


def _cparams(sem):
    return pltpu.CompilerParams(dimension_semantics=sem, vmem_limit_bytes=VMEM_LIMIT)


def _dot(a, b):
    return jnp.dot(a, b, preferred_element_type=F32)


def _split_bf16(a):
    hi = a.astype(BF16)
    lo = (a - hi.astype(F32)).astype(BF16)
    return hi, lo


def _rms_mod(x, g, shift, scale):
    y = x * lax.rsqrt(jnp.mean(x * x, axis=-1, keepdims=True) + EPS) * g
    return y * (1.0 + scale) + shift


def _mod_kernel(c_ref, w_ref, b_ref, o_ref):
    c = c_ref[...]
    s_hi, s_lo = _split_bf16(c * jax.nn.sigmoid(c))
    w_hi, w_lo = _split_bf16(w_ref[0])
    o_ref[0] = _dot(s_hi, w_hi) + _dot(s_lo, w_hi) + _dot(s_hi, w_lo) + b_ref[0]


def _modulation(c_rows, w_mod, b_mod):
    depth, d, n = w_mod.shape
    nb = n // 4
    return pl.pallas_call(
        _mod_kernel,
        grid=(depth, n // nb),
        in_specs=[
            pl.BlockSpec((MOD_ROWS, d), lambda l, j: (0, 0)),
            pl.BlockSpec((1, d, nb), lambda l, j: (l, 0, j)),
            pl.BlockSpec((1, 1, nb), lambda l, j: (l, 0, j)),
        ],
        out_specs=pl.BlockSpec((1, MOD_ROWS, nb), lambda l, j: (l, 0, j)),
        out_shape=jax.ShapeDtypeStruct((depth, MOD_ROWS, n), F32),
        compiler_params=_cparams(("arbitrary", "arbitrary")),
        name="modulation",
    )(c_rows, w_mod, b_mod.reshape(depth, 1, n))


def _moe_residual(x_ref, y0_ref, y1_ref, rt_ref, mod_ref):
    rt = rt_ref[...]
    f = rt[:, 2:3] * y0_ref[...].astype(F32) + rt[:, 3:4] * y1_ref[...].astype(F32)
    return x_ref[...] + mod_ref[0, 5:6, :] * f


def _inproj_kernel(has_f, n_lat_tiles, *refs):
    if has_f:
        (x_ref, y0_ref, y1_ref, rt_ref, modp_ref, mod_ref, g_ref, cos_ref, sin_ref, w_ref,
         xo_ref, q_ref, k_ref, v_ref, u_ref, ga_ref, gs_ref) = refs
        x = _moe_residual(x_ref, y0_ref, y1_ref, rt_ref, modp_ref)
        xo_ref[...] = x
    else:
        (xl_ref, xc_ref, mod_ref, g_ref, cos_ref, sin_ref, w_ref,
         q_ref, k_ref, v_ref, u_ref, ga_ref, gs_ref) = refs
        x = jnp.where(pl.program_id(0) >= n_lat_tiles, xc_ref[...], xl_ref[...])
    m = mod_ref[0]
    h = _rms_mod(x, g_ref[...], m[0:1], m[1:2]).astype(BF16)
    cos = cos_ref[...]
    sin = sin_ref[...]
    lane = lax.broadcasted_iota(jnp.int32, cos.shape, 1)
    first = (lane % (HEAD_DIM // 2)) < (HEAD_DIM // 4)

    def rope(t):
        sw = jnp.where(first, pltpu.roll(t, 128 - HEAD_DIM // 4, 1), pltpu.roll(t, HEAD_DIM // 4, 1))
        return t * cos + sw * sin

    def proj(lo, hi):
        return _dot(h, w_ref[:, lo:hi])

    q = proj(0, O_K)
    for j in range(Q_W // 128):
        q_ref[:, 128 * j:128 * (j + 1)] = (rope(q[:, 128 * j:128 * (j + 1)]) * HEAD_DIM ** -0.5).astype(BF16)
    kv = proj(O_K, O_U)
    k_ref[...] = rope(kv[:, :KV_W]).astype(BF16)
    v_ref[...] = kv[:, KV_W:].astype(BF16)
    u = proj(O_U, O_GA)
    for j in range(SSM_LANE_TILES):
        u_ref[j] = u[:, 128 * j:128 * (j + 1)]
    ga_ref[...] = jax.nn.sigmoid(proj(O_GA, O_GS)).astype(BF16)
    gs_ref[...] = jax.nn.sigmoid(proj(O_GS, D_IN)).astype(BF16)


def _inproj(x_parts, moe_out, mod_prev, mod, g1, cos_t, sin_t, w_in, n_tiles, tiles_per_seq, n_lat_tiles, batch):
    d = D_MODEL
    has_f = moe_out is not None
    row = lambda i: (i, 0)
    modi = lambda i: (jnp.minimum(i // tiles_per_seq, batch), 0, 0)
    const = lambda i: (0, 0)
    ropei = lambda i: (jnp.where(i < n_lat_tiles, i % tiles_per_seq, tiles_per_seq), 0)
    if has_f:
        in_specs = [pl.BlockSpec((ROW_TILE, d), row), pl.BlockSpec((ROW_TILE, d), row),
                    pl.BlockSpec((ROW_TILE, d), row), pl.BlockSpec((ROW_TILE, ROUTER_PAD), row),
                    pl.BlockSpec((1, 6, d), modi)]
        args = [*x_parts, *moe_out, mod_prev]
    else:
        in_specs = [pl.BlockSpec((ROW_TILE, d), lambda i: (jnp.minimum(i, n_lat_tiles - 1), 0)),
                    pl.BlockSpec((ROW_TILE, d), lambda i: (jnp.maximum(i - n_lat_tiles, 0), 0))]
        args = list(x_parts)
    in_specs += [
        pl.BlockSpec((1, 6, d), modi),
        pl.BlockSpec((1, d), const),
        pl.BlockSpec((ROW_TILE, 128), ropei),
        pl.BlockSpec((ROW_TILE, 128), ropei),
        pl.BlockSpec((d, D_IN), const),
    ]
    args += [mod, g1, cos_t, sin_t, w_in]
    widths = [Q_W, KV_W, KV_W, D_SSM, D_MODEL, D_MODEL]
    out_specs = [pl.BlockSpec((ROW_TILE, w), row) for w in widths]
    out_shape = [jax.ShapeDtypeStruct((n_tiles * ROW_TILE, w), BF16) for w in widths]
    out_specs[3] = pl.BlockSpec((SSM_LANE_TILES, ROW_TILE, 128), lambda i: (0, i, 0))
    out_shape[3] = jax.ShapeDtypeStruct((SSM_LANE_TILES, n_tiles * ROW_TILE, 128), F32)
    if has_f:
        out_specs = [pl.BlockSpec((ROW_TILE, d), row)] + out_specs
        out_shape = [jax.ShapeDtypeStruct((n_tiles * ROW_TILE, d), F32)] + out_shape
    outs = pl.pallas_call(
        functools.partial(_inproj_kernel, has_f, n_lat_tiles),
        grid=(n_tiles,),
        in_specs=in_specs,
        out_specs=out_specs,
        out_shape=out_shape,
        compiler_params=_cparams(("parallel",)),
        name="inproj",
    )(*args)
    if has_f:
        return (outs[0],), outs[1:]
    return x_parts, outs


def _attn_block(sink_ref, q, k_tiles, v_tiles, biases):
    blk = q.shape[0]
    outs = []
    for g in range(N_KV_HEADS):
        gs = slice(g * HEAD_DIM, (g + 1) * HEAD_DIM)
        k_all = jnp.concatenate([t[:, gs] for t in k_tiles], axis=0)
        v_all = jnp.concatenate([t[:, gs] for t in v_tiles], axis=0)
        v_ext = jnp.concatenate([v_all, jnp.ones_like(v_all)], axis=1)
        qg = jnp.concatenate(
            [q[:, (g * Q_GROUP + h) * HEAD_DIM:(g * Q_GROUP + h + 1) * HEAD_DIM] for h in range(Q_GROUP)], axis=0)
        s_all = lax.dot_general(qg, k_all, (((1,), (1,)), ((), ())), preferred_element_type=F32)
        for h in range(Q_GROUP):
            s = s_all[h * blk:(h + 1) * blk]
            tiles, col = [], 0
            for kt, bias in zip(k_tiles, biases):
                t = s[:, col:col + kt.shape[0]]
                tiles.append(t if bias is None else t + bias)
                col += kt.shape[0]
            mx = tiles[0]
            for t in tiles[1:]:
                for c0 in range(0, t.shape[1], blk):
                    mx = jnp.maximum(mx, t[:, c0:c0 + blk])
            sink = sink_ref[g * Q_GROUP + h]
            m = jnp.maximum(jnp.max(mx, axis=-1, keepdims=True), sink)
            p = jnp.exp(jnp.concatenate([(t - m).astype(BF16) for t in tiles], axis=1))
            o_ext = _dot(p, v_ext)
            denom = o_ext[:, HEAD_DIM:HEAD_DIM + 1] + jnp.exp(sink - m)
            outs.append((o_ext[:, :HEAD_DIM] * (1.0 / denom)).astype(BF16))
    return jnp.concatenate(outs, axis=1)


def _attn_kernel(band, nb_per_seq, sink_ref, *refs):
    blk = ATTN_BLOCK
    if not band:
        q_ref, kx_ref, vx_ref, o_ref = refs
        for a in range(q_ref.shape[0] // blk):
            rows = slice(a * blk, (a + 1) * blk)
            o_ref[rows, :] = _attn_block(sink_ref, q_ref[rows, :], [kx_ref[...]], [vx_ref[...]], [None])
        return
    q_ref, kp_ref, kc_ref, kn_ref, vp_ref, vc_ref, vn_ref, kx_ref, vx_ref, o_ref = refs
    n_sub = q_ref.shape[0] // blk
    j0 = (pl.program_id(0) * n_sub) % nb_per_seq
    r = lax.broadcasted_iota(jnp.int32, (blk, blk), 0)
    c = lax.broadcasted_iota(jnp.int32, (blk, blk), 1)
    k_blocks = [kp_ref[...]] + [kc_ref[a * blk:(a + 1) * blk, :] for a in range(n_sub)] + [kn_ref[...]]
    v_blocks = [vp_ref[...]] + [vc_ref[a * blk:(a + 1) * blk, :] for a in range(n_sub)] + [vn_ref[...]]
    for a in range(n_sub):
        edge_p = jnp.where(j0 + a > 0, 0.0, NEG_INF).astype(F32)
        edge_n = jnp.where(j0 + a < nb_per_seq - 1, 0.0, NEG_INF).astype(F32)
        bias_p = jnp.where(c >= r, edge_p, NEG_INF).astype(F32)
        bias_n = jnp.where(c <= r, edge_n, NEG_INF).astype(F32)
        rows = slice(a * blk, (a + 1) * blk)
        o_ref[rows, :] = _attn_block(sink_ref, q_ref[rows, :], k_blocks[a:a + 3] + [kx_ref[...]],
                                     v_blocks[a:a + 3] + [vx_ref[...]], [bias_p, None, bias_n, None])


def _attention(sink, q, k, v, n_blocks, nb_per_seq, kx_block0, ctx_len, band, q_block0):
    blk = ATTN_BLOCK
    n_sub = ATTN_STEP_BLOCKS
    assert n_blocks % n_sub == 0 and nb_per_seq % n_sub == 0 and q_block0 % n_sub == 0
    step = n_sub * blk
    n_steps = n_blocks // n_sub
    qi = lambda i, s: (q_block0 // n_sub + i, 0)
    cur = lambda i, s: (i, 0)
    prv = lambda i, s: (jnp.maximum(i * n_sub - 1, 0), 0)
    nxt = lambda i, s: (jnp.minimum((i + 1) * n_sub, n_blocks - 1), 0)
    kxi = lambda i, s: (kx_block0 + (i * n_sub) // nb_per_seq, 0)
    kspec = lambda f: pl.BlockSpec((blk, KV_W), f)
    cspec = pl.BlockSpec((step, KV_W), cur)
    xspec = pl.BlockSpec((ctx_len, KV_W), kxi)
    if band:
        in_specs = [pl.BlockSpec((step, Q_W), qi), kspec(prv), cspec, kspec(nxt),
                    kspec(prv), cspec, kspec(nxt), xspec, xspec]
        args = (q, k, k, k, v, v, v, k, v)
    else:
        in_specs = [pl.BlockSpec((step, Q_W), qi), xspec, xspec]
        args = (q, k, v)
    return pl.pallas_call(
        functools.partial(_attn_kernel, band, nb_per_seq),
        grid_spec=pltpu.PrefetchScalarGridSpec(
            num_scalar_prefetch=1,
            grid=(n_steps,),
            in_specs=in_specs,
            out_specs=pl.BlockSpec((step, Q_W), lambda i, s: (i, 0)),
        ),
        out_shape=jax.ShapeDtypeStruct((n_blocks * blk, Q_W), BF16),
        compiler_params=_cparams(("parallel",)),
        name="band_attention" if band else "context_attention",
    )(sink, *args)


def _dot_f32(a, b_t):
    a_hi, a_lo = _split_bf16(a)
    b_hi, b_lo = _split_bf16(b_t)
    dn = (((1,), (1,)), ((), ()))
    dg = functools.partial(lax.dot_general, dimension_numbers=dn, preferred_element_type=F32)
    return dg(a_hi, b_hi) + dg(a_lo, b_hi) + dg(a_hi, b_lo)


def _ssm_weight_kernel(batch, lam_r_ref, bt_ref, c_ref, d_ref, w1_ref, abar_ref, t_ref, wo_ref):
    lc, mm, p = SSM_CHUNK, SSM_GROUP, SSM_STATE
    up_r = lax.broadcasted_iota(jnp.int32, (lc, 1), 0).astype(F32)
    lane = lax.broadcasted_iota(jnp.int32, (mm, lc * mm), 1)
    row = lax.broadcasted_iota(jnp.int32, (mm, lc * mm), 0)
    planes_in, planes_out, abar_rows, kt = [], [], [], []
    for d in range(2):
        lr, li, dt = lam_r_ref[0, d, 0:1, :], lam_r_ref[0, d, 1:2, :], jnp.exp(lam_r_ref[0, d, 2:3, :])

        def powers(expo):
            mag = jnp.exp(lr * dt * expo)
            return mag * jnp.cos(li * dt * expo), mag * jnp.sin(li * dt * expo)

        a_re, a_im = powers(jnp.ones((1, 1), F32))
        den = lr * lr + li * li
        nr = a_re - 1.0
        f_re = (nr * lr + a_im * li) / den
        f_im = (a_im * lr - nr * li) / den
        bt_re, bt_im = bt_ref[0, d, 0], bt_ref[0, d, 1]
        bbt_re = f_re * bt_re - f_im * bt_im
        bbt_im = f_re * bt_im + f_im * bt_re
        c_re, c_im = c_ref[0, d, 0], c_ref[0, d, 1]

        pr, pi = powers(lc - 1.0 - up_r if d == 0 else up_r)
        planes_in.append(jnp.concatenate(
            [pr[s:s + 1] * bbt_re - pi[s:s + 1] * bbt_im for s in range(lc)], axis=0))
        planes_in.append(jnp.concatenate(
            [pr[s:s + 1] * bbt_im + pi[s:s + 1] * bbt_re for s in range(lc)], axis=0))

        qr, qi = powers(up_r + 1.0 if d == 0 else lc - up_r)
        planes_out.append(jnp.concatenate(
            [c_re * qr[t:t + 1] - c_im * qi[t:t + 1] for t in range(lc)], axis=0).T)
        planes_out.append(jnp.concatenate(
            [-c_re * qi[t:t + 1] - c_im * qr[t:t + 1] for t in range(lc)], axis=0).T)

        kr, ki = powers(up_r if d == 0 else lc - 1.0 - up_r)
        cp_re = jnp.concatenate([c_re * kr[j:j + 1] - c_im * ki[j:j + 1] for j in range(lc)], axis=0)
        cp_im = jnp.concatenate([c_re * ki[j:j + 1] + c_im * kr[j:j + 1] for j in range(lc)], axis=0)
        kt.append(_dot_f32(jnp.concatenate([bbt_re, bbt_im], axis=1), jnp.concatenate([cp_re, -cp_im], axis=1)))

        e_re, e_im = powers(jnp.full((1, 1), float(lc), F32))
        abar_rows += [jnp.concatenate([e_re] * batch, axis=1), jnp.concatenate([e_im] * batch, axis=1)]

    dvec = d_ref[0]
    blocks = []
    for s in range(lc):
        fwd = kt[0] if s == 0 else pltpu.roll(kt[0], mm * s, 1)
        bwd = kt[1] if s == lc - 1 else pltpu.roll(kt[1], mm * (s + 1), 1)
        blk = jnp.where(lane >= mm * s, fwd, 0.0) + jnp.where(lane < mm * (s + 1), bwd, 0.0)
        blocks.append(blk + jnp.where(lane == mm * s + row, dvec, 0.0))
    t_ref[0] = jnp.concatenate(blocks, axis=0).astype(BF16)
    abar_ref[0] = jnp.concatenate(abar_rows, axis=0)
    zero_in = jnp.zeros((lc * mm, p), F32)
    zero_out = jnp.zeros((p, lc * mm), F32)
    for b in range(batch):
        w1_ref[0, b] = jnp.concatenate(
            [pl_ if bb == b else zero_in for pl_ in planes_in for bb in range(batch)], axis=1).astype(BF16)
        wo_ref[0, b] = jnp.concatenate(
            [pl_ if bb == b else zero_out for pl_ in planes_out for bb in range(batch)], axis=0).astype(BF16)


def _ssm_weights_all(lam_re, lam_im, log_dt, b_re, b_im, c_re, c_im, d_skip, batch):
    depth = lam_re.shape[0]
    g, p, mm, lc = N_SSM_GROUPS, SSM_STATE, SSM_GROUP, SSM_CHUNK
    n = depth * g
    cw, sw = lc * mm, 4 * batch * p

    def per_group(a):
        return jnp.moveaxis(a.astype(F32), 2, 1).reshape((n, 2) + a.shape[3:])

    lam = jnp.stack([per_group(lam_re), per_group(lam_im),
                     jnp.broadcast_to(per_group(log_dt)[..., None], (n, 2, p))], axis=2)
    bt = jnp.stack([per_group(b_re), per_group(b_im)], axis=2).swapaxes(-1, -2)
    c = jnp.stack([per_group(c_re), per_group(c_im)], axis=2)
    d_t = jnp.tile(d_skip.astype(F32).reshape(n, 1, mm), (1, 1, lc))
    full = lambda *shape: pl.BlockSpec((1,) + shape, lambda i: (i,) + (0,) * len(shape))
    w1, abar, tmat, wout = pl.pallas_call(
        functools.partial(_ssm_weight_kernel, batch),
        grid=(n,),
        in_specs=[full(2, 3, p), full(2, 2, mm, p), full(2, 2, mm, p), full(1, cw)],
        out_specs=[full(batch, cw, sw), full(4, batch * p), full(cw, cw), full(batch, sw, cw)],
        out_shape=[jax.ShapeDtypeStruct((n, batch, cw, sw), BF16), jax.ShapeDtypeStruct((n, 4, batch * p), F32),
                   jax.ShapeDtypeStruct((n, cw, cw), BF16), jax.ShapeDtypeStruct((n, batch, sw, cw), BF16)],
        compiler_params=_cparams(("parallel",)),
        name="ssm_weights",
    )(lam, bt, c, d_t)
    abar = abar.reshape(depth, g, 4, batch * p).transpose(0, 2, 1, 3)
    return (w1.reshape(depth, g, batch, cw, sw), abar, tmat.reshape(depth, g, cw, cw),
            wout.reshape(depth, g, batch, sw, cw))


def _ssm_p1_kernel(ul_ref, uc_ref, w_ref, v_ref, *s_refs):
    b = pl.program_id(1)
    ncc, ncl = uc_ref.shape[0] // SSM_CHUNK, ul_ref.shape[0] // SSM_CHUNK
    nc = ncc + ncl
    pw = s_refs[0].shape[1]
    xs = [jnp.concatenate([uc_ref[pl.ds(s, ncc, stride=SSM_CHUNK), :], ul_ref[pl.ds(s, ncl, stride=SSM_CHUNK), :]],
                          axis=0) for s in range(SSM_CHUNK)]
    for j in range(SSM_LANE_GROUPS):
        v = jnp.concatenate([x[:, j * SSM_GROUP:(j + 1) * SSM_GROUP] for x in xs], axis=1).astype(BF16)
        v_ref[j] = v
        acc = _dot(v, w_ref[j])
        for k, s_ref in enumerate(s_refs):
            part = acc[:, k * pw:(k + 1) * pw]
            rows = slice(j * nc, (j + 1) * nc)

            @pl.when(b == 0)
            def _():
                s_ref[rows, :] = part

            @pl.when(b > 0)
            def _():
                s_ref[rows, :] = s_ref[rows, :] + part


def _ssm_scan_kernel(ncc, ncl, a_ref, sfr_ref, sfi_ref, sbr_ref, sbi_ref, xfr_ref, xfi_ref, xbr_ref, xbi_ref):
    nc = ncc + ncl
    gb = a_ref.shape[1]
    pw = a_ref.shape[2]
    afr, afi, abr, abi = a_ref[0], a_ref[1], a_ref[2], a_ref[3]

    def rows(r):
        return pl.ds(r, gb, stride=nc)

    def step(rf, rb, carry):
        xfr, xfi, xbr, xbi = carry
        xfr_ref[rows(rf), :] = xfr
        xfi_ref[rows(rf), :] = xfi
        xbr_ref[rows(rb), :] = xbr
        xbi_ref[rows(rb), :] = xbi
        sfr = sfr_ref[rows(rf), :]
        sfi = sfi_ref[rows(rf), :]
        sbr = sbr_ref[rows(rb), :]
        sbi = sbi_ref[rows(rb), :]
        return (afr * xfr - afi * xfi + sfr, afr * xfi + afi * xfr + sfi,
                abr * xbr - abi * xbi + sbr, abr * xbi + abi * xbr + sbi)

    zero = jnp.zeros((gb, pw), F32)
    carry = lax.fori_loop(0, ncc, lambda t, c: step(t, ncc - 1 - t, c), (zero, zero, zero, zero))
    lax.fori_loop(0, ncl, lambda t, c: step(ncc + t, nc - 1 - t, c), carry)


def _ssm_p3_kernel(v_ref, xfr_ref, xfi_ref, xbr_ref, xbi_ref, t_ref, wo_ref, zl_ref, zc_ref):
    ncc, ncl = zc_ref.shape[0] // SSM_CHUNK, zl_ref.shape[0] // SSM_CHUNK
    nc = ncc + ncl
    ys = []
    for j in range(SSM_LANE_GROUPS):
        rows = slice(j * nc, (j + 1) * nc)
        xin = jnp.concatenate([xfr_ref[rows, :], xfi_ref[rows, :], xbr_ref[rows, :], xbi_ref[rows, :]],
                              axis=1).astype(BF16)
        y = _dot(v_ref[j], t_ref[j]) + _dot(xin, wo_ref[j])
        ys.append(jax.nn.gelu(y, approximate=True))
    for t in range(SSM_CHUNK):
        zt = jnp.concatenate([y[:, t * SSM_GROUP:(t + 1) * SSM_GROUP] for y in ys], axis=1)
        zc_ref[pl.ds(t, ncc, stride=SSM_CHUNK), :] = zt[:ncc]
        zl_ref[pl.ds(t, ncl, stride=SSM_CHUNK), :] = zt[ncc:]


def _ssm_mixer(u4, layer, w1, abar, tmat, wout, batch, seq, ctx_len):
    n_lt = u4.shape[0]
    g = N_SSM_GROUPS
    lg = SSM_LANE_GROUPS
    ncc, ncl = ctx_len // SSM_CHUNK, seq // SSM_CHUNK
    nc = ncc + ncl
    cw = SSM_CW
    sw = w1.shape[-1]
    pw = sw // 4
    ctx_blk0 = (batch * seq) // ctx_len
    plane_shape = [jax.ShapeDtypeStruct((g * nc, pw), F32)] * 4
    lat_spec = pl.BlockSpec((None, seq, 128), lambda k, b: (k, b, 0))
    ctx_spec = pl.BlockSpec((None, ctx_len, 128), lambda k, b: (k, ctx_blk0 + b, 0))
    v_spec = pl.BlockSpec((lg, None, nc, cw), lambda k, b: (k, b, 0, 0))
    plane_spec = pl.BlockSpec((lg * nc, pw), lambda k, b: (k, 0))
    v_chunks, *s_planes = pl.pallas_call(
        _ssm_p1_kernel,
        grid=(n_lt, batch),
        in_specs=[lat_spec, ctx_spec,
                  pl.BlockSpec((None, lg, None, cw, sw), lambda k, b: (layer, k, b, 0, 0))],
        out_specs=[v_spec] + [plane_spec] * 4,
        out_shape=[jax.ShapeDtypeStruct((g, batch, nc, cw), BF16)] + plane_shape,
        compiler_params=_cparams(("parallel", "arbitrary")),
        name="ssm_chunk_states",
    )(u4, u4, w1)
    gb = SSM_SCAN_GROUPS
    x_planes = pl.pallas_call(
        functools.partial(_ssm_scan_kernel, ncc, ncl),
        grid=(g // gb,),
        in_specs=[pl.BlockSpec((None, 4, gb, pw), lambda i: (layer, 0, i, 0))]
        + [pl.BlockSpec((gb * nc, pw), lambda i: (i, 0))] * 4,
        out_specs=[pl.BlockSpec((gb * nc, pw), lambda i: (i, 0))] * 4,
        out_shape=plane_shape,
        compiler_params=_cparams(("parallel",)),
        name="ssm_chunk_scan",
    )(abar, *s_planes)
    return pl.pallas_call(
        _ssm_p3_kernel,
        grid=(n_lt, batch),
        in_specs=[v_spec] + [plane_spec] * 4
        + [pl.BlockSpec((None, lg, cw, cw), lambda k, b: (layer, k, 0, 0)),
           pl.BlockSpec((None, lg, None, sw, cw), lambda k, b: (layer, k, b, 0, 0))],
        out_specs=[pl.BlockSpec((None, seq, 128), lambda k, b: (k, b, 0)),
                   pl.BlockSpec((None, ctx_len, 128), lambda k, b: (k, b, 0))],
        out_shape=[jax.ShapeDtypeStruct((n_lt, batch * seq, 128), F32),
                   jax.ShapeDtypeStruct((n_lt, batch * ctx_len, 128), F32)],
        compiler_params=_cparams(("parallel", "arbitrary")),
        name="ssm_chunk_outputs",
    )(v_chunks, *x_planes, tmat, wout)


def _merge_kernel(n_lat_tiles, *refs):
    if n_lat_tiles is None:
        attn_ref, z_ref, x_ref = refs[:3]
        refs = refs[3:]
    else:
        attn_ref, z_ref, x_ref, attn_c_ref, z_c_ref, x_c_ref = refs[:6]
        refs = refs[6:]
        is_ctx = pl.program_id(0) >= n_lat_tiles
    (ga_ref, gs_ref, mod_ref, g2_ref, wglu_ref, wba_ref, wbs_ref, wout_ref, wrh_ref, wrl_ref,
     xo_ref, h2_ref, rt_ref, cnt_ref) = refs
    m = mod_ref[0]
    tile_counts = None
    for r in range(MERGE_TILE // MERGE_SUB):
        rows = slice(r * MERGE_SUB, (r + 1) * MERGE_SUB)
        zf = jnp.concatenate([z_ref[j, rows, :] for j in range(SSM_LANE_TILES)], axis=1)
        attn = attn_ref[rows, :]
        x_in = x_ref[rows, :]
        if n_lat_tiles is not None:
            zf = jnp.where(is_ctx, jnp.concatenate([z_c_ref[j, rows, :] for j in range(SSM_LANE_TILES)], axis=1), zf)
            attn = jnp.where(is_ctx, attn_c_ref[rows, :], attn)
            x_in = jnp.where(is_ctx, x_c_ref[rows, :], x_in)
        z = zf.astype(BF16)
        glu = (z.astype(F32) * jax.nn.sigmoid(_dot(z, wglu_ref[...]))).astype(BF16)
        mix = (ga_ref[rows, :].astype(F32) * _dot(attn, wba_ref[...])
               + gs_ref[rows, :].astype(F32) * _dot(glu, wbs_ref[...])).astype(BF16)
        x = x_in + m[2:3] * _dot(mix, wout_ref[...])
        xo_ref[rows, :] = x
        h2 = _rms_mod(x, g2_ref[...], m[3:4], m[4:5])
        hi, lo = _split_bf16(h2)
        h2_ref[rows, :] = hi.astype(F32)
        logits = _dot(hi, wrh_ref[...]) + _dot(lo, wrh_ref[...]) + _dot(hi, wrl_ref[...])
        rt = _route(logits)
        rt_ref[rows, :] = rt
        oh0, oh1 = _expert_onehots(rt)
        part = jnp.sum(oh0 + oh1, axis=0, keepdims=True)
        tile_counts = part if tile_counts is None else tile_counts + part

    @pl.when(pl.program_id(0) == 0)
    def _():
        cnt_ref[...] = jnp.zeros_like(cnt_ref)

    cnt_ref[0:1, :] = cnt_ref[0:1, :] + tile_counts


def _route(lg):
    ng, epg = N_EXPERT_GROUPS, EXPERTS_PER_GROUP
    lane_i = lax.broadcasted_iota(jnp.int32, lg.shape, 1)
    lane = lane_i.astype(F32)
    big = float(ROUTER_PAD)

    def rmax(mask_val):
        return jnp.max(mask_val, axis=-1, keepdims=True)

    def first_lane(mask, val, mx):
        return jnp.min(jnp.where(mask, jnp.where(val == mx, lane, big), big), axis=-1, keepdims=True)

    gmask = lane_i < ng
    lgm = jnp.where(gmask, lg, NEG_INF)
    mg = rmax(lgm)
    g_prob = 1.0 / jnp.sum(jnp.exp(lgm - mg), axis=-1, keepdims=True)
    g_idx = first_lane(gmask, lg, mg)
    egroup = jnp.floor((lane - float(ng)) * (1.0 / epg))
    emask = egroup == g_idx
    l1 = jnp.where(emask, lg, NEG_INF)
    m1 = rmax(l1)
    i1 = first_lane(emask, lg, m1)
    l2 = jnp.where(lane == i1, NEG_INF, l1)
    m2 = rmax(l2)
    i2 = jnp.min(jnp.where(l2 == m2, jnp.where(emask, lane, big), big), axis=-1, keepdims=True)
    r = jnp.exp(m2 - m1)
    w1 = g_prob / (1.0 + r)
    w2 = w1 * r
    return jnp.where(lane_i == 0, i1 - float(ng),
                     jnp.where(lane_i == 1, i2 - float(ng),
                               jnp.where(lane_i == 2, w1, jnp.where(lane_i == 3, w2, 0.0))))


def _merge(attn, z, x, ctx_parts, ga, gs, mod, g2, w_glu, w_ba, w_bs, w_out, wr_hi, wr_lo, n_rows, t_lat, seq, batch):
    d = D_MODEL
    tile = MERGE_TILE
    assert n_rows % tile == 0 and t_lat % tile == 0 and seq % tile == 0
    n_tiles = n_rows // tile
    n_lat = t_lat // tile
    row = lambda i: (i, 0)
    lat = lambda i: (jnp.minimum(i, n_lat - 1), 0)
    cxt = lambda i: (jnp.maximum(i - n_lat, 0), 0)
    const = lambda i: (0, 0)
    modi = lambda i: (jnp.minimum(i // (seq // tile), batch), 0, 0)
    in_specs = [pl.BlockSpec((tile, Q_W), lat),
                pl.BlockSpec((SSM_LANE_TILES, tile, 128), lambda i: (0,) + lat(i)),
                pl.BlockSpec((tile, d), lat)]
    args = [attn, z, x]
    if ctx_parts is not None:
        in_specs += [pl.BlockSpec((tile, Q_W), cxt),
                     pl.BlockSpec((SSM_LANE_TILES, tile, 128), lambda i: (0,) + cxt(i)),
                     pl.BlockSpec((tile, d), cxt)]
        args += list(ctx_parts)
    in_specs += [
        pl.BlockSpec((tile, d), row),
        pl.BlockSpec((tile, d), row),
        pl.BlockSpec((1, 6, d), modi),
        pl.BlockSpec((1, d), const),
        pl.BlockSpec((D_SSM, D_SSM), const),
        pl.BlockSpec((Q_W, d), const),
        pl.BlockSpec((D_SSM, d), const),
        pl.BlockSpec((d, d), const),
        pl.BlockSpec((d, ROUTER_PAD), const),
        pl.BlockSpec((d, ROUTER_PAD), const),
    ]
    args += [ga, gs, mod, g2, w_glu, w_ba, w_bs, w_out, wr_hi, wr_lo]
    return pl.pallas_call(
        functools.partial(_merge_kernel, n_lat if ctx_parts is not None else None),
        grid=(n_tiles,),
        in_specs=in_specs,
        out_specs=[pl.BlockSpec((tile, d), row), pl.BlockSpec((tile, d), row),
                   pl.BlockSpec((tile, ROUTER_PAD), row), pl.BlockSpec((8, ROUTER_PAD), const)],
        out_shape=[jax.ShapeDtypeStruct((n_rows, d), F32), jax.ShapeDtypeStruct((n_rows, d), F32),
                   jax.ShapeDtypeStruct((n_rows, ROUTER_PAD), F32), jax.ShapeDtypeStruct((8, ROUTER_PAD), F32)],
        compiler_params=_cparams(("arbitrary",)),
        name="merge_router",
    )(*args)


def _expert_kernel(layer, be_ref, nu_ref, nx_ref, x_ref, wg_hbm, wu_hbm, wd_hbm, y_ref,
                   wg_f, wu_f, wd_f, wg_s, wu_s, wd_s, slot_s, sems):
    i = pl.program_id(0)
    e = be_ref[i]
    changed = jnp.logical_or(i == 0, e != be_ref[jnp.maximum(i - 1, 0)])

    def weight_copies(expert, slot):
        return [pltpu.make_async_copy(src.at[layer, expert], dst.at[slot], sems.at[slot, j])
                for j, (src, dst) in enumerate(((wg_hbm, wg_f), (wu_hbm, wu_f), (wd_hbm, wd_f)))]

    @pl.when(i == 0)
    def _():
        slot_s[0] = 0
        for cp in weight_copies(e, 0):
            cp.start()

    @pl.when(changed)
    def _():
        slot = slot_s[0]
        for cp in weight_copies(e, slot):
            cp.wait()
        nxt = nx_ref[e]

        @pl.when(nxt >= 0)
        def _():
            for cp in weight_copies(nxt, 1 - slot):
                cp.start()

        wg_s[...] = wg_f[slot].astype(BF16)
        wu_s[...] = wu_f[slot].astype(BF16)
        wd_s[...] = wd_f[slot].astype(BF16)
        slot_s[0] = 1 - slot

    @pl.when(i < nu_ref[0])
    def _():
        x = x_ref[...].astype(BF16)
        gate = _dot(x, wg_s[...])
        up = _dot(x, wu_s[...])
        hid = (gate * jax.nn.sigmoid(gate) * up).astype(BF16)
        y_ref[...] = _dot(hid, wd_s[...]).astype(y_ref.dtype)

    @pl.when(i >= nu_ref[0])
    def _():
        y_ref[...] = jnp.zeros_like(y_ref)


def _experts(blk_exp, n_used, next_exp, xs, layer, w_gate, w_up, w_down):
    n_slots, d = xs.shape
    n_blk = n_slots // MOE_BLOCK
    de = w_gate.shape[-1]
    return pl.pallas_call(
        functools.partial(_expert_kernel, layer),
        grid_spec=pltpu.PrefetchScalarGridSpec(
            num_scalar_prefetch=3,
            grid=(n_blk,),
            in_specs=[
                pl.BlockSpec((MOE_BLOCK, d), lambda i, be, nu, nx: (jnp.minimum(i, nu[0] - 1), 0)),
                pl.BlockSpec(memory_space=pl.ANY),
                pl.BlockSpec(memory_space=pl.ANY),
                pl.BlockSpec(memory_space=pl.ANY),
            ],
            out_specs=pl.BlockSpec((MOE_BLOCK, d), lambda i, be, nu, nx: (i, 0)),
            scratch_shapes=[pltpu.VMEM((2, d, de), F32), pltpu.VMEM((2, d, de), F32), pltpu.VMEM((2, de, d), F32),
                            pltpu.VMEM((d, de), BF16), pltpu.VMEM((d, de), BF16), pltpu.VMEM((de, d), BF16),
                            pltpu.SMEM((1,), jnp.int32), pltpu.SemaphoreType.DMA((2, 3))],
        ),
        out_shape=jax.ShapeDtypeStruct((n_slots, d), BF16),
        compiler_params=_cparams(("arbitrary",)),
        name="expert_mlp",
    )(blk_exp, n_used, next_exp, xs, w_gate, w_up, w_down)


def _expert_onehots(rt):
    lane_f = lax.broadcasted_iota(jnp.int32, rt.shape, 1).astype(F32)
    return jnp.where(lane_f == rt[:, 0:1], 1.0, 0.0), jnp.where(lane_f == rt[:, 1:2], 1.0, 0.0)


def _plan_kernel(rt_ref, cnt_ref, dest_ref, pend_ref, run_s):
    i = pl.program_id(0)
    rows = rt_ref.shape[0]
    lane = lax.broadcasted_iota(jnp.int32, (rows, ROUTER_PAD), 1)
    oh0, oh1 = _expert_onehots(rt_ref[...])
    tot0 = jnp.sum(oh0, axis=0, keepdims=True)
    tot1 = jnp.sum(oh1, axis=0, keepdims=True)

    @pl.when(i == 0)
    def _():
        counts = cnt_ref[...]
        padded = jnp.floor((counts + float(MOE_BLOCK - 1)) * (1.0 / MOE_BLOCK)) * float(MOE_BLOCK)
        r = lax.broadcasted_iota(jnp.int32, (ROUTER_PAD, ROUTER_PAD), 0)
        c = lax.broadcasted_iota(jnp.int32, (ROUTER_PAD, ROUTER_PAD), 1)
        incl = jnp.where(r <= c, 1.0, 0.0).astype(BF16)
        hi, lo = _split_bf16(padded)
        pend = _dot(hi, incl) + _dot(lo, incl)
        pend_ref[...] = pend
        run_s[...] = pend - padded

    r = lax.broadcasted_iota(jnp.int32, (rows, rows), 0)
    c = lax.broadcasted_iota(jnp.int32, (rows, rows), 1)
    before = jnp.where(c < r, 1.0, 0.0).astype(BF16)
    run = run_s[0:1, :]
    pos0 = jnp.sum(oh0 * (run + _dot(before, oh0.astype(BF16))), axis=-1, keepdims=True)
    pos1 = jnp.sum(oh1 * (run + tot0 + _dot(before, oh1.astype(BF16))), axis=-1, keepdims=True)
    run_s[0:1, :] = run + tot0 + tot1
    dest_ref[...] = jnp.where(lane == 0, pos0, jnp.where(lane == 1, pos1, 0.0)).astype(jnp.int32)


def _plan(route, counts, n_rows):
    rows = PLAN_TILE
    return pl.pallas_call(
        _plan_kernel,
        grid=(n_rows // rows,),
        in_specs=[pl.BlockSpec((rows, ROUTER_PAD), lambda i: (i, 0)),
                  pl.BlockSpec((8, ROUTER_PAD), lambda i: (0, 0))],
        out_specs=[pl.BlockSpec((rows, ROUTER_PAD), lambda i: (i, 0)),
                   pl.BlockSpec((8, ROUTER_PAD), lambda i: (0, 0))],
        out_shape=[jax.ShapeDtypeStruct((n_rows, ROUTER_PAD), jnp.int32),
                   jax.ShapeDtypeStruct((8, ROUTER_PAD), F32)],
        scratch_shapes=[pltpu.VMEM((8, ROUTER_PAD), F32)],
        compiler_params=_cparams(("arbitrary",)),
        name="dispatch_plan",
    )(route, counts)


def _dispatch_kernel(pend_ref, dest_ref, h_ref, xs_ref, zero_s, sem, zsem):
    rows = h_ref.shape[0]

    n_blk = xs_ref.shape[0] // MOE_BLOCK
    n_used = pend_ref[N_EXPERTS - 1] // MOE_BLOCK

    def zero_block(start):
        return pltpu.make_async_copy(zero_s, xs_ref.at[pl.ds(pl.multiple_of(start, MOE_BLOCK), MOE_BLOCK)], zsem)

    @pl.when(pl.program_id(0) == 0)
    def _():
        zero_s[...] = jnp.zeros_like(zero_s)
        for e in range(N_EXPERTS):
            @pl.when(pend_ref[e] > 0)
            def _():
                zero_block(pend_ref[e] - MOE_BLOCK).start()
        lax.fori_loop(n_used, n_blk, lambda j, c: (zero_block(j * MOE_BLOCK).start(), c)[1], 0)
        for e in range(N_EXPERTS):
            @pl.when(pend_ref[e] > 0)
            def _():
                zero_block(pend_ref[e] - MOE_BLOCK).wait()
        lax.fori_loop(n_used, n_blk, lambda j, c: (zero_block(j * MOE_BLOCK).wait(), c)[1], 0)

    def row_copy(r, d):
        return pltpu.make_async_copy(h_ref.at[pl.ds(r, 1)], xs_ref.at[pl.ds(d, 1)], sem)

    def issue(r, carry):
        for k in range(TOP_K):
            row_copy(r, dest_ref[0, 0, k * rows + r]).start(priority=k % 2)
        return carry

    lax.fori_loop(0, rows, issue, 0, unroll=8)
    for _ in range(TOP_K * rows):
        row_copy(0, 0).wait()


def _dispatch(pend, dest_tiles, h2, n_slots, n_tiles):
    d = h2.shape[1]
    return pl.pallas_call(
        _dispatch_kernel,
        grid_spec=pltpu.PrefetchScalarGridSpec(
            num_scalar_prefetch=1,
            grid=(n_tiles,),
            in_specs=[pl.BlockSpec((1, 1, TOP_K * ROW_TILE), lambda i, pe: (i, 0, 0), memory_space=pltpu.SMEM),
                      pl.BlockSpec((ROW_TILE, d), lambda i, pe: (i, 0))],
            out_specs=pl.BlockSpec(memory_space=pl.ANY),
            scratch_shapes=[pltpu.VMEM((MOE_BLOCK, d), F32), pltpu.SemaphoreType.DMA, pltpu.SemaphoreType.DMA],
        ),
        out_shape=jax.ShapeDtypeStruct((n_slots, d), F32),
        compiler_params=_cparams(("arbitrary",)),
        name="dispatch_rows",
    )(pend, dest_tiles, h2)


def _moe(h2, route, counts, n_tiles, layer, w_gate, w_up, w_down):
    t, d = h2.shape
    dest, pend_f = _plan(route, counts, t)
    pend = pend_f[0, :N_EXPERTS].astype(jnp.int32)
    n_blk = -(-(t * TOP_K) // MOE_BLOCK) + N_EXPERTS
    blk_start = jnp.arange(n_blk, dtype=jnp.int32) * MOE_BLOCK
    blk_exp = jnp.minimum(jnp.sum((pend[None, :] <= blk_start[:, None]).astype(jnp.int32), axis=1), N_EXPERTS - 1)
    n_used = (pend[N_EXPERTS - 1] // MOE_BLOCK).reshape(1)
    blk_exp = jnp.where(jnp.arange(n_blk) < n_used[0], blk_exp, blk_exp[n_used[0] - 1])
    has_rows = pend > jnp.concatenate([jnp.zeros((1,), jnp.int32), pend[:-1]])
    e_ids = jnp.arange(N_EXPERTS, dtype=jnp.int32)
    later = jnp.where(jnp.logical_and(has_rows[None, :], e_ids[None, :] > e_ids[:, None]), e_ids[None, :], N_EXPERTS)
    next_exp = jnp.min(later, axis=1)
    next_exp = jnp.where(next_exp < N_EXPERTS, next_exp, -1).astype(jnp.int32)
    dest2 = dest[:, :TOP_K]
    dest_tiles = dest2.reshape(n_tiles, ROW_TILE, TOP_K).transpose(0, 2, 1).reshape(n_tiles, 1, TOP_K * ROW_TILE)
    xs = _dispatch(pend, dest_tiles, h2, n_blk * MOE_BLOCK, n_tiles)
    ys = _experts(blk_exp.astype(jnp.int32), n_used, next_exp, xs, layer, w_gate, w_up, w_down)
    return ys[dest2[:, 0]], ys[dest2[:, 1]]


def _final_kernel(x_ref, y0_ref, y1_ref, rt_ref, mod_ref, g_ref, o_ref):
    x = _moe_residual(x_ref, y0_ref, y1_ref, rt_ref, mod_ref)
    o_ref[...] = x * lax.rsqrt(jnp.mean(x * x, axis=-1, keepdims=True) + EPS) * g_ref[...]


def _final(x_lat, moe_out, mod, g_final, n_tiles, tiles_per_seq, batch):
    d = D_MODEL
    row = lambda i: (i, 0)
    return pl.pallas_call(
        _final_kernel,
        grid=(n_tiles,),
        in_specs=[pl.BlockSpec((ROW_TILE, d), row), pl.BlockSpec((ROW_TILE, d), row),
                  pl.BlockSpec((ROW_TILE, d), row), pl.BlockSpec((ROW_TILE, ROUTER_PAD), row),
                  pl.BlockSpec((1, 6, d), lambda i: (jnp.minimum(i // tiles_per_seq, batch), 0, 0)),
                  pl.BlockSpec((1, d), lambda i: (0, 0))],
        out_specs=pl.BlockSpec((ROW_TILE, d), row),
        out_shape=jax.ShapeDtypeStruct((n_tiles * ROW_TILE, d), F32),
        compiler_params=_cparams(("parallel",)),
        name="final_norm",
    )(x_lat, *moe_out, mod, g_final)


def _rope_tables(seq):
    quarter = HEAD_DIM // 4
    freqs = ROPE_THETA ** (-jnp.arange(quarter, dtype=F32) / quarter)
    pos = jnp.arange(seq)
    ang_r = (pos // GRID_W).astype(F32)[:, None] * freqs[None, :]
    ang_c = (pos % GRID_W).astype(F32)[:, None] * freqs[None, :]
    cos = jnp.concatenate([jnp.cos(ang_r)] * 2 + [jnp.cos(ang_c)] * 2, axis=-1)
    sin = jnp.concatenate([-jnp.sin(ang_r), jnp.sin(ang_r), -jnp.sin(ang_c), jnp.sin(ang_c)], axis=-1)
    reps = 128 // HEAD_DIM
    cos = jnp.tile(cos, (1, reps))
    sin = jnp.tile(sin, (1, reps))
    cos = jnp.concatenate([cos, jnp.ones((ROW_TILE, 128), F32)], axis=0)
    sin = jnp.concatenate([sin, jnp.zeros((ROW_TILE, 128), F32)], axis=0)
    return cos, sin


def kernel(x, c, ctx, c_ctx, w_mod, b_mod, g_norm1, g_norm2, w_in, attn_sink, ssm_lam_re, ssm_lam_im,
           ssm_log_dt, ssm_b_re, ssm_b_im, ssm_c_re, ssm_c_im, ssm_d, w_glu, w_br_attn, w_br_ssm, w_out,
           w_router_group, w_router_expert, w_exp_gate, w_exp_up, w_exp_down, g_final):
    batch, seq, d = x.shape
    ctx_len = ctx.shape[1]
    depth = w_mod.shape[0]
    assert d == D_MODEL and batch * SSM_STATE == 128
    assert seq % ROW_TILE == 0 and ctx_len % ROW_TILE == 0 and (batch * seq) % ctx_len == 0
    assert batch + 1 <= MOD_ROWS
    t_lat, t_ctx = batch * seq, batch * ctx_len
    tiles_per_seq = seq // ROW_TILE
    n_lat_tiles = t_lat // ROW_TILE
    n_all_tiles = (t_lat + t_ctx) // ROW_TILE
    nb_per_seq = seq // ATTN_BLOCK
    ncc, ncl = ctx_len // SSM_CHUNK, seq // SSM_CHUNK

    c_rows = jnp.zeros((MOD_ROWS, d), F32).at[:batch].set(c).at[batch].set(c_ctx)
    mod_all = _modulation(c_rows, w_mod, b_mod).reshape(depth, MOD_ROWS, 6, d)
    cos_t, sin_t = _rope_tables(seq)
    ssm_w = _ssm_weights_all(ssm_lam_re, ssm_lam_im, ssm_log_dt, ssm_b_re, ssm_b_im, ssm_c_re, ssm_c_im, ssm_d, batch)
    x_parts = (x.reshape(t_lat, d), ctx.reshape(t_ctx, d))
    f_all = None
    for l in range(depth):
        ctx_out = l < depth - 1
        mod = mod_all[l]
        x_parts, (q, k, v, u, ga, gs) = _inproj(
            x_parts, f_all, mod_all[l - 1] if l > 0 else None, mod, g_norm1[l].reshape(1, d), cos_t, sin_t,
            w_in[l].astype(BF16), n_all_tiles, tiles_per_seq, n_lat_tiles, batch)
        sink = attn_sink[l].astype(F32)
        attn = _attention(sink, q, k, v, t_lat // ATTN_BLOCK, nb_per_seq, t_lat // ctx_len, ctx_len, True, 0)
        z, z_ctx = _ssm_mixer(u, l, *ssm_w, batch, seq, ctx_len)
        ctx_parts = None
        if ctx_out:
            attn_c = _attention(sink, q, k, v, t_ctx // ATTN_BLOCK, ctx_len // ATTN_BLOCK, t_lat // ctx_len,
                                ctx_len, False, t_lat // ATTN_BLOCK)
            ctx_parts = (attn_c, z_ctx, x_parts[1])
        w_r = jnp.zeros((d, ROUTER_PAD), F32)
        w_r = w_r.at[:, :N_EXPERT_GROUPS].set(w_router_group[l])
        w_r = w_r.at[:, N_EXPERT_GROUPS:N_EXPERT_GROUPS + N_EXPERTS].set(w_router_expert[l])
        wr_hi, wr_lo = _split_bf16(w_r)
        n_rows = t_lat + t_ctx if ctx_out else t_lat
        x_all, h2, route, counts = _merge(
            attn, z, x_parts[0], ctx_parts, ga, gs, mod, g_norm2[l].reshape(1, d), w_glu[l].astype(BF16),
            w_br_attn[l].astype(BF16), w_br_ssm[l].astype(BF16), w_out[l].astype(BF16),
            wr_hi, wr_lo, n_rows, t_lat, seq, batch)
        y0, y1 = _moe(h2, route, counts, n_rows // ROW_TILE, l, w_exp_gate, w_exp_up, w_exp_down)
        f_all = (y0, y1, route)
        x_parts = (x_all,)
    out = _final(x_all, f_all, mod_all[depth - 1], g_final.reshape(1, d), n_lat_tiles, tiles_per_seq, batch)
    return out.reshape(batch, seq, d)
```

```python
import functools
import math

import jax
import jax.numpy as jnp
from jax import lax
from jax.experimental import pallas as pl
from jax.experimental.pallas import tpu as pltpu

F32 = jnp.float32
BF16 = jnp.bfloat16

D_MODEL = 1024
GRID_W = 64
N_HEADS = 8
N_KV_HEADS = 2
HEAD_DIM = 64
Q_GROUP = N_HEADS // N_KV_HEADS
ATTN_BLOCK = 128
ATTN_STEP_BLOCKS = 2
ROPE_THETA = 10000.0
D_SSM = D_MODEL // 2
SSM_GROUP = 16
N_SSM_GROUPS = D_SSM // SSM_GROUP
SSM_STATE = 64
N_EXPERT_GROUPS = 4
EXPERTS_PER_GROUP = 8
N_EXPERTS = N_EXPERT_GROUPS * EXPERTS_PER_GROUP
TOP_K = 2
D_EXPERT = D_MODEL // 2
Q_W = N_HEADS * HEAD_DIM
KV_W = N_KV_HEADS * HEAD_DIM
O_K = Q_W
O_V = O_K + KV_W
O_U = O_V + KV_W
O_GA = O_U + D_SSM
O_GS = O_GA + D_MODEL
D_IN = O_GS + D_MODEL
EPS = 1e-6
NEG_INF = -1e30

ROW_TILE = 256
MERGE_TILE = 512
MERGE_SUB = 256
SSM_CHUNK = 16
SSM_CW = SSM_CHUNK * SSM_GROUP
SSM_SCAN_GROUPS = 8
SSM_LANE_GROUPS = 128 // SSM_GROUP
SSM_LANE_TILES = D_SSM // 128
MOE_BLOCK = 256
ROW_SUBLANES = D_MODEL // 128
PLAN_TILE = 512
ROUTER_PAD = 128
MOD_ROWS = 8
VMEM_LIMIT = 48 * 1024 * 1024


def _cparams(sem):
    return pltpu.CompilerParams(dimension_semantics=sem, vmem_limit_bytes=VMEM_LIMIT)


def _dot(a, b):
    return jnp.dot(a, b, preferred_element_type=F32)


def _split_bf16(a):
    hi = a.astype(BF16)
    lo = (a - hi.astype(F32)).astype(BF16)
    return hi, lo


def _rms_mod(x, g, shift, scale):
    y = x * lax.rsqrt(jnp.mean(x * x, axis=-1, keepdims=True) + EPS) * g
    return y * (1.0 + scale) + shift


def _mod_kernel(c_ref, w_ref, b_ref, o_ref):
    c = c_ref[...]
    s_hi, s_lo = _split_bf16(c * jax.nn.sigmoid(c))
    w_hi, w_lo = _split_bf16(w_ref[0])
    o_ref[0] = _dot(s_hi, w_hi) + _dot(s_lo, w_hi) + _dot(s_hi, w_lo) + b_ref[0]


def _modulation(c_rows, w_mod, b_mod):
    depth, d, n = w_mod.shape
    nb = n // 4
    return pl.pallas_call(
        _mod_kernel,
        grid=(depth, n // nb),
        in_specs=[
            pl.BlockSpec((MOD_ROWS, d), lambda l, j: (0, 0)),
            pl.BlockSpec((1, d, nb), lambda l, j: (l, 0, j)),
            pl.BlockSpec((1, 1, nb), lambda l, j: (l, 0, j)),
        ],
        out_specs=pl.BlockSpec((1, MOD_ROWS, nb), lambda l, j: (l, 0, j)),
        out_shape=jax.ShapeDtypeStruct((depth, MOD_ROWS, n), F32),
        compiler_params=_cparams(("arbitrary", "arbitrary")),
        name="modulation",
    )(c_rows, w_mod, b_mod.reshape(depth, 1, n))


def _moe_residual(x_ref, y0_ref, y1_ref, rt_ref, mod_ref):
    rt = rt_ref[...]
    f = rt[:, 2:3] * y0_ref[...].astype(F32) + rt[:, 3:4] * y1_ref[...].astype(F32)
    return x_ref[...] + mod_ref[0, 5:6, :] * f


def _inproj_kernel(has_f, n_lat_tiles, *refs):
    if has_f:
        (x_ref, y0_ref, y1_ref, rt_ref, modp_ref, mod_ref, g_ref, cos_ref, sin_ref, w_ref,
         xo_ref, q_ref, k_ref, v_ref, u_ref, ga_ref, gs_ref) = refs
        x = _moe_residual(x_ref, y0_ref, y1_ref, rt_ref, modp_ref)
        xo_ref[...] = x
    else:
        (xl_ref, xc_ref, mod_ref, g_ref, cos_ref, sin_ref, w_ref,
         q_ref, k_ref, v_ref, u_ref, ga_ref, gs_ref) = refs
        x = jnp.where(pl.program_id(0) >= n_lat_tiles, xc_ref[...], xl_ref[...])
    m = mod_ref[0]
    h = _rms_mod(x, g_ref[...], m[0:1], m[1:2]).astype(BF16)
    cos = cos_ref[...]
    sin = sin_ref[...]
    lane = lax.broadcasted_iota(jnp.int32, cos.shape, 1)
    first = (lane % (HEAD_DIM // 2)) < (HEAD_DIM // 4)

    def rope(t):
        sw = jnp.where(first, pltpu.roll(t, 128 - HEAD_DIM // 4, 1), pltpu.roll(t, HEAD_DIM // 4, 1))
        return t * cos + sw * sin

    def proj(lo, hi):
        return _dot(h, w_ref[:, lo:hi])

    q = proj(0, O_K)
    for j in range(Q_W // 128):
        q_ref[:, 128 * j:128 * (j + 1)] = (rope(q[:, 128 * j:128 * (j + 1)]) * HEAD_DIM ** -0.5).astype(BF16)
    kv = proj(O_K, O_U)
    k_ref[...] = rope(kv[:, :KV_W]).astype(BF16)
    v_ref[...] = kv[:, KV_W:].astype(BF16)
    u = proj(O_U, O_GA)
    for j in range(SSM_LANE_TILES):
        u_ref[j] = u[:, 128 * j:128 * (j + 1)]
    ga_ref[...] = jax.nn.sigmoid(proj(O_GA, O_GS)).astype(BF16)
    gs_ref[...] = jax.nn.sigmoid(proj(O_GS, D_IN)).astype(BF16)


def _inproj(x_parts, moe_out, mod_prev, mod, g1, cos_t, sin_t, w_in, n_tiles, tiles_per_seq, n_lat_tiles, batch):
    d = D_MODEL
    has_f = moe_out is not None
    row = lambda i: (i, 0)
    modi = lambda i: (jnp.minimum(i // tiles_per_seq, batch), 0, 0)
    const = lambda i: (0, 0)
    ropei = lambda i: (jnp.where(i < n_lat_tiles, i % tiles_per_seq, tiles_per_seq), 0)
    if has_f:
        in_specs = [pl.BlockSpec((ROW_TILE, d), row), pl.BlockSpec((ROW_TILE, d), row),
                    pl.BlockSpec((ROW_TILE, d), row), pl.BlockSpec((ROW_TILE, ROUTER_PAD), row),
                    pl.BlockSpec((1, 6, d), modi)]
        args = [*x_parts, *moe_out, mod_prev]
    else:
        in_specs = [pl.BlockSpec((ROW_TILE, d), lambda i: (jnp.minimum(i, n_lat_tiles - 1), 0)),
                    pl.BlockSpec((ROW_TILE, d), lambda i: (jnp.maximum(i - n_lat_tiles, 0), 0))]
        args = list(x_parts)
    in_specs += [
        pl.BlockSpec((1, 6, d), modi),
        pl.BlockSpec((1, d), const),
        pl.BlockSpec((ROW_TILE, 128), ropei),
        pl.BlockSpec((ROW_TILE, 128), ropei),
        pl.BlockSpec((d, D_IN), const),
    ]
    args += [mod, g1, cos_t, sin_t, w_in]
    widths = [Q_W, KV_W, KV_W, D_SSM, D_MODEL, D_MODEL]
    out_specs = [pl.BlockSpec((ROW_TILE, w), row) for w in widths]
    out_shape = [jax.ShapeDtypeStruct((n_tiles * ROW_TILE, w), BF16) for w in widths]
    out_specs[3] = pl.BlockSpec((SSM_LANE_TILES, ROW_TILE, 128), lambda i: (0, i, 0))
    out_shape[3] = jax.ShapeDtypeStruct((SSM_LANE_TILES, n_tiles * ROW_TILE, 128), F32)
    if has_f:
        out_specs = [pl.BlockSpec((ROW_TILE, d), row)] + out_specs
        out_shape = [jax.ShapeDtypeStruct((n_tiles * ROW_TILE, d), F32)] + out_shape
    outs = pl.pallas_call(
        functools.partial(_inproj_kernel, has_f, n_lat_tiles),
        grid=(n_tiles,),
        in_specs=in_specs,
        out_specs=out_specs,
        out_shape=out_shape,
        compiler_params=_cparams(("parallel",)),
        name="inproj",
    )(*args)
    if has_f:
        return (outs[0],), outs[1:]
    return x_parts, outs


def _attn_block(sink_ref, q, k_tiles, v_tiles, biases):
    blk = q.shape[0]
    outs = []
    for g in range(N_KV_HEADS):
        gs = slice(g * HEAD_DIM, (g + 1) * HEAD_DIM)
        k_all = jnp.concatenate([t[:, gs] for t in k_tiles], axis=0)
        v_all = jnp.concatenate([t[:, gs] for t in v_tiles], axis=0)
        v_ext = jnp.concatenate([v_all, jnp.ones_like(v_all)], axis=1)
        qg = jnp.concatenate(
            [q[:, (g * Q_GROUP + h) * HEAD_DIM:(g * Q_GROUP + h + 1) * HEAD_DIM] for h in range(Q_GROUP)], axis=0)
        s_all = lax.dot_general(qg, k_all, (((1,), (1,)), ((), ())), preferred_element_type=F32)
        for h in range(Q_GROUP):
            s = s_all[h * blk:(h + 1) * blk]
            tiles, col = [], 0
            for kt, bias in zip(k_tiles, biases):
                t = s[:, col:col + kt.shape[0]]
                tiles.append(t if bias is None else t + bias)
                col += kt.shape[0]
            mx = tiles[0]
            for t in tiles[1:]:
                for c0 in range(0, t.shape[1], blk):
                    mx = jnp.maximum(mx, t[:, c0:c0 + blk])
            sink = sink_ref[g * Q_GROUP + h]
            m = jnp.maximum(jnp.max(mx, axis=-1, keepdims=True), sink)
            p = jnp.exp(jnp.concatenate([(t - m).astype(BF16) for t in tiles], axis=1))
            o_ext = _dot(p, v_ext)
            denom = o_ext[:, HEAD_DIM:HEAD_DIM + 1] + jnp.exp(sink - m)
            outs.append((o_ext[:, :HEAD_DIM] * (1.0 / denom)).astype(BF16))
    return jnp.concatenate(outs, axis=1)


def _attn_kernel(band, nb_per_seq, sink_ref, *refs):
    blk = ATTN_BLOCK
    if not band:
        q_ref, kx_ref, vx_ref, o_ref = refs
        for a in range(q_ref.shape[0] // blk):
            rows = slice(a * blk, (a + 1) * blk)
            o_ref[rows, :] = _attn_block(sink_ref, q_ref[rows, :], [kx_ref[...]], [vx_ref[...]], [None])
        return
    q_ref, kp_ref, kc_ref, kn_ref, vp_ref, vc_ref, vn_ref, kx_ref, vx_ref, o_ref = refs
    n_sub = q_ref.shape[0] // blk
    j0 = (pl.program_id(0) * n_sub) % nb_per_seq
    r = lax.broadcasted_iota(jnp.int32, (blk, blk), 0)
    c = lax.broadcasted_iota(jnp.int32, (blk, blk), 1)
    k_blocks = [kp_ref[...]] + [kc_ref[a * blk:(a + 1) * blk, :] for a in range(n_sub)] + [kn_ref[...]]
    v_blocks = [vp_ref[...]] + [vc_ref[a * blk:(a + 1) * blk, :] for a in range(n_sub)] + [vn_ref[...]]
    for a in range(n_sub):
        edge_p = jnp.where(j0 + a > 0, 0.0, NEG_INF).astype(F32)
        edge_n = jnp.where(j0 + a < nb_per_seq - 1, 0.0, NEG_INF).astype(F32)
        bias_p = jnp.where(c >= r, edge_p, NEG_INF).astype(F32)
        bias_n = jnp.where(c <= r, edge_n, NEG_INF).astype(F32)
        rows = slice(a * blk, (a + 1) * blk)
        o_ref[rows, :] = _attn_block(sink_ref, q_ref[rows, :], k_blocks[a:a + 3] + [kx_ref[...]],
                                     v_blocks[a:a + 3] + [vx_ref[...]], [bias_p, None, bias_n, None])


def _attention(sink, q, k, v, n_blocks, nb_per_seq, kx_block0, ctx_len, band, q_block0):
    blk = ATTN_BLOCK
    n_sub = ATTN_STEP_BLOCKS
    assert n_blocks % n_sub == 0 and nb_per_seq % n_sub == 0 and q_block0 % n_sub == 0
    step = n_sub * blk
    n_steps = n_blocks // n_sub
    qi = lambda i, s: (q_block0 // n_sub + i, 0)
    cur = lambda i, s: (i, 0)
    prv = lambda i, s: (jnp.maximum(i * n_sub - 1, 0), 0)
    nxt = lambda i, s: (jnp.minimum((i + 1) * n_sub, n_blocks - 1), 0)
    kxi = lambda i, s: (kx_block0 + (i * n_sub) // nb_per_seq, 0)
    kspec = lambda f: pl.BlockSpec((blk, KV_W), f)
    cspec = pl.BlockSpec((step, KV_W), cur)
    xspec = pl.BlockSpec((ctx_len, KV_W), kxi)
    if band:
        in_specs = [pl.BlockSpec((step, Q_W), qi), kspec(prv), cspec, kspec(nxt),
                    kspec(prv), cspec, kspec(nxt), xspec, xspec]
        args = (q, k, k, k, v, v, v, k, v)
    else:
        in_specs = [pl.BlockSpec((step, Q_W), qi), xspec, xspec]
        args = (q, k, v)
    return pl.pallas_call(
        functools.partial(_attn_kernel, band, nb_per_seq),
        grid_spec=pltpu.PrefetchScalarGridSpec(
            num_scalar_prefetch=1,
            grid=(n_steps,),
            in_specs=in_specs,
            out_specs=pl.BlockSpec((step, Q_W), lambda i, s: (i, 0)),
        ),
        out_shape=jax.ShapeDtypeStruct((n_blocks * blk, Q_W), BF16),
        compiler_params=_cparams(("parallel",)),
        name="band_attention" if band else "context_attention",
    )(sink, *args)


def _dot_f32(a, b_t):
    a_hi, a_lo = _split_bf16(a)
    b_hi, b_lo = _split_bf16(b_t)
    dn = (((1,), (1,)), ((), ()))
    dg = functools.partial(lax.dot_general, dimension_numbers=dn, preferred_element_type=F32)
    return dg(a_hi, b_hi) + dg(a_lo, b_hi) + dg(a_hi, b_lo)


def _ssm_weight_kernel(batch, lam_r_ref, bt_ref, c_ref, d_ref, w1_ref, abar_ref, t_ref, wo_ref):
    lc, mm, p = SSM_CHUNK, SSM_GROUP, SSM_STATE
    up_r = lax.broadcasted_iota(jnp.int32, (lc, 1), 0).astype(F32)
    lane = lax.broadcasted_iota(jnp.int32, (mm, lc * mm), 1)
    row = lax.broadcasted_iota(jnp.int32, (mm, lc * mm), 0)
    planes_in, planes_out, abar_rows, kt = [], [], [], []
    for d in range(2):
        lr, li, dt = lam_r_ref[0, d, 0:1, :], lam_r_ref[0, d, 1:2, :], jnp.exp(lam_r_ref[0, d, 2:3, :])

        def powers(expo):
            mag = jnp.exp(lr * dt * expo)
            return mag * jnp.cos(li * dt * expo), mag * jnp.sin(li * dt * expo)

        a_re, a_im = powers(jnp.ones((1, 1), F32))
        den = lr * lr + li * li
        nr = a_re - 1.0
        f_re = (nr * lr + a_im * li) / den
        f_im = (a_im * lr - nr * li) / den
        bt_re, bt_im = bt_ref[0, d, 0], bt_ref[0, d, 1]
        bbt_re = f_re * bt_re - f_im * bt_im
        bbt_im = f_re * bt_im + f_im * bt_re
        c_re, c_im = c_ref[0, d, 0], c_ref[0, d, 1]

        pr, pi = powers(lc - 1.0 - up_r if d == 0 else up_r)
        planes_in.append(jnp.concatenate(
            [pr[s:s + 1] * bbt_re - pi[s:s + 1] * bbt_im for s in range(lc)], axis=0))
        planes_in.append(jnp.concatenate(
            [pr[s:s + 1] * bbt_im + pi[s:s + 1] * bbt_re for s in range(lc)], axis=0))

        qr, qi = powers(up_r + 1.0 if d == 0 else lc - up_r)
        planes_out.append(jnp.concatenate(
            [c_re * qr[t:t + 1] - c_im * qi[t:t + 1] for t in range(lc)], axis=0).T)
        planes_out.append(jnp.concatenate(
            [-c_re * qi[t:t + 1] - c_im * qr[t:t + 1] for t in range(lc)], axis=0).T)

        kr, ki = powers(up_r if d == 0 else lc - 1.0 - up_r)
        cp_re = jnp.concatenate([c_re * kr[j:j + 1] - c_im * ki[j:j + 1] for j in range(lc)], axis=0)
        cp_im = jnp.concatenate([c_re * ki[j:j + 1] + c_im * kr[j:j + 1] for j in range(lc)], axis=0)
        kt.append(_dot_f32(jnp.concatenate([bbt_re, bbt_im], axis=1), jnp.concatenate([cp_re, -cp_im], axis=1)))

        e_re, e_im = powers(jnp.full((1, 1), float(lc), F32))
        abar_rows += [jnp.concatenate([e_re] * batch, axis=1), jnp.concatenate([e_im] * batch, axis=1)]

    dvec = d_ref[0]
    blocks = []
    for s in range(lc):
        fwd = kt[0] if s == 0 else pltpu.roll(kt[0], mm * s, 1)
        bwd = kt[1] if s == lc - 1 else pltpu.roll(kt[1], mm * (s + 1), 1)
        blk = jnp.where(lane >= mm * s, fwd, 0.0) + jnp.where(lane < mm * (s + 1), bwd, 0.0)
        blocks.append(blk + jnp.where(lane == mm * s + row, dvec, 0.0))
    t_ref[0] = jnp.concatenate(blocks, axis=0).astype(BF16)
    abar_ref[0] = jnp.concatenate(abar_rows, axis=0)
    zero_in = jnp.zeros((lc * mm, p), F32)
    zero_out = jnp.zeros((p, lc * mm), F32)
    for b in range(batch):
        w1_ref[0, b] = jnp.concatenate(
            [pl_ if bb == b else zero_in for pl_ in planes_in for bb in range(batch)], axis=1).astype(BF16)
        wo_ref[0, b] = jnp.concatenate(
            [pl_ if bb == b else zero_out for pl_ in planes_out for bb in range(batch)], axis=0).astype(BF16)


def _ssm_weights_all(lam_re, lam_im, log_dt, b_re, b_im, c_re, c_im, d_skip, batch):
    depth = lam_re.shape[0]
    g, p, mm, lc = N_SSM_GROUPS, SSM_STATE, SSM_GROUP, SSM_CHUNK
    n = depth * g
    cw, sw = lc * mm, 4 * batch * p

    def per_group(a):
        return jnp.moveaxis(a.astype(F32), 2, 1).reshape((n, 2) + a.shape[3:])

    lam = jnp.stack([per_group(lam_re), per_group(lam_im),
                     jnp.broadcast_to(per_group(log_dt)[..., None], (n, 2, p))], axis=2)
    bt = jnp.stack([per_group(b_re), per_group(b_im)], axis=2).swapaxes(-1, -2)
    c = jnp.stack([per_group(c_re), per_group(c_im)], axis=2)
    d_t = jnp.tile(d_skip.astype(F32).reshape(n, 1, mm), (1, 1, lc))
    full = lambda *shape: pl.BlockSpec((1,) + shape, lambda i: (i,) + (0,) * len(shape))
    w1, abar, tmat, wout = pl.pallas_call(
        functools.partial(_ssm_weight_kernel, batch),
        grid=(n,),
        in_specs=[full(2, 3, p), full(2, 2, mm, p), full(2, 2, mm, p), full(1, cw)],
        out_specs=[full(batch, cw, sw), full(4, batch * p), full(cw, cw), full(batch, sw, cw)],
        out_shape=[jax.ShapeDtypeStruct((n, batch, cw, sw), BF16), jax.ShapeDtypeStruct((n, 4, batch * p), F32),
                   jax.ShapeDtypeStruct((n, cw, cw), BF16), jax.ShapeDtypeStruct((n, batch, sw, cw), BF16)],
        compiler_params=_cparams(("parallel",)),
        name="ssm_weights",
    )(lam, bt, c, d_t)
    abar = abar.reshape(depth, g, 4, batch * p).transpose(0, 2, 1, 3)
    return (w1.reshape(depth, g, batch, cw, sw), abar, tmat.reshape(depth, g, cw, cw),
            wout.reshape(depth, g, batch, sw, cw))


def _ssm_p1_kernel(ul_ref, uc_ref, w_ref, v_ref, *s_refs):
    b = pl.program_id(1)
    ncc, ncl = uc_ref.shape[0] // SSM_CHUNK, ul_ref.shape[0] // SSM_CHUNK
    nc = ncc + ncl
    pw = s_refs[0].shape[1]
    xs = [jnp.concatenate([uc_ref[pl.ds(s, ncc, stride=SSM_CHUNK), :], ul_ref[pl.ds(s, ncl, stride=SSM_CHUNK), :]],
                          axis=0) for s in range(SSM_CHUNK)]
    for j in range(SSM_LANE_GROUPS):
        v = jnp.concatenate([x[:, j * SSM_GROUP:(j + 1) * SSM_GROUP] for x in xs], axis=1).astype(BF16)
        v_ref[j] = v
        acc = _dot(v, w_ref[j])
        for k, s_ref in enumerate(s_refs):
            part = acc[:, k * pw:(k + 1) * pw]
            rows = slice(j * nc, (j + 1) * nc)

            @pl.when(b == 0)
            def _():
                s_ref[rows, :] = part

            @pl.when(b > 0)
            def _():
                s_ref[rows, :] = s_ref[rows, :] + part


def _ssm_scan_kernel(ncc, ncl, a_ref, sfr_ref, sfi_ref, sbr_ref, sbi_ref, xfr_ref, xfi_ref, xbr_ref, xbi_ref):
    nc = ncc + ncl
    gb = a_ref.shape[1]
    pw = a_ref.shape[2]
    afr, afi, abr, abi = a_ref[0], a_ref[1], a_ref[2], a_ref[3]

    def rows(r):
        return pl.ds(r, gb, stride=nc)

    def step(rf, rb, carry):
        xfr, xfi, xbr, xbi = carry
        xfr_ref[rows(rf), :] = xfr
        xfi_ref[rows(rf), :] = xfi
        xbr_ref[rows(rb), :] = xbr
        xbi_ref[rows(rb), :] = xbi
        sfr = sfr_ref[rows(rf), :]
        sfi = sfi_ref[rows(rf), :]
        sbr = sbr_ref[rows(rb), :]
        sbi = sbi_ref[rows(rb), :]
        return (afr * xfr - afi * xfi + sfr, afr * xfi + afi * xfr + sfi,
                abr * xbr - abi * xbi + sbr, abr * xbi + abi * xbr + sbi)

    zero = jnp.zeros((gb, pw), F32)
    carry = lax.fori_loop(0, ncc, lambda t, c: step(t, ncc - 1 - t, c), (zero, zero, zero, zero))
    lax.fori_loop(0, ncl, lambda t, c: step(ncc + t, nc - 1 - t, c), carry)


def _ssm_p3_kernel(v_ref, xfr_ref, xfi_ref, xbr_ref, xbi_ref, t_ref, wo_ref, zl_ref, zc_ref):
    ncc, ncl = zc_ref.shape[0] // SSM_CHUNK, zl_ref.shape[0] // SSM_CHUNK
    nc = ncc + ncl
    ys = []
    for j in range(SSM_LANE_GROUPS):
        rows = slice(j * nc, (j + 1) * nc)
        xin = jnp.concatenate([xfr_ref[rows, :], xfi_ref[rows, :], xbr_ref[rows, :], xbi_ref[rows, :]],
                              axis=1).astype(BF16)
        y = _dot(v_ref[j], t_ref[j]) + _dot(xin, wo_ref[j])
        ys.append(jax.nn.gelu(y, approximate=True))
    for t in range(SSM_CHUNK):
        zt = jnp.concatenate([y[:, t * SSM_GROUP:(t + 1) * SSM_GROUP] for y in ys], axis=1)
        zc_ref[pl.ds(t, ncc, stride=SSM_CHUNK), :] = zt[:ncc]
        zl_ref[pl.ds(t, ncl, stride=SSM_CHUNK), :] = zt[ncc:]


def _ssm_mixer(u4, layer, w1, abar, tmat, wout, batch, seq, ctx_len):
    n_lt = u4.shape[0]
    g = N_SSM_GROUPS
    lg = SSM_LANE_GROUPS
    ncc, ncl = ctx_len // SSM_CHUNK, seq // SSM_CHUNK
    nc = ncc + ncl
    cw = SSM_CW
    sw = w1.shape[-1]
    pw = sw // 4
    ctx_blk0 = (batch * seq) // ctx_len
    plane_shape = [jax.ShapeDtypeStruct((g * nc, pw), F32)] * 4
    lat_spec = pl.BlockSpec((None, seq, 128), lambda k, b: (k, b, 0))
    ctx_spec = pl.BlockSpec((None, ctx_len, 128), lambda k, b: (k, ctx_blk0 + b, 0))
    v_spec = pl.BlockSpec((lg, None, nc, cw), lambda k, b: (k, b, 0, 0))
    plane_spec = pl.BlockSpec((lg * nc, pw), lambda k, b: (k, 0))
    v_chunks, *s_planes = pl.pallas_call(
        _ssm_p1_kernel,
        grid=(n_lt, batch),
        in_specs=[lat_spec, ctx_spec,
                  pl.BlockSpec((None, lg, None, cw, sw), lambda k, b: (layer, k, b, 0, 0))],
        out_specs=[v_spec] + [plane_spec] * 4,
        out_shape=[jax.ShapeDtypeStruct((g, batch, nc, cw), BF16)] + plane_shape,
        compiler_params=_cparams(("parallel", "arbitrary")),
        name="ssm_chunk_states",
    )(u4, u4, w1)
    gb = SSM_SCAN_GROUPS
    x_planes = pl.pallas_call(
        functools.partial(_ssm_scan_kernel, ncc, ncl),
        grid=(g // gb,),
        in_specs=[pl.BlockSpec((None, 4, gb, pw), lambda i: (layer, 0, i, 0))]
        + [pl.BlockSpec((gb * nc, pw), lambda i: (i, 0))] * 4,
        out_specs=[pl.BlockSpec((gb * nc, pw), lambda i: (i, 0))] * 4,
        out_shape=plane_shape,
        compiler_params=_cparams(("parallel",)),
        name="ssm_chunk_scan",
    )(abar, *s_planes)
    return pl.pallas_call(
        _ssm_p3_kernel,
        grid=(n_lt, batch),
        in_specs=[v_spec] + [plane_spec] * 4
        + [pl.BlockSpec((None, lg, cw, cw), lambda k, b: (layer, k, 0, 0)),
           pl.BlockSpec((None, lg, None, sw, cw), lambda k, b: (layer, k, b, 0, 0))],
        out_specs=[pl.BlockSpec((None, seq, 128), lambda k, b: (k, b, 0)),
                   pl.BlockSpec((None, ctx_len, 128), lambda k, b: (k, b, 0))],
        out_shape=[jax.ShapeDtypeStruct((n_lt, batch * seq, 128), F32),
                   jax.ShapeDtypeStruct((n_lt, batch * ctx_len, 128), F32)],
        compiler_params=_cparams(("parallel", "arbitrary")),
        name="ssm_chunk_outputs",
    )(v_chunks, *x_planes, tmat, wout)


def _merge_kernel(n_lat_tiles, *refs):
    if n_lat_tiles is None:
        attn_ref, z_ref, x_ref = refs[:3]
        refs = refs[3:]
    else:
        attn_ref, z_ref, x_ref, attn_c_ref, z_c_ref, x_c_ref = refs[:6]
        refs = refs[6:]
        is_ctx = pl.program_id(0) >= n_lat_tiles
    (ga_ref, gs_ref, mod_ref, g2_ref, wglu_ref, wba_ref, wbs_ref, wout_ref, wrh_ref, wrl_ref,
     xo_ref, h2_ref, rt_ref, cnt_ref) = refs
    m = mod_ref[0]
    tile_counts = None
    for r in range(MERGE_TILE // MERGE_SUB):
        rows = slice(r * MERGE_SUB, (r + 1) * MERGE_SUB)
        zf = jnp.concatenate([z_ref[j, rows, :] for j in range(SSM_LANE_TILES)], axis=1)
        attn = attn_ref[rows, :]
        x_in = x_ref[rows, :]
        if n_lat_tiles is not None:
            zf = jnp.where(is_ctx, jnp.concatenate([z_c_ref[j, rows, :] for j in range(SSM_LANE_TILES)], axis=1), zf)
            attn = jnp.where(is_ctx, attn_c_ref[rows, :], attn)
            x_in = jnp.where(is_ctx, x_c_ref[rows, :], x_in)
        z = zf.astype(BF16)
        glu = (z.astype(F32) * jax.nn.sigmoid(_dot(z, wglu_ref[...]))).astype(BF16)
        mix = (ga_ref[rows, :].astype(F32) * _dot(attn, wba_ref[...])
               + gs_ref[rows, :].astype(F32) * _dot(glu, wbs_ref[...])).astype(BF16)
        x = x_in + m[2:3] * _dot(mix, wout_ref[...])
        xo_ref[rows, :] = x
        h2 = _rms_mod(x, g2_ref[...], m[3:4], m[4:5])
        hi, lo = _split_bf16(h2)
        h2_ref[rows, :] = hi.astype(F32)
        logits = _dot(hi, wrh_ref[...]) + _dot(lo, wrh_ref[...]) + _dot(hi, wrl_ref[...])
        rt = _route(logits)
        rt_ref[rows, :] = rt
        oh0, oh1 = _expert_onehots(rt)
        part = jnp.sum(oh0 + oh1, axis=0, keepdims=True)
        tile_counts = part if tile_counts is None else tile_counts + part

    @pl.when(pl.program_id(0) == 0)
    def _():
        cnt_ref[...] = jnp.zeros_like(cnt_ref)

    cnt_ref[0:1, :] = cnt_ref[0:1, :] + tile_counts


def _route(lg):
    ng, epg = N_EXPERT_GROUPS, EXPERTS_PER_GROUP
    lane_i = lax.broadcasted_iota(jnp.int32, lg.shape, 1)
    lane = lane_i.astype(F32)
    big = float(ROUTER_PAD)

    def rmax(mask_val):
        return jnp.max(mask_val, axis=-1, keepdims=True)

    def first_lane(mask, val, mx):
        return jnp.min(jnp.where(mask, jnp.where(val == mx, lane, big), big), axis=-1, keepdims=True)

    gmask = lane_i < ng
    lgm = jnp.where(gmask, lg, NEG_INF)
    mg = rmax(lgm)
    g_prob = 1.0 / jnp.sum(jnp.exp(lgm - mg), axis=-1, keepdims=True)
    g_idx = first_lane(gmask, lg, mg)
    egroup = jnp.floor((lane - float(ng)) * (1.0 / epg))
    emask = egroup == g_idx
    l1 = jnp.where(emask, lg, NEG_INF)
    m1 = rmax(l1)
    i1 = first_lane(emask, lg, m1)
    l2 = jnp.where(lane == i1, NEG_INF, l1)
    m2 = rmax(l2)
    i2 = jnp.min(jnp.where(l2 == m2, jnp.where(emask, lane, big), big), axis=-1, keepdims=True)
    r = jnp.exp(m2 - m1)
    w1 = g_prob / (1.0 + r)
    w2 = w1 * r
    return jnp.where(lane_i == 0, i1 - float(ng),
                     jnp.where(lane_i == 1, i2 - float(ng),
                               jnp.where(lane_i == 2, w1, jnp.where(lane_i == 3, w2, 0.0))))


def _merge(attn, z, x, ctx_parts, ga, gs, mod, g2, w_glu, w_ba, w_bs, w_out, wr_hi, wr_lo, n_rows, t_lat, seq, batch):
    d = D_MODEL
    tile = MERGE_TILE
    assert n_rows % tile == 0 and t_lat % tile == 0 and seq % tile == 0
    n_tiles = n_rows // tile
    n_lat = t_lat // tile
    row = lambda i: (i, 0)
    lat = lambda i: (jnp.minimum(i, n_lat - 1), 0)
    cxt = lambda i: (jnp.maximum(i - n_lat, 0), 0)
    const = lambda i: (0, 0)
    modi = lambda i: (jnp.minimum(i // (seq // tile), batch), 0, 0)
    in_specs = [pl.BlockSpec((tile, Q_W), lat),
                pl.BlockSpec((SSM_LANE_TILES, tile, 128), lambda i: (0,) + lat(i)),
                pl.BlockSpec((tile, d), lat)]
    args = [attn, z, x]
    if ctx_parts is not None:
        in_specs += [pl.BlockSpec((tile, Q_W), cxt),
                     pl.BlockSpec((SSM_LANE_TILES, tile, 128), lambda i: (0,) + cxt(i)),
                     pl.BlockSpec((tile, d), cxt)]
        args += list(ctx_parts)
    in_specs += [
        pl.BlockSpec((tile, d), row),
        pl.BlockSpec((tile, d), row),
        pl.BlockSpec((1, 6, d), modi),
        pl.BlockSpec((1, d), const),
        pl.BlockSpec((D_SSM, D_SSM), const),
        pl.BlockSpec((Q_W, d), const),
        pl.BlockSpec((D_SSM, d), const),
        pl.BlockSpec((d, d), const),
        pl.BlockSpec((d, ROUTER_PAD), const),
        pl.BlockSpec((d, ROUTER_PAD), const),
    ]
    args += [ga, gs, mod, g2, w_glu, w_ba, w_bs, w_out, wr_hi, wr_lo]
    return pl.pallas_call(
        functools.partial(_merge_kernel, n_lat if ctx_parts is not None else None),
        grid=(n_tiles,),
        in_specs=in_specs,
        out_specs=[pl.BlockSpec((tile, d), row), pl.BlockSpec((tile, d), row),
                   pl.BlockSpec((tile, ROUTER_PAD), row), pl.BlockSpec((8, ROUTER_PAD), const)],
        out_shape=[jax.ShapeDtypeStruct((n_rows, d), F32), jax.ShapeDtypeStruct((n_rows, d), F32),
                   jax.ShapeDtypeStruct((n_rows, ROUTER_PAD), F32), jax.ShapeDtypeStruct((8, ROUTER_PAD), F32)],
        compiler_params=_cparams(("arbitrary",)),
        name="merge_router",
    )(*args)


def _expert_kernel(layer, be_ref, nu_ref, nx_ref, x_ref, wg_hbm, wu_hbm, wd_hbm, y_ref,
                   wg_f, wu_f, wd_f, wg_s, wu_s, wd_s, slot_s, sems):
    i = pl.program_id(0)
    e = be_ref[i]
    changed = jnp.logical_or(i == 0, e != be_ref[jnp.maximum(i - 1, 0)])

    def weight_copies(expert, slot):
        return [pltpu.make_async_copy(src.at[layer, expert], dst.at[slot], sems.at[slot, j])
                for j, (src, dst) in enumerate(((wg_hbm, wg_f), (wu_hbm, wu_f), (wd_hbm, wd_f)))]

    @pl.when(i == 0)
    def _():
        slot_s[0] = 0
        for cp in weight_copies(e, 0):
            cp.start()

    @pl.when(changed)
    def _():
        slot = slot_s[0]
        for cp in weight_copies(e, slot):
            cp.wait()
        nxt = nx_ref[e]

        @pl.when(nxt >= 0)
        def _():
            for cp in weight_copies(nxt, 1 - slot):
                cp.start()

        wg_s[...] = wg_f[slot].astype(BF16)
        wu_s[...] = wu_f[slot].astype(BF16)
        wd_s[...] = wd_f[slot].astype(BF16)
        slot_s[0] = 1 - slot

    @pl.when(i < nu_ref[0])
    def _():
        x = jnp.concatenate([x_ref[pl.ds(j, MOE_BLOCK, stride=ROW_SUBLANES), :] for j in range(ROW_SUBLANES)],
                            axis=1).astype(BF16)
        gate = _dot(x, wg_s[...])
        up = _dot(x, wu_s[...])
        hid = (gate * jax.nn.sigmoid(gate) * up).astype(BF16)
        y_ref[...] = _dot(hid, wd_s[...]).astype(y_ref.dtype)

    @pl.when(i >= nu_ref[0])
    def _():
        y_ref[...] = jnp.zeros_like(y_ref)


def _experts(blk_exp, n_used, next_exp, xs, layer, w_gate, w_up, w_down):
    d, de = w_gate.shape[-2:]
    assert d == ROW_SUBLANES * 128
    n_slots = xs.shape[0] // ROW_SUBLANES
    n_blk = n_slots // MOE_BLOCK
    return pl.pallas_call(
        functools.partial(_expert_kernel, layer),
        grid_spec=pltpu.PrefetchScalarGridSpec(
            num_scalar_prefetch=3,
            grid=(n_blk,),
            in_specs=[
                pl.BlockSpec((MOE_BLOCK * ROW_SUBLANES, 128), lambda i, be, nu, nx: (jnp.minimum(i, nu[0] - 1), 0)),
                pl.BlockSpec(memory_space=pl.ANY),
                pl.BlockSpec(memory_space=pl.ANY),
                pl.BlockSpec(memory_space=pl.ANY),
            ],
            out_specs=pl.BlockSpec((MOE_BLOCK, d), lambda i, be, nu, nx: (i, 0)),
            scratch_shapes=[pltpu.VMEM((2, d, de), F32), pltpu.VMEM((2, d, de), F32), pltpu.VMEM((2, de, d), F32),
                            pltpu.VMEM((d, de), BF16), pltpu.VMEM((d, de), BF16), pltpu.VMEM((de, d), BF16),
                            pltpu.SMEM((1,), jnp.int32), pltpu.SemaphoreType.DMA((2, 3))],
        ),
        out_shape=jax.ShapeDtypeStruct((n_slots, d), BF16),
        compiler_params=_cparams(("arbitrary",)),
        name="expert_mlp",
    )(blk_exp, n_used, next_exp, xs, w_gate, w_up, w_down)


def _expert_onehots(rt):
    lane_f = lax.broadcasted_iota(jnp.int32, rt.shape, 1).astype(F32)
    return jnp.where(lane_f == rt[:, 0:1], 1.0, 0.0), jnp.where(lane_f == rt[:, 1:2], 1.0, 0.0)


def _plan_kernel(rt_ref, cnt_ref, dest_ref, pend_ref, run_s):
    i = pl.program_id(0)
    rows = rt_ref.shape[0]
    lane = lax.broadcasted_iota(jnp.int32, (rows, ROUTER_PAD), 1)
    oh0, oh1 = _expert_onehots(rt_ref[...])
    tot0 = jnp.sum(oh0, axis=0, keepdims=True)
    tot1 = jnp.sum(oh1, axis=0, keepdims=True)

    @pl.when(i == 0)
    def _():
        counts = cnt_ref[...]
        padded = jnp.floor((counts + float(MOE_BLOCK - 1)) * (1.0 / MOE_BLOCK)) * float(MOE_BLOCK)
        r = lax.broadcasted_iota(jnp.int32, (ROUTER_PAD, ROUTER_PAD), 0)
        c = lax.broadcasted_iota(jnp.int32, (ROUTER_PAD, ROUTER_PAD), 1)
        incl = jnp.where(r <= c, 1.0, 0.0).astype(BF16)
        hi, lo = _split_bf16(padded)
        pend = _dot(hi, incl) + _dot(lo, incl)
        pend_ref[...] = pend
        run_s[...] = pend - padded

    r = lax.broadcasted_iota(jnp.int32, (rows, rows), 0)
    c = lax.broadcasted_iota(jnp.int32, (rows, rows), 1)
    before = jnp.where(c < r, 1.0, 0.0).astype(BF16)
    run = run_s[0:1, :]
    pos0 = jnp.sum(oh0 * (run + _dot(before, oh0.astype(BF16))), axis=-1, keepdims=True)
    pos1 = jnp.sum(oh1 * (run + tot0 + _dot(before, oh1.astype(BF16))), axis=-1, keepdims=True)
    run_s[0:1, :] = run + tot0 + tot1
    dest_ref[...] = jnp.where(lane == 0, pos0, jnp.where(lane == 1, pos1, 0.0)).astype(jnp.int32)


def _plan(route, counts, n_rows):
    rows = PLAN_TILE
    return pl.pallas_call(
        _plan_kernel,
        grid=(n_rows // rows,),
        in_specs=[pl.BlockSpec((rows, ROUTER_PAD), lambda i: (i, 0)),
                  pl.BlockSpec((8, ROUTER_PAD), lambda i: (0, 0))],
        out_specs=[pl.BlockSpec((rows, ROUTER_PAD), lambda i: (i, 0)),
                   pl.BlockSpec((8, ROUTER_PAD), lambda i: (0, 0))],
        out_shape=[jax.ShapeDtypeStruct((n_rows, ROUTER_PAD), jnp.int32),
                   jax.ShapeDtypeStruct((8, ROUTER_PAD), F32)],
        scratch_shapes=[pltpu.VMEM((8, ROUTER_PAD), F32)],
        compiler_params=_cparams(("arbitrary",)),
        name="dispatch_plan",
    )(route, counts)


def _dispatch_kernel(pend_ref, dest_ref, h_ref, xs_ref, hs, zero_s, sem, zsem):
    rows = h_ref.shape[0]
    sub = ROW_SUBLANES
    n_blk = xs_ref.shape[0] // (MOE_BLOCK * sub)
    n_used = pend_ref[N_EXPERTS - 1] // MOE_BLOCK

    def zero_block(start):
        return pltpu.make_async_copy(
            zero_s, xs_ref.at[pl.ds(pl.multiple_of(start * sub, MOE_BLOCK * sub), MOE_BLOCK * sub)], zsem)

    @pl.when(pl.program_id(0) == 0)
    def _():
        zero_s[...] = jnp.zeros_like(zero_s)
        for e in range(N_EXPERTS):
            @pl.when(pend_ref[e] > 0)
            def _():
                zero_block(pend_ref[e] - MOE_BLOCK).start()
        lax.fori_loop(n_used, n_blk, lambda j, c: (zero_block(j * MOE_BLOCK).start(), c)[1], 0)
        for e in range(N_EXPERTS):
            @pl.when(pend_ref[e] > 0)
            def _():
                zero_block(pend_ref[e] - MOE_BLOCK).wait()
        lax.fori_loop(n_used, n_blk, lambda j, c: (zero_block(j * MOE_BLOCK).wait(), c)[1], 0)

    for j in range(sub):
        hs[pl.ds(j, rows, stride=sub), :] = h_ref[:, 128 * j:128 * (j + 1)]

    def row_copy(r, d):
        return pltpu.make_async_copy(hs.at[pl.ds(pl.multiple_of(r * sub, sub), sub)],
                                     xs_ref.at[pl.ds(pl.multiple_of(d * sub, sub), sub)], sem)

    def issue(r, carry):
        for k in range(TOP_K):
            row_copy(r, dest_ref[0, 0, k * rows + r]).start(priority=k % 2)
        return carry

    lax.fori_loop(0, rows, issue, 0, unroll=8)
    for _ in range(TOP_K * rows):
        row_copy(0, 0).wait()


def _dispatch(pend, dest_tiles, h2, n_slots, n_tiles):
    d = h2.shape[1]
    return pl.pallas_call(
        _dispatch_kernel,
        grid_spec=pltpu.PrefetchScalarGridSpec(
            num_scalar_prefetch=1,
            grid=(n_tiles,),
            in_specs=[pl.BlockSpec((1, 1, TOP_K * ROW_TILE), lambda i, pe: (i, 0, 0), memory_space=pltpu.SMEM),
                      pl.BlockSpec((ROW_TILE, d), lambda i, pe: (i, 0))],
            out_specs=pl.BlockSpec(memory_space=pl.ANY),
            scratch_shapes=[pltpu.VMEM((ROW_TILE * ROW_SUBLANES, 128), F32),
                            pltpu.VMEM((MOE_BLOCK * ROW_SUBLANES, 128), F32),
                            pltpu.SemaphoreType.DMA, pltpu.SemaphoreType.DMA],
        ),
        out_shape=jax.ShapeDtypeStruct((n_slots * ROW_SUBLANES, 128), F32),
        compiler_params=_cparams(("arbitrary",)),
        name="dispatch_rows",
    )(pend, dest_tiles, h2)


def _moe(h2, route, counts, n_tiles, layer, w_gate, w_up, w_down):
    t, d = h2.shape
    dest, pend_f = _plan(route, counts, t)
    pend = pend_f[0, :N_EXPERTS].astype(jnp.int32)
    n_blk = -(-(t * TOP_K) // MOE_BLOCK) + N_EXPERTS
    blk_start = jnp.arange(n_blk, dtype=jnp.int32) * MOE_BLOCK
    blk_exp = jnp.minimum(jnp.sum((pend[None, :] <= blk_start[:, None]).astype(jnp.int32), axis=1), N_EXPERTS - 1)
    n_used = (pend[N_EXPERTS - 1] // MOE_BLOCK).reshape(1)
    blk_exp = jnp.where(jnp.arange(n_blk) < n_used[0], blk_exp, blk_exp[n_used[0] - 1])
    has_rows = pend > jnp.concatenate([jnp.zeros((1,), jnp.int32), pend[:-1]])
    e_ids = jnp.arange(N_EXPERTS, dtype=jnp.int32)
    later = jnp.where(jnp.logical_and(has_rows[None, :], e_ids[None, :] > e_ids[:, None]), e_ids[None, :], N_EXPERTS)
    next_exp = jnp.min(later, axis=1)
    next_exp = jnp.where(next_exp < N_EXPERTS, next_exp, -1).astype(jnp.int32)
    dest2 = dest[:, :TOP_K]
    dest_tiles = dest2.reshape(n_tiles, ROW_TILE, TOP_K).transpose(0, 2, 1).reshape(n_tiles, 1, TOP_K * ROW_TILE)
    xs = _dispatch(pend, dest_tiles, h2, n_blk * MOE_BLOCK, n_tiles)
    ys = _experts(blk_exp.astype(jnp.int32), n_used, next_exp, xs, layer, w_gate, w_up, w_down)
    return ys[dest2[:, 0]], ys[dest2[:, 1]]


def _final_kernel(x_ref, y0_ref, y1_ref, rt_ref, mod_ref, g_ref, o_ref):
    x = _moe_residual(x_ref, y0_ref, y1_ref, rt_ref, mod_ref)
    o_ref[...] = x * lax.rsqrt(jnp.mean(x * x, axis=-1, keepdims=True) + EPS) * g_ref[...]


def _final(x_lat, moe_out, mod, g_final, n_tiles, tiles_per_seq, batch):
    d = D_MODEL
    row = lambda i: (i, 0)
    return pl.pallas_call(
        _final_kernel,
        grid=(n_tiles,),
        in_specs=[pl.BlockSpec((ROW_TILE, d), row), pl.BlockSpec((ROW_TILE, d), row),
                  pl.BlockSpec((ROW_TILE, d), row), pl.BlockSpec((ROW_TILE, ROUTER_PAD), row),
                  pl.BlockSpec((1, 6, d), lambda i: (jnp.minimum(i // tiles_per_seq, batch), 0, 0)),
                  pl.BlockSpec((1, d), lambda i: (0, 0))],
        out_specs=pl.BlockSpec((ROW_TILE, d), row),
        out_shape=jax.ShapeDtypeStruct((n_tiles * ROW_TILE, d), F32),
        compiler_params=_cparams(("parallel",)),
        name="final_norm",
    )(x_lat, *moe_out, mod, g_final)


def _rope_tables(seq):
    quarter = HEAD_DIM // 4
    freqs = ROPE_THETA ** (-jnp.arange(quarter, dtype=F32) / quarter)
    pos = jnp.arange(seq)
    ang_r = (pos // GRID_W).astype(F32)[:, None] * freqs[None, :]
    ang_c = (pos % GRID_W).astype(F32)[:, None] * freqs[None, :]
    cos = jnp.concatenate([jnp.cos(ang_r)] * 2 + [jnp.cos(ang_c)] * 2, axis=-1)
    sin = jnp.concatenate([-jnp.sin(ang_r), jnp.sin(ang_r), -jnp.sin(ang_c), jnp.sin(ang_c)], axis=-1)
    reps = 128 // HEAD_DIM
    cos = jnp.tile(cos, (1, reps))
    sin = jnp.tile(sin, (1, reps))
    cos = jnp.concatenate([cos, jnp.ones((ROW_TILE, 128), F32)], axis=0)
    sin = jnp.concatenate([sin, jnp.zeros((ROW_TILE, 128), F32)], axis=0)
    return cos, sin


def kernel(x, c, ctx, c_ctx, w_mod, b_mod, g_norm1, g_norm2, w_in, attn_sink, ssm_lam_re, ssm_lam_im,
           ssm_log_dt, ssm_b_re, ssm_b_im, ssm_c_re, ssm_c_im, ssm_d, w_glu, w_br_attn, w_br_ssm, w_out,
           w_router_group, w_router_expert, w_exp_gate, w_exp_up, w_exp_down, g_final):
    batch, seq, d = x.shape
    ctx_len = ctx.shape[1]
    depth = w_mod.shape[0]
    assert d == D_MODEL and batch * SSM_STATE == 128
    assert seq % ROW_TILE == 0 and ctx_len % ROW_TILE == 0 and (batch * seq) % ctx_len == 0
    assert batch + 1 <= MOD_ROWS
    t_lat, t_ctx = batch * seq, batch * ctx_len
    tiles_per_seq = seq // ROW_TILE
    n_lat_tiles = t_lat // ROW_TILE
    n_all_tiles = (t_lat + t_ctx) // ROW_TILE
    nb_per_seq = seq // ATTN_BLOCK
    ncc, ncl = ctx_len // SSM_CHUNK, seq // SSM_CHUNK

    c_rows = jnp.zeros((MOD_ROWS, d), F32).at[:batch].set(c).at[batch].set(c_ctx)
    mod_all = _modulation(c_rows, w_mod, b_mod).reshape(depth, MOD_ROWS, 6, d)
    cos_t, sin_t = _rope_tables(seq)
    ssm_w = _ssm_weights_all(ssm_lam_re, ssm_lam_im, ssm_log_dt, ssm_b_re, ssm_b_im, ssm_c_re, ssm_c_im, ssm_d, batch)
    x_parts = (x.reshape(t_lat, d), ctx.reshape(t_ctx, d))
    f_all = None
    for l in range(depth):
        ctx_out = l < depth - 1
        mod = mod_all[l]
        x_parts, (q, k, v, u, ga, gs) = _inproj(
            x_parts, f_all, mod_all[l - 1] if l > 0 else None, mod, g_norm1[l].reshape(1, d), cos_t, sin_t,
            w_in[l].astype(BF16), n_all_tiles, tiles_per_seq, n_lat_tiles, batch)
        sink = attn_sink[l].astype(F32)
        attn = _attention(sink, q, k, v, t_lat // ATTN_BLOCK, nb_per_seq, t_lat // ctx_len, ctx_len, True, 0)
        z, z_ctx = _ssm_mixer(u, l, *ssm_w, batch, seq, ctx_len)
        ctx_parts = None
        if ctx_out:
            attn_c = _attention(sink, q, k, v, t_ctx // ATTN_BLOCK, ctx_len // ATTN_BLOCK, t_lat // ctx_len,
                                ctx_len, False, t_lat // ATTN_BLOCK)
            ctx_parts = (attn_c, z_ctx, x_parts[1])
        w_r = jnp.zeros((d, ROUTER_PAD), F32)
        w_r = w_r.at[:, :N_EXPERT_GROUPS].set(w_router_group[l])
        w_r = w_r.at[:, N_EXPERT_GROUPS:N_EXPERT_GROUPS + N_EXPERTS].set(w_router_expert[l])
        wr_hi, wr_lo = _split_bf16(w_r)
        n_rows = t_lat + t_ctx if ctx_out else t_lat
        x_all, h2, route, counts = _merge(
            attn, z, x_parts[0], ctx_parts, ga, gs, mod, g_norm2[l].reshape(1, d), w_glu[l].astype(BF16),
            w_br_attn[l].astype(BF16), w_br_ssm[l].astype(BF16), w_out[l].astype(BF16),
            wr_hi, wr_lo, n_rows, t_lat, seq, batch)
        y0, y1 = _moe(h2, route, counts, n_rows // ROW_TILE, l, w_exp_gate, w_exp_up, w_exp_down)
        f_all = (y0, y1, route)
        x_parts = (x_all,)
    out = _final(x_all, f_all, mod_all[depth - 1], g_final.reshape(1, d), n_lat_tiles, tiles_per_seq, batch)
    return out.reshape(batch, seq, d)
```

```python
import functools
import math

import jax
import jax.numpy as jnp
from jax import lax
from jax.experimental import pallas as pl
from jax.experimental.pallas import tpu as pltpu

F32 = jnp.float32
BF16 = jnp.bfloat16

D_MODEL = 1024
GRID_W = 64
N_HEADS = 8
N_KV_HEADS = 2
HEAD_DIM = 64
Q_GROUP = N_HEADS // N_KV_HEADS
ATTN_BLOCK = 128
ATTN_STEP_BLOCKS = 4
ROPE_THETA = 10000.0
D_SSM = D_MODEL // 2
SSM_GROUP = 16
N_SSM_GROUPS = D_SSM // SSM_GROUP
SSM_STATE = 64
N_EXPERT_GROUPS = 4
EXPERTS_PER_GROUP = 8
N_EXPERTS = N_EXPERT_GROUPS * EXPERTS_PER_GROUP
TOP_K = 2
D_EXPERT = D_MODEL // 2
Q_W = N_HEADS * HEAD_DIM
KV_W = N_KV_HEADS * HEAD_DIM
O_K = Q_W
O_V = O_K + KV_W
O_U = O_V + KV_W
O_GA = O_U + D_SSM
O_GS = O_GA + D_MODEL
D_IN = O_GS + D_MODEL
EPS = 1e-6
NEG_INF = -1e30

ROW_TILE = 256
MERGE_TILE = 512
MERGE_SUB = 256
SSM_CHUNK = 16
SSM_CW = SSM_CHUNK * SSM_GROUP
SSM_SCAN_GROUPS = 8
SSM_LANE_GROUPS = 128 // SSM_GROUP
SSM_LANE_TILES = D_SSM // 128
MOE_BLOCK = 256
ROW_SUBLANES = D_MODEL // 128
PLAN_TILE = 512
ROUTER_PAD = 128
MOD_ROWS = 8
VMEM_LIMIT = 48 * 1024 * 1024


def _cparams(sem):
    return pltpu.CompilerParams(dimension_semantics=sem, vmem_limit_bytes=VMEM_LIMIT)


def _dot(a, b):
    return jnp.dot(a, b, preferred_element_type=F32)


def _split_bf16(a):
    hi = a.astype(BF16)
    lo = (a - hi.astype(F32)).astype(BF16)
    return hi, lo


def _rms_mod(x, g, shift, scale):
    y = x * lax.rsqrt(jnp.mean(x * x, axis=-1, keepdims=True) + EPS) * g
    return y * (1.0 + scale) + shift


def _mod_kernel(c_ref, w_ref, b_ref, o_ref):
    c = c_ref[...]
    s_hi, s_lo = _split_bf16(c * jax.nn.sigmoid(c))
    w_hi, w_lo = _split_bf16(w_ref[0])
    o_ref[0] = _dot(s_hi, w_hi) + _dot(s_lo, w_hi) + _dot(s_hi, w_lo) + b_ref[0]


def _modulation(c_rows, w_mod, b_mod):
    depth, d, n = w_mod.shape
    nb = n // 4
    return pl.pallas_call(
        _mod_kernel,
        grid=(depth, n // nb),
        in_specs=[
            pl.BlockSpec((MOD_ROWS, d), lambda l, j: (0, 0)),
            pl.BlockSpec((1, d, nb), lambda l, j: (l, 0, j)),
            pl.BlockSpec((1, 1, nb), lambda l, j: (l, 0, j)),
        ],
        out_specs=pl.BlockSpec((1, MOD_ROWS, nb), lambda l, j: (l, 0, j)),
        out_shape=jax.ShapeDtypeStruct((depth, MOD_ROWS, n), F32),
        compiler_params=_cparams(("arbitrary", "arbitrary")),
        name="modulation",
    )(c_rows, w_mod, b_mod.reshape(depth, 1, n))


def _moe_residual(x_ref, y0_ref, y1_ref, rt_ref, mod_ref):
    rt = rt_ref[...]
    f = rt[:, 2:3] * y0_ref[...].astype(F32) + rt[:, 3:4] * y1_ref[...].astype(F32)
    return x_ref[...] + mod_ref[0, 5:6, :] * f


def _inproj_kernel(has_f, n_lat_tiles, *refs):
    if has_f:
        (x_ref, y0_ref, y1_ref, rt_ref, modp_ref, mod_ref, g_ref, cos_ref, sin_ref, w_ref,
         xo_ref, q_ref, k_ref, v_ref, u_ref, ga_ref, gs_ref) = refs
        x = _moe_residual(x_ref, y0_ref, y1_ref, rt_ref, modp_ref)
        xo_ref[...] = x
    else:
        (xl_ref, xc_ref, mod_ref, g_ref, cos_ref, sin_ref, w_ref,
         q_ref, k_ref, v_ref, u_ref, ga_ref, gs_ref) = refs
        x = jnp.where(pl.program_id(0) >= n_lat_tiles, xc_ref[...], xl_ref[...])
    m = mod_ref[0]
    h = _rms_mod(x, g_ref[...], m[0:1], m[1:2]).astype(BF16)
    cos = cos_ref[...]
    sin = sin_ref[...]
    lane = lax.broadcasted_iota(jnp.int32, cos.shape, 1)
    first = (lane % (HEAD_DIM // 2)) < (HEAD_DIM // 4)

    def rope(t):
        sw = jnp.where(first, pltpu.roll(t, 128 - HEAD_DIM // 4, 1), pltpu.roll(t, HEAD_DIM // 4, 1))
        return t * cos + sw * sin

    def proj(lo, hi):
        return _dot(h, w_ref[:, lo:hi])

    q = proj(0, O_K)
    for j in range(Q_W // 128):
        q_ref[:, 128 * j:128 * (j + 1)] = (rope(q[:, 128 * j:128 * (j + 1)]) * HEAD_DIM ** -0.5).astype(BF16)
    kv = proj(O_K, O_U)
    k_ref[...] = rope(kv[:, :KV_W]).astype(BF16)
    v_ref[...] = kv[:, KV_W:].astype(BF16)
    u = proj(O_U, O_GA)
    for j in range(SSM_LANE_TILES):
        u_ref[j] = u[:, 128 * j:128 * (j + 1)]
    ga_ref[...] = jax.nn.sigmoid(proj(O_GA, O_GS)).astype(BF16)
    gs_ref[...] = jax.nn.sigmoid(proj(O_GS, D_IN)).astype(BF16)


def _inproj(x_parts, moe_out, mod_prev, mod, g1, cos_t, sin_t, w_in, n_tiles, tiles_per_seq, n_lat_tiles, batch):
    d = D_MODEL
    has_f = moe_out is not None
    row = lambda i: (i, 0)
    modi = lambda i: (jnp.minimum(i // tiles_per_seq, batch), 0, 0)
    const = lambda i: (0, 0)
    ropei = lambda i: (jnp.where(i < n_lat_tiles, i % tiles_per_seq, tiles_per_seq), 0)
    if has_f:
        in_specs = [pl.BlockSpec((ROW_TILE, d), row), pl.BlockSpec((ROW_TILE, d), row),
                    pl.BlockSpec((ROW_TILE, d), row), pl.BlockSpec((ROW_TILE, ROUTER_PAD), row),
                    pl.BlockSpec((1, 6, d), modi)]
        args = [*x_parts, *moe_out, mod_prev]
    else:
        in_specs = [pl.BlockSpec((ROW_TILE, d), lambda i: (jnp.minimum(i, n_lat_tiles - 1), 0)),
                    pl.BlockSpec((ROW_TILE, d), lambda i: (jnp.maximum(i - n_lat_tiles, 0), 0))]
        args = list(x_parts)
    in_specs += [
        pl.BlockSpec((1, 6, d), modi),
        pl.BlockSpec((1, d), const),
        pl.BlockSpec((ROW_TILE, 128), ropei),
        pl.BlockSpec((ROW_TILE, 128), ropei),
        pl.BlockSpec((d, D_IN), const),
    ]
    args += [mod, g1, cos_t, sin_t, w_in]
    widths = [Q_W, KV_W, KV_W, D_SSM, D_MODEL, D_MODEL]
    out_specs = [pl.BlockSpec((ROW_TILE, w), row) for w in widths]
    out_shape = [jax.ShapeDtypeStruct((n_tiles * ROW_TILE, w), BF16) for w in widths]
    out_specs[3] = pl.BlockSpec((SSM_LANE_TILES, ROW_TILE, 128), lambda i: (0, i, 0))
    out_shape[3] = jax.ShapeDtypeStruct((SSM_LANE_TILES, n_tiles * ROW_TILE, 128), F32)
    if has_f:
        out_specs = [pl.BlockSpec((ROW_TILE, d), row)] + out_specs
        out_shape = [jax.ShapeDtypeStruct((n_tiles * ROW_TILE, d), F32)] + out_shape
    outs = pl.pallas_call(
        functools.partial(_inproj_kernel, has_f, n_lat_tiles),
        grid=(n_tiles,),
        in_specs=in_specs,
        out_specs=out_specs,
        out_shape=out_shape,
        compiler_params=_cparams(("parallel",)),
        name="inproj",
    )(*args)
    if has_f:
        return (outs[0],), outs[1:]
    return x_parts, outs


def _attn_scores(q, k_tiles, v_tiles):
    res = []
    for g in range(N_KV_HEADS):
        gs = slice(g * HEAD_DIM, (g + 1) * HEAD_DIM)
        k_all = jnp.concatenate([t[:, gs] for t in k_tiles], axis=0)
        v_all = jnp.concatenate([t[:, gs] for t in v_tiles], axis=0)
        v_ext = jnp.concatenate([v_all, jnp.ones_like(v_all)], axis=1)
        qg = jnp.concatenate(
            [q[:, (g * Q_GROUP + h) * HEAD_DIM:(g * Q_GROUP + h + 1) * HEAD_DIM] for h in range(Q_GROUP)], axis=0)
        res.append((lax.dot_general(k_all, qg, (((1,), (1,)), ((), ())), preferred_element_type=F32), v_ext))
    return res


def _attn_finish(sink_ref, scores, tile_rows, biases, blk):
    outs = []
    for g, (st, v_ext) in enumerate(scores):
        tiles, row = [], 0
        for n, bias in zip(tile_rows, biases):
            t = st[row:row + n, :]
            tiles.append(t if bias is None else t + jnp.concatenate([bias] * Q_GROUP, axis=1))
            row += n
        mx = None
        for t in tiles:
            for r0 in range(0, t.shape[0], blk):
                mx = t[r0:r0 + blk] if mx is None else jnp.maximum(mx, t[r0:r0 + blk])
        sink = jnp.concatenate([jnp.full((1, blk), sink_ref[g * Q_GROUP + h], F32) for h in range(Q_GROUP)], axis=1)
        m = jnp.maximum(jnp.max(mx, axis=0, keepdims=True), sink)
        p = jnp.exp(jnp.concatenate([(t - m).astype(BF16) for t in tiles], axis=0))
        o_t = lax.dot_general(v_ext, p, (((0,), (0,)), ((), ())), preferred_element_type=F32)
        denom = o_t[HEAD_DIM:HEAD_DIM + 1, :] + jnp.exp(sink - m)
        o_n = o_t[:HEAD_DIM, :] * (1.0 / denom)
        for h in range(Q_GROUP):
            outs.append(o_n[:, h * blk:(h + 1) * blk].T.astype(BF16))
    return jnp.concatenate(outs, axis=1)


def _attn_kernel(band, nb_per_seq, sink_ref, *refs):
    blk = ATTN_BLOCK
    if not band:
        q_ref, kx_ref, vx_ref, o_ref = refs
        n_sub = q_ref.shape[0] // blk
        scores = [_attn_scores(q_ref[a * blk:(a + 1) * blk, :], [kx_ref[...]], [vx_ref[...]]) for a in range(n_sub)]
        for a in range(n_sub):
            o_ref[a * blk:(a + 1) * blk, :] = _attn_finish(sink_ref, scores[a], [kx_ref.shape[0]], [None], blk)
        return
    q_ref, kp_ref, kc_ref, kn_ref, vp_ref, vc_ref, vn_ref, kx_ref, vx_ref, o_ref = refs
    n_sub = q_ref.shape[0] // blk
    j0 = (pl.program_id(0) * n_sub) % nb_per_seq
    r = lax.broadcasted_iota(jnp.int32, (blk, blk), 0)
    c = lax.broadcasted_iota(jnp.int32, (blk, blk), 1)
    k_blocks = [kp_ref[...]] + [kc_ref[a * blk:(a + 1) * blk, :] for a in range(n_sub)] + [kn_ref[...]]
    v_blocks = [vp_ref[...]] + [vc_ref[a * blk:(a + 1) * blk, :] for a in range(n_sub)] + [vn_ref[...]]
    scores = [_attn_scores(q_ref[a * blk:(a + 1) * blk, :], k_blocks[a:a + 3] + [kx_ref[...]],
                           v_blocks[a:a + 3] + [vx_ref[...]]) for a in range(n_sub)]
    for a in range(n_sub):
        edge_p = jnp.where(j0 + a > 0, 0.0, NEG_INF).astype(F32)
        edge_n = jnp.where(j0 + a < nb_per_seq - 1, 0.0, NEG_INF).astype(F32)
        bias_p = jnp.where(r >= c, edge_p, NEG_INF).astype(F32)
        bias_n = jnp.where(r <= c, edge_n, NEG_INF).astype(F32)
        o_ref[a * blk:(a + 1) * blk, :] = _attn_finish(
            sink_ref, scores[a], [blk, blk, blk, kx_ref.shape[0]], [bias_p, None, bias_n, None], blk)


def _attention(sink, q, k, v, n_blocks, nb_per_seq, kx_block0, ctx_len, band, q_block0):
    blk = ATTN_BLOCK
    n_sub = math.gcd(ATTN_STEP_BLOCKS, nb_per_seq, n_blocks, q_block0)
    step = n_sub * blk
    n_steps = n_blocks // n_sub
    qi = lambda i, s: (q_block0 // n_sub + i, 0)
    cur = lambda i, s: (i, 0)
    prv = lambda i, s: (jnp.maximum(i * n_sub - 1, 0), 0)
    nxt = lambda i, s: (jnp.minimum((i + 1) * n_sub, n_blocks - 1), 0)
    kxi = lambda i, s: (kx_block0 + (i * n_sub) // nb_per_seq, 0)
    kspec = lambda f: pl.BlockSpec((blk, KV_W), f)
    cspec = pl.BlockSpec((step, KV_W), cur)
    xspec = pl.BlockSpec((ctx_len, KV_W), kxi)
    if band:
        in_specs = [pl.BlockSpec((step, Q_W), qi), kspec(prv), cspec, kspec(nxt),
                    kspec(prv), cspec, kspec(nxt), xspec, xspec]
        args = (q, k, k, k, v, v, v, k, v)
    else:
        in_specs = [pl.BlockSpec((step, Q_W), qi), xspec, xspec]
        args = (q, k, v)
    return pl.pallas_call(
        functools.partial(_attn_kernel, band, nb_per_seq),
        grid_spec=pltpu.PrefetchScalarGridSpec(
            num_scalar_prefetch=1,
            grid=(n_steps,),
            in_specs=in_specs,
            out_specs=pl.BlockSpec((step, Q_W), lambda i, s: (i, 0)),
        ),
        out_shape=jax.ShapeDtypeStruct((n_blocks * blk, Q_W), BF16),
        compiler_params=_cparams(("parallel",)),
        name="band_attention" if band else "context_attention",
    )(sink, *args)


def _dot_f32(a, b_t):
    a_hi, a_lo = _split_bf16(a)
    b_hi, b_lo = _split_bf16(b_t)
    dn = (((1,), (1,)), ((), ()))
    dg = functools.partial(lax.dot_general, dimension_numbers=dn, preferred_element_type=F32)
    return dg(a_hi, b_hi) + dg(a_lo, b_hi) + dg(a_hi, b_lo)


def _ssm_weight_kernel(batch, lam_r_ref, bt_ref, c_ref, d_ref, w1_ref, abar_ref, t_ref, wo_ref):
    lc, mm, p = SSM_CHUNK, SSM_GROUP, SSM_STATE
    up_r = lax.broadcasted_iota(jnp.int32, (lc, 1), 0).astype(F32)
    lane = lax.broadcasted_iota(jnp.int32, (mm, lc * mm), 1)
    row = lax.broadcasted_iota(jnp.int32, (mm, lc * mm), 0)
    planes_in, planes_out, abar_rows, kt = [], [], [], []
    for d in range(2):
        lr, li, dt = lam_r_ref[0, d, 0:1, :], lam_r_ref[0, d, 1:2, :], jnp.exp(lam_r_ref[0, d, 2:3, :])

        def powers(expo):
            mag = jnp.exp(lr * dt * expo)
            return mag * jnp.cos(li * dt * expo), mag * jnp.sin(li * dt * expo)

        a_re, a_im = powers(jnp.ones((1, 1), F32))
        den = lr * lr + li * li
        nr = a_re - 1.0
        f_re = (nr * lr + a_im * li) / den
        f_im = (a_im * lr - nr * li) / den
        bt_re, bt_im = bt_ref[0, d, 0], bt_ref[0, d, 1]
        bbt_re = f_re * bt_re - f_im * bt_im
        bbt_im = f_re * bt_im + f_im * bt_re
        c_re, c_im = c_ref[0, d, 0], c_ref[0, d, 1]

        pr, pi = powers(lc - 1.0 - up_r if d == 0 else up_r)
        planes_in.append(jnp.concatenate(
            [pr[s:s + 1] * bbt_re - pi[s:s + 1] * bbt_im for s in range(lc)], axis=0))
        planes_in.append(jnp.concatenate(
            [pr[s:s + 1] * bbt_im + pi[s:s + 1] * bbt_re for s in range(lc)], axis=0))

        qr, qi = powers(up_r + 1.0 if d == 0 else lc - up_r)
        planes_out.append(jnp.concatenate(
            [c_re * qr[t:t + 1] - c_im * qi[t:t + 1] for t in range(lc)], axis=0).T)
        planes_out.append(jnp.concatenate(
            [-c_re * qi[t:t + 1] - c_im * qr[t:t + 1] for t in range(lc)], axis=0).T)

        kr, ki = powers(up_r if d == 0 else lc - 1.0 - up_r)
        cp_re = jnp.concatenate([c_re * kr[j:j + 1] - c_im * ki[j:j + 1] for j in range(lc)], axis=0)
        cp_im = jnp.concatenate([c_re * ki[j:j + 1] + c_im * kr[j:j + 1] for j in range(lc)], axis=0)
        kt.append(_dot_f32(jnp.concatenate([bbt_re, bbt_im], axis=1), jnp.concatenate([cp_re, -cp_im], axis=1)))

        e_re, e_im = powers(jnp.full((1, 1), float(lc), F32))
        abar_rows += [jnp.concatenate([e_re] * batch, axis=1), jnp.concatenate([e_im] * batch, axis=1)]

    dvec = d_ref[0]
    blocks = []
    for s in range(lc):
        fwd = kt[0] if s == 0 else pltpu.roll(kt[0], mm * s, 1)
        bwd = kt[1] if s == lc - 1 else pltpu.roll(kt[1], mm * (s + 1), 1)
        blk = jnp.where(lane >= mm * s, fwd, 0.0) + jnp.where(lane < mm * (s + 1), bwd, 0.0)
        blocks.append(blk + jnp.where(lane == mm * s + row, dvec, 0.0))
    t_ref[0] = jnp.concatenate(blocks, axis=0).astype(BF16)
    abar_ref[0] = jnp.concatenate(abar_rows, axis=0)
    zero_in = jnp.zeros((lc * mm, p), F32)
    zero_out = jnp.zeros((p, lc * mm), F32)
    for b in range(batch):
        w1_ref[0, b] = jnp.concatenate(
            [pl_ if bb == b else zero_in for pl_ in planes_in for bb in range(batch)], axis=1).astype(BF16)
        wo_ref[0, b] = jnp.concatenate(
            [pl_ if bb == b else zero_out for pl_ in planes_out for bb in range(batch)], axis=0).astype(BF16)


def _ssm_weights_all(lam_re, lam_im, log_dt, b_re, b_im, c_re, c_im, d_skip, batch):
    depth = lam_re.shape[0]
    g, p, mm, lc = N_SSM_GROUPS, SSM_STATE, SSM_GROUP, SSM_CHUNK
    n = depth * g
    cw, sw = lc * mm, 4 * batch * p

    def per_group(a):
        return jnp.moveaxis(a.astype(F32), 2, 1).reshape((n, 2) + a.shape[3:])

    lam = jnp.stack([per_group(lam_re), per_group(lam_im),
                     jnp.broadcast_to(per_group(log_dt)[..., None], (n, 2, p))], axis=2)
    bt = jnp.stack([per_group(b_re), per_group(b_im)], axis=2).swapaxes(-1, -2)
    c = jnp.stack([per_group(c_re), per_group(c_im)], axis=2)
    d_t = jnp.tile(d_skip.astype(F32).reshape(n, 1, mm), (1, 1, lc))
    full = lambda *shape: pl.BlockSpec((1,) + shape, lambda i: (i,) + (0,) * len(shape))
    w1, abar, tmat, wout = pl.pallas_call(
        functools.partial(_ssm_weight_kernel, batch),
        grid=(n,),
        in_specs=[full(2, 3, p), full(2, 2, mm, p), full(2, 2, mm, p), full(1, cw)],
        out_specs=[full(batch, cw, sw), full(4, batch * p), full(cw, cw), full(batch, sw, cw)],
        out_shape=[jax.ShapeDtypeStruct((n, batch, cw, sw), BF16), jax.ShapeDtypeStruct((n, 4, batch * p), F32),
                   jax.ShapeDtypeStruct((n, cw, cw), BF16), jax.ShapeDtypeStruct((n, batch, sw, cw), BF16)],
        compiler_params=_cparams(("parallel",)),
        name="ssm_weights",
    )(lam, bt, c, d_t)
    abar = abar.reshape(depth, g, 4, batch * p).transpose(0, 2, 1, 3)
    return (w1.reshape(depth, g, batch, cw, sw), abar, tmat.reshape(depth, g, cw, cw),
            wout.reshape(depth, g, batch, sw, cw))


def _ssm_p1_kernel(ul_ref, uc_ref, w_ref, v_ref, *s_refs):
    b = pl.program_id(1)
    ncc, ncl = uc_ref.shape[0] // SSM_CHUNK, ul_ref.shape[0] // SSM_CHUNK
    nc = ncc + ncl
    pw = s_refs[0].shape[1]
    xs = [jnp.concatenate([uc_ref[pl.ds(s, ncc, stride=SSM_CHUNK), :], ul_ref[pl.ds(s, ncl, stride=SSM_CHUNK), :]],
                          axis=0).astype(BF16) for s in range(SSM_CHUNK)]
    for j in range(SSM_LANE_GROUPS):
        v = jnp.concatenate([x[:, j * SSM_GROUP:(j + 1) * SSM_GROUP] for x in xs], axis=1)
        v_ref[j] = v
        acc = _dot(v, w_ref[j])
        for k, s_ref in enumerate(s_refs):
            part = acc[:, k * pw:(k + 1) * pw]
            rows = slice(j * nc, (j + 1) * nc)

            @pl.when(b == 0)
            def _():
                s_ref[rows, :] = part

            @pl.when(b > 0)
            def _():
                s_ref[rows, :] = s_ref[rows, :] + part


def _ssm_scan_kernel(ncc, ncl, a_ref, sfr_ref, sfi_ref, sbr_ref, sbi_ref, xfr_ref, xfi_ref, xbr_ref, xbi_ref):
    nc = ncc + ncl
    gb = a_ref.shape[1]
    pw = a_ref.shape[2]
    afr, afi, abr, abi = a_ref[0], a_ref[1], a_ref[2], a_ref[3]

    def rows(r):
        return pl.ds(r, gb, stride=nc)

    def step(rf, rb, carry):
        xfr, xfi, xbr, xbi = carry
        xfr_ref[rows(rf), :] = xfr
        xfi_ref[rows(rf), :] = xfi
        xbr_ref[rows(rb), :] = xbr
        xbi_ref[rows(rb), :] = xbi
        sfr = sfr_ref[rows(rf), :]
        sfi = sfi_ref[rows(rf), :]
        sbr = sbr_ref[rows(rb), :]
        sbi = sbi_ref[rows(rb), :]
        return (afr * xfr - afi * xfi + sfr, afr * xfi + afi * xfr + sfi,
                abr * xbr - abi * xbi + sbr, abr * xbi + abi * xbr + sbi)

    zero = jnp.zeros((gb, pw), F32)
    carry = lax.fori_loop(0, ncc, lambda t, c: step(t, ncc - 1 - t, c), (zero, zero, zero, zero))
    lax.fori_loop(0, ncl, lambda t, c: step(ncc + t, nc - 1 - t, c), carry)


def _ssm_p3_kernel(v_ref, xfr_ref, xfi_ref, xbr_ref, xbi_ref, t_ref, wo_ref, zl_ref, zc_ref):
    ncc, ncl = zc_ref.shape[0] // SSM_CHUNK, zl_ref.shape[0] // SSM_CHUNK
    nc = ncc + ncl
    ys = []
    for j in range(SSM_LANE_GROUPS):
        rows = slice(j * nc, (j + 1) * nc)
        xin = jnp.concatenate([xfr_ref[rows, :], xfi_ref[rows, :], xbr_ref[rows, :], xbi_ref[rows, :]],
                              axis=1).astype(BF16)
        y = _dot(v_ref[j], t_ref[j]) + _dot(xin, wo_ref[j])
        ys.append(jax.nn.gelu(y, approximate=True))
    for t in range(SSM_CHUNK):
        zt = jnp.concatenate([y[:, t * SSM_GROUP:(t + 1) * SSM_GROUP] for y in ys], axis=1)
        zc_ref[pl.ds(t, ncc, stride=SSM_CHUNK), :] = zt[:ncc]
        zl_ref[pl.ds(t, ncl, stride=SSM_CHUNK), :] = zt[ncc:]


def _ssm_mixer(u4, layer, w1, abar, tmat, wout, batch, seq, ctx_len):
    n_lt = u4.shape[0]
    g = N_SSM_GROUPS
    lg = SSM_LANE_GROUPS
    ncc, ncl = ctx_len // SSM_CHUNK, seq // SSM_CHUNK
    nc = ncc + ncl
    cw = SSM_CW
    sw = w1.shape[-1]
    pw = sw // 4
    ctx_blk0 = (batch * seq) // ctx_len
    plane_shape = [jax.ShapeDtypeStruct((g * nc, pw), F32)] * 4
    lat_spec = pl.BlockSpec((None, seq, 128), lambda k, b: (k, b, 0))
    ctx_spec = pl.BlockSpec((None, ctx_len, 128), lambda k, b: (k, ctx_blk0 + b, 0))
    v_spec = pl.BlockSpec((lg, None, nc, cw), lambda k, b: (k, b, 0, 0))
    plane_spec = pl.BlockSpec((lg * nc, pw), lambda k, b: (k, 0))
    v_chunks, *s_planes = pl.pallas_call(
        _ssm_p1_kernel,
        grid=(n_lt, batch),
        in_specs=[lat_spec, ctx_spec,
                  pl.BlockSpec((None, lg, None, cw, sw), lambda k, b: (layer, k, b, 0, 0))],
        out_specs=[v_spec] + [plane_spec] * 4,
        out_shape=[jax.ShapeDtypeStruct((g, batch, nc, cw), BF16)] + plane_shape,
        compiler_params=_cparams(("parallel", "arbitrary")),
        name="ssm_chunk_states",
    )(u4, u4, w1)
    gb = SSM_SCAN_GROUPS
    x_planes = pl.pallas_call(
        functools.partial(_ssm_scan_kernel, ncc, ncl),
        grid=(g // gb,),
        in_specs=[pl.BlockSpec((None, 4, gb, pw), lambda i: (layer, 0, i, 0))]
        + [pl.BlockSpec((gb * nc, pw), lambda i: (i, 0))] * 4,
        out_specs=[pl.BlockSpec((gb * nc, pw), lambda i: (i, 0))] * 4,
        out_shape=plane_shape,
        compiler_params=_cparams(("parallel",)),
        name="ssm_chunk_scan",
    )(abar, *s_planes)
    return pl.pallas_call(
        _ssm_p3_kernel,
        grid=(n_lt, batch),
        in_specs=[v_spec] + [plane_spec] * 4
        + [pl.BlockSpec((None, lg, cw, cw), lambda k, b: (layer, k, 0, 0)),
           pl.BlockSpec((None, lg, None, sw, cw), lambda k, b: (layer, k, b, 0, 0))],
        out_specs=[pl.BlockSpec((None, seq, 128), lambda k, b: (k, b, 0)),
                   pl.BlockSpec((None, ctx_len, 128), lambda k, b: (k, b, 0))],
        out_shape=[jax.ShapeDtypeStruct((n_lt, batch * seq, 128), F32),
                   jax.ShapeDtypeStruct((n_lt, batch * ctx_len, 128), F32)],
        compiler_params=_cparams(("parallel", "arbitrary")),
        name="ssm_chunk_outputs",
    )(v_chunks, *x_planes, tmat, wout)


def _merge_kernel(n_lat_tiles, *refs):
    if n_lat_tiles is None:
        attn_ref, z_ref, x_ref = refs[:3]
        refs = refs[3:]
    else:
        attn_ref, z_ref, x_ref, attn_c_ref, z_c_ref, x_c_ref = refs[:6]
        refs = refs[6:]
        is_ctx = pl.program_id(0) >= n_lat_tiles
    (ga_ref, gs_ref, mod_ref, g2_ref, wglu_ref, wba_ref, wbs_ref, wout_ref, wrh_ref, wrl_ref,
     xo_ref, h2_ref, rt_ref, cnt_ref) = refs
    m = mod_ref[0]
    tile_counts = None
    for r in range(MERGE_TILE // MERGE_SUB):
        rows = slice(r * MERGE_SUB, (r + 1) * MERGE_SUB)
        zf = jnp.concatenate([z_ref[j, rows, :] for j in range(SSM_LANE_TILES)], axis=1)
        attn = attn_ref[rows, :]
        x_in = x_ref[rows, :]
        if n_lat_tiles is not None:
            zf = jnp.where(is_ctx, jnp.concatenate([z_c_ref[j, rows, :] for j in range(SSM_LANE_TILES)], axis=1), zf)
            attn = jnp.where(is_ctx, attn_c_ref[rows, :], attn)
            x_in = jnp.where(is_ctx, x_c_ref[rows, :], x_in)
        z = zf.astype(BF16)
        glu = (z.astype(F32) * jax.nn.sigmoid(_dot(z, wglu_ref[...]))).astype(BF16)
        mix = (ga_ref[rows, :].astype(F32) * _dot(attn, wba_ref[...])
               + gs_ref[rows, :].astype(F32) * _dot(glu, wbs_ref[...])).astype(BF16)
        x = x_in + m[2:3] * _dot(mix, wout_ref[...])
        xo_ref[rows, :] = x
        h2 = _rms_mod(x, g2_ref[...], m[3:4], m[4:5])
        hi, lo = _split_bf16(h2)
        h2_ref[rows, :] = hi.astype(F32)
        logits = _dot(hi, wrh_ref[...]) + _dot(lo, wrh_ref[...]) + _dot(hi, wrl_ref[...])
        rt = _route(logits)
        rt_ref[rows, :] = rt
        oh0, oh1 = _expert_onehots(rt)
        part = jnp.sum(oh0 + oh1, axis=0, keepdims=True)
        tile_counts = part if tile_counts is None else tile_counts + part

    @pl.when(pl.program_id(0) == 0)
    def _():
        cnt_ref[...] = jnp.zeros_like(cnt_ref)

    cnt_ref[0:1, :] = cnt_ref[0:1, :] + tile_counts


def _route(lg):
    ng, epg = N_EXPERT_GROUPS, EXPERTS_PER_GROUP
    lane_i = lax.broadcasted_iota(jnp.int32, lg.shape, 1)
    lane = lane_i.astype(F32)
    big = float(ROUTER_PAD)

    def rmax(mask_val):
        return jnp.max(mask_val, axis=-1, keepdims=True)

    def first_lane(mask, val, mx):
        return jnp.min(jnp.where(mask, jnp.where(val == mx, lane, big), big), axis=-1, keepdims=True)

    gmask = lane_i < ng
    lgm = jnp.where(gmask, lg, NEG_INF)
    mg = rmax(lgm)
    g_prob = 1.0 / jnp.sum(jnp.exp(lgm - mg), axis=-1, keepdims=True)
    g_idx = first_lane(gmask, lg, mg)
    egroup = jnp.floor((lane - float(ng)) * (1.0 / epg))
    emask = egroup == g_idx
    l1 = jnp.where(emask, lg, NEG_INF)
    m1 = rmax(l1)
    i1 = first_lane(emask, lg, m1)
    l2 = jnp.where(lane == i1, NEG_INF, l1)
    m2 = rmax(l2)
    i2 = jnp.min(jnp.where(l2 == m2, jnp.where(emask, lane, big), big), axis=-1, keepdims=True)
    r = jnp.exp(m2 - m1)
    w1 = g_prob / (1.0 + r)
    w2 = w1 * r
    return jnp.where(lane_i == 0, i1 - float(ng),
                     jnp.where(lane_i == 1, i2 - float(ng),
                               jnp.where(lane_i == 2, w1, jnp.where(lane_i == 3, w2, 0.0))))


def _merge(attn, z, x, ctx_parts, ga, gs, mod, g2, w_glu, w_ba, w_bs, w_out, wr_hi, wr_lo, n_rows, t_lat, seq, batch):
    d = D_MODEL
    tile = MERGE_TILE
    assert n_rows % tile == 0 and t_lat % tile == 0 and seq % tile == 0
    n_tiles = n_rows // tile
    n_lat = t_lat // tile
    row = lambda i: (i, 0)
    lat = lambda i: (jnp.minimum(i, n_lat - 1), 0)
    cxt = lambda i: (jnp.maximum(i - n_lat, 0), 0)
    const = lambda i: (0, 0)
    modi = lambda i: (jnp.minimum(i // (seq // tile), batch), 0, 0)
    in_specs = [pl.BlockSpec((tile, Q_W), lat),
                pl.BlockSpec((SSM_LANE_TILES, tile, 128), lambda i: (0,) + lat(i)),
                pl.BlockSpec((tile, d), lat)]
    args = [attn, z, x]
    if ctx_parts is not None:
        in_specs += [pl.BlockSpec((tile, Q_W), cxt),
                     pl.BlockSpec((SSM_LANE_TILES, tile, 128), lambda i: (0,) + cxt(i)),
                     pl.BlockSpec((tile, d), cxt)]
        args += list(ctx_parts)
    in_specs += [
        pl.BlockSpec((tile, d), row),
        pl.BlockSpec((tile, d), row),
        pl.BlockSpec((1, 6, d), modi),
        pl.BlockSpec((1, d), const),
        pl.BlockSpec((D_SSM, D_SSM), const),
        pl.BlockSpec((Q_W, d), const),
        pl.BlockSpec((D_SSM, d), const),
        pl.BlockSpec((d, d), const),
        pl.BlockSpec((d, ROUTER_PAD), const),
        pl.BlockSpec((d, ROUTER_PAD), const),
    ]
    args += [ga, gs, mod, g2, w_glu, w_ba, w_bs, w_out, wr_hi, wr_lo]
    return pl.pallas_call(
        functools.partial(_merge_kernel, n_lat if ctx_parts is not None else None),
        grid=(n_tiles,),
        in_specs=in_specs,
        out_specs=[pl.BlockSpec((tile, d), row), pl.BlockSpec((tile, d), row),
                   pl.BlockSpec((tile, ROUTER_PAD), row), pl.BlockSpec((8, ROUTER_PAD), const)],
        out_shape=[jax.ShapeDtypeStruct((n_rows, d), F32), jax.ShapeDtypeStruct((n_rows, d), F32),
                   jax.ShapeDtypeStruct((n_rows, ROUTER_PAD), F32), jax.ShapeDtypeStruct((8, ROUTER_PAD), F32)],
        compiler_params=_cparams(("arbitrary",)),
        name="merge_router",
    )(*args)


def _expert_kernel(layer, be_ref, nu_ref, nx_ref, x_ref, wg_hbm, wu_hbm, wd_hbm, y_ref,
                   wg_f, wu_f, wd_f, wg_s, wu_s, wd_s, slot_s, sems):
    i = pl.program_id(0)
    e = be_ref[i]
    changed = jnp.logical_or(i == 0, e != be_ref[jnp.maximum(i - 1, 0)])

    def weight_copies(expert, slot):
        return [pltpu.make_async_copy(src.at[layer, expert], dst.at[slot], sems.at[slot, j])
                for j, (src, dst) in enumerate(((wg_hbm, wg_f), (wu_hbm, wu_f), (wd_hbm, wd_f)))]

    @pl.when(i == 0)
    def _():
        slot_s[0] = 0
        for cp in weight_copies(e, 0):
            cp.start()

    @pl.when(changed)
    def _():
        slot = slot_s[0]
        for cp in weight_copies(e, slot):
            cp.wait()
        nxt = nx_ref[e]

        @pl.when(nxt >= 0)
        def _():
            for cp in weight_copies(nxt, 1 - slot):
                cp.start()

        wg_s[...] = wg_f[slot].astype(BF16)
        wu_s[...] = wu_f[slot].astype(BF16)
        wd_s[...] = wd_f[slot].astype(BF16)
        slot_s[0] = 1 - slot

    @pl.when(i < nu_ref[0])
    def _():
        x = jnp.concatenate([x_ref[pl.ds(j, MOE_BLOCK, stride=ROW_SUBLANES), :] for j in range(ROW_SUBLANES)],
                            axis=1).astype(BF16)
        gate = _dot(x, wg_s[...])
        up = _dot(x, wu_s[...])
        hid = (gate * jax.nn.sigmoid(gate) * up).astype(BF16)
        y_ref[...] = _dot(hid, wd_s[...]).astype(y_ref.dtype)

    @pl.when(i >= nu_ref[0])
    def _():
        y_ref[...] = jnp.zeros_like(y_ref)


def _experts(blk_exp, n_used, next_exp, xs, layer, w_gate, w_up, w_down):
    d, de = w_gate.shape[-2:]
    assert d == ROW_SUBLANES * 128
    n_slots = xs.shape[0] // ROW_SUBLANES
    n_blk = n_slots // MOE_BLOCK
    return pl.pallas_call(
        functools.partial(_expert_kernel, layer),
        grid_spec=pltpu.PrefetchScalarGridSpec(
            num_scalar_prefetch=3,
            grid=(n_blk,),
            in_specs=[
                pl.BlockSpec((MOE_BLOCK * ROW_SUBLANES, 128), lambda i, be, nu, nx: (jnp.minimum(i, nu[0] - 1), 0)),
                pl.BlockSpec(memory_space=pl.ANY),
                pl.BlockSpec(memory_space=pl.ANY),
                pl.BlockSpec(memory_space=pl.ANY),
            ],
            out_specs=pl.BlockSpec((MOE_BLOCK, d), lambda i, be, nu, nx: (i, 0)),
            scratch_shapes=[pltpu.VMEM((2, d, de), F32), pltpu.VMEM((2, d, de), F32), pltpu.VMEM((2, de, d), F32),
                            pltpu.VMEM((d, de), BF16), pltpu.VMEM((d, de), BF16), pltpu.VMEM((de, d), BF16),
                            pltpu.SMEM((1,), jnp.int32), pltpu.SemaphoreType.DMA((2, 3))],
        ),
        out_shape=jax.ShapeDtypeStruct((n_slots, d), BF16),
        compiler_params=_cparams(("arbitrary",)),
        name="expert_mlp",
    )(blk_exp, n_used, next_exp, xs, w_gate, w_up, w_down)


def _expert_onehots(rt):
    lane_f = lax.broadcasted_iota(jnp.int32, rt.shape, 1).astype(F32)
    return jnp.where(lane_f == rt[:, 0:1], 1.0, 0.0), jnp.where(lane_f == rt[:, 1:2], 1.0, 0.0)


def _plan_kernel(rt_ref, cnt_ref, dest_ref, pend_ref, run_s):
    i = pl.program_id(0)
    rows = rt_ref.shape[0]
    lane = lax.broadcasted_iota(jnp.int32, (rows, ROUTER_PAD), 1)
    oh0, oh1 = _expert_onehots(rt_ref[...])
    tot0 = jnp.sum(oh0, axis=0, keepdims=True)
    tot1 = jnp.sum(oh1, axis=0, keepdims=True)

    @pl.when(i == 0)
    def _():
        counts = cnt_ref[...]
        padded = jnp.floor((counts + float(MOE_BLOCK - 1)) * (1.0 / MOE_BLOCK)) * float(MOE_BLOCK)
        r = lax.broadcasted_iota(jnp.int32, (ROUTER_PAD, ROUTER_PAD), 0)
        c = lax.broadcasted_iota(jnp.int32, (ROUTER_PAD, ROUTER_PAD), 1)
        incl = jnp.where(r <= c, 1.0, 0.0).astype(BF16)
        hi, lo = _split_bf16(padded)
        pend = _dot(hi, incl) + _dot(lo, incl)
        pend_ref[...] = pend
        run_s[...] = pend - padded

    r = lax.broadcasted_iota(jnp.int32, (rows, rows), 0)
    c = lax.broadcasted_iota(jnp.int32, (rows, rows), 1)
    before = jnp.where(c < r, 1.0, 0.0).astype(BF16)
    run = run_s[0:1, :]
    pos0 = jnp.sum(oh0 * (run + _dot(before, oh0.astype(BF16))), axis=-1, keepdims=True)
    pos1 = jnp.sum(oh1 * (run + tot0 + _dot(before, oh1.astype(BF16))), axis=-1, keepdims=True)
    run_s[0:1, :] = run + tot0 + tot1
    dest_ref[...] = jnp.where(lane == 0, pos0, jnp.where(lane == 1, pos1, 0.0)).astype(jnp.int32)


def _plan(route, counts, n_rows):
    rows = PLAN_TILE
    return pl.pallas_call(
        _plan_kernel,
        grid=(n_rows // rows,),
        in_specs=[pl.BlockSpec((rows, ROUTER_PAD), lambda i: (i, 0)),
                  pl.BlockSpec((8, ROUTER_PAD), lambda i: (0, 0))],
        out_specs=[pl.BlockSpec((rows, ROUTER_PAD), lambda i: (i, 0)),
                   pl.BlockSpec((8, ROUTER_PAD), lambda i: (0, 0))],
        out_shape=[jax.ShapeDtypeStruct((n_rows, ROUTER_PAD), jnp.int32),
                   jax.ShapeDtypeStruct((8, ROUTER_PAD), F32)],
        scratch_shapes=[pltpu.VMEM((8, ROUTER_PAD), F32)],
        compiler_params=_cparams(("arbitrary",)),
        name="dispatch_plan",
    )(route, counts)


def _dispatch_kernel(pend_ref, dest_ref, h_ref, xs_ref, hs, zero_s, sem, zsem):
    rows = h_ref.shape[0]
    sub = ROW_SUBLANES
    n_blk = xs_ref.shape[0] // (MOE_BLOCK * sub)
    n_used = pend_ref[N_EXPERTS - 1] // MOE_BLOCK

    def zero_block(start):
        return pltpu.make_async_copy(
            zero_s, xs_ref.at[pl.ds(pl.multiple_of(start * sub, MOE_BLOCK * sub), MOE_BLOCK * sub)], zsem)

    @pl.when(pl.program_id(0) == 0)
    def _():
        zero_s[...] = jnp.zeros_like(zero_s)
        for e in range(N_EXPERTS):
            @pl.when(pend_ref[e] > 0)
            def _():
                zero_block(pend_ref[e] - MOE_BLOCK).start()
        lax.fori_loop(n_used, n_blk, lambda j, c: (zero_block(j * MOE_BLOCK).start(), c)[1], 0)
        for e in range(N_EXPERTS):
            @pl.when(pend_ref[e] > 0)
            def _():
                zero_block(pend_ref[e] - MOE_BLOCK).wait()
        lax.fori_loop(n_used, n_blk, lambda j, c: (zero_block(j * MOE_BLOCK).wait(), c)[1], 0)

    for j in range(sub):
        hs[pl.ds(j, rows, stride=sub), :] = h_ref[:, 128 * j:128 * (j + 1)]

    def row_copy(r, d):
        return pltpu.make_async_copy(hs.at[pl.ds(pl.multiple_of(r * sub, sub), sub)],
                                     xs_ref.at[pl.ds(pl.multiple_of(d * sub, sub), sub)], sem)

    def issue(r, carry):
        for k in range(TOP_K):
            row_copy(r, dest_ref[0, 0, k * rows + r]).start(priority=k % 2)
        return carry

    lax.fori_loop(0, rows, issue, 0, unroll=8)
    for _ in range(TOP_K * rows):
        row_copy(0, 0).wait()


def _dispatch(pend, dest_tiles, h2, n_slots, n_tiles):
    d = h2.shape[1]
    return pl.pallas_call(
        _dispatch_kernel,
        grid_spec=pltpu.PrefetchScalarGridSpec(
            num_scalar_prefetch=1,
            grid=(n_tiles,),
            in_specs=[pl.BlockSpec((1, 1, TOP_K * ROW_TILE), lambda i, pe: (i, 0, 0), memory_space=pltpu.SMEM),
                      pl.BlockSpec((ROW_TILE, d), lambda i, pe: (i, 0))],
            out_specs=pl.BlockSpec(memory_space=pl.ANY),
            scratch_shapes=[pltpu.VMEM((ROW_TILE * ROW_SUBLANES, 128), F32),
                            pltpu.VMEM((MOE_BLOCK * ROW_SUBLANES, 128), F32),
                            pltpu.SemaphoreType.DMA, pltpu.SemaphoreType.DMA],
        ),
        out_shape=jax.ShapeDtypeStruct((n_slots * ROW_SUBLANES, 128), F32),
        compiler_params=_cparams(("arbitrary",)),
        name="dispatch_rows",
    )(pend, dest_tiles, h2)


def _moe(h2, route, counts, n_tiles, layer, w_gate, w_up, w_down):
    t, d = h2.shape
    dest, pend_f = _plan(route, counts, t)
    pend = pend_f[0, :N_EXPERTS].astype(jnp.int32)
    n_blk = -(-(t * TOP_K) // MOE_BLOCK) + N_EXPERTS
    blk_start = jnp.arange(n_blk, dtype=jnp.int32) * MOE_BLOCK
    blk_exp = jnp.minimum(jnp.sum((pend[None, :] <= blk_start[:, None]).astype(jnp.int32), axis=1), N_EXPERTS - 1)
    n_used = (pend[N_EXPERTS - 1] // MOE_BLOCK).reshape(1)
    blk_exp = jnp.where(jnp.arange(n_blk) < n_used[0], blk_exp, blk_exp[n_used[0] - 1])
    has_rows = pend > jnp.concatenate([jnp.zeros((1,), jnp.int32), pend[:-1]])
    e_ids = jnp.arange(N_EXPERTS, dtype=jnp.int32)
    later = jnp.where(jnp.logical_and(has_rows[None, :], e_ids[None, :] > e_ids[:, None]), e_ids[None, :], N_EXPERTS)
    next_exp = jnp.min(later, axis=1)
    next_exp = jnp.where(next_exp < N_EXPERTS, next_exp, -1).astype(jnp.int32)
    dest2 = dest[:, :TOP_K]
    dest_tiles = dest2.reshape(n_tiles, ROW_TILE, TOP_K).transpose(0, 2, 1).reshape(n_tiles, 1, TOP_K * ROW_TILE)
    xs = _dispatch(pend, dest_tiles, h2, n_blk * MOE_BLOCK, n_tiles)
    ys = _experts(blk_exp.astype(jnp.int32), n_used, next_exp, xs, layer, w_gate, w_up, w_down)
    return ys[dest2[:, 0]], ys[dest2[:, 1]]


def _final_kernel(x_ref, y0_ref, y1_ref, rt_ref, mod_ref, g_ref, o_ref):
    x = _moe_residual(x_ref, y0_ref, y1_ref, rt_ref, mod_ref)
    o_ref[...] = x * lax.rsqrt(jnp.mean(x * x, axis=-1, keepdims=True) + EPS) * g_ref[...]


def _final(x_lat, moe_out, mod, g_final, n_tiles, tiles_per_seq, batch):
    d = D_MODEL
    row = lambda i: (i, 0)
    return pl.pallas_call(
        _final_kernel,
        grid=(n_tiles,),
        in_specs=[pl.BlockSpec((ROW_TILE, d), row), pl.BlockSpec((ROW_TILE, d), row),
                  pl.BlockSpec((ROW_TILE, d), row), pl.BlockSpec((ROW_TILE, ROUTER_PAD), row),
                  pl.BlockSpec((1, 6, d), lambda i: (jnp.minimum(i // tiles_per_seq, batch), 0, 0)),
                  pl.BlockSpec((1, d), lambda i: (0, 0))],
        out_specs=pl.BlockSpec((ROW_TILE, d), row),
        out_shape=jax.ShapeDtypeStruct((n_tiles * ROW_TILE, d), F32),
        compiler_params=_cparams(("parallel",)),
        name="final_norm",
    )(x_lat, *moe_out, mod, g_final)


def _rope_tables(seq):
    quarter = HEAD_DIM // 4
    freqs = ROPE_THETA ** (-jnp.arange(quarter, dtype=F32) / quarter)
    pos = jnp.arange(seq)
    ang_r = (pos // GRID_W).astype(F32)[:, None] * freqs[None, :]
    ang_c = (pos % GRID_W).astype(F32)[:, None] * freqs[None, :]
    cos = jnp.concatenate([jnp.cos(ang_r)] * 2 + [jnp.cos(ang_c)] * 2, axis=-1)
    sin = jnp.concatenate([-jnp.sin(ang_r), jnp.sin(ang_r), -jnp.sin(ang_c), jnp.sin(ang_c)], axis=-1)
    reps = 128 // HEAD_DIM
    cos = jnp.tile(cos, (1, reps))
    sin = jnp.tile(sin, (1, reps))
    cos = jnp.concatenate([cos, jnp.ones((ROW_TILE, 128), F32)], axis=0)
    sin = jnp.concatenate([sin, jnp.zeros((ROW_TILE, 128), F32)], axis=0)
    return cos, sin


def kernel(x, c, ctx, c_ctx, w_mod, b_mod, g_norm1, g_norm2, w_in, attn_sink, ssm_lam_re, ssm_lam_im,
           ssm_log_dt, ssm_b_re, ssm_b_im, ssm_c_re, ssm_c_im, ssm_d, w_glu, w_br_attn, w_br_ssm, w_out,
           w_router_group, w_router_expert, w_exp_gate, w_exp_up, w_exp_down, g_final):
    batch, seq, d = x.shape
    ctx_len = ctx.shape[1]
    depth = w_mod.shape[0]
    assert d == D_MODEL and batch * SSM_STATE == 128
    assert seq % ROW_TILE == 0 and ctx_len % ROW_TILE == 0 and (batch * seq) % ctx_len == 0
    assert batch + 1 <= MOD_ROWS
    t_lat, t_ctx = batch * seq, batch * ctx_len
    tiles_per_seq = seq // ROW_TILE
    n_lat_tiles = t_lat // ROW_TILE
    n_all_tiles = (t_lat + t_ctx) // ROW_TILE
    nb_per_seq = seq // ATTN_BLOCK
    ncc, ncl = ctx_len // SSM_CHUNK, seq // SSM_CHUNK

    c_rows = jnp.zeros((MOD_ROWS, d), F32).at[:batch].set(c).at[batch].set(c_ctx)
    mod_all = _modulation(c_rows, w_mod, b_mod).reshape(depth, MOD_ROWS, 6, d)
    cos_t, sin_t = _rope_tables(seq)
    ssm_w = _ssm_weights_all(ssm_lam_re, ssm_lam_im, ssm_log_dt, ssm_b_re, ssm_b_im, ssm_c_re, ssm_c_im, ssm_d, batch)
    x_parts = (x.reshape(t_lat, d), ctx.reshape(t_ctx, d))
    f_all = None
    for l in range(depth):
        ctx_out = l < depth - 1
        mod = mod_all[l]
        x_parts, (q, k, v, u, ga, gs) = _inproj(
            x_parts, f_all, mod_all[l - 1] if l > 0 else None, mod, g_norm1[l].reshape(1, d), cos_t, sin_t,
            w_in[l].astype(BF16), n_all_tiles, tiles_per_seq, n_lat_tiles, batch)
        sink = attn_sink[l].astype(F32)
        attn = _attention(sink, q, k, v, t_lat // ATTN_BLOCK, nb_per_seq, t_lat // ctx_len, ctx_len, True, 0)
        z, z_ctx = _ssm_mixer(u, l, *ssm_w, batch, seq, ctx_len)
        ctx_parts = None
        if ctx_out:
            attn_c = _attention(sink, q, k, v, t_ctx // ATTN_BLOCK, ctx_len // ATTN_BLOCK, t_lat // ctx_len,
                                ctx_len, False, t_lat // ATTN_BLOCK)
            ctx_parts = (attn_c, z_ctx, x_parts[1])
        w_r = jnp.zeros((d, ROUTER_PAD), F32)
        w_r = w_r.at[:, :N_EXPERT_GROUPS].set(w_router_group[l])
        w_r = w_r.at[:, N_EXPERT_GROUPS:N_EXPERT_GROUPS + N_EXPERTS].set(w_router_expert[l])
        wr_hi, wr_lo = _split_bf16(w_r)
        n_rows = t_lat + t_ctx if ctx_out else t_lat
        x_all, h2, route, counts = _merge(
            attn, z, x_parts[0], ctx_parts, ga, gs, mod, g_norm2[l].reshape(1, d), w_glu[l].astype(BF16),
            w_br_attn[l].astype(BF16), w_br_ssm[l].astype(BF16), w_out[l].astype(BF16),
            wr_hi, wr_lo, n_rows, t_lat, seq, batch)
        y0, y1 = _moe(h2, route, counts, n_rows // ROW_TILE, l, w_exp_gate, w_exp_up, w_exp_down)
        f_all = (y0, y1, route)
        x_parts = (x_all,)
    out = _final(x_all, f_all, mod_all[depth - 1], g_final.reshape(1, d), n_lat_tiles, tiles_per_seq, batch)
    return out.reshape(batch, seq, d)
```

```python
import functools
import math

import jax
import jax.numpy as jnp
from jax import lax
from jax.experimental import pallas as pl
from jax.experimental.pallas import tpu as pltpu

F32 = jnp.float32
BF16 = jnp.bfloat16

D_MODEL = 1024
GRID_W = 64
N_HEADS = 8
N_KV_HEADS = 2
HEAD_DIM = 64
Q_GROUP = N_HEADS // N_KV_HEADS
ATTN_BLOCK = 128
ATTN_STEP_BLOCKS = 4
ROPE_THETA = 10000.0
D_SSM = D_MODEL // 2
SSM_GROUP = 16
N_SSM_GROUPS = D_SSM // SSM_GROUP
SSM_STATE = 64
N_EXPERT_GROUPS = 4
EXPERTS_PER_GROUP = 8
N_EXPERTS = N_EXPERT_GROUPS * EXPERTS_PER_GROUP
TOP_K = 2
D_EXPERT = D_MODEL // 2
Q_W = N_HEADS * HEAD_DIM
KV_W = N_KV_HEADS * HEAD_DIM
O_K = Q_W
O_V = O_K + KV_W
O_U = O_V + KV_W
O_GA = O_U + D_SSM
O_GS = O_GA + D_MODEL
D_IN = O_GS + D_MODEL
EPS = 1e-6
NEG_INF = -1e30

ROW_TILE = 256
MERGE_TILE = 512
MERGE_SUB = 256
SSM_CHUNK = 16
SSM_CW = SSM_CHUNK * SSM_GROUP
SSM_SCAN_GROUPS = 8
SSM_LANE_GROUPS = 128 // SSM_GROUP
SSM_LANE_TILES = D_SSM // 128
MOE_BLOCK = 256
ROW_SUBLANES = D_MODEL // 128
PLAN_TILE = 512
ROUTER_PAD = 128
MOD_ROWS = 8
VMEM_LIMIT = 48 * 1024 * 1024


def _cparams(sem):
    return pltpu.CompilerParams(dimension_semantics=sem, vmem_limit_bytes=VMEM_LIMIT)


def _dot(a, b):
    return jnp.dot(a, b, preferred_element_type=F32)


def _split_bf16(a):
    hi = a.astype(BF16)
    lo = (a - hi.astype(F32)).astype(BF16)
    return hi, lo


def _rms_mod(x, g, shift, scale):
    y = x * lax.rsqrt(jnp.mean(x * x, axis=-1, keepdims=True) + EPS) * g
    return y * (1.0 + scale) + shift


def _mod_kernel(c_ref, w_ref, b_ref, o_ref):
    c = c_ref[...]
    s_hi, s_lo = _split_bf16(c * jax.nn.sigmoid(c))
    w_hi, w_lo = _split_bf16(w_ref[0])
    o_ref[0] = _dot(s_hi, w_hi) + _dot(s_lo, w_hi) + _dot(s_hi, w_lo) + b_ref[0]


def _modulation(c_rows, w_mod, b_mod):
    depth, d, n = w_mod.shape
    nb = n // 4
    return pl.pallas_call(
        _mod_kernel,
        grid=(depth, n // nb),
        in_specs=[
            pl.BlockSpec((MOD_ROWS, d), lambda l, j: (0, 0)),
            pl.BlockSpec((1, d, nb), lambda l, j: (l, 0, j)),
            pl.BlockSpec((1, 1, nb), lambda l, j: (l, 0, j)),
        ],
        out_specs=pl.BlockSpec((1, MOD_ROWS, nb), lambda l, j: (l, 0, j)),
        out_shape=jax.ShapeDtypeStruct((depth, MOD_ROWS, n), F32),
        compiler_params=_cparams(("arbitrary", "arbitrary")),
        name="modulation",
    )(c_rows, w_mod, b_mod.reshape(depth, 1, n))


def _moe_residual(x_ref, y0_ref, y1_ref, rt_ref, mod_ref):
    rt = rt_ref[...]
    f = rt[:, 2:3] * y0_ref[...].astype(F32) + rt[:, 3:4] * y1_ref[...].astype(F32)
    return x_ref[...] + mod_ref[0, 5:6, :] * f


def _inproj_kernel(has_f, n_lat_tiles, *refs):
    if has_f:
        (x_ref, y0_ref, y1_ref, rt_ref, modp_ref, mod_ref, g_ref, cos_ref, sin_ref, w_ref,
         xo_ref, q_ref, k_ref, v_ref, u_ref, ga_ref, gs_ref) = refs
        x = _moe_residual(x_ref, y0_ref, y1_ref, rt_ref, modp_ref)
        xo_ref[...] = x
    else:
        (xl_ref, xc_ref, mod_ref, g_ref, cos_ref, sin_ref, w_ref,
         q_ref, k_ref, v_ref, u_ref, ga_ref, gs_ref) = refs
        x = jnp.where(pl.program_id(0) >= n_lat_tiles, xc_ref[...], xl_ref[...])
    m = mod_ref[0]
    h = _rms_mod(x, g_ref[...], m[0:1], m[1:2]).astype(BF16)
    cos = cos_ref[...]
    sin = sin_ref[...]
    lane = lax.broadcasted_iota(jnp.int32, cos.shape, 1)
    first = (lane % (HEAD_DIM // 2)) < (HEAD_DIM // 4)

    def rope(t):
        sw = jnp.where(first, pltpu.roll(t, 128 - HEAD_DIM // 4, 1), pltpu.roll(t, HEAD_DIM // 4, 1))
        return t * cos + sw * sin

    def proj(lo, hi):
        return _dot(h, w_ref[:, lo:hi])

    q = proj(0, O_K)
    for j in range(Q_W // 128):
        q_ref[:, 128 * j:128 * (j + 1)] = (rope(q[:, 128 * j:128 * (j + 1)]) * HEAD_DIM ** -0.5).astype(BF16)
    kv = proj(O_K, O_U)
    k_ref[...] = rope(kv[:, :KV_W]).astype(BF16)
    v_ref[...] = kv[:, KV_W:].astype(BF16)
    u = proj(O_U, O_GA)
    for j in range(SSM_LANE_TILES):
        u_ref[j] = u[:, 128 * j:128 * (j + 1)]
    ga_ref[...] = jax.nn.sigmoid(proj(O_GA, O_GS)).astype(BF16)
    gs_ref[...] = jax.nn.sigmoid(proj(O_GS, D_IN)).astype(BF16)


def _inproj(x_parts, moe_out, mod_prev, mod, g1, cos_t, sin_t, w_in, n_tiles, tiles_per_seq, n_lat_tiles, batch):
    d = D_MODEL
    has_f = moe_out is not None
    row = lambda i: (i, 0)
    modi = lambda i: (jnp.minimum(i // tiles_per_seq, batch), 0, 0)
    const = lambda i: (0, 0)
    ropei = lambda i: (jnp.where(i < n_lat_tiles, i % tiles_per_seq, tiles_per_seq), 0)
    if has_f:
        in_specs = [pl.BlockSpec((ROW_TILE, d), row), pl.BlockSpec((ROW_TILE, d), row),
                    pl.BlockSpec((ROW_TILE, d), row), pl.BlockSpec((ROW_TILE, ROUTER_PAD), row),
                    pl.BlockSpec((1, 6, d), modi)]
        args = [*x_parts, *moe_out, mod_prev]
    else:
        in_specs = [pl.BlockSpec((ROW_TILE, d), lambda i: (jnp.minimum(i, n_lat_tiles - 1), 0)),
                    pl.BlockSpec((ROW_TILE, d), lambda i: (jnp.maximum(i - n_lat_tiles, 0), 0))]
        args = list(x_parts)
    in_specs += [
        pl.BlockSpec((1, 6, d), modi),
        pl.BlockSpec((1, d), const),
        pl.BlockSpec((ROW_TILE, 128), ropei),
        pl.BlockSpec((ROW_TILE, 128), ropei),
        pl.BlockSpec((d, D_IN), const),
    ]
    args += [mod, g1, cos_t, sin_t, w_in]
    widths = [Q_W, KV_W, KV_W, D_SSM, D_MODEL, D_MODEL]
    out_specs = [pl.BlockSpec((ROW_TILE, w), row) for w in widths]
    out_shape = [jax.ShapeDtypeStruct((n_tiles * ROW_TILE, w), BF16) for w in widths]
    out_specs[3] = pl.BlockSpec((SSM_LANE_TILES, ROW_TILE, 128), lambda i: (0, i, 0))
    out_shape[3] = jax.ShapeDtypeStruct((SSM_LANE_TILES, n_tiles * ROW_TILE, 128), F32)
    if has_f:
        out_specs = [pl.BlockSpec((ROW_TILE, d), row)] + out_specs
        out_shape = [jax.ShapeDtypeStruct((n_tiles * ROW_TILE, d), F32)] + out_shape
    outs = pl.pallas_call(
        functools.partial(_inproj_kernel, has_f, n_lat_tiles),
        grid=(n_tiles,),
        in_specs=in_specs,
        out_specs=out_specs,
        out_shape=out_shape,
        compiler_params=_cparams(("parallel",)),
        name="inproj",
    )(*args)
    if has_f:
        return (outs[0],), outs[1:]
    return x_parts, outs


def _attn_scores(q, k_tiles, v_tiles):
    res = []
    for g in range(N_KV_HEADS):
        gs = slice(g * HEAD_DIM, (g + 1) * HEAD_DIM)
        k_all = jnp.concatenate([t[:, gs] for t in k_tiles], axis=0)
        v_all = jnp.concatenate([t[:, gs] for t in v_tiles], axis=0)
        v_ext = jnp.concatenate([v_all, jnp.ones_like(v_all)], axis=1)
        qg = jnp.concatenate(
            [q[:, (g * Q_GROUP + h) * HEAD_DIM:(g * Q_GROUP + h + 1) * HEAD_DIM] for h in range(Q_GROUP)], axis=0)
        res.append((lax.dot_general(k_all, qg, (((1,), (1,)), ((), ())), preferred_element_type=F32), v_ext))
    return res


def _attn_finish(sink_ref, scores, tile_rows, biases, blk):
    outs = []
    for g, (st, v_ext) in enumerate(scores):
        tiles, row = [], 0
        for n, bias in zip(tile_rows, biases):
            t = st[row:row + n, :]
            tiles.append(t if bias is None else t + jnp.concatenate([bias] * Q_GROUP, axis=1))
            row += n
        mx = None
        for t in tiles:
            for r0 in range(0, t.shape[0], blk):
                mx = t[r0:r0 + blk] if mx is None else jnp.maximum(mx, t[r0:r0 + blk])
        sink = jnp.concatenate([jnp.full((1, blk), sink_ref[g * Q_GROUP + h], F32) for h in range(Q_GROUP)], axis=1)
        m = jnp.maximum(jnp.max(mx, axis=0, keepdims=True), sink)
        p = jnp.exp(jnp.concatenate([(t - m).astype(BF16) for t in tiles], axis=0))
        o_t = lax.dot_general(v_ext, p, (((0,), (0,)), ((), ())), preferred_element_type=F32)
        denom = o_t[HEAD_DIM:HEAD_DIM + 1, :] + jnp.exp(sink - m)
        o_n = o_t[:HEAD_DIM, :] * (1.0 / denom)
        for h in range(Q_GROUP):
            outs.append(o_n[:, h * blk:(h + 1) * blk].T.astype(BF16))
    return jnp.concatenate(outs, axis=1)


def _attn_kernel(band, nb_per_seq, sink_ref, *refs):
    blk = ATTN_BLOCK
    if not band:
        q_ref, kx_ref, vx_ref, o_ref = refs
        n_sub = q_ref.shape[0] // blk
        scores = [_attn_scores(q_ref[a * blk:(a + 1) * blk, :], [kx_ref[...]], [vx_ref[...]]) for a in range(n_sub)]
        for a in range(n_sub):
            o_ref[a * blk:(a + 1) * blk, :] = _attn_finish(sink_ref, scores[a], [kx_ref.shape[0]], [None], blk)
        return
    q_ref, kp_ref, kc_ref, kn_ref, vp_ref, vc_ref, vn_ref, kx_ref, vx_ref, o_ref = refs
    n_sub = q_ref.shape[0] // blk
    j0 = (pl.program_id(0) * n_sub) % nb_per_seq
    r = lax.broadcasted_iota(jnp.int32, (blk, blk), 0)
    c = lax.broadcasted_iota(jnp.int32, (blk, blk), 1)
    k_blocks = [kp_ref[...]] + [kc_ref[a * blk:(a + 1) * blk, :] for a in range(n_sub)] + [kn_ref[...]]
    v_blocks = [vp_ref[...]] + [vc_ref[a * blk:(a + 1) * blk, :] for a in range(n_sub)] + [vn_ref[...]]
    scores = [_attn_scores(q_ref[a * blk:(a + 1) * blk, :], k_blocks[a:a + 3] + [kx_ref[...]],
                           v_blocks[a:a + 3] + [vx_ref[...]]) for a in range(n_sub)]
    for a in range(n_sub):
        edge_p = jnp.where(j0 + a > 0, 0.0, NEG_INF).astype(F32)
        edge_n = jnp.where(j0 + a < nb_per_seq - 1, 0.0, NEG_INF).astype(F32)
        bias_p = jnp.where(r >= c, edge_p, NEG_INF).astype(F32)
        bias_n = jnp.where(r <= c, edge_n, NEG_INF).astype(F32)
        o_ref[a * blk:(a + 1) * blk, :] = _attn_finish(
            sink_ref, scores[a], [blk, blk, blk, kx_ref.shape[0]], [bias_p, None, bias_n, None], blk)


def _attention(sink, q, k, v, n_blocks, nb_per_seq, kx_block0, ctx_len, band, q_block0):
    blk = ATTN_BLOCK
    n_sub = math.gcd(ATTN_STEP_BLOCKS, nb_per_seq, n_blocks, q_block0)
    step = n_sub * blk
    n_steps = n_blocks // n_sub
    qi = lambda i, s: (q_block0 // n_sub + i, 0)
    cur = lambda i, s: (i, 0)
    prv = lambda i, s: (jnp.maximum(i * n_sub - 1, 0), 0)
    nxt = lambda i, s: (jnp.minimum((i + 1) * n_sub, n_blocks - 1), 0)
    kxi = lambda i, s: (kx_block0 + (i * n_sub) // nb_per_seq, 0)
    kspec = lambda f: pl.BlockSpec((blk, KV_W), f)
    cspec = pl.BlockSpec((step, KV_W), cur)
    xspec = pl.BlockSpec((ctx_len, KV_W), kxi)
    if band:
        in_specs = [pl.BlockSpec((step, Q_W), qi), kspec(prv), cspec, kspec(nxt),
                    kspec(prv), cspec, kspec(nxt), xspec, xspec]
        args = (q, k, k, k, v, v, v, k, v)
    else:
        in_specs = [pl.BlockSpec((step, Q_W), qi), xspec, xspec]
        args = (q, k, v)
    return pl.pallas_call(
        functools.partial(_attn_kernel, band, nb_per_seq),
        grid_spec=pltpu.PrefetchScalarGridSpec(
            num_scalar_prefetch=1,
            grid=(n_steps,),
            in_specs=in_specs,
            out_specs=pl.BlockSpec((step, Q_W), lambda i, s: (i, 0)),
        ),
        out_shape=jax.ShapeDtypeStruct((n_blocks * blk, Q_W), BF16),
        compiler_params=_cparams(("parallel",)),
        name="band_attention" if band else "context_attention",
    )(sink, *args)


def _dot_f32(a, b_t):
    a_hi, a_lo = _split_bf16(a)
    b_hi, b_lo = _split_bf16(b_t)
    dn = (((1,), (1,)), ((), ()))
    dg = functools.partial(lax.dot_general, dimension_numbers=dn, preferred_element_type=F32)
    return dg(a_hi, b_hi) + dg(a_lo, b_hi) + dg(a_hi, b_lo)


def _ssm_weight_kernel(batch, lam_r_ref, bt_ref, c_ref, d_ref, w1_ref, abar_ref, t_ref, wo_ref):
    lc, mm, p = SSM_CHUNK, SSM_GROUP, SSM_STATE
    up_r = lax.broadcasted_iota(jnp.int32, (lc, 1), 0).astype(F32)
    lane = lax.broadcasted_iota(jnp.int32, (mm, lc * mm), 1)
    row = lax.broadcasted_iota(jnp.int32, (mm, lc * mm), 0)
    planes_in, planes_out, abar_rows, kt = [], [], [], []
    for d in range(2):
        lr, li, dt = lam_r_ref[0, d, 0:1, :], lam_r_ref[0, d, 1:2, :], jnp.exp(lam_r_ref[0, d, 2:3, :])

        def powers(expo):
            mag = jnp.exp(lr * dt * expo)
            return mag * jnp.cos(li * dt * expo), mag * jnp.sin(li * dt * expo)

        a_re, a_im = powers(jnp.ones((1, 1), F32))
        den = lr * lr + li * li
        nr = a_re - 1.0
        f_re = (nr * lr + a_im * li) / den
        f_im = (a_im * lr - nr * li) / den
        bt_re, bt_im = bt_ref[0, d, 0], bt_ref[0, d, 1]
        bbt_re = f_re * bt_re - f_im * bt_im
        bbt_im = f_re * bt_im + f_im * bt_re
        c_re, c_im = c_ref[0, d, 0], c_ref[0, d, 1]

        pr, pi = powers(lc - 1.0 - up_r if d == 0 else up_r)
        planes_in.append(jnp.concatenate(
            [pr[s:s + 1] * bbt_re - pi[s:s + 1] * bbt_im for s in range(lc)], axis=0))
        planes_in.append(jnp.concatenate(
            [pr[s:s + 1] * bbt_im + pi[s:s + 1] * bbt_re for s in range(lc)], axis=0))

        qr, qi = powers(up_r + 1.0 if d == 0 else lc - up_r)
        planes_out.append(jnp.concatenate(
            [c_re * qr[t:t + 1] - c_im * qi[t:t + 1] for t in range(lc)], axis=0).T)
        planes_out.append(jnp.concatenate(
            [-c_re * qi[t:t + 1] - c_im * qr[t:t + 1] for t in range(lc)], axis=0).T)

        kr, ki = powers(up_r if d == 0 else lc - 1.0 - up_r)
        cp_re = jnp.concatenate([c_re * kr[j:j + 1] - c_im * ki[j:j + 1] for j in range(lc)], axis=0)
        cp_im = jnp.concatenate([c_re * ki[j:j + 1] + c_im * kr[j:j + 1] for j in range(lc)], axis=0)
        kt.append(_dot_f32(jnp.concatenate([bbt_re, bbt_im], axis=1), jnp.concatenate([cp_re, -cp_im], axis=1)))

        e_re, e_im = powers(jnp.full((1, 1), float(lc), F32))
        abar_rows += [jnp.concatenate([e_re] * batch, axis=1), jnp.concatenate([e_im] * batch, axis=1)]

    dvec = d_ref[0]
    blocks = []
    for s in range(lc):
        fwd = kt[0] if s == 0 else pltpu.roll(kt[0], mm * s, 1)
        bwd = kt[1] if s == lc - 1 else pltpu.roll(kt[1], mm * (s + 1), 1)
        blk = jnp.where(lane >= mm * s, fwd, 0.0) + jnp.where(lane < mm * (s + 1), bwd, 0.0)
        blocks.append(blk + jnp.where(lane == mm * s + row, dvec, 0.0))
    t_ref[0] = jnp.concatenate(blocks, axis=0).astype(BF16)
    abar_ref[0] = jnp.concatenate(abar_rows, axis=0)
    zero_in = jnp.zeros((lc * mm, p), F32)
    zero_out = jnp.zeros((p, lc * mm), F32)
    for b in range(batch):
        w1_ref[0, b] = jnp.concatenate(
            [pl_ if bb == b else zero_in for pl_ in planes_in for bb in range(batch)], axis=1).astype(BF16)
        wo_ref[0, b] = jnp.concatenate(
            [pl_ if bb == b else zero_out for pl_ in planes_out for bb in range(batch)], axis=0).astype(BF16)


def _ssm_weights_all(lam_re, lam_im, log_dt, b_re, b_im, c_re, c_im, d_skip, batch):
    depth = lam_re.shape[0]
    g, p, mm, lc = N_SSM_GROUPS, SSM_STATE, SSM_GROUP, SSM_CHUNK
    n = depth * g
    cw, sw = lc * mm, 4 * batch * p

    def per_group(a):
        return jnp.moveaxis(a.astype(F32), 2, 1).reshape((n, 2) + a.shape[3:])

    lam = jnp.stack([per_group(lam_re), per_group(lam_im),
                     jnp.broadcast_to(per_group(log_dt)[..., None], (n, 2, p))], axis=2)
    bt = jnp.stack([per_group(b_re), per_group(b_im)], axis=2).swapaxes(-1, -2)
    c = jnp.stack([per_group(c_re), per_group(c_im)], axis=2)
    d_t = jnp.tile(d_skip.astype(F32).reshape(n, 1, mm), (1, 1, lc))
    full = lambda *shape: pl.BlockSpec((1,) + shape, lambda i: (i,) + (0,) * len(shape))
    w1, abar, tmat, wout = pl.pallas_call(
        functools.partial(_ssm_weight_kernel, batch),
        grid=(n,),
        in_specs=[full(2, 3, p), full(2, 2, mm, p), full(2, 2, mm, p), full(1, cw)],
        out_specs=[full(batch, cw, sw), full(4, batch * p), full(cw, cw), full(batch, sw, cw)],
        out_shape=[jax.ShapeDtypeStruct((n, batch, cw, sw), BF16), jax.ShapeDtypeStruct((n, 4, batch * p), F32),
                   jax.ShapeDtypeStruct((n, cw, cw), BF16), jax.ShapeDtypeStruct((n, batch, sw, cw), BF16)],
        compiler_params=_cparams(("parallel",)),
        name="ssm_weights",
    )(lam, bt, c, d_t)
    abar = abar.reshape(depth, g, 4, batch * p).transpose(0, 2, 1, 3)
    return (w1.reshape(depth, g, batch, cw, sw), abar, tmat.reshape(depth, g, cw, cw),
            wout.reshape(depth, g, batch, sw, cw))


def _ssm_p1_kernel(ul_ref, uc_ref, w_ref, v_ref, *s_refs):
    b = pl.program_id(1)
    ncc, ncl = uc_ref.shape[0] // SSM_CHUNK, ul_ref.shape[0] // SSM_CHUNK
    nc = ncc + ncl
    pw = s_refs[0].shape[1]
    xs = [jnp.concatenate([uc_ref[pl.ds(s, ncc, stride=SSM_CHUNK), :], ul_ref[pl.ds(s, ncl, stride=SSM_CHUNK), :]],
                          axis=0).astype(BF16) for s in range(SSM_CHUNK)]
    for j in range(SSM_LANE_GROUPS):
        v = jnp.concatenate([x[:, j * SSM_GROUP:(j + 1) * SSM_GROUP] for x in xs], axis=1)
        v_ref[j] = v
        acc = _dot(v, w_ref[j])
        for k, s_ref in enumerate(s_refs):
            part = acc[:, k * pw:(k + 1) * pw]
            rows = slice(j * nc, (j + 1) * nc)

            @pl.when(b == 0)
            def _():
                s_ref[rows, :] = part

            @pl.when(b > 0)
            def _():
                s_ref[rows, :] = s_ref[rows, :] + part


def _ssm_scan_kernel(ncc, ncl, a_ref, sfr_ref, sfi_ref, sbr_ref, sbi_ref, xfr_ref, xfi_ref, xbr_ref, xbi_ref):
    nc = ncc + ncl
    gb = a_ref.shape[1]
    pw = a_ref.shape[2]
    afr, afi, abr, abi = a_ref[0], a_ref[1], a_ref[2], a_ref[3]

    def rows(r):
        return pl.ds(r, gb, stride=nc)

    def step(rf, rb, carry):
        xfr, xfi, xbr, xbi = carry
        xfr_ref[rows(rf), :] = xfr
        xfi_ref[rows(rf), :] = xfi
        xbr_ref[rows(rb), :] = xbr
        xbi_ref[rows(rb), :] = xbi
        sfr = sfr_ref[rows(rf), :]
        sfi = sfi_ref[rows(rf), :]
        sbr = sbr_ref[rows(rb), :]
        sbi = sbi_ref[rows(rb), :]
        return (afr * xfr - afi * xfi + sfr, afr * xfi + afi * xfr + sfi,
                abr * xbr - abi * xbi + sbr, abr * xbi + abi * xbr + sbi)

    zero = jnp.zeros((gb, pw), F32)
    carry = lax.fori_loop(0, ncc, lambda t, c: step(t, ncc - 1 - t, c), (zero, zero, zero, zero), unroll=4)
    lax.fori_loop(0, ncl, lambda t, c: step(ncc + t, nc - 1 - t, c), carry, unroll=4)


def _ssm_p3_kernel(v_ref, xfr_ref, xfi_ref, xbr_ref, xbi_ref, t_ref, wo_ref, zl_ref, zc_ref):
    ncc, ncl = zc_ref.shape[0] // SSM_CHUNK, zl_ref.shape[0] // SSM_CHUNK
    nc = ncc + ncl
    ys = []
    for j in range(SSM_LANE_GROUPS):
        rows = slice(j * nc, (j + 1) * nc)
        xin = jnp.concatenate([xfr_ref[rows, :], xfi_ref[rows, :], xbr_ref[rows, :], xbi_ref[rows, :]],
                              axis=1).astype(BF16)
        y = _dot(v_ref[j], t_ref[j]) + _dot(xin, wo_ref[j])
        ys.append(jax.nn.gelu(y, approximate=True))
    for t in range(SSM_CHUNK):
        zt = jnp.concatenate([y[:, t * SSM_GROUP:(t + 1) * SSM_GROUP] for y in ys], axis=1)
        zc_ref[pl.ds(t, ncc, stride=SSM_CHUNK), :] = zt[:ncc]
        zl_ref[pl.ds(t, ncl, stride=SSM_CHUNK), :] = zt[ncc:]


def _ssm_mixer(u4, layer, w1, abar, tmat, wout, batch, seq, ctx_len):
    n_lt = u4.shape[0]
    g = N_SSM_GROUPS
    lg = SSM_LANE_GROUPS
    ncc, ncl = ctx_len // SSM_CHUNK, seq // SSM_CHUNK
    nc = ncc + ncl
    cw = SSM_CW
    sw = w1.shape[-1]
    pw = sw // 4
    ctx_blk0 = (batch * seq) // ctx_len
    plane_shape = [jax.ShapeDtypeStruct((g * nc, pw), F32)] * 4
    lat_spec = pl.BlockSpec((None, seq, 128), lambda k, b: (k, b, 0))
    ctx_spec = pl.BlockSpec((None, ctx_len, 128), lambda k, b: (k, ctx_blk0 + b, 0))
    v_spec = pl.BlockSpec((lg, None, nc, cw), lambda k, b: (k, b, 0, 0))
    plane_spec = pl.BlockSpec((lg * nc, pw), lambda k, b: (k, 0))
    v_chunks, *s_planes = pl.pallas_call(
        _ssm_p1_kernel,
        grid=(n_lt, batch),
        in_specs=[lat_spec, ctx_spec,
                  pl.BlockSpec((None, lg, None, cw, sw), lambda k, b: (layer, k, b, 0, 0))],
        out_specs=[v_spec] + [plane_spec] * 4,
        out_shape=[jax.ShapeDtypeStruct((g, batch, nc, cw), BF16)] + plane_shape,
        compiler_params=_cparams(("parallel", "arbitrary")),
        name="ssm_chunk_states",
    )(u4, u4, w1)
    gb = SSM_SCAN_GROUPS
    x_planes = pl.pallas_call(
        functools.partial(_ssm_scan_kernel, ncc, ncl),
        grid=(g // gb,),
        in_specs=[pl.BlockSpec((None, 4, gb, pw), lambda i: (layer, 0, i, 0))]
        + [pl.BlockSpec((gb * nc, pw), lambda i: (i, 0))] * 4,
        out_specs=[pl.BlockSpec((gb * nc, pw), lambda i: (i, 0))] * 4,
        out_shape=plane_shape,
        compiler_params=_cparams(("parallel",)),
        name="ssm_chunk_scan",
    )(abar, *s_planes)
    return pl.pallas_call(
        _ssm_p3_kernel,
        grid=(n_lt, batch),
        in_specs=[v_spec] + [plane_spec] * 4
        + [pl.BlockSpec((None, lg, cw, cw), lambda k, b: (layer, k, 0, 0)),
           pl.BlockSpec((None, lg, None, sw, cw), lambda k, b: (layer, k, b, 0, 0))],
        out_specs=[pl.BlockSpec((None, seq, 128), lambda k, b: (k, b, 0)),
                   pl.BlockSpec((None, ctx_len, 128), lambda k, b: (k, b, 0))],
        out_shape=[jax.ShapeDtypeStruct((n_lt, batch * seq, 128), F32),
                   jax.ShapeDtypeStruct((n_lt, batch * ctx_len, 128), F32)],
        compiler_params=_cparams(("parallel", "arbitrary")),
        name="ssm_chunk_outputs",
    )(v_chunks, *x_planes, tmat, wout)


def _merge_kernel(n_lat_tiles, *refs):
    if n_lat_tiles is None:
        attn_ref, z_ref, x_ref = refs[:3]
        refs = refs[3:]
    else:
        attn_ref, z_ref, x_ref, attn_c_ref, z_c_ref, x_c_ref = refs[:6]
        refs = refs[6:]
        is_ctx = pl.program_id(0) >= n_lat_tiles
    (ga_ref, gs_ref, mod_ref, g2_ref, wglu_ref, wba_ref, wbs_ref, wout_ref, wrh_ref, wrl_ref,
     xo_ref, h2_ref, rt_ref, cnt_ref) = refs
    m = mod_ref[0]
    tile_counts = None
    for r in range(MERGE_TILE // MERGE_SUB):
        rows = slice(r * MERGE_SUB, (r + 1) * MERGE_SUB)
        zf = jnp.concatenate([z_ref[j, rows, :] for j in range(SSM_LANE_TILES)], axis=1)
        attn = attn_ref[rows, :]
        x_in = x_ref[rows, :]
        if n_lat_tiles is not None:
            zf = jnp.where(is_ctx, jnp.concatenate([z_c_ref[j, rows, :] for j in range(SSM_LANE_TILES)], axis=1), zf)
            attn = jnp.where(is_ctx, attn_c_ref[rows, :], attn)
            x_in = jnp.where(is_ctx, x_c_ref[rows, :], x_in)
        z = zf.astype(BF16)
        glu = (z.astype(F32) * jax.nn.sigmoid(_dot(z, wglu_ref[...]))).astype(BF16)
        mix = (ga_ref[rows, :].astype(F32) * _dot(attn, wba_ref[...])
               + gs_ref[rows, :].astype(F32) * _dot(glu, wbs_ref[...])).astype(BF16)
        x = x_in + m[2:3] * _dot(mix, wout_ref[...])
        xo_ref[rows, :] = x
        h2 = _rms_mod(x, g2_ref[...], m[3:4], m[4:5])
        hi, lo = _split_bf16(h2)
        h2_ref[rows, :] = hi.astype(F32)
        logits = _dot(hi, wrh_ref[...]) + _dot(lo, wrh_ref[...]) + _dot(hi, wrl_ref[...])
        rt = _route(logits)
        rt_ref[rows, :] = rt
        oh0, oh1 = _expert_onehots(rt)
        part = jnp.sum(oh0 + oh1, axis=0, keepdims=True)
        tile_counts = part if tile_counts is None else tile_counts + part

    @pl.when(pl.program_id(0) == 0)
    def _():
        cnt_ref[...] = jnp.zeros_like(cnt_ref)

    cnt_ref[0:1, :] = cnt_ref[0:1, :] + tile_counts


def _route(lg):
    ng, epg = N_EXPERT_GROUPS, EXPERTS_PER_GROUP
    lane_i = lax.broadcasted_iota(jnp.int32, lg.shape, 1)
    lane = lane_i.astype(F32)
    big = float(ROUTER_PAD)

    def rmax(mask_val):
        return jnp.max(mask_val, axis=-1, keepdims=True)

    def first_lane(mask, val, mx):
        return jnp.min(jnp.where(mask, jnp.where(val == mx, lane, big), big), axis=-1, keepdims=True)

    gmask = lane_i < ng
    lgm = jnp.where(gmask, lg, NEG_INF)
    mg = rmax(lgm)
    g_prob = 1.0 / jnp.sum(jnp.exp(lgm - mg), axis=-1, keepdims=True)
    g_idx = first_lane(gmask, lg, mg)
    egroup = jnp.floor((lane - float(ng)) * (1.0 / epg))
    emask = egroup == g_idx
    l1 = jnp.where(emask, lg, NEG_INF)
    m1 = rmax(l1)
    i1 = first_lane(emask, lg, m1)
    l2 = jnp.where(lane == i1, NEG_INF, l1)
    m2 = rmax(l2)
    i2 = jnp.min(jnp.where(l2 == m2, jnp.where(emask, lane, big), big), axis=-1, keepdims=True)
    r = jnp.exp(m2 - m1)
    w1 = g_prob / (1.0 + r)
    w2 = w1 * r
    return jnp.where(lane_i == 0, i1 - float(ng),
                     jnp.where(lane_i == 1, i2 - float(ng),
                               jnp.where(lane_i == 2, w1, jnp.where(lane_i == 3, w2, 0.0))))


def _merge(attn, z, x, ctx_parts, ga, gs, mod, g2, w_glu, w_ba, w_bs, w_out, wr_hi, wr_lo, n_rows, t_lat, seq, batch):
    d = D_MODEL
    tile = MERGE_TILE
    assert n_rows % tile == 0 and t_lat % tile == 0 and seq % tile == 0
    n_tiles = n_rows // tile
    n_lat = t_lat // tile
    row = lambda i: (i, 0)
    lat = lambda i: (jnp.minimum(i, n_lat - 1), 0)
    cxt = lambda i: (jnp.maximum(i - n_lat, 0), 0)
    const = lambda i: (0, 0)
    modi = lambda i: (jnp.minimum(i // (seq // tile), batch), 0, 0)
    in_specs = [pl.BlockSpec((tile, Q_W), lat),
                pl.BlockSpec((SSM_LANE_TILES, tile, 128), lambda i: (0,) + lat(i)),
                pl.BlockSpec((tile, d), lat)]
    args = [attn, z, x]
    if ctx_parts is not None:
        in_specs += [pl.BlockSpec((tile, Q_W), cxt),
                     pl.BlockSpec((SSM_LANE_TILES, tile, 128), lambda i: (0,) + cxt(i)),
                     pl.BlockSpec((tile, d), cxt)]
        args += list(ctx_parts)
    in_specs += [
        pl.BlockSpec((tile, d), row),
        pl.BlockSpec((tile, d), row),
        pl.BlockSpec((1, 6, d), modi),
        pl.BlockSpec((1, d), const),
        pl.BlockSpec((D_SSM, D_SSM), const),
        pl.BlockSpec((Q_W, d), const),
        pl.BlockSpec((D_SSM, d), const),
        pl.BlockSpec((d, d), const),
        pl.BlockSpec((d, ROUTER_PAD), const),
        pl.BlockSpec((d, ROUTER_PAD), const),
    ]
    args += [ga, gs, mod, g2, w_glu, w_ba, w_bs, w_out, wr_hi, wr_lo]
    return pl.pallas_call(
        functools.partial(_merge_kernel, n_lat if ctx_parts is not None else None),
        grid=(n_tiles,),
        in_specs=in_specs,
        out_specs=[pl.BlockSpec((tile, d), row), pl.BlockSpec((tile, d), row),
                   pl.BlockSpec((tile, ROUTER_PAD), row), pl.BlockSpec((8, ROUTER_PAD), const)],
        out_shape=[jax.ShapeDtypeStruct((n_rows, d), F32), jax.ShapeDtypeStruct((n_rows, d), F32),
                   jax.ShapeDtypeStruct((n_rows, ROUTER_PAD), F32), jax.ShapeDtypeStruct((8, ROUTER_PAD), F32)],
        compiler_params=_cparams(("arbitrary",)),
        name="merge_router",
    )(*args)


def _expert_kernel(layer, be_ref, nu_ref, nx_ref, x_ref, wg_hbm, wu_hbm, wd_hbm, y_ref,
                   wg_f, wu_f, wd_f, wg_s, wu_s, wd_s, slot_s, sems):
    i = pl.program_id(0)
    e = be_ref[i]
    changed = jnp.logical_or(i == 0, e != be_ref[jnp.maximum(i - 1, 0)])

    def weight_copies(expert, slot):
        return [pltpu.make_async_copy(src.at[layer, expert], dst.at[slot], sems.at[slot, j])
                for j, (src, dst) in enumerate(((wg_hbm, wg_f), (wu_hbm, wu_f), (wd_hbm, wd_f)))]

    @pl.when(i == 0)
    def _():
        slot_s[0] = 0
        for cp in weight_copies(e, 0):
            cp.start()

    @pl.when(changed)
    def _():
        slot = slot_s[0]
        for cp in weight_copies(e, slot):
            cp.wait()
        nxt = nx_ref[e]

        @pl.when(nxt >= 0)
        def _():
            for cp in weight_copies(nxt, 1 - slot):
                cp.start()

        wg_s[...] = wg_f[slot].astype(BF16)
        wu_s[...] = wu_f[slot].astype(BF16)
        wd_s[...] = wd_f[slot].astype(BF16)
        slot_s[0] = 1 - slot

    @pl.when(i < nu_ref[0])
    def _():
        x = jnp.concatenate([x_ref[pl.ds(j, MOE_BLOCK, stride=ROW_SUBLANES), :] for j in range(ROW_SUBLANES)],
                            axis=1).astype(BF16)
        gate = _dot(x, wg_s[...])
        up = _dot(x, wu_s[...])
        hid = (gate * jax.nn.sigmoid(gate) * up).astype(BF16)
        y_ref[...] = _dot(hid, wd_s[...]).astype(y_ref.dtype)

    @pl.when(i >= nu_ref[0])
    def _():
        y_ref[...] = jnp.zeros_like(y_ref)


def _experts(blk_exp, n_used, next_exp, xs, layer, w_gate, w_up, w_down):
    d, de = w_gate.shape[-2:]
    assert d == ROW_SUBLANES * 128
    n_slots = xs.shape[0] // ROW_SUBLANES
    n_blk = n_slots // MOE_BLOCK
    return pl.pallas_call(
        functools.partial(_expert_kernel, layer),
        grid_spec=pltpu.PrefetchScalarGridSpec(
            num_scalar_prefetch=3,
            grid=(n_blk,),
            in_specs=[
                pl.BlockSpec((MOE_BLOCK * ROW_SUBLANES, 128), lambda i, be, nu, nx: (jnp.minimum(i, nu[0] - 1), 0)),
                pl.BlockSpec(memory_space=pl.ANY),
                pl.BlockSpec(memory_space=pl.ANY),
                pl.BlockSpec(memory_space=pl.ANY),
            ],
            out_specs=pl.BlockSpec((MOE_BLOCK, d), lambda i, be, nu, nx: (i, 0)),
            scratch_shapes=[pltpu.VMEM((2, d, de), F32), pltpu.VMEM((2, d, de), F32), pltpu.VMEM((2, de, d), F32),
                            pltpu.VMEM((d, de), BF16), pltpu.VMEM((d, de), BF16), pltpu.VMEM((de, d), BF16),
                            pltpu.SMEM((1,), jnp.int32), pltpu.SemaphoreType.DMA((2, 3))],
        ),
        out_shape=jax.ShapeDtypeStruct((n_slots, d), BF16),
        compiler_params=_cparams(("arbitrary",)),
        name="expert_mlp",
    )(blk_exp, n_used, next_exp, xs, w_gate, w_up, w_down)


def _expert_onehots(rt):
    lane_f = lax.broadcasted_iota(jnp.int32, rt.shape, 1).astype(F32)
    return jnp.where(lane_f == rt[:, 0:1], 1.0, 0.0), jnp.where(lane_f == rt[:, 1:2], 1.0, 0.0)


def _plan_kernel(rt_ref, cnt_ref, dest_ref, pend_ref, run_s, before_s):
    i = pl.program_id(0)
    rows = rt_ref.shape[0]
    lane = lax.broadcasted_iota(jnp.int32, (rows, ROUTER_PAD), 1)
    oh0, oh1 = _expert_onehots(rt_ref[...])
    tot0 = jnp.sum(oh0, axis=0, keepdims=True)
    tot1 = jnp.sum(oh1, axis=0, keepdims=True)

    @pl.when(i == 0)
    def _():
        counts = cnt_ref[...]
        padded = jnp.floor((counts + float(MOE_BLOCK - 1)) * (1.0 / MOE_BLOCK)) * float(MOE_BLOCK)
        r = lax.broadcasted_iota(jnp.int32, (ROUTER_PAD, ROUTER_PAD), 0)
        c = lax.broadcasted_iota(jnp.int32, (ROUTER_PAD, ROUTER_PAD), 1)
        incl = jnp.where(r <= c, 1.0, 0.0).astype(BF16)
        hi, lo = _split_bf16(padded)
        pend = _dot(hi, incl) + _dot(lo, incl)
        pend_ref[...] = pend
        run_s[...] = pend - padded
        r = lax.broadcasted_iota(jnp.int32, (rows, rows), 0)
        c = lax.broadcasted_iota(jnp.int32, (rows, rows), 1)
        before_s[...] = jnp.where(c < r, 1.0, 0.0).astype(BF16)

    prior = _dot(before_s[...], jnp.concatenate([oh0, oh1], axis=1).astype(BF16))
    run = run_s[0:1, :]
    pos0 = jnp.sum(oh0 * (run + prior[:, :ROUTER_PAD]), axis=-1, keepdims=True)
    pos1 = jnp.sum(oh1 * (run + tot0 + prior[:, ROUTER_PAD:]), axis=-1, keepdims=True)
    run_s[0:1, :] = run + tot0 + tot1
    dest_ref[...] = jnp.where(lane == 0, pos0, jnp.where(lane == 1, pos1, 0.0)).astype(jnp.int32)


def _plan(route, counts, n_rows):
    rows = PLAN_TILE
    return pl.pallas_call(
        _plan_kernel,
        grid=(n_rows // rows,),
        in_specs=[pl.BlockSpec((rows, ROUTER_PAD), lambda i: (i, 0)),
                  pl.BlockSpec((8, ROUTER_PAD), lambda i: (0, 0))],
        out_specs=[pl.BlockSpec((rows, ROUTER_PAD), lambda i: (i, 0)),
                   pl.BlockSpec((8, ROUTER_PAD), lambda i: (0, 0))],
        out_shape=[jax.ShapeDtypeStruct((n_rows, ROUTER_PAD), jnp.int32),
                   jax.ShapeDtypeStruct((8, ROUTER_PAD), F32)],
        scratch_shapes=[pltpu.VMEM((8, ROUTER_PAD), F32), pltpu.VMEM((rows, rows), BF16)],
        compiler_params=_cparams(("arbitrary",)),
        name="dispatch_plan",
    )(route, counts)


def _dispatch_kernel(pend_ref, dest_ref, h_ref, xs_ref, hs, zero_s, sem, zsem):
    rows = h_ref.shape[0]
    sub = ROW_SUBLANES
    n_blk = xs_ref.shape[0] // (MOE_BLOCK * sub)
    n_used = pend_ref[N_EXPERTS - 1] // MOE_BLOCK

    def zero_block(start):
        return pltpu.make_async_copy(
            zero_s, xs_ref.at[pl.ds(pl.multiple_of(start * sub, MOE_BLOCK * sub), MOE_BLOCK * sub)], zsem)

    @pl.when(pl.program_id(0) == 0)
    def _():
        zero_s[...] = jnp.zeros_like(zero_s)
        for e in range(N_EXPERTS):
            @pl.when(pend_ref[e] > 0)
            def _():
                zero_block(pend_ref[e] - MOE_BLOCK).start()
        lax.fori_loop(n_used, n_blk, lambda j, c: (zero_block(j * MOE_BLOCK).start(), c)[1], 0)
        for e in range(N_EXPERTS):
            @pl.when(pend_ref[e] > 0)
            def _():
                zero_block(pend_ref[e] - MOE_BLOCK).wait()
        lax.fori_loop(n_used, n_blk, lambda j, c: (zero_block(j * MOE_BLOCK).wait(), c)[1], 0)

    for j in range(sub):
        hs[pl.ds(j, rows, stride=sub), :] = h_ref[:, 128 * j:128 * (j + 1)]

    def row_copy(r, d):
        return pltpu.make_async_copy(hs.at[pl.ds(pl.multiple_of(r * sub, sub), sub)],
                                     xs_ref.at[pl.ds(pl.multiple_of(d * sub, sub), sub)], sem)

    def issue(r, carry):
        for k in range(TOP_K):
            row_copy(r, dest_ref[0, 0, k * rows + r]).start(priority=k % 2)
        return carry

    lax.fori_loop(0, rows, issue, 0, unroll=8)
    for _ in range(TOP_K * rows):
        row_copy(0, 0).wait()


def _dispatch(pend, dest_tiles, h2, n_slots, n_tiles):
    d = h2.shape[1]
    return pl.pallas_call(
        _dispatch_kernel,
        grid_spec=pltpu.PrefetchScalarGridSpec(
            num_scalar_prefetch=1,
            grid=(n_tiles,),
            in_specs=[pl.BlockSpec((1, 1, TOP_K * ROW_TILE), lambda i, pe: (i, 0, 0), memory_space=pltpu.SMEM),
                      pl.BlockSpec((ROW_TILE, d), lambda i, pe: (i, 0))],
            out_specs=pl.BlockSpec(memory_space=pl.ANY),
            scratch_shapes=[pltpu.VMEM((ROW_TILE * ROW_SUBLANES, 128), F32),
                            pltpu.VMEM((MOE_BLOCK * ROW_SUBLANES, 128), F32),
                            pltpu.SemaphoreType.DMA, pltpu.SemaphoreType.DMA],
        ),
        out_shape=jax.ShapeDtypeStruct((n_slots * ROW_SUBLANES, 128), F32),
        compiler_params=_cparams(("arbitrary",)),
        name="dispatch_rows",
    )(pend, dest_tiles, h2)


def _moe(h2, route, counts, n_tiles, layer, w_gate, w_up, w_down):
    t, d = h2.shape
    dest, pend_f = _plan(route, counts, t)
    pend = pend_f[0, :N_EXPERTS].astype(jnp.int32)
    n_blk = -(-(t * TOP_K) // MOE_BLOCK) + N_EXPERTS
    blk_start = jnp.arange(n_blk, dtype=jnp.int32) * MOE_BLOCK
    blk_exp = jnp.minimum(jnp.sum((pend[None, :] <= blk_start[:, None]).astype(jnp.int32), axis=1), N_EXPERTS - 1)
    n_used = (pend[N_EXPERTS - 1] // MOE_BLOCK).reshape(1)
    blk_exp = jnp.where(jnp.arange(n_blk) < n_used[0], blk_exp, blk_exp[n_used[0] - 1])
    has_rows = pend > jnp.concatenate([jnp.zeros((1,), jnp.int32), pend[:-1]])
    e_ids = jnp.arange(N_EXPERTS, dtype=jnp.int32)
    later = jnp.where(jnp.logical_and(has_rows[None, :], e_ids[None, :] > e_ids[:, None]), e_ids[None, :], N_EXPERTS)
    next_exp = jnp.min(later, axis=1)
    next_exp = jnp.where(next_exp < N_EXPERTS, next_exp, -1).astype(jnp.int32)
    dest2 = dest[:, :TOP_K]
    dest_tiles = dest2.reshape(n_tiles, ROW_TILE, TOP_K).transpose(0, 2, 1).reshape(n_tiles, 1, TOP_K * ROW_TILE)
    xs = _dispatch(pend, dest_tiles, h2, n_blk * MOE_BLOCK, n_tiles)
    ys = _experts(blk_exp.astype(jnp.int32), n_used, next_exp, xs, layer, w_gate, w_up, w_down)
    return ys[dest2[:, 0]], ys[dest2[:, 1]]


def _final_kernel(x_ref, y0_ref, y1_ref, rt_ref, mod_ref, g_ref, o_ref):
    x = _moe_residual(x_ref, y0_ref, y1_ref, rt_ref, mod_ref)
    o_ref[...] = x * lax.rsqrt(jnp.mean(x * x, axis=-1, keepdims=True) + EPS) * g_ref[...]


def _final(x_lat, moe_out, mod, g_final, n_rows, seq, batch):
    d = D_MODEL
    tile = MERGE_TILE
    assert n_rows % tile == 0 and seq % tile == 0
    row = lambda i: (i, 0)
    return pl.pallas_call(
        _final_kernel,
        grid=(n_rows // tile,),
        in_specs=[pl.BlockSpec((tile, d), row), pl.BlockSpec((tile, d), row),
                  pl.BlockSpec((tile, d), row), pl.BlockSpec((tile, ROUTER_PAD), row),
                  pl.BlockSpec((1, 6, d), lambda i: (jnp.minimum(i // (seq // tile), batch), 0, 0)),
                  pl.BlockSpec((1, d), lambda i: (0, 0))],
        out_specs=pl.BlockSpec((tile, d), row),
        out_shape=jax.ShapeDtypeStruct((n_rows, d), F32),
        compiler_params=_cparams(("parallel",)),
        name="final_norm",
    )(x_lat, *moe_out, mod, g_final)


def _rope_tables(seq):
    quarter = HEAD_DIM // 4
    freqs = ROPE_THETA ** (-jnp.arange(quarter, dtype=F32) / quarter)
    pos = jnp.arange(seq)
    ang_r = (pos // GRID_W).astype(F32)[:, None] * freqs[None, :]
    ang_c = (pos % GRID_W).astype(F32)[:, None] * freqs[None, :]
    cos = jnp.concatenate([jnp.cos(ang_r)] * 2 + [jnp.cos(ang_c)] * 2, axis=-1)
    sin = jnp.concatenate([-jnp.sin(ang_r), jnp.sin(ang_r), -jnp.sin(ang_c), jnp.sin(ang_c)], axis=-1)
    reps = 128 // HEAD_DIM
    cos = jnp.tile(cos, (1, reps))
    sin = jnp.tile(sin, (1, reps))
    cos = jnp.concatenate([cos, jnp.ones((ROW_TILE, 128), F32)], axis=0)
    sin = jnp.concatenate([sin, jnp.zeros((ROW_TILE, 128), F32)], axis=0)
    return cos, sin


def kernel(x, c, ctx, c_ctx, w_mod, b_mod, g_norm1, g_norm2, w_in, attn_sink, ssm_lam_re, ssm_lam_im,
           ssm_log_dt, ssm_b_re, ssm_b_im, ssm_c_re, ssm_c_im, ssm_d, w_glu, w_br_attn, w_br_ssm, w_out,
           w_router_group, w_router_expert, w_exp_gate, w_exp_up, w_exp_down, g_final):
    batch, seq, d = x.shape
    ctx_len = ctx.shape[1]
    depth = w_mod.shape[0]
    assert d == D_MODEL and batch * SSM_STATE == 128
    assert seq % ROW_TILE == 0 and ctx_len % ROW_TILE == 0 and (batch * seq) % ctx_len == 0
    assert batch + 1 <= MOD_ROWS
    t_lat, t_ctx = batch * seq, batch * ctx_len
    tiles_per_seq = seq // ROW_TILE
    n_lat_tiles = t_lat // ROW_TILE
    n_all_tiles = (t_lat + t_ctx) // ROW_TILE
    nb_per_seq = seq // ATTN_BLOCK
    ncc, ncl = ctx_len // SSM_CHUNK, seq // SSM_CHUNK

    c_rows = jnp.zeros((MOD_ROWS, d), F32).at[:batch].set(c).at[batch].set(c_ctx)
    mod_all = _modulation(c_rows, w_mod, b_mod).reshape(depth, MOD_ROWS, 6, d)
    cos_t, sin_t = _rope_tables(seq)
    ssm_w = _ssm_weights_all(ssm_lam_re, ssm_lam_im, ssm_log_dt, ssm_b_re, ssm_b_im, ssm_c_re, ssm_c_im, ssm_d, batch)
    x_parts = (x.reshape(t_lat, d), ctx.reshape(t_ctx, d))
    f_all = None
    for l in range(depth):
        ctx_out = l < depth - 1
        mod = mod_all[l]
        x_parts, (q, k, v, u, ga, gs) = _inproj(
            x_parts, f_all, mod_all[l - 1] if l > 0 else None, mod, g_norm1[l].reshape(1, d), cos_t, sin_t,
            w_in[l].astype(BF16), n_all_tiles, tiles_per_seq, n_lat_tiles, batch)
        sink = attn_sink[l].astype(F32)
        attn = _attention(sink, q, k, v, t_lat // ATTN_BLOCK, nb_per_seq, t_lat // ctx_len, ctx_len, True, 0)
        z, z_ctx = _ssm_mixer(u, l, *ssm_w, batch, seq, ctx_len)
        ctx_parts = None
        if ctx_out:
            attn_c = _attention(sink, q, k, v, t_ctx // ATTN_BLOCK, ctx_len // ATTN_BLOCK, t_lat // ctx_len,
                                ctx_len, False, t_lat // ATTN_BLOCK)
            ctx_parts = (attn_c, z_ctx, x_parts[1])
        w_r = jnp.zeros((d, ROUTER_PAD), F32)
        w_r = w_r.at[:, :N_EXPERT_GROUPS].set(w_router_group[l])
        w_r = w_r.at[:, N_EXPERT_GROUPS:N_EXPERT_GROUPS + N_EXPERTS].set(w_router_expert[l])
        wr_hi, wr_lo = _split_bf16(w_r)
        n_rows = t_lat + t_ctx if ctx_out else t_lat
        x_all, h2, route, counts = _merge(
            attn, z, x_parts[0], ctx_parts, ga, gs, mod, g_norm2[l].reshape(1, d), w_glu[l].astype(BF16),
            w_br_attn[l].astype(BF16), w_br_ssm[l].astype(BF16), w_out[l].astype(BF16),
            wr_hi, wr_lo, n_rows, t_lat, seq, batch)
        y0, y1 = _moe(h2, route, counts, n_rows // ROW_TILE, l, w_exp_gate, w_exp_up, w_exp_down)
        f_all = (y0, y1, route)
        x_parts = (x_all,)
    out = _final(x_all, f_all, mod_all[depth - 1], g_final.reshape(1, d), t_lat, seq, batch)
    return out.reshape(batch, seq, d)
```

```python
import functools
import math

import jax
import jax.numpy as jnp
from jax import lax
from jax.experimental import pallas as pl
from jax.experimental.pallas import tpu as pltpu

F32 = jnp.float32
BF16 = jnp.bfloat16

D_MODEL = 1024
GRID_W = 64
N_HEADS = 8
N_KV_HEADS = 2
HEAD_DIM = 64
Q_GROUP = N_HEADS // N_KV_HEADS
ATTN_BLOCK = 128
ATTN_STEP_BLOCKS = 4
ROPE_THETA = 10000.0
D_SSM = D_MODEL // 2
SSM_GROUP = 16
N_SSM_GROUPS = D_SSM // SSM_GROUP
SSM_STATE = 64
N_EXPERT_GROUPS = 4
EXPERTS_PER_GROUP = 8
N_EXPERTS = N_EXPERT_GROUPS * EXPERTS_PER_GROUP
TOP_K = 2
D_EXPERT = D_MODEL // 2
Q_W = N_HEADS * HEAD_DIM
KV_W = N_KV_HEADS * HEAD_DIM
O_K = Q_W
O_V = O_K + KV_W
O_U = O_V + KV_W
O_GA = O_U + D_SSM
O_GS = O_GA + D_MODEL
D_IN = O_GS + D_MODEL
EPS = 1e-6
NEG_INF = -1e30

ROW_TILE = 256
MERGE_TILE = 512
MERGE_SUB = 256
SSM_CHUNK = 16
SSM_CW = SSM_CHUNK * SSM_GROUP
SSM_SCAN_GROUPS = 8
SSM_LANE_GROUPS = 128 // SSM_GROUP
SSM_LANE_TILES = D_SSM // 128
MOE_BLOCK = 256
ROW_SUBLANES = D_MODEL // 128
PLAN_TILE = 512
ROUTER_PAD = 128
MOD_ROWS = 8
VMEM_LIMIT = 48 * 1024 * 1024


def _cparams(sem):
    return pltpu.CompilerParams(dimension_semantics=sem, vmem_limit_bytes=VMEM_LIMIT)


def _dot(a, b):
    return jnp.dot(a, b, preferred_element_type=F32)


def _split_bf16(a):
    hi = a.astype(BF16)
    lo = (a - hi.astype(F32)).astype(BF16)
    return hi, lo


def _rms_mod(x, g, shift, scale):
    y = x * lax.rsqrt(jnp.mean(x * x, axis=-1, keepdims=True) + EPS) * g
    return y * (1.0 + scale) + shift


def _mod_kernel(c_ref, w_ref, b_ref, o_ref):
    c = c_ref[...]
    s_hi, s_lo = _split_bf16(c * jax.nn.sigmoid(c))
    w_hi, w_lo = _split_bf16(w_ref[0])
    o_ref[0] = _dot(s_hi, w_hi) + _dot(s_lo, w_hi) + _dot(s_hi, w_lo) + b_ref[0]


def _modulation(c_rows, w_mod, b_mod):
    depth, d, n = w_mod.shape
    nb = n // 4
    return pl.pallas_call(
        _mod_kernel,
        grid=(depth, n // nb),
        in_specs=[
            pl.BlockSpec((MOD_ROWS, d), lambda l, j: (0, 0)),
            pl.BlockSpec((1, d, nb), lambda l, j: (l, 0, j)),
            pl.BlockSpec((1, 1, nb), lambda l, j: (l, 0, j)),
        ],
        out_specs=pl.BlockSpec((1, MOD_ROWS, nb), lambda l, j: (l, 0, j)),
        out_shape=jax.ShapeDtypeStruct((depth, MOD_ROWS, n), F32),
        compiler_params=_cparams(("arbitrary", "arbitrary")),
        name="modulation",
    )(c_rows, w_mod, b_mod.reshape(depth, 1, n))


def _moe_residual(x_ref, y0_ref, y1_ref, rt_ref, mod_ref):
    rt = rt_ref[...]
    f = rt[:, 2:3] * y0_ref[...].astype(F32) + rt[:, 3:4] * y1_ref[...].astype(F32)
    return x_ref[...] + mod_ref[0, 5:6, :] * f


def _inproj_kernel(has_f, n_lat_tiles, *refs):
    if has_f:
        (x_ref, y0_ref, y1_ref, rt_ref, modp_ref, mod_ref, g_ref, cos_ref, sin_ref, w_ref,
         xo_ref, q_ref, k_ref, v_ref, u_ref, ga_ref, gs_ref) = refs
        x = _moe_residual(x_ref, y0_ref, y1_ref, rt_ref, modp_ref)
        xo_ref[...] = x
    else:
        (xl_ref, xc_ref, mod_ref, g_ref, cos_ref, sin_ref, w_ref,
         q_ref, k_ref, v_ref, u_ref, ga_ref, gs_ref) = refs
        x = jnp.where(pl.program_id(0) >= n_lat_tiles, xc_ref[...], xl_ref[...])
    m = mod_ref[0]
    h = _rms_mod(x, g_ref[...], m[0:1], m[1:2]).astype(BF16)
    cos = cos_ref[...]
    sin = sin_ref[...]
    lane = lax.broadcasted_iota(jnp.int32, cos.shape, 1)
    first = (lane % (HEAD_DIM // 2)) < (HEAD_DIM // 4)

    def rope(t):
        sw = jnp.where(first, pltpu.roll(t, 128 - HEAD_DIM // 4, 1), pltpu.roll(t, HEAD_DIM // 4, 1))
        return t * cos + sw * sin

    def proj(lo, hi):
        return _dot(h, w_ref[:, lo:hi])

    q = proj(0, O_K)
    for j in range(Q_W // 128):
        q_ref[:, 128 * j:128 * (j + 1)] = (rope(q[:, 128 * j:128 * (j + 1)]) * HEAD_DIM ** -0.5).astype(BF16)
    kv = proj(O_K, O_U)
    k_ref[...] = rope(kv[:, :KV_W]).astype(BF16)
    v_ref[...] = kv[:, KV_W:].astype(BF16)
    u = proj(O_U, O_GA)
    for j in range(SSM_LANE_TILES):
        u_ref[j] = u[:, 128 * j:128 * (j + 1)]
    ga_ref[...] = jax.nn.sigmoid(proj(O_GA, O_GS)).astype(BF16)
    gs_ref[...] = jax.nn.sigmoid(proj(O_GS, D_IN)).astype(BF16)


def _inproj(x_parts, moe_out, mod_prev, mod, g1, cos_t, sin_t, w_in, n_tiles, tiles_per_seq, n_lat_tiles, batch):
    d = D_MODEL
    has_f = moe_out is not None
    row = lambda i: (i, 0)
    modi = lambda i: (jnp.minimum(i // tiles_per_seq, batch), 0, 0)
    const = lambda i: (0, 0)
    ropei = lambda i: (jnp.where(i < n_lat_tiles, i % tiles_per_seq, tiles_per_seq), 0)
    if has_f:
        in_specs = [pl.BlockSpec((ROW_TILE, d), row), pl.BlockSpec((ROW_TILE, d), row),
                    pl.BlockSpec((ROW_TILE, d), row), pl.BlockSpec((ROW_TILE, ROUTER_PAD), row),
                    pl.BlockSpec((1, 6, d), modi)]
        args = [*x_parts, *moe_out, mod_prev]
    else:
        in_specs = [pl.BlockSpec((ROW_TILE, d), lambda i: (jnp.minimum(i, n_lat_tiles - 1), 0)),
                    pl.BlockSpec((ROW_TILE, d), lambda i: (jnp.maximum(i - n_lat_tiles, 0), 0))]
        args = list(x_parts)
    in_specs += [
        pl.BlockSpec((1, 6, d), modi),
        pl.BlockSpec((1, d), const),
        pl.BlockSpec((ROW_TILE, 128), ropei),
        pl.BlockSpec((ROW_TILE, 128), ropei),
        pl.BlockSpec((d, D_IN), const),
    ]
    args += [mod, g1, cos_t, sin_t, w_in]
    widths = [Q_W, KV_W, KV_W, D_SSM, D_MODEL, D_MODEL]
    out_specs = [pl.BlockSpec((ROW_TILE, w), row) for w in widths]
    out_shape = [jax.ShapeDtypeStruct((n_tiles * ROW_TILE, w), BF16) for w in widths]
    out_specs[3] = pl.BlockSpec((SSM_LANE_TILES, ROW_TILE, 128), lambda i: (0, i, 0))
    out_shape[3] = jax.ShapeDtypeStruct((SSM_LANE_TILES, n_tiles * ROW_TILE, 128), F32)
    if has_f:
        out_specs = [pl.BlockSpec((ROW_TILE, d), row)] + out_specs
        out_shape = [jax.ShapeDtypeStruct((n_tiles * ROW_TILE, d), F32)] + out_shape
    outs = pl.pallas_call(
        functools.partial(_inproj_kernel, has_f, n_lat_tiles),
        grid=(n_tiles,),
        in_specs=in_specs,
        out_specs=out_specs,
        out_shape=out_shape,
        compiler_params=_cparams(("parallel",)),
        name="inproj",
    )(*args)
    if has_f:
        return (outs[0],), outs[1:]
    return x_parts, outs


def _attn_scores(q, k_tiles, v_tiles):
    res = []
    for g in range(N_KV_HEADS):
        gs = slice(g * HEAD_DIM, (g + 1) * HEAD_DIM)
        k_all = jnp.concatenate([t[:, gs] for t in k_tiles], axis=0)
        v_all = jnp.concatenate([t[:, gs] for t in v_tiles], axis=0)
        v_ext = jnp.concatenate([v_all, jnp.ones_like(v_all)], axis=1)
        qg = jnp.concatenate(
            [q[:, (g * Q_GROUP + h) * HEAD_DIM:(g * Q_GROUP + h + 1) * HEAD_DIM] for h in range(Q_GROUP)], axis=0)
        res.append((lax.dot_general(k_all, qg, (((1,), (1,)), ((), ())), preferred_element_type=F32), v_ext))
    return res


def _attn_finish(sink_ref, scores, tile_rows, biases, blk):
    outs = []
    for g, (st, v_ext) in enumerate(scores):
        tiles, row = [], 0
        for n, bias in zip(tile_rows, biases):
            t = st[row:row + n, :]
            tiles.append(t if bias is None else t + jnp.concatenate([bias] * Q_GROUP, axis=1))
            row += n
        mx = None
        for t in tiles:
            for r0 in range(0, t.shape[0], blk):
                mx = t[r0:r0 + blk] if mx is None else jnp.maximum(mx, t[r0:r0 + blk])
        sink = jnp.concatenate([jnp.full((1, blk), sink_ref[g * Q_GROUP + h], F32) for h in range(Q_GROUP)], axis=1)
        m = jnp.maximum(jnp.max(mx, axis=0, keepdims=True), sink)
        p = jnp.exp(jnp.concatenate([(t - m).astype(BF16) for t in tiles], axis=0))
        o_t = lax.dot_general(v_ext, p, (((0,), (0,)), ((), ())), preferred_element_type=F32)
        denom = o_t[HEAD_DIM:HEAD_DIM + 1, :] + jnp.exp(sink - m)
        o_n = o_t[:HEAD_DIM, :] * (1.0 / denom)
        for h in range(Q_GROUP):
            outs.append(o_n[:, h * blk:(h + 1) * blk].T.astype(BF16))
    return jnp.concatenate(outs, axis=1)


def _attn_kernel(band, nb_per_seq, sink_ref, *refs):
    blk = ATTN_BLOCK
    if not band:
        q_ref, kx_ref, vx_ref, o_ref = refs
        n_sub = q_ref.shape[0] // blk
        scores = [_attn_scores(q_ref[a * blk:(a + 1) * blk, :], [kx_ref[...]], [vx_ref[...]]) for a in range(n_sub)]
        for a in range(n_sub):
            o_ref[a * blk:(a + 1) * blk, :] = _attn_finish(sink_ref, scores[a], [kx_ref.shape[0]], [None], blk)
        return
    q_ref, kp_ref, kc_ref, kn_ref, vp_ref, vc_ref, vn_ref, kx_ref, vx_ref, o_ref = refs
    n_sub = q_ref.shape[0] // blk
    j0 = (pl.program_id(0) * n_sub) % nb_per_seq
    r = lax.broadcasted_iota(jnp.int32, (blk, blk), 0)
    c = lax.broadcasted_iota(jnp.int32, (blk, blk), 1)
    k_blocks = [kp_ref[...]] + [kc_ref[a * blk:(a + 1) * blk, :] for a in range(n_sub)] + [kn_ref[...]]
    v_blocks = [vp_ref[...]] + [vc_ref[a * blk:(a + 1) * blk, :] for a in range(n_sub)] + [vn_ref[...]]
    scores = [_attn_scores(q_ref[a * blk:(a + 1) * blk, :], k_blocks[a:a + 3] + [kx_ref[...]],
                           v_blocks[a:a + 3] + [vx_ref[...]]) for a in range(n_sub)]
    for a in range(n_sub):
        edge_p = jnp.where(j0 + a > 0, 0.0, NEG_INF).astype(F32)
        edge_n = jnp.where(j0 + a < nb_per_seq - 1, 0.0, NEG_INF).astype(F32)
        bias_p = jnp.where(r >= c, edge_p, NEG_INF).astype(F32)
        bias_n = jnp.where(r <= c, edge_n, NEG_INF).astype(F32)
        o_ref[a * blk:(a + 1) * blk, :] = _attn_finish(
            sink_ref, scores[a], [blk, blk, blk, kx_ref.shape[0]], [bias_p, None, bias_n, None], blk)


def _attention(sink, q, k, v, n_blocks, nb_per_seq, kx_block0, ctx_len, band, q_block0):
    blk = ATTN_BLOCK
    n_sub = math.gcd(ATTN_STEP_BLOCKS, nb_per_seq, n_blocks, q_block0)
    step = n_sub * blk
    n_steps = n_blocks // n_sub
    qi = lambda i, s: (q_block0 // n_sub + i, 0)
    cur = lambda i, s: (i, 0)
    prv = lambda i, s: (jnp.maximum(i * n_sub - 1, 0), 0)
    nxt = lambda i, s: (jnp.minimum((i + 1) * n_sub, n_blocks - 1), 0)
    kxi = lambda i, s: (kx_block0 + (i * n_sub) // nb_per_seq, 0)
    kspec = lambda f: pl.BlockSpec((blk, KV_W), f)
    cspec = pl.BlockSpec((step, KV_W), cur)
    xspec = pl.BlockSpec((ctx_len, KV_W), kxi)
    if band:
        in_specs = [pl.BlockSpec((step, Q_W), qi), kspec(prv), cspec, kspec(nxt),
                    kspec(prv), cspec, kspec(nxt), xspec, xspec]
        args = (q, k, k, k, v, v, v, k, v)
    else:
        in_specs = [pl.BlockSpec((step, Q_W), qi), xspec, xspec]
        args = (q, k, v)
    return pl.pallas_call(
        functools.partial(_attn_kernel, band, nb_per_seq),
        grid_spec=pltpu.PrefetchScalarGridSpec(
            num_scalar_prefetch=1,
            grid=(n_steps,),
            in_specs=in_specs,
            out_specs=pl.BlockSpec((step, Q_W), lambda i, s: (i, 0)),
        ),
        out_shape=jax.ShapeDtypeStruct((n_blocks * blk, Q_W), BF16),
        compiler_params=_cparams(("parallel",)),
        name="band_attention" if band else "context_attention",
    )(sink, *args)


def _dot_f32(a, b_t):
    a_hi, a_lo = _split_bf16(a)
    b_hi, b_lo = _split_bf16(b_t)
    dn = (((1,), (1,)), ((), ()))
    dg = functools.partial(lax.dot_general, dimension_numbers=dn, preferred_element_type=F32)
    return dg(a_hi, b_hi) + dg(a_lo, b_hi) + dg(a_hi, b_lo)


def _ssm_weight_kernel(batch, lam_r_ref, bt_ref, c_ref, d_ref, w1_ref, abar_ref, t_ref, wo_ref):
    lc, mm, p = SSM_CHUNK, SSM_GROUP, SSM_STATE
    up_r = lax.broadcasted_iota(jnp.int32, (lc, 1), 0).astype(F32)
    lane = lax.broadcasted_iota(jnp.int32, (mm, lc * mm), 1)
    row = lax.broadcasted_iota(jnp.int32, (mm, lc * mm), 0)
    planes_in, planes_out, abar_rows, kt = [], [], [], []
    for d in range(2):
        lr, li, dt = lam_r_ref[0, d, 0:1, :], lam_r_ref[0, d, 1:2, :], jnp.exp(lam_r_ref[0, d, 2:3, :])

        def powers(expo):
            mag = jnp.exp(lr * dt * expo)
            return mag * jnp.cos(li * dt * expo), mag * jnp.sin(li * dt * expo)

        a_re, a_im = powers(jnp.ones((1, 1), F32))
        den = lr * lr + li * li
        nr = a_re - 1.0
        f_re = (nr * lr + a_im * li) / den
        f_im = (a_im * lr - nr * li) / den
        bt_re, bt_im = bt_ref[0, d, 0], bt_ref[0, d, 1]
        bbt_re = f_re * bt_re - f_im * bt_im
        bbt_im = f_re * bt_im + f_im * bt_re
        c_re, c_im = c_ref[0, d, 0], c_ref[0, d, 1]

        pr, pi = powers(lc - 1.0 - up_r if d == 0 else up_r)
        planes_in.append(jnp.concatenate(
            [pr[s:s + 1] * bbt_re - pi[s:s + 1] * bbt_im for s in range(lc)], axis=0))
        planes_in.append(jnp.concatenate(
            [pr[s:s + 1] * bbt_im + pi[s:s + 1] * bbt_re for s in range(lc)], axis=0))

        qr, qi = powers(up_r + 1.0 if d == 0 else lc - up_r)
        planes_out.append(jnp.concatenate(
            [c_re * qr[t:t + 1] - c_im * qi[t:t + 1] for t in range(lc)], axis=0).T)
        planes_out.append(jnp.concatenate(
            [-c_re * qi[t:t + 1] - c_im * qr[t:t + 1] for t in range(lc)], axis=0).T)

        kr, ki = powers(up_r if d == 0 else lc - 1.0 - up_r)
        cp_re = jnp.concatenate([c_re * kr[j:j + 1] - c_im * ki[j:j + 1] for j in range(lc)], axis=0)
        cp_im = jnp.concatenate([c_re * ki[j:j + 1] + c_im * kr[j:j + 1] for j in range(lc)], axis=0)
        kt.append(_dot_f32(jnp.concatenate([bbt_re, bbt_im], axis=1), jnp.concatenate([cp_re, -cp_im], axis=1)))

        e_re, e_im = powers(jnp.full((1, 1), float(lc), F32))
        abar_rows += [jnp.concatenate([e_re] * batch, axis=1), jnp.concatenate([e_im] * batch, axis=1)]

    dvec = d_ref[0]
    blocks = []
    for s in range(lc):
        fwd = kt[0] if s == 0 else pltpu.roll(kt[0], mm * s, 1)
        bwd = kt[1] if s == lc - 1 else pltpu.roll(kt[1], mm * (s + 1), 1)
        blk = jnp.where(lane >= mm * s, fwd, 0.0) + jnp.where(lane < mm * (s + 1), bwd, 0.0)
        blocks.append(blk + jnp.where(lane == mm * s + row, dvec, 0.0))
    t_ref[0] = jnp.concatenate(blocks, axis=0).astype(BF16)
    abar_ref[0] = jnp.concatenate(abar_rows, axis=0)
    zero_in = jnp.zeros((lc * mm, p), F32)
    zero_out = jnp.zeros((p, lc * mm), F32)
    for b in range(batch):
        w1_ref[0, b] = jnp.concatenate(
            [pl_ if bb == b else zero_in for pl_ in planes_in for bb in range(batch)], axis=1).astype(BF16)
        wo_ref[0, b] = jnp.concatenate(
            [pl_ if bb == b else zero_out for pl_ in planes_out for bb in range(batch)], axis=0).astype(BF16)


def _ssm_weights_all(lam_re, lam_im, log_dt, b_re, b_im, c_re, c_im, d_skip, batch):
    depth = lam_re.shape[0]
    g, p, mm, lc = N_SSM_GROUPS, SSM_STATE, SSM_GROUP, SSM_CHUNK
    n = depth * g
    cw, sw = lc * mm, 4 * batch * p

    def per_group(a):
        return jnp.moveaxis(a.astype(F32), 2, 1).reshape((n, 2) + a.shape[3:])

    lam = jnp.stack([per_group(lam_re), per_group(lam_im),
                     jnp.broadcast_to(per_group(log_dt)[..., None], (n, 2, p))], axis=2)
    bt = jnp.stack([per_group(b_re), per_group(b_im)], axis=2).swapaxes(-1, -2)
    c = jnp.stack([per_group(c_re), per_group(c_im)], axis=2)
    d_t = jnp.tile(d_skip.astype(F32).reshape(n, 1, mm), (1, 1, lc))
    full = lambda *shape: pl.BlockSpec((1,) + shape, lambda i: (i,) + (0,) * len(shape))
    w1, abar, tmat, wout = pl.pallas_call(
        functools.partial(_ssm_weight_kernel, batch),
        grid=(n,),
        in_specs=[full(2, 3, p), full(2, 2, mm, p), full(2, 2, mm, p), full(1, cw)],
        out_specs=[full(batch, cw, sw), full(4, batch * p), full(cw, cw), full(batch, sw, cw)],
        out_shape=[jax.ShapeDtypeStruct((n, batch, cw, sw), BF16), jax.ShapeDtypeStruct((n, 4, batch * p), F32),
                   jax.ShapeDtypeStruct((n, cw, cw), BF16), jax.ShapeDtypeStruct((n, batch, sw, cw), BF16)],
        compiler_params=_cparams(("parallel",)),
        name="ssm_weights",
    )(lam, bt, c, d_t)
    abar = abar.reshape(depth, g, 4, batch * p).transpose(0, 2, 1, 3)
    return (w1.reshape(depth, g, batch, cw, sw), abar, tmat.reshape(depth, g, cw, cw),
            wout.reshape(depth, g, batch, sw, cw))


def _ssm_p1_kernel(ul_ref, uc_ref, w_ref, v_ref, *s_refs):
    b = pl.program_id(1)
    ncc, ncl = uc_ref.shape[0] // SSM_CHUNK, ul_ref.shape[0] // SSM_CHUNK
    nc = ncc + ncl
    pw = s_refs[0].shape[1]
    xs = [jnp.concatenate([uc_ref[pl.ds(s, ncc, stride=SSM_CHUNK), :], ul_ref[pl.ds(s, ncl, stride=SSM_CHUNK), :]],
                          axis=0).astype(BF16) for s in range(SSM_CHUNK)]
    for j in range(SSM_LANE_GROUPS):
        v = jnp.concatenate([x[:, j * SSM_GROUP:(j + 1) * SSM_GROUP] for x in xs], axis=1)
        v_ref[j] = v
        acc = _dot(v, w_ref[j])
        for k, s_ref in enumerate(s_refs):
            part = acc[:, k * pw:(k + 1) * pw]
            rows = slice(j * nc, (j + 1) * nc)

            @pl.when(b == 0)
            def _():
                s_ref[rows, :] = part

            @pl.when(b > 0)
            def _():
                s_ref[rows, :] = s_ref[rows, :] + part


def _ssm_scan_kernel(ncc, ncl, a_ref, sfr_ref, sfi_ref, sbr_ref, sbi_ref, xfr_ref, xfi_ref, xbr_ref, xbi_ref):
    nc = ncc + ncl
    gb = a_ref.shape[1]
    pw = a_ref.shape[2]
    afr, afi, abr, abi = a_ref[0], a_ref[1], a_ref[2], a_ref[3]

    def rows(r):
        return pl.ds(r, gb, stride=nc)

    def step(rf, rb, carry):
        xfr, xfi, xbr, xbi = carry
        xfr_ref[rows(rf), :] = xfr
        xfi_ref[rows(rf), :] = xfi
        xbr_ref[rows(rb), :] = xbr
        xbi_ref[rows(rb), :] = xbi
        sfr = sfr_ref[rows(rf), :]
        sfi = sfi_ref[rows(rf), :]
        sbr = sbr_ref[rows(rb), :]
        sbi = sbi_ref[rows(rb), :]
        return (afr * xfr - afi * xfi + sfr, afr * xfi + afi * xfr + sfi,
                abr * xbr - abi * xbi + sbr, abr * xbi + abi * xbr + sbi)

    zero = jnp.zeros((gb, pw), F32)
    carry = lax.fori_loop(0, ncc, lambda t, c: step(t, ncc - 1 - t, c), (zero, zero, zero, zero), unroll=4)
    lax.fori_loop(0, ncl, lambda t, c: step(ncc + t, nc - 1 - t, c), carry, unroll=4)


def _ssm_p3_kernel(v_ref, xfr_ref, xfi_ref, xbr_ref, xbi_ref, t_ref, wo_ref, zl_ref, zc_ref):
    ncc, ncl = zc_ref.shape[0] // SSM_CHUNK, zl_ref.shape[0] // SSM_CHUNK
    nc = ncc + ncl
    ys = []
    for j in range(SSM_LANE_GROUPS):
        rows = slice(j * nc, (j + 1) * nc)
        xin = jnp.concatenate([xfr_ref[rows, :], xfi_ref[rows, :], xbr_ref[rows, :], xbi_ref[rows, :]],
                              axis=1).astype(BF16)
        y = _dot(v_ref[j], t_ref[j]) + _dot(xin, wo_ref[j])
        ys.append(jax.nn.gelu(y, approximate=True))
    for t in range(SSM_CHUNK):
        zt = jnp.concatenate([y[:, t * SSM_GROUP:(t + 1) * SSM_GROUP] for y in ys], axis=1)
        zc_ref[pl.ds(t, ncc, stride=SSM_CHUNK), :] = zt[:ncc]
        zl_ref[pl.ds(t, ncl, stride=SSM_CHUNK), :] = zt[ncc:]


def _ssm_mixer(u4, layer, w1, abar, tmat, wout, batch, seq, ctx_len):
    n_lt = u4.shape[0]
    g = N_SSM_GROUPS
    lg = SSM_LANE_GROUPS
    ncc, ncl = ctx_len // SSM_CHUNK, seq // SSM_CHUNK
    nc = ncc + ncl
    cw = SSM_CW
    sw = w1.shape[-1]
    pw = sw // 4
    ctx_blk0 = (batch * seq) // ctx_len
    plane_shape = [jax.ShapeDtypeStruct((g * nc, pw), F32)] * 4
    lat_spec = pl.BlockSpec((None, seq, 128), lambda k, b: (k, b, 0))
    ctx_spec = pl.BlockSpec((None, ctx_len, 128), lambda k, b: (k, ctx_blk0 + b, 0))
    v_spec = pl.BlockSpec((lg, None, nc, cw), lambda k, b: (k, b, 0, 0))
    plane_spec = pl.BlockSpec((lg * nc, pw), lambda k, b: (k, 0))
    v_chunks, *s_planes = pl.pallas_call(
        _ssm_p1_kernel,
        grid=(n_lt, batch),
        in_specs=[lat_spec, ctx_spec,
                  pl.BlockSpec((None, lg, None, cw, sw), lambda k, b: (layer, k, b, 0, 0))],
        out_specs=[v_spec] + [plane_spec] * 4,
        out_shape=[jax.ShapeDtypeStruct((g, batch, nc, cw), BF16)] + plane_shape,
        compiler_params=_cparams(("parallel", "arbitrary")),
        name="ssm_chunk_states",
    )(u4, u4, w1)
    gb = SSM_SCAN_GROUPS
    x_planes = pl.pallas_call(
        functools.partial(_ssm_scan_kernel, ncc, ncl),
        grid=(g // gb,),
        in_specs=[pl.BlockSpec((None, 4, gb, pw), lambda i: (layer, 0, i, 0))]
        + [pl.BlockSpec((gb * nc, pw), lambda i: (i, 0))] * 4,
        out_specs=[pl.BlockSpec((gb * nc, pw), lambda i: (i, 0))] * 4,
        out_shape=plane_shape,
        compiler_params=_cparams(("parallel",)),
        name="ssm_chunk_scan",
    )(abar, *s_planes)
    return pl.pallas_call(
        _ssm_p3_kernel,
        grid=(n_lt, batch),
        in_specs=[v_spec] + [plane_spec] * 4
        + [pl.BlockSpec((None, lg, cw, cw), lambda k, b: (layer, k, 0, 0)),
           pl.BlockSpec((None, lg, None, sw, cw), lambda k, b: (layer, k, b, 0, 0))],
        out_specs=[pl.BlockSpec((None, seq, 128), lambda k, b: (k, b, 0)),
                   pl.BlockSpec((None, ctx_len, 128), lambda k, b: (k, b, 0))],
        out_shape=[jax.ShapeDtypeStruct((n_lt, batch * seq, 128), F32),
                   jax.ShapeDtypeStruct((n_lt, batch * ctx_len, 128), F32)],
        compiler_params=_cparams(("parallel", "arbitrary")),
        name="ssm_chunk_outputs",
    )(v_chunks, *x_planes, tmat, wout)


def _merge_kernel(n_lat_tiles, *refs):
    if n_lat_tiles is None:
        attn_ref, z_ref, x_ref = refs[:3]
        refs = refs[3:]
    else:
        attn_ref, z_ref, x_ref, attn_c_ref, z_c_ref, x_c_ref = refs[:6]
        refs = refs[6:]
        is_ctx = pl.program_id(0) >= n_lat_tiles
    (ga_ref, gs_ref, mod_ref, modl_ref, g2_ref, wglu_ref, wba_ref, wbs_ref, wout_ref, wrh_ref, wrl_ref,
     xo_ref, h2_ref, rt_ref, cnt_ref, x_s) = refs
    step = pl.program_id(0)
    m = mod_ref[0]
    ml = modl_ref[0]
    tile_counts = None

    @pl.when(step == 0)
    def _():
        x_s[...] = jnp.zeros_like(x_s)

    for r in range(MERGE_TILE // MERGE_SUB):
        rows = slice(r * MERGE_SUB, (r + 1) * MERGE_SUB)
        h2 = _rms_mod(x_s[rows, :], g2_ref[...], ml[3:4], ml[4:5])
        hi, lo = _split_bf16(h2)
        h2_ref[rows, :] = hi.astype(F32)
        logits = _dot(hi, wrh_ref[...]) + _dot(lo, wrh_ref[...]) + _dot(hi, wrl_ref[...])
        rt = _route(logits)
        rt_ref[rows, :] = rt
        oh0, oh1 = _expert_onehots(rt)
        part = jnp.sum(oh0 + oh1, axis=0, keepdims=True)
        tile_counts = part if tile_counts is None else tile_counts + part

        zf = jnp.concatenate([z_ref[j, rows, :] for j in range(SSM_LANE_TILES)], axis=1)
        attn = attn_ref[rows, :]
        x_in = x_ref[rows, :]
        if n_lat_tiles is not None:
            zf = jnp.where(is_ctx, jnp.concatenate([z_c_ref[j, rows, :] for j in range(SSM_LANE_TILES)], axis=1), zf)
            attn = jnp.where(is_ctx, attn_c_ref[rows, :], attn)
            x_in = jnp.where(is_ctx, x_c_ref[rows, :], x_in)
        z = zf.astype(BF16)
        glu = (z.astype(F32) * jax.nn.sigmoid(_dot(z, wglu_ref[...]))).astype(BF16)
        mix = (ga_ref[rows, :].astype(F32) * _dot(attn, wba_ref[...])
               + gs_ref[rows, :].astype(F32) * _dot(glu, wbs_ref[...])).astype(BF16)
        x = x_in + m[2:3] * _dot(mix, wout_ref[...])
        xo_ref[rows, :] = x
        x_s[rows, :] = x

    @pl.when(step == 0)
    def _():
        cnt_ref[...] = jnp.zeros_like(cnt_ref)

    cnt_ref[0:1, :] = cnt_ref[0:1, :] + jnp.where(step > 0, tile_counts, 0.0)


def _route(lg):
    ng, epg = N_EXPERT_GROUPS, EXPERTS_PER_GROUP
    lane_i = lax.broadcasted_iota(jnp.int32, lg.shape, 1)
    lane = lane_i.astype(F32)
    big = float(ROUTER_PAD)

    def rmax(mask_val):
        return jnp.max(mask_val, axis=-1, keepdims=True)

    def first_lane(mask, val, mx):
        return jnp.min(jnp.where(mask, jnp.where(val == mx, lane, big), big), axis=-1, keepdims=True)

    gmask = lane_i < ng
    lgm = jnp.where(gmask, lg, NEG_INF)
    mg = rmax(lgm)
    g_prob = 1.0 / jnp.sum(jnp.exp(lgm - mg), axis=-1, keepdims=True)
    g_idx = first_lane(gmask, lg, mg)
    egroup = jnp.floor((lane - float(ng)) * (1.0 / epg))
    emask = egroup == g_idx
    l1 = jnp.where(emask, lg, NEG_INF)
    m1 = rmax(l1)
    i1 = first_lane(emask, lg, m1)
    l2 = jnp.where(lane == i1, NEG_INF, l1)
    m2 = rmax(l2)
    i2 = jnp.min(jnp.where(l2 == m2, jnp.where(emask, lane, big), big), axis=-1, keepdims=True)
    r = jnp.exp(m2 - m1)
    w1 = g_prob / (1.0 + r)
    w2 = w1 * r
    return jnp.where(lane_i == 0, i1 - float(ng),
                     jnp.where(lane_i == 1, i2 - float(ng),
                               jnp.where(lane_i == 2, w1, jnp.where(lane_i == 3, w2, 0.0))))


def _merge(attn, z, x, ctx_parts, ga, gs, mod, g2, w_glu, w_ba, w_bs, w_out, wr_hi, wr_lo, n_rows, t_lat, seq, batch):
    d = D_MODEL
    tile = MERGE_TILE
    assert n_rows % tile == 0 and t_lat % tile == 0 and seq % tile == 0
    n_tiles = n_rows // tile
    n_lat = t_lat // tile
    cur = lambda i: jnp.minimum(i, n_tiles - 1)
    prev = lambda i: jnp.maximum(i - 1, 0)
    row = lambda i: (cur(i), 0)
    row_lag = lambda i: (prev(i), 0)
    lat = lambda i: (jnp.minimum(i, n_lat - 1), 0)
    cxt = lambda i: (jnp.maximum(cur(i) - n_lat, 0), 0)
    const = lambda i: (0, 0)
    mod_of = lambda t: (jnp.minimum(t // (seq // tile), batch), 0, 0)
    modi = lambda i: mod_of(cur(i))
    modi_lag = lambda i: mod_of(prev(i))
    in_specs = [pl.BlockSpec((tile, Q_W), lat),
                pl.BlockSpec((SSM_LANE_TILES, tile, 128), lambda i: (0,) + lat(i)),
                pl.BlockSpec((tile, d), lat)]
    args = [attn, z, x]
    if ctx_parts is not None:
        in_specs += [pl.BlockSpec((tile, Q_W), cxt),
                     pl.BlockSpec((SSM_LANE_TILES, tile, 128), lambda i: (0,) + cxt(i)),
                     pl.BlockSpec((tile, d), cxt)]
        args += list(ctx_parts)
    in_specs += [
        pl.BlockSpec((tile, d), row),
        pl.BlockSpec((tile, d), row),
        pl.BlockSpec((1, 6, d), modi),
        pl.BlockSpec((1, 6, d), modi_lag),
        pl.BlockSpec((1, d), const),
        pl.BlockSpec((D_SSM, D_SSM), const),
        pl.BlockSpec((Q_W, d), const),
        pl.BlockSpec((D_SSM, d), const),
        pl.BlockSpec((d, d), const),
        pl.BlockSpec((d, ROUTER_PAD), const),
        pl.BlockSpec((d, ROUTER_PAD), const),
    ]
    args += [ga, gs, mod, mod, g2, w_glu, w_ba, w_bs, w_out, wr_hi, wr_lo]
    return pl.pallas_call(
        functools.partial(_merge_kernel, n_lat if ctx_parts is not None else None),
        grid=(n_tiles + 1,),
        in_specs=in_specs,
        out_specs=[pl.BlockSpec((tile, d), row), pl.BlockSpec((tile, d), row_lag),
                   pl.BlockSpec((tile, ROUTER_PAD), row_lag), pl.BlockSpec((8, ROUTER_PAD), const)],
        out_shape=[jax.ShapeDtypeStruct((n_rows, d), F32), jax.ShapeDtypeStruct((n_rows, d), F32),
                   jax.ShapeDtypeStruct((n_rows, ROUTER_PAD), F32), jax.ShapeDtypeStruct((8, ROUTER_PAD), F32)],
        scratch_shapes=[pltpu.VMEM((tile, d), F32)],
        compiler_params=_cparams(("arbitrary",)),
        name="merge_router",
    )(*args)


def _expert_kernel(layer, be_ref, nu_ref, nx_ref, x_ref, wg_hbm, wu_hbm, wd_hbm, y_ref,
                   wg_f, wu_f, wd_f, wg_s, wu_s, wd_s, slot_s, sems):
    i = pl.program_id(0)
    e = be_ref[i]
    changed = jnp.logical_or(i == 0, e != be_ref[jnp.maximum(i - 1, 0)])

    def weight_copies(expert, slot):
        return [pltpu.make_async_copy(src.at[layer, expert], dst.at[slot], sems.at[slot, j])
                for j, (src, dst) in enumerate(((wg_hbm, wg_f), (wu_hbm, wu_f), (wd_hbm, wd_f)))]

    @pl.when(i == 0)
    def _():
        slot_s[0] = 0
        for cp in weight_copies(e, 0):
            cp.start()

    @pl.when(changed)
    def _():
        slot = slot_s[0]
        for cp in weight_copies(e, slot):
            cp.wait()
        nxt = nx_ref[e]

        @pl.when(nxt >= 0)
        def _():
            for cp in weight_copies(nxt, 1 - slot):
                cp.start()

        wg_s[...] = wg_f[slot].astype(BF16)
        wu_s[...] = wu_f[slot].astype(BF16)
        wd_s[...] = wd_f[slot].astype(BF16)
        slot_s[0] = 1 - slot

    @pl.when(i < nu_ref[0])
    def _():
        x = jnp.concatenate([x_ref[pl.ds(j, MOE_BLOCK, stride=ROW_SUBLANES), :] for j in range(ROW_SUBLANES)],
                            axis=1).astype(BF16)
        gate = _dot(x, wg_s[...])
        up = _dot(x, wu_s[...])
        hid = (gate * jax.nn.sigmoid(gate) * up).astype(BF16)
        y_ref[...] = _dot(hid, wd_s[...]).astype(y_ref.dtype)

    @pl.when(i >= nu_ref[0])
    def _():
        y_ref[...] = jnp.zeros_like(y_ref)


def _experts(blk_exp, n_used, next_exp, xs, layer, w_gate, w_up, w_down):
    d, de = w_gate.shape[-2:]
    assert d == ROW_SUBLANES * 128
    n_slots = xs.shape[0] // ROW_SUBLANES
    n_blk = n_slots // MOE_BLOCK
    return pl.pallas_call(
        functools.partial(_expert_kernel, layer),
        grid_spec=pltpu.PrefetchScalarGridSpec(
            num_scalar_prefetch=3,
            grid=(n_blk,),
            in_specs=[
                pl.BlockSpec((MOE_BLOCK * ROW_SUBLANES, 128), lambda i, be, nu, nx: (jnp.minimum(i, nu[0] - 1), 0)),
                pl.BlockSpec(memory_space=pl.ANY),
                pl.BlockSpec(memory_space=pl.ANY),
                pl.BlockSpec(memory_space=pl.ANY),
            ],
            out_specs=pl.BlockSpec((MOE_BLOCK, d), lambda i, be, nu, nx: (i, 0)),
            scratch_shapes=[pltpu.VMEM((2, d, de), F32), pltpu.VMEM((2, d, de), F32), pltpu.VMEM((2, de, d), F32),
                            pltpu.VMEM((d, de), BF16), pltpu.VMEM((d, de), BF16), pltpu.VMEM((de, d), BF16),
                            pltpu.SMEM((1,), jnp.int32), pltpu.SemaphoreType.DMA((2, 3))],
        ),
        out_shape=jax.ShapeDtypeStruct((n_slots, d), BF16),
        compiler_params=_cparams(("arbitrary",)),
        name="expert_mlp",
    )(blk_exp, n_used, next_exp, xs, w_gate, w_up, w_down)


def _expert_onehots(rt):
    lane_f = lax.broadcasted_iota(jnp.int32, rt.shape, 1).astype(F32)
    return jnp.where(lane_f == rt[:, 0:1], 1.0, 0.0), jnp.where(lane_f == rt[:, 1:2], 1.0, 0.0)


def _plan_kernel(rt_ref, cnt_ref, dest_ref, pend_ref, run_s, before_s):
    i = pl.program_id(0)
    rows = rt_ref.shape[0]
    lane = lax.broadcasted_iota(jnp.int32, (rows, ROUTER_PAD), 1)
    oh0, oh1 = _expert_onehots(rt_ref[...])
    tot0 = jnp.sum(oh0, axis=0, keepdims=True)
    tot1 = jnp.sum(oh1, axis=0, keepdims=True)

    @pl.when(i == 0)
    def _():
        counts = cnt_ref[...]
        padded = jnp.floor((counts + float(MOE_BLOCK - 1)) * (1.0 / MOE_BLOCK)) * float(MOE_BLOCK)
        r = lax.broadcasted_iota(jnp.int32, (ROUTER_PAD, ROUTER_PAD), 0)
        c = lax.broadcasted_iota(jnp.int32, (ROUTER_PAD, ROUTER_PAD), 1)
        incl = jnp.where(r <= c, 1.0, 0.0).astype(BF16)
        hi, lo = _split_bf16(padded)
        pend = _dot(hi, incl) + _dot(lo, incl)
        pend_ref[...] = pend
        run_s[...] = pend - padded
        r = lax.broadcasted_iota(jnp.int32, (rows, rows), 0)
        c = lax.broadcasted_iota(jnp.int32, (rows, rows), 1)
        before_s[...] = jnp.where(c < r, 1.0, 0.0).astype(BF16)

    prior = _dot(before_s[...], jnp.concatenate([oh0, oh1], axis=1).astype(BF16))
    run = run_s[0:1, :]
    pos0 = jnp.sum(oh0 * (run + prior[:, :ROUTER_PAD]), axis=-1, keepdims=True)
    pos1 = jnp.sum(oh1 * (run + tot0 + prior[:, ROUTER_PAD:]), axis=-1, keepdims=True)
    run_s[0:1, :] = run + tot0 + tot1
    dest_ref[...] = jnp.where(lane == 0, pos0, jnp.where(lane == 1, pos1, 0.0)).astype(jnp.int32)


def _plan(route, counts, n_rows):
    rows = PLAN_TILE
    return pl.pallas_call(
        _plan_kernel,
        grid=(n_rows // rows,),
        in_specs=[pl.BlockSpec((rows, ROUTER_PAD), lambda i: (i, 0)),
                  pl.BlockSpec((8, ROUTER_PAD), lambda i: (0, 0))],
        out_specs=[pl.BlockSpec((rows, ROUTER_PAD), lambda i: (i, 0)),
                   pl.BlockSpec((8, ROUTER_PAD), lambda i: (0, 0))],
        out_shape=[jax.ShapeDtypeStruct((n_rows, ROUTER_PAD), jnp.int32),
                   jax.ShapeDtypeStruct((8, ROUTER_PAD), F32)],
        scratch_shapes=[pltpu.VMEM((8, ROUTER_PAD), F32), pltpu.VMEM((rows, rows), BF16)],
        compiler_params=_cparams(("arbitrary",)),
        name="dispatch_plan",
    )(route, counts)


def _dispatch_kernel(pend_ref, dest_ref, h_ref, xs_ref, hs, zero_s, sem, zsem):
    rows = h_ref.shape[0]
    sub = ROW_SUBLANES
    n_blk = xs_ref.shape[0] // (MOE_BLOCK * sub)
    n_used = pend_ref[N_EXPERTS - 1] // MOE_BLOCK

    def zero_block(start):
        return pltpu.make_async_copy(
            zero_s, xs_ref.at[pl.ds(pl.multiple_of(start * sub, MOE_BLOCK * sub), MOE_BLOCK * sub)], zsem)

    @pl.when(pl.program_id(0) == 0)
    def _():
        zero_s[...] = jnp.zeros_like(zero_s)
        for e in range(N_EXPERTS):
            @pl.when(pend_ref[e] > 0)
            def _():
                zero_block(pend_ref[e] - MOE_BLOCK).start()
        lax.fori_loop(n_used, n_blk, lambda j, c: (zero_block(j * MOE_BLOCK).start(), c)[1], 0)
        for e in range(N_EXPERTS):
            @pl.when(pend_ref[e] > 0)
            def _():
                zero_block(pend_ref[e] - MOE_BLOCK).wait()
        lax.fori_loop(n_used, n_blk, lambda j, c: (zero_block(j * MOE_BLOCK).wait(), c)[1], 0)

    for j in range(sub):
        hs[pl.ds(j, rows, stride=sub), :] = h_ref[:, 128 * j:128 * (j + 1)]

    def row_copy(r, d):
        return pltpu.make_async_copy(hs.at[pl.ds(pl.multiple_of(r * sub, sub), sub)],
                                     xs_ref.at[pl.ds(pl.multiple_of(d * sub, sub), sub)], sem)

    def issue(r, carry):
        for k in range(TOP_K):
            row_copy(r, dest_ref[0, 0, k * rows + r]).start(priority=k % 2)
        return carry

    lax.fori_loop(0, rows, issue, 0, unroll=8)
    for _ in range(TOP_K * rows):
        row_copy(0, 0).wait()


def _dispatch(pend, dest_tiles, h2, n_slots, n_tiles):
    d = h2.shape[1]
    return pl.pallas_call(
        _dispatch_kernel,
        grid_spec=pltpu.PrefetchScalarGridSpec(
            num_scalar_prefetch=1,
            grid=(n_tiles,),
            in_specs=[pl.BlockSpec((1, 1, TOP_K * ROW_TILE), lambda i, pe: (i, 0, 0), memory_space=pltpu.SMEM),
                      pl.BlockSpec((ROW_TILE, d), lambda i, pe: (i, 0))],
            out_specs=pl.BlockSpec(memory_space=pl.ANY),
            scratch_shapes=[pltpu.VMEM((ROW_TILE * ROW_SUBLANES, 128), F32),
                            pltpu.VMEM((MOE_BLOCK * ROW_SUBLANES, 128), F32),
                            pltpu.SemaphoreType.DMA, pltpu.SemaphoreType.DMA],
        ),
        out_shape=jax.ShapeDtypeStruct((n_slots * ROW_SUBLANES, 128), F32),
        compiler_params=_cparams(("arbitrary",)),
        name="dispatch_rows",
    )(pend, dest_tiles, h2)


def _moe(h2, route, counts, n_tiles, layer, w_gate, w_up, w_down):
    t, d = h2.shape
    dest, pend_f = _plan(route, counts, t)
    pend = pend_f[0, :N_EXPERTS].astype(jnp.int32)
    n_blk = -(-(t * TOP_K) // MOE_BLOCK) + N_EXPERTS
    blk_start = jnp.arange(n_blk, dtype=jnp.int32) * MOE_BLOCK
    blk_exp = jnp.minimum(jnp.sum((pend[None, :] <= blk_start[:, None]).astype(jnp.int32), axis=1), N_EXPERTS - 1)
    n_used = (pend[N_EXPERTS - 1] // MOE_BLOCK).reshape(1)
    blk_exp = jnp.where(jnp.arange(n_blk) < n_used[0], blk_exp, blk_exp[n_used[0] - 1])
    has_rows = pend > jnp.concatenate([jnp.zeros((1,), jnp.int32), pend[:-1]])
    e_ids = jnp.arange(N_EXPERTS, dtype=jnp.int32)
    later = jnp.where(jnp.logical_and(has_rows[None, :], e_ids[None, :] > e_ids[:, None]), e_ids[None, :], N_EXPERTS)
    next_exp = jnp.min(later, axis=1)
    next_exp = jnp.where(next_exp < N_EXPERTS, next_exp, -1).astype(jnp.int32)
    dest2 = dest[:, :TOP_K]
    dest_tiles = dest2.reshape(n_tiles, ROW_TILE, TOP_K).transpose(0, 2, 1).reshape(n_tiles, 1, TOP_K * ROW_TILE)
    xs = _dispatch(pend, dest_tiles, h2, n_blk * MOE_BLOCK, n_tiles)
    ys = _experts(blk_exp.astype(jnp.int32), n_used, next_exp, xs, layer, w_gate, w_up, w_down)
    return ys[dest2[:, 0]], ys[dest2[:, 1]]


def _final_kernel(x_ref, y0_ref, y1_ref, rt_ref, mod_ref, g_ref, o_ref):
    x = _moe_residual(x_ref, y0_ref, y1_ref, rt_ref, mod_ref)
    o_ref[...] = x * lax.rsqrt(jnp.mean(x * x, axis=-1, keepdims=True) + EPS) * g_ref[...]


def _final(x_lat, moe_out, mod, g_final, n_rows, seq, batch):
    d = D_MODEL
    tile = MERGE_TILE
    assert n_rows % tile == 0 and seq % tile == 0
    row = lambda i: (i, 0)
    return pl.pallas_call(
        _final_kernel,
        grid=(n_rows // tile,),
        in_specs=[pl.BlockSpec((tile, d), row), pl.BlockSpec((tile, d), row),
                  pl.BlockSpec((tile, d), row), pl.BlockSpec((tile, ROUTER_PAD), row),
                  pl.BlockSpec((1, 6, d), lambda i: (jnp.minimum(i // (seq // tile), batch), 0, 0)),
                  pl.BlockSpec((1, d), lambda i: (0, 0))],
        out_specs=pl.BlockSpec((tile, d), row),
        out_shape=jax.ShapeDtypeStruct((n_rows, d), F32),
        compiler_params=_cparams(("parallel",)),
        name="final_norm",
    )(x_lat, *moe_out, mod, g_final)


def _rope_tables(seq):
    quarter = HEAD_DIM // 4
    freqs = ROPE_THETA ** (-jnp.arange(quarter, dtype=F32) / quarter)
    pos = jnp.arange(seq)
    ang_r = (pos // GRID_W).astype(F32)[:, None] * freqs[None, :]
    ang_c = (pos % GRID_W).astype(F32)[:, None] * freqs[None, :]
    cos = jnp.concatenate([jnp.cos(ang_r)] * 2 + [jnp.cos(ang_c)] * 2, axis=-1)
    sin = jnp.concatenate([-jnp.sin(ang_r), jnp.sin(ang_r), -jnp.sin(ang_c), jnp.sin(ang_c)], axis=-1)
    reps = 128 // HEAD_DIM
    cos = jnp.tile(cos, (1, reps))
    sin = jnp.tile(sin, (1, reps))
    cos = jnp.concatenate([cos, jnp.ones((ROW_TILE, 128), F32)], axis=0)
    sin = jnp.concatenate([sin, jnp.zeros((ROW_TILE, 128), F32)], axis=0)
    return cos, sin


def kernel(x, c, ctx, c_ctx, w_mod, b_mod, g_norm1, g_norm2, w_in, attn_sink, ssm_lam_re, ssm_lam_im,
           ssm_log_dt, ssm_b_re, ssm_b_im, ssm_c_re, ssm_c_im, ssm_d, w_glu, w_br_attn, w_br_ssm, w_out,
           w_router_group, w_router_expert, w_exp_gate, w_exp_up, w_exp_down, g_final):
    batch, seq, d = x.shape
    ctx_len = ctx.shape[1]
    depth = w_mod.shape[0]
    assert d == D_MODEL and batch * SSM_STATE == 128
    assert seq % ROW_TILE == 0 and ctx_len % ROW_TILE == 0 and (batch * seq) % ctx_len == 0
    assert batch + 1 <= MOD_ROWS
    t_lat, t_ctx = batch * seq, batch * ctx_len
    tiles_per_seq = seq // ROW_TILE
    n_lat_tiles = t_lat // ROW_TILE
    n_all_tiles = (t_lat + t_ctx) // ROW_TILE
    nb_per_seq = seq // ATTN_BLOCK
    ncc, ncl = ctx_len // SSM_CHUNK, seq // SSM_CHUNK

    c_rows = jnp.zeros((MOD_ROWS, d), F32).at[:batch].set(c).at[batch].set(c_ctx)
    mod_all = _modulation(c_rows, w_mod, b_mod).reshape(depth, MOD_ROWS, 6, d)
    cos_t, sin_t = _rope_tables(seq)
    ssm_w = _ssm_weights_all(ssm_lam_re, ssm_lam_im, ssm_log_dt, ssm_b_re, ssm_b_im, ssm_c_re, ssm_c_im, ssm_d, batch)
    x_parts = (x.reshape(t_lat, d), ctx.reshape(t_ctx, d))
    f_all = None
    for l in range(depth):
        ctx_out = l < depth - 1
        mod = mod_all[l]
        x_parts, (q, k, v, u, ga, gs) = _inproj(
            x_parts, f_all, mod_all[l - 1] if l > 0 else None, mod, g_norm1[l].reshape(1, d), cos_t, sin_t,
            w_in[l].astype(BF16), n_all_tiles, tiles_per_seq, n_lat_tiles, batch)
        sink = attn_sink[l].astype(F32)
        attn = _attention(sink, q, k, v, t_lat // ATTN_BLOCK, nb_per_seq, t_lat // ctx_len, ctx_len, True, 0)
        z, z_ctx = _ssm_mixer(u, l, *ssm_w, batch, seq, ctx_len)
        ctx_parts = None
        if ctx_out:
            attn_c = _attention(sink, q, k, v, t_ctx // ATTN_BLOCK, ctx_len // ATTN_BLOCK, t_lat // ctx_len,
                                ctx_len, False, t_lat // ATTN_BLOCK)
            ctx_parts = (attn_c, z_ctx, x_parts[1])
        w_r = jnp.zeros((d, ROUTER_PAD), F32)
        w_r = w_r.at[:, :N_EXPERT_GROUPS].set(w_router_group[l])
        w_r = w_r.at[:, N_EXPERT_GROUPS:N_EXPERT_GROUPS + N_EXPERTS].set(w_router_expert[l])
        wr_hi, wr_lo = _split_bf16(w_r)
        n_rows = t_lat + t_ctx if ctx_out else t_lat
        x_all, h2, route, counts = _merge(
            attn, z, x_parts[0], ctx_parts, ga, gs, mod, g_norm2[l].reshape(1, d), w_glu[l].astype(BF16),
            w_br_attn[l].astype(BF16), w_br_ssm[l].astype(BF16), w_out[l].astype(BF16),
            wr_hi, wr_lo, n_rows, t_lat, seq, batch)
        y0, y1 = _moe(h2, route, counts, n_rows // ROW_TILE, l, w_exp_gate, w_exp_up, w_exp_down)
        f_all = (y0, y1, route)
        x_parts = (x_all,)
    out = _final(x_all, f_all, mod_all[depth - 1], g_final.reshape(1, d), t_lat, seq, batch)
    return out.reshape(batch, seq, d)
```

```python
import functools
import math

import jax
import jax.numpy as jnp
from jax import lax
from jax.experimental import pallas as pl
from jax.experimental.pallas import tpu as pltpu

F32 = jnp.float32
BF16 = jnp.bfloat16

D_MODEL = 1024
GRID_W = 64
N_HEADS = 8
N_KV_HEADS = 2
HEAD_DIM = 64
Q_GROUP = N_HEADS // N_KV_HEADS
ATTN_BLOCK = 128
ATTN_STEP_BLOCKS = 4
ROPE_THETA = 10000.0
D_SSM = D_MODEL // 2
SSM_GROUP = 16
N_SSM_GROUPS = D_SSM // SSM_GROUP
SSM_STATE = 64
N_EXPERT_GROUPS = 4
EXPERTS_PER_GROUP = 8
N_EXPERTS = N_EXPERT_GROUPS * EXPERTS_PER_GROUP
TOP_K = 2
D_EXPERT = D_MODEL // 2
Q_W = N_HEADS * HEAD_DIM
KV_W = N_KV_HEADS * HEAD_DIM
O_K = Q_W
O_V = O_K + KV_W
O_U = O_V + KV_W
O_GA = O_U + D_SSM
O_GS = O_GA + D_MODEL
D_IN = O_GS + D_MODEL
EPS = 1e-6
NEG_INF = -1e30

ROW_TILE = 256
MERGE_TILE = 512
MERGE_SUB = 256
SSM_CHUNK = 16
SSM_CW = SSM_CHUNK * SSM_GROUP
SSM_SCAN_GROUPS = 8
SSM_LANE_GROUPS = 128 // SSM_GROUP
SSM_LANE_TILES = D_SSM // 128
MOE_BLOCK = 256
ROW_SUBLANES = D_MODEL // 128
PLAN_TILE = 512
ROUTER_PAD = 128
MOD_ROWS = 8
VMEM_LIMIT = 48 * 1024 * 1024


def _cparams(sem):
    return pltpu.CompilerParams(dimension_semantics=sem, vmem_limit_bytes=VMEM_LIMIT)


def _dot(a, b):
    return jnp.dot(a, b, preferred_element_type=F32)


def _split_bf16(a):
    hi = a.astype(BF16)
    lo = (a - hi.astype(F32)).astype(BF16)
    return hi, lo


def _rms_mod(x, g, shift, scale):
    y = x * lax.rsqrt(jnp.mean(x * x, axis=-1, keepdims=True) + EPS) * g
    return y * (1.0 + scale) + shift


def _mod_kernel(c_ref, w_ref, b_ref, o_ref):
    c = c_ref[...]
    s_hi, s_lo = _split_bf16(c * jax.nn.sigmoid(c))
    w_hi, w_lo = _split_bf16(w_ref[0])
    o_ref[0] = _dot(s_hi, w_hi) + _dot(s_lo, w_hi) + _dot(s_hi, w_lo) + b_ref[0]


def _modulation(c_rows, w_mod, b_mod):
    depth, d, n = w_mod.shape
    nb = n // 4
    return pl.pallas_call(
        _mod_kernel,
        grid=(depth, n // nb),
        in_specs=[
            pl.BlockSpec((MOD_ROWS, d), lambda l, j: (0, 0)),
            pl.BlockSpec((1, d, nb), lambda l, j: (l, 0, j)),
            pl.BlockSpec((1, 1, nb), lambda l, j: (l, 0, j)),
        ],
        out_specs=pl.BlockSpec((1, MOD_ROWS, nb), lambda l, j: (l, 0, j)),
        out_shape=jax.ShapeDtypeStruct((depth, MOD_ROWS, n), F32),
        compiler_params=_cparams(("arbitrary", "arbitrary")),
        name="modulation",
    )(c_rows, w_mod, b_mod.reshape(depth, 1, n))


def _moe_residual(x_ref, y0_ref, y1_ref, rt_ref, mod_ref):
    rt = rt_ref[...]
    f = rt[:, 2:3] * y0_ref[...].astype(F32) + rt[:, 3:4] * y1_ref[...].astype(F32)
    return x_ref[...] + mod_ref[0, 5:6, :] * f


def _inproj_kernel(has_f, n_lat_tiles, *refs):
    if has_f:
        (x_ref, y0_ref, y1_ref, rt_ref, modp_ref, mod_ref, g_ref, cos_ref, sin_ref, w_ref,
         xo_ref, q_ref, k_ref, v_ref, u_ref, ga_ref, gs_ref) = refs
        x = _moe_residual(x_ref, y0_ref, y1_ref, rt_ref, modp_ref)
        xo_ref[...] = x
    else:
        (xl_ref, xc_ref, mod_ref, g_ref, cos_ref, sin_ref, w_ref,
         q_ref, k_ref, v_ref, u_ref, ga_ref, gs_ref) = refs
        x = jnp.where(pl.program_id(0) >= n_lat_tiles, xc_ref[...], xl_ref[...])
    m = mod_ref[0]
    h = _rms_mod(x, g_ref[...], m[0:1], m[1:2]).astype(BF16)
    cos = cos_ref[...]
    sin = sin_ref[...]
    lane = lax.broadcasted_iota(jnp.int32, cos.shape, 1)
    first = (lane % (HEAD_DIM // 2)) < (HEAD_DIM // 4)

    def rope(t):
        sw = jnp.where(first, pltpu.roll(t, 128 - HEAD_DIM // 4, 1), pltpu.roll(t, HEAD_DIM // 4, 1))
        return t * cos + sw * sin

    def proj(lo, hi):
        return _dot(h, w_ref[:, lo:hi])

    q = proj(0, O_K)
    for j in range(Q_W // 128):
        q_ref[:, 128 * j:128 * (j + 1)] = (rope(q[:, 128 * j:128 * (j + 1)]) * HEAD_DIM ** -0.5).astype(BF16)
    kv = proj(O_K, O_U)
    k_ref[...] = rope(kv[:, :KV_W]).astype(BF16)
    v_ref[...] = kv[:, KV_W:].astype(BF16)
    u = proj(O_U, O_GA)
    for j in range(SSM_LANE_TILES):
        u_ref[j] = u[:, 128 * j:128 * (j + 1)]
    ga_ref[...] = jax.nn.sigmoid(proj(O_GA, O_GS)).astype(BF16)
    gs_ref[...] = jax.nn.sigmoid(proj(O_GS, D_IN)).astype(BF16)


def _inproj(x_parts, moe_out, mod_prev, mod, g1, cos_t, sin_t, w_in, n_tiles, tiles_per_seq, n_lat_tiles, batch):
    d = D_MODEL
    has_f = moe_out is not None
    row = lambda i: (i, 0)
    modi = lambda i: (jnp.minimum(i // tiles_per_seq, batch), 0, 0)
    const = lambda i: (0, 0)
    ropei = lambda i: (jnp.where(i < n_lat_tiles, i % tiles_per_seq, tiles_per_seq), 0)
    if has_f:
        in_specs = [pl.BlockSpec((ROW_TILE, d), row), pl.BlockSpec((ROW_TILE, d), row),
                    pl.BlockSpec((ROW_TILE, d), row), pl.BlockSpec((ROW_TILE, ROUTER_PAD), row),
                    pl.BlockSpec((1, 6, d), modi)]
        args = [*x_parts, *moe_out, mod_prev]
    else:
        in_specs = [pl.BlockSpec((ROW_TILE, d), lambda i: (jnp.minimum(i, n_lat_tiles - 1), 0)),
                    pl.BlockSpec((ROW_TILE, d), lambda i: (jnp.maximum(i - n_lat_tiles, 0), 0))]
        args = list(x_parts)
    in_specs += [
        pl.BlockSpec((1, 6, d), modi),
        pl.BlockSpec((1, d), const),
        pl.BlockSpec((ROW_TILE, 128), ropei),
        pl.BlockSpec((ROW_TILE, 128), ropei),
        pl.BlockSpec((d, D_IN), const),
    ]
    args += [mod, g1, cos_t, sin_t, w_in]
    widths = [Q_W, KV_W, KV_W, D_SSM, D_MODEL, D_MODEL]
    out_specs = [pl.BlockSpec((ROW_TILE, w), row) for w in widths]
    out_shape = [jax.ShapeDtypeStruct((n_tiles * ROW_TILE, w), BF16) for w in widths]
    out_specs[3] = pl.BlockSpec((SSM_LANE_TILES, ROW_TILE, 128), lambda i: (0, i, 0))
    out_shape[3] = jax.ShapeDtypeStruct((SSM_LANE_TILES, n_tiles * ROW_TILE, 128), F32)
    if has_f:
        out_specs = [pl.BlockSpec((ROW_TILE, d), row)] + out_specs
        out_shape = [jax.ShapeDtypeStruct((n_tiles * ROW_TILE, d), F32)] + out_shape
    outs = pl.pallas_call(
        functools.partial(_inproj_kernel, has_f, n_lat_tiles),
        grid=(n_tiles,),
        in_specs=in_specs,
        out_specs=out_specs,
        out_shape=out_shape,
        compiler_params=_cparams(("parallel",)),
        name="inproj",
    )(*args)
    if has_f:
        return (outs[0],), outs[1:]
    return x_parts, outs


def _attn_scores(q, k_tiles, v_tiles):
    res = []
    for g in range(N_KV_HEADS):
        gs = slice(g * HEAD_DIM, (g + 1) * HEAD_DIM)
        k_all = jnp.concatenate([t[:, gs] for t in k_tiles], axis=0)
        v_all = jnp.concatenate([t[:, gs] for t in v_tiles], axis=0)
        v_ext = jnp.concatenate([v_all, jnp.ones_like(v_all)], axis=1)
        qg = jnp.concatenate(
            [q[:, (g * Q_GROUP + h) * HEAD_DIM:(g * Q_GROUP + h + 1) * HEAD_DIM] for h in range(Q_GROUP)], axis=0)
        res.append((lax.dot_general(k_all, qg, (((1,), (1,)), ((), ())), preferred_element_type=F32), v_ext))
    return res


def _attn_finish(sink_ref, scores, tile_rows, biases, blk):
    outs = []
    for g, (st, v_ext) in enumerate(scores):
        tiles, row = [], 0
        for n, bias in zip(tile_rows, biases):
            t = st[row:row + n, :]
            tiles.append(t if bias is None else t + jnp.concatenate([bias] * Q_GROUP, axis=1))
            row += n
        mx = None
        for t in tiles:
            for r0 in range(0, t.shape[0], blk):
                mx = t[r0:r0 + blk] if mx is None else jnp.maximum(mx, t[r0:r0 + blk])
        sink = jnp.concatenate([jnp.full((1, blk), sink_ref[g * Q_GROUP + h], F32) for h in range(Q_GROUP)], axis=1)
        m = jnp.maximum(jnp.max(mx, axis=0, keepdims=True), sink)
        p = jnp.exp(jnp.concatenate([(t - m).astype(BF16) for t in tiles], axis=0))
        o_t = lax.dot_general(v_ext, p, (((0,), (0,)), ((), ())), preferred_element_type=F32)
        denom = o_t[HEAD_DIM:HEAD_DIM + 1, :] + jnp.exp(sink - m)
        o_n = o_t[:HEAD_DIM, :] * (1.0 / denom)
        for h in range(Q_GROUP):
            outs.append(o_n[:, h * blk:(h + 1) * blk].T.astype(BF16))
    return jnp.concatenate(outs, axis=1)


def _attn_kernel(band, nb_per_seq, sink_ref, *refs):
    blk = ATTN_BLOCK
    if not band:
        q_ref, kx_ref, vx_ref, o_ref = refs
        n_sub = q_ref.shape[0] // blk
        scores = [_attn_scores(q_ref[a * blk:(a + 1) * blk, :], [kx_ref[...]], [vx_ref[...]]) for a in range(n_sub)]
        for a in range(n_sub):
            o_ref[a * blk:(a + 1) * blk, :] = _attn_finish(sink_ref, scores[a], [kx_ref.shape[0]], [None], blk)
        return
    q_ref, kp_ref, kc_ref, kn_ref, vp_ref, vc_ref, vn_ref, kx_ref, vx_ref, o_ref = refs
    n_sub = q_ref.shape[0] // blk
    j0 = (pl.program_id(0) * n_sub) % nb_per_seq
    r = lax.broadcasted_iota(jnp.int32, (blk, blk), 0)
    c = lax.broadcasted_iota(jnp.int32, (blk, blk), 1)
    k_blocks = [kp_ref[...]] + [kc_ref[a * blk:(a + 1) * blk, :] for a in range(n_sub)] + [kn_ref[...]]
    v_blocks = [vp_ref[...]] + [vc_ref[a * blk:(a + 1) * blk, :] for a in range(n_sub)] + [vn_ref[...]]
    scores = [_attn_scores(q_ref[a * blk:(a + 1) * blk, :], k_blocks[a:a + 3] + [kx_ref[...]],
                           v_blocks[a:a + 3] + [vx_ref[...]]) for a in range(n_sub)]
    for a in range(n_sub):
        edge_p = jnp.where(j0 + a > 0, 0.0, NEG_INF).astype(F32)
        edge_n = jnp.where(j0 + a < nb_per_seq - 1, 0.0, NEG_INF).astype(F32)
        bias_p = jnp.where(r >= c, edge_p, NEG_INF).astype(F32)
        bias_n = jnp.where(r <= c, edge_n, NEG_INF).astype(F32)
        o_ref[a * blk:(a + 1) * blk, :] = _attn_finish(
            sink_ref, scores[a], [blk, blk, blk, kx_ref.shape[0]], [bias_p, None, bias_n, None], blk)


def _attention(sink, q, k, v, n_blocks, nb_per_seq, kx_block0, ctx_len, band, q_block0):
    blk = ATTN_BLOCK
    n_sub = math.gcd(ATTN_STEP_BLOCKS, nb_per_seq, n_blocks, q_block0)
    step = n_sub * blk
    n_steps = n_blocks // n_sub
    qi = lambda i, s: (q_block0 // n_sub + i, 0)
    cur = lambda i, s: (i, 0)
    prv = lambda i, s: (jnp.maximum(i * n_sub - 1, 0), 0)
    nxt = lambda i, s: (jnp.minimum((i + 1) * n_sub, n_blocks - 1), 0)
    kxi = lambda i, s: (kx_block0 + (i * n_sub) // nb_per_seq, 0)
    kspec = lambda f: pl.BlockSpec((blk, KV_W), f)
    cspec = pl.BlockSpec((step, KV_W), cur)
    xspec = pl.BlockSpec((ctx_len, KV_W), kxi)
    if band:
        in_specs = [pl.BlockSpec((step, Q_W), qi), kspec(prv), cspec, kspec(nxt),
                    kspec(prv), cspec, kspec(nxt), xspec, xspec]
        args = (q, k, k, k, v, v, v, k, v)
    else:
        in_specs = [pl.BlockSpec((step, Q_W), qi), xspec, xspec]
        args = (q, k, v)
    return pl.pallas_call(
        functools.partial(_attn_kernel, band, nb_per_seq),
        grid_spec=pltpu.PrefetchScalarGridSpec(
            num_scalar_prefetch=1,
            grid=(n_steps,),
            in_specs=in_specs,
            out_specs=pl.BlockSpec((step, Q_W), lambda i, s: (i, 0)),
        ),
        out_shape=jax.ShapeDtypeStruct((n_blocks * blk, Q_W), BF16),
        compiler_params=_cparams(("parallel",)),
        name="band_attention" if band else "context_attention",
    )(sink, *args)


def _dot_f32(a, b_t):
    a_hi, a_lo = _split_bf16(a)
    b_hi, b_lo = _split_bf16(b_t)
    dn = (((1,), (1,)), ((), ()))
    dg = functools.partial(lax.dot_general, dimension_numbers=dn, preferred_element_type=F32)
    return dg(a_hi, b_hi) + dg(a_lo, b_hi) + dg(a_hi, b_lo)


def _ssm_weight_kernel(batch, lam_r_ref, bt_ref, c_ref, d_ref, w1_ref, abar_ref, t_ref, wo_ref):
    lc, mm, p = SSM_CHUNK, SSM_GROUP, SSM_STATE
    up_r = lax.broadcasted_iota(jnp.int32, (lc, 1), 0).astype(F32)
    lane = lax.broadcasted_iota(jnp.int32, (mm, lc * mm), 1)
    row = lax.broadcasted_iota(jnp.int32, (mm, lc * mm), 0)
    planes_in, planes_out, abar_rows, kt = [], [], [], []
    for d in range(2):
        lr, li, dt = lam_r_ref[0, d, 0:1, :], lam_r_ref[0, d, 1:2, :], jnp.exp(lam_r_ref[0, d, 2:3, :])

        def powers(expo):
            mag = jnp.exp(lr * dt * expo)
            return mag * jnp.cos(li * dt * expo), mag * jnp.sin(li * dt * expo)

        a_re, a_im = powers(jnp.ones((1, 1), F32))
        den = lr * lr + li * li
        nr = a_re - 1.0
        f_re = (nr * lr + a_im * li) / den
        f_im = (a_im * lr - nr * li) / den
        bt_re, bt_im = bt_ref[0, d, 0], bt_ref[0, d, 1]
        bbt_re = f_re * bt_re - f_im * bt_im
        bbt_im = f_re * bt_im + f_im * bt_re
        c_re, c_im = c_ref[0, d, 0], c_ref[0, d, 1]

        pr, pi = powers(lc - 1.0 - up_r if d == 0 else up_r)
        planes_in.append(jnp.concatenate(
            [pr[s:s + 1] * bbt_re - pi[s:s + 1] * bbt_im for s in range(lc)], axis=0))
        planes_in.append(jnp.concatenate(
            [pr[s:s + 1] * bbt_im + pi[s:s + 1] * bbt_re for s in range(lc)], axis=0))

        qr, qi = powers(up_r + 1.0 if d == 0 else lc - up_r)
        planes_out.append(jnp.concatenate(
            [c_re * qr[t:t + 1] - c_im * qi[t:t + 1] for t in range(lc)], axis=0).T)
        planes_out.append(jnp.concatenate(
            [-c_re * qi[t:t + 1] - c_im * qr[t:t + 1] for t in range(lc)], axis=0).T)

        kr, ki = powers(up_r if d == 0 else lc - 1.0 - up_r)
        cp_re = jnp.concatenate([c_re * kr[j:j + 1] - c_im * ki[j:j + 1] for j in range(lc)], axis=0)
        cp_im = jnp.concatenate([c_re * ki[j:j + 1] + c_im * kr[j:j + 1] for j in range(lc)], axis=0)
        kt.append(_dot_f32(jnp.concatenate([bbt_re, bbt_im], axis=1), jnp.concatenate([cp_re, -cp_im], axis=1)))

        e_re, e_im = powers(jnp.full((1, 1), float(lc), F32))
        abar_rows += [jnp.concatenate([e_re] * batch, axis=1), jnp.concatenate([e_im] * batch, axis=1)]

    dvec = d_ref[0]
    blocks = []
    for s in range(lc):
        fwd = kt[0] if s == 0 else pltpu.roll(kt[0], mm * s, 1)
        bwd = kt[1] if s == lc - 1 else pltpu.roll(kt[1], mm * (s + 1), 1)
        blk = jnp.where(lane >= mm * s, fwd, 0.0) + jnp.where(lane < mm * (s + 1), bwd, 0.0)
        blocks.append(blk + jnp.where(lane == mm * s + row, dvec, 0.0))
    t_ref[0] = jnp.concatenate(blocks, axis=0).astype(BF16)
    abar_ref[0] = jnp.concatenate(abar_rows, axis=0)
    zero_in = jnp.zeros((lc * mm, p), F32)
    zero_out = jnp.zeros((p, lc * mm), F32)
    for b in range(batch):
        w1_ref[0, b] = jnp.concatenate(
            [pl_ if bb == b else zero_in for pl_ in planes_in for bb in range(batch)], axis=1).astype(BF16)
        wo_ref[0, b] = jnp.concatenate(
            [pl_ if bb == b else zero_out for pl_ in planes_out for bb in range(batch)], axis=0).astype(BF16)


def _ssm_weights_all(lam_re, lam_im, log_dt, b_re, b_im, c_re, c_im, d_skip, batch):
    depth = lam_re.shape[0]
    g, p, mm, lc = N_SSM_GROUPS, SSM_STATE, SSM_GROUP, SSM_CHUNK
    n = depth * g
    cw, sw = lc * mm, 4 * batch * p

    def per_group(a):
        return jnp.moveaxis(a.astype(F32), 2, 1).reshape((n, 2) + a.shape[3:])

    lam = jnp.stack([per_group(lam_re), per_group(lam_im),
                     jnp.broadcast_to(per_group(log_dt)[..., None], (n, 2, p))], axis=2)
    bt = jnp.stack([per_group(b_re), per_group(b_im)], axis=2).swapaxes(-1, -2)
    c = jnp.stack([per_group(c_re), per_group(c_im)], axis=2)
    d_t = jnp.tile(d_skip.astype(F32).reshape(n, 1, mm), (1, 1, lc))
    full = lambda *shape: pl.BlockSpec((1,) + shape, lambda i: (i,) + (0,) * len(shape))
    w1, abar, tmat, wout = pl.pallas_call(
        functools.partial(_ssm_weight_kernel, batch),
        grid=(n,),
        in_specs=[full(2, 3, p), full(2, 2, mm, p), full(2, 2, mm, p), full(1, cw)],
        out_specs=[full(batch, cw, sw), full(4, batch * p), full(cw, cw), full(batch, sw, cw)],
        out_shape=[jax.ShapeDtypeStruct((n, batch, cw, sw), BF16), jax.ShapeDtypeStruct((n, 4, batch * p), F32),
                   jax.ShapeDtypeStruct((n, cw, cw), BF16), jax.ShapeDtypeStruct((n, batch, sw, cw), BF16)],
        compiler_params=_cparams(("parallel",)),
        name="ssm_weights",
    )(lam, bt, c, d_t)
    abar = abar.reshape(depth, g, 4, batch * p).transpose(0, 2, 1, 3)
    return (w1.reshape(depth, g, batch, cw, sw), abar, tmat.reshape(depth, g, cw, cw),
            wout.reshape(depth, g, batch, sw, cw))


def _ssm_p1_kernel(ul_ref, uc_ref, w_ref, v_ref, *s_refs):
    b = pl.program_id(1)
    ncc, ncl = uc_ref.shape[0] // SSM_CHUNK, ul_ref.shape[0] // SSM_CHUNK
    nc = ncc + ncl
    pw = s_refs[0].shape[1]
    xs = [jnp.concatenate([uc_ref[pl.ds(s, ncc, stride=SSM_CHUNK), :], ul_ref[pl.ds(s, ncl, stride=SSM_CHUNK), :]],
                          axis=0).astype(BF16) for s in range(SSM_CHUNK)]
    for j in range(SSM_LANE_GROUPS):
        v = jnp.concatenate([x[:, j * SSM_GROUP:(j + 1) * SSM_GROUP] for x in xs], axis=1)
        v_ref[j] = v
        acc = _dot(v, w_ref[j])
        for k, s_ref in enumerate(s_refs):
            part = acc[:, k * pw:(k + 1) * pw]
            rows = slice(j * nc, (j + 1) * nc)

            @pl.when(b == 0)
            def _():
                s_ref[rows, :] = part

            @pl.when(b > 0)
            def _():
                s_ref[rows, :] = s_ref[rows, :] + part


def _ssm_scan_kernel(ncc, ncl, a_ref, sfr_ref, sfi_ref, sbr_ref, sbi_ref, xfr_ref, xfi_ref, xbr_ref, xbi_ref):
    nc = ncc + ncl
    gb = a_ref.shape[1]
    pw = a_ref.shape[2]
    afr, afi, abr, abi = a_ref[0], a_ref[1], a_ref[2], a_ref[3]

    def rows(r):
        return pl.ds(r, gb, stride=nc)

    def step(rf, rb, carry):
        xfr, xfi, xbr, xbi = carry
        xfr_ref[rows(rf), :] = xfr
        xfi_ref[rows(rf), :] = xfi
        xbr_ref[rows(rb), :] = xbr
        xbi_ref[rows(rb), :] = xbi
        sfr = sfr_ref[rows(rf), :]
        sfi = sfi_ref[rows(rf), :]
        sbr = sbr_ref[rows(rb), :]
        sbi = sbi_ref[rows(rb), :]
        return (afr * xfr - afi * xfi + sfr, afr * xfi + afi * xfr + sfi,
                abr * xbr - abi * xbi + sbr, abr * xbi + abi * xbr + sbi)

    zero = jnp.zeros((gb, pw), F32)
    carry = lax.fori_loop(0, ncc, lambda t, c: step(t, ncc - 1 - t, c), (zero, zero, zero, zero), unroll=4)
    lax.fori_loop(0, ncl, lambda t, c: step(ncc + t, nc - 1 - t, c), carry, unroll=4)


def _ssm_p3_kernel(v_ref, xfr_ref, xfi_ref, xbr_ref, xbi_ref, t_ref, wo_ref, zl_ref, zc_ref):
    ncc, ncl = zc_ref.shape[0] // SSM_CHUNK, zl_ref.shape[0] // SSM_CHUNK
    nc = ncc + ncl
    ys = []
    for j in range(SSM_LANE_GROUPS):
        rows = slice(j * nc, (j + 1) * nc)
        xin = jnp.concatenate([xfr_ref[rows, :], xfi_ref[rows, :], xbr_ref[rows, :], xbi_ref[rows, :]],
                              axis=1).astype(BF16)
        y = _dot(v_ref[j], t_ref[j]) + _dot(xin, wo_ref[j])
        ys.append(jax.nn.gelu(y, approximate=True))
    for t in range(SSM_CHUNK):
        zt = jnp.concatenate([y[:, t * SSM_GROUP:(t + 1) * SSM_GROUP] for y in ys], axis=1)
        zc_ref[pl.ds(t, ncc, stride=SSM_CHUNK), :] = zt[:ncc]
        zl_ref[pl.ds(t, ncl, stride=SSM_CHUNK), :] = zt[ncc:]


def _ssm_mixer(u4, layer, w1, abar, tmat, wout, batch, seq, ctx_len):
    n_lt = u4.shape[0]
    g = N_SSM_GROUPS
    lg = SSM_LANE_GROUPS
    ncc, ncl = ctx_len // SSM_CHUNK, seq // SSM_CHUNK
    nc = ncc + ncl
    cw = SSM_CW
    sw = w1.shape[-1]
    pw = sw // 4
    ctx_blk0 = (batch * seq) // ctx_len
    plane_shape = [jax.ShapeDtypeStruct((g * nc, pw), F32)] * 4
    lat_spec = pl.BlockSpec((None, seq, 128), lambda k, b: (k, b, 0))
    ctx_spec = pl.BlockSpec((None, ctx_len, 128), lambda k, b: (k, ctx_blk0 + b, 0))
    v_spec = pl.BlockSpec((lg, None, nc, cw), lambda k, b: (k, b, 0, 0))
    plane_spec = pl.BlockSpec((lg * nc, pw), lambda k, b: (k, 0))
    v_chunks, *s_planes = pl.pallas_call(
        _ssm_p1_kernel,
        grid=(n_lt, batch),
        in_specs=[lat_spec, ctx_spec,
                  pl.BlockSpec((None, lg, None, cw, sw), lambda k, b: (layer, k, b, 0, 0))],
        out_specs=[v_spec] + [plane_spec] * 4,
        out_shape=[jax.ShapeDtypeStruct((g, batch, nc, cw), BF16)] + plane_shape,
        compiler_params=_cparams(("parallel", "arbitrary")),
        name="ssm_chunk_states",
    )(u4, u4, w1)
    gb = SSM_SCAN_GROUPS
    x_planes = pl.pallas_call(
        functools.partial(_ssm_scan_kernel, ncc, ncl),
        grid=(g // gb,),
        in_specs=[pl.BlockSpec((None, 4, gb, pw), lambda i: (layer, 0, i, 0))]
        + [pl.BlockSpec((gb * nc, pw), lambda i: (i, 0))] * 4,
        out_specs=[pl.BlockSpec((gb * nc, pw), lambda i: (i, 0))] * 4,
        out_shape=plane_shape,
        compiler_params=_cparams(("parallel",)),
        name="ssm_chunk_scan",
    )(abar, *s_planes)
    return pl.pallas_call(
        _ssm_p3_kernel,
        grid=(n_lt, batch),
        in_specs=[v_spec] + [plane_spec] * 4
        + [pl.BlockSpec((None, lg, cw, cw), lambda k, b: (layer, k, 0, 0)),
           pl.BlockSpec((None, lg, None, sw, cw), lambda k, b: (layer, k, b, 0, 0))],
        out_specs=[pl.BlockSpec((None, seq, 128), lambda k, b: (k, b, 0)),
                   pl.BlockSpec((None, ctx_len, 128), lambda k, b: (k, b, 0))],
        out_shape=[jax.ShapeDtypeStruct((n_lt, batch * seq, 128), F32),
                   jax.ShapeDtypeStruct((n_lt, batch * ctx_len, 128), F32)],
        compiler_params=_cparams(("parallel", "arbitrary")),
        name="ssm_chunk_outputs",
    )(v_chunks, *x_planes, tmat, wout)


def _merge_kernel(n_lat_tiles, *refs):
    if n_lat_tiles is None:
        attn_ref, z_ref, x_ref = refs[:3]
        refs = refs[3:]
    else:
        attn_ref, z_ref, x_ref, attn_c_ref, z_c_ref, x_c_ref = refs[:6]
        refs = refs[6:]
        is_ctx = pl.program_id(0) >= n_lat_tiles
    (ga_ref, gs_ref, mod_ref, modl_ref, g2_ref, wglu_ref, wba_ref, wbs_ref, wout_ref, wrh_ref, wrl_ref,
     xo_ref, h2_ref, rt_ref, cnt_ref, x_s) = refs
    step = pl.program_id(0)
    m = mod_ref[0]
    ml = modl_ref[0]
    tile_counts = None

    @pl.when(step == 0)
    def _():
        x_s[...] = jnp.zeros_like(x_s)

    for r in range(MERGE_TILE // MERGE_SUB):
        rows = slice(r * MERGE_SUB, (r + 1) * MERGE_SUB)
        h2 = _rms_mod(x_s[rows, :], g2_ref[...], ml[3:4], ml[4:5])
        hi, lo = _split_bf16(h2)
        h2_ref[rows, :] = hi.astype(F32)
        logits = _dot(hi, wrh_ref[...]) + _dot(lo, wrh_ref[...]) + _dot(hi, wrl_ref[...])
        rt = _route(logits)
        rt_ref[rows, :] = rt
        oh0, oh1 = _expert_onehots(rt)
        part = jnp.sum(oh0 + oh1, axis=0, keepdims=True)
        tile_counts = part if tile_counts is None else tile_counts + part

        zf = jnp.concatenate([z_ref[j, rows, :] for j in range(SSM_LANE_TILES)], axis=1)
        attn = attn_ref[rows, :]
        x_in = x_ref[rows, :]
        if n_lat_tiles is not None:
            zf = jnp.where(is_ctx, jnp.concatenate([z_c_ref[j, rows, :] for j in range(SSM_LANE_TILES)], axis=1), zf)
            attn = jnp.where(is_ctx, attn_c_ref[rows, :], attn)
            x_in = jnp.where(is_ctx, x_c_ref[rows, :], x_in)
        z = zf.astype(BF16)
        glu = (z.astype(F32) * jax.nn.sigmoid(_dot(z, wglu_ref[...]))).astype(BF16)
        mix = (ga_ref[rows, :].astype(F32) * _dot(attn, wba_ref[...])
               + gs_ref[rows, :].astype(F32) * _dot(glu, wbs_ref[...])).astype(BF16)
        x = x_in + m[2:3] * _dot(mix, wout_ref[...])
        xo_ref[rows, :] = x
        x_s[rows, :] = x

    @pl.when(step == 0)
    def _():
        cnt_ref[...] = jnp.zeros_like(cnt_ref)

    cnt_ref[0:1, :] = cnt_ref[0:1, :] + jnp.where(step > 0, tile_counts, 0.0)


def _route(lg):
    ng, epg = N_EXPERT_GROUPS, EXPERTS_PER_GROUP
    lane_i = lax.broadcasted_iota(jnp.int32, lg.shape, 1)
    lane = lane_i.astype(F32)
    big = float(ROUTER_PAD)

    def rmax(mask_val):
        return jnp.max(mask_val, axis=-1, keepdims=True)

    def first_lane(mask, val, mx):
        return jnp.min(jnp.where(mask, jnp.where(val == mx, lane, big), big), axis=-1, keepdims=True)

    gmask = lane_i < ng
    lgm = jnp.where(gmask, lg, NEG_INF)
    mg = rmax(lgm)
    g_prob = 1.0 / jnp.sum(jnp.exp(lgm - mg), axis=-1, keepdims=True)
    g_idx = first_lane(gmask, lg, mg)
    egroup = jnp.floor((lane - float(ng)) * (1.0 / epg))
    emask = egroup == g_idx
    l1 = jnp.where(emask, lg, NEG_INF)
    m1 = rmax(l1)
    i1 = first_lane(emask, lg, m1)
    l2 = jnp.where(lane == i1, NEG_INF, l1)
    m2 = rmax(l2)
    i2 = jnp.min(jnp.where(l2 == m2, jnp.where(emask, lane, big), big), axis=-1, keepdims=True)
    r = jnp.exp(m2 - m1)
    w1 = g_prob / (1.0 + r)
    w2 = w1 * r
    return jnp.where(lane_i == 0, i1 - float(ng),
                     jnp.where(lane_i == 1, i2 - float(ng),
                               jnp.where(lane_i == 2, w1, jnp.where(lane_i == 3, w2, 0.0))))


def _merge(attn, z, x, ctx_parts, ga, gs, mod, g2, w_glu, w_ba, w_bs, w_out, wr_hi, wr_lo, n_rows, t_lat, seq, batch):
    d = D_MODEL
    tile = MERGE_TILE
    assert n_rows % tile == 0 and t_lat % tile == 0 and seq % tile == 0
    n_tiles = n_rows // tile
    n_lat = t_lat // tile
    cur = lambda i: jnp.minimum(i, n_tiles - 1)
    prev = lambda i: jnp.maximum(i - 1, 0)
    row = lambda i: (cur(i), 0)
    row_lag = lambda i: (prev(i), 0)
    lat = lambda i: (jnp.minimum(i, n_lat - 1), 0)
    cxt = lambda i: (jnp.maximum(cur(i) - n_lat, 0), 0)
    const = lambda i: (0, 0)
    mod_of = lambda t: (jnp.minimum(t // (seq // tile), batch), 0, 0)
    modi = lambda i: mod_of(cur(i))
    modi_lag = lambda i: mod_of(prev(i))
    in_specs = [pl.BlockSpec((tile, Q_W), lat),
                pl.BlockSpec((SSM_LANE_TILES, tile, 128), lambda i: (0,) + lat(i)),
                pl.BlockSpec((tile, d), lat)]
    args = [attn, z, x]
    if ctx_parts is not None:
        in_specs += [pl.BlockSpec((tile, Q_W), cxt),
                     pl.BlockSpec((SSM_LANE_TILES, tile, 128), lambda i: (0,) + cxt(i)),
                     pl.BlockSpec((tile, d), cxt)]
        args += list(ctx_parts)
    in_specs += [
        pl.BlockSpec((tile, d), row),
        pl.BlockSpec((tile, d), row),
        pl.BlockSpec((1, 6, d), modi),
        pl.BlockSpec((1, 6, d), modi_lag),
        pl.BlockSpec((1, d), const),
        pl.BlockSpec((D_SSM, D_SSM), const),
        pl.BlockSpec((Q_W, d), const),
        pl.BlockSpec((D_SSM, d), const),
        pl.BlockSpec((d, d), const),
        pl.BlockSpec((d, ROUTER_PAD), const),
        pl.BlockSpec((d, ROUTER_PAD), const),
    ]
    args += [ga, gs, mod, mod, g2, w_glu, w_ba, w_bs, w_out, wr_hi, wr_lo]
    return pl.pallas_call(
        functools.partial(_merge_kernel, n_lat if ctx_parts is not None else None),
        grid=(n_tiles + 1,),
        in_specs=in_specs,
        out_specs=[pl.BlockSpec((tile, d), row), pl.BlockSpec((tile, d), row_lag),
                   pl.BlockSpec((tile, ROUTER_PAD), row_lag), pl.BlockSpec((8, ROUTER_PAD), const)],
        out_shape=[jax.ShapeDtypeStruct((n_rows, d), F32), jax.ShapeDtypeStruct((n_rows, d), F32),
                   jax.ShapeDtypeStruct((n_rows, ROUTER_PAD), F32), jax.ShapeDtypeStruct((8, ROUTER_PAD), F32)],
        scratch_shapes=[pltpu.VMEM((tile, d), F32)],
        compiler_params=_cparams(("arbitrary",)),
        name="merge_router",
    )(*args)


def _expert_kernel(layer, be_ref, nu_ref, nx_ref, x_ref, wg_hbm, wu_hbm, wd_hbm, y_ref,
                   wg_f, wu_f, wd_f, wg_s, wu_s, wd_s, slot_s, sems):
    i = pl.program_id(0)
    e = be_ref[i]
    changed = jnp.logical_or(i == 0, e != be_ref[jnp.maximum(i - 1, 0)])

    def weight_copies(expert, slot):
        return [pltpu.make_async_copy(src.at[layer, expert], dst.at[slot], sems.at[slot, j])
                for j, (src, dst) in enumerate(((wg_hbm, wg_f), (wu_hbm, wu_f), (wd_hbm, wd_f)))]

    @pl.when(i == 0)
    def _():
        slot_s[0] = 0
        for cp in weight_copies(e, 0):
            cp.start()

    @pl.when(changed)
    def _():
        slot = slot_s[0]
        for cp in weight_copies(e, slot):
            cp.wait()
        nxt = nx_ref[e]

        @pl.when(nxt >= 0)
        def _():
            for cp in weight_copies(nxt, 1 - slot):
                cp.start()

        wg_s[...] = wg_f[slot].astype(BF16)
        wu_s[...] = wu_f[slot].astype(BF16)
        wd_s[...] = wd_f[slot].astype(BF16)
        slot_s[0] = 1 - slot

    @pl.when(i < nu_ref[0])
    def _():
        x = jnp.concatenate([x_ref[pl.ds(j, MOE_BLOCK, stride=ROW_SUBLANES), :] for j in range(ROW_SUBLANES)],
                            axis=1).astype(BF16)
        gate = _dot(x, wg_s[...])
        up = _dot(x, wu_s[...])
        hid = (gate * jax.nn.sigmoid(gate) * up).astype(BF16)
        y_ref[...] = _dot(hid, wd_s[...]).astype(y_ref.dtype)

    @pl.when(i >= nu_ref[0])
    def _():
        y_ref[...] = jnp.zeros_like(y_ref)


def _experts(blk_exp, n_used, next_exp, xs, layer, w_gate, w_up, w_down):
    d, de = w_gate.shape[-2:]
    assert d == ROW_SUBLANES * 128
    n_slots = xs.shape[0] // ROW_SUBLANES
    n_blk = n_slots // MOE_BLOCK
    return pl.pallas_call(
        functools.partial(_expert_kernel, layer),
        grid_spec=pltpu.PrefetchScalarGridSpec(
            num_scalar_prefetch=3,
            grid=(n_blk,),
            in_specs=[
                pl.BlockSpec((MOE_BLOCK * ROW_SUBLANES, 128), lambda i, be, nu, nx: (jnp.minimum(i, nu[0] - 1), 0)),
                pl.BlockSpec(memory_space=pl.ANY),
                pl.BlockSpec(memory_space=pl.ANY),
                pl.BlockSpec(memory_space=pl.ANY),
            ],
            out_specs=pl.BlockSpec((MOE_BLOCK, d), lambda i, be, nu, nx: (i, 0)),
            scratch_shapes=[pltpu.VMEM((2, d, de), F32), pltpu.VMEM((2, d, de), F32), pltpu.VMEM((2, de, d), F32),
                            pltpu.VMEM((d, de), BF16), pltpu.VMEM((d, de), BF16), pltpu.VMEM((de, d), BF16),
                            pltpu.SMEM((1,), jnp.int32), pltpu.SemaphoreType.DMA((2, 3))],
        ),
        out_shape=jax.ShapeDtypeStruct((n_slots, d), BF16),
        compiler_params=_cparams(("arbitrary",)),
        name="expert_mlp",
    )(blk_exp, n_used, next_exp, xs, w_gate, w_up, w_down)


def _expert_onehots(rt):
    lane_f = lax.broadcasted_iota(jnp.int32, rt.shape, 1).astype(F32)
    return jnp.where(lane_f == rt[:, 0:1], 1.0, 0.0), jnp.where(lane_f == rt[:, 1:2], 1.0, 0.0)


def _plan_kernel(rt_ref, cnt_ref, dest_ref, pend_ref, run_s, before_s):
    i = pl.program_id(0)
    rows = rt_ref.shape[0]
    lane = lax.broadcasted_iota(jnp.int32, (rows, ROUTER_PAD), 1)
    oh0, oh1 = _expert_onehots(rt_ref[...])
    tot0 = jnp.sum(oh0, axis=0, keepdims=True)
    tot1 = jnp.sum(oh1, axis=0, keepdims=True)

    @pl.when(i == 0)
    def _():
        counts = cnt_ref[...]
        padded = jnp.floor((counts + float(MOE_BLOCK - 1)) * (1.0 / MOE_BLOCK)) * float(MOE_BLOCK)
        r = lax.broadcasted_iota(jnp.int32, (ROUTER_PAD, ROUTER_PAD), 0)
        c = lax.broadcasted_iota(jnp.int32, (ROUTER_PAD, ROUTER_PAD), 1)
        incl = jnp.where(r <= c, 1.0, 0.0).astype(BF16)
        hi, lo = _split_bf16(padded)
        pend = _dot(hi, incl) + _dot(lo, incl)
        pend_ref[...] = pend
        run_s[...] = pend - padded
        r = lax.broadcasted_iota(jnp.int32, (rows, rows), 0)
        c = lax.broadcasted_iota(jnp.int32, (rows, rows), 1)
        before_s[...] = jnp.where(c < r, 1.0, 0.0).astype(BF16)

    prior = _dot(before_s[...], jnp.concatenate([oh0, oh1], axis=1).astype(BF16))
    run = run_s[0:1, :]
    pos0 = jnp.sum(oh0 * (run + prior[:, :ROUTER_PAD]), axis=-1, keepdims=True)
    pos1 = jnp.sum(oh1 * (run + tot0 + prior[:, ROUTER_PAD:]), axis=-1, keepdims=True)
    run_s[0:1, :] = run + tot0 + tot1
    dest_ref[...] = jnp.where(lane == 0, pos0, jnp.where(lane == 1, pos1, 0.0)).astype(jnp.int32)


def _plan(route, counts, n_rows):
    rows = PLAN_TILE
    return pl.pallas_call(
        _plan_kernel,
        grid=(n_rows // rows,),
        in_specs=[pl.BlockSpec((rows, ROUTER_PAD), lambda i: (i, 0)),
                  pl.BlockSpec((8, ROUTER_PAD), lambda i: (0, 0))],
        out_specs=[pl.BlockSpec((rows, ROUTER_PAD), lambda i: (i, 0)),
                   pl.BlockSpec((8, ROUTER_PAD), lambda i: (0, 0))],
        out_shape=[jax.ShapeDtypeStruct((n_rows, ROUTER_PAD), jnp.int32),
                   jax.ShapeDtypeStruct((8, ROUTER_PAD), F32)],
        scratch_shapes=[pltpu.VMEM((8, ROUTER_PAD), F32), pltpu.VMEM((rows, rows), BF16)],
        compiler_params=_cparams(("arbitrary",)),
        name="dispatch_plan",
    )(route, counts)


def _dispatch_kernel(pend_ref, dest_ref, h_ref, xs_ref, hs, zero_s, sems, zsem):
    step = pl.program_id(0)
    rows = h_ref.shape[0] // 2
    sub = ROW_SUBLANES
    n_blk = xs_ref.shape[0] // (MOE_BLOCK * sub)
    n_used = pend_ref[N_EXPERTS - 1] // MOE_BLOCK

    def zero_block(start):
        return pltpu.make_async_copy(
            zero_s, xs_ref.at[pl.ds(pl.multiple_of(start * sub, MOE_BLOCK * sub), MOE_BLOCK * sub)], zsem)

    @pl.when(pl.program_id(0) == 0)
    def _():
        zero_s[...] = jnp.zeros_like(zero_s)
        for e in range(N_EXPERTS):
            @pl.when(pend_ref[e] > 0)
            def _():
                zero_block(pend_ref[e] - MOE_BLOCK).start()
        lax.fori_loop(n_used, n_blk, lambda j, c: (zero_block(j * MOE_BLOCK).start(), c)[1], 0)
        for e in range(N_EXPERTS):
            @pl.when(pend_ref[e] > 0)
            def _():
                zero_block(pend_ref[e] - MOE_BLOCK).wait()
        lax.fori_loop(n_used, n_blk, lambda j, c: (zero_block(j * MOE_BLOCK).wait(), c)[1], 0)

    def row_copy(half, r, d):
        return pltpu.make_async_copy(hs.at[half, pl.ds(pl.multiple_of(r * sub, sub), sub)],
                                     xs_ref.at[pl.ds(pl.multiple_of(d * sub, sub), sub)], sems.at[half])

    def drain(half):
        for _ in range(TOP_K * rows):
            row_copy(half, 0, 0).wait()

    for half in range(2):
        @pl.when(step > 0)
        def _():
            drain(half)

        for j in range(sub):
            hs[half, pl.ds(j, rows, stride=sub), :] = h_ref[half * rows:(half + 1) * rows, 128 * j:128 * (j + 1)]

        def issue(r, carry):
            for k in range(TOP_K):
                row_copy(half, r, dest_ref[0, 0, (half * TOP_K + k) * rows + r]).start(priority=k % 2)
            return carry

        lax.fori_loop(0, rows, issue, 0, unroll=8)

    @pl.when(step == pl.num_programs(0) - 1)
    def _():
        drain(0)
        drain(1)


def _dispatch(pend, dest_tiles, h2, n_slots, n_tiles):
    d = h2.shape[1]
    assert n_tiles % 2 == 0
    return pl.pallas_call(
        _dispatch_kernel,
        grid_spec=pltpu.PrefetchScalarGridSpec(
            num_scalar_prefetch=1,
            grid=(n_tiles // 2,),
            in_specs=[pl.BlockSpec((1, 1, 2 * TOP_K * ROW_TILE), lambda i, pe: (i, 0, 0), memory_space=pltpu.SMEM),
                      pl.BlockSpec((2 * ROW_TILE, d), lambda i, pe: (i, 0))],
            out_specs=pl.BlockSpec(memory_space=pl.ANY),
            scratch_shapes=[pltpu.VMEM((2, ROW_TILE * ROW_SUBLANES, 128), F32),
                            pltpu.VMEM((MOE_BLOCK * ROW_SUBLANES, 128), F32),
                            pltpu.SemaphoreType.DMA((2,)), pltpu.SemaphoreType.DMA],
        ),
        out_shape=jax.ShapeDtypeStruct((n_slots * ROW_SUBLANES, 128), F32),
        compiler_params=_cparams(("arbitrary",)),
        name="dispatch_rows",
    )(pend, dest_tiles.reshape(n_tiles // 2, 1, 2 * TOP_K * ROW_TILE), h2)


def _moe(h2, route, counts, n_tiles, layer, w_gate, w_up, w_down):
    t, d = h2.shape
    dest, pend_f = _plan(route, counts, t)
    pend = pend_f[0, :N_EXPERTS].astype(jnp.int32)
    n_blk = -(-(t * TOP_K) // MOE_BLOCK) + N_EXPERTS
    blk_start = jnp.arange(n_blk, dtype=jnp.int32) * MOE_BLOCK
    blk_exp = jnp.minimum(jnp.sum((pend[None, :] <= blk_start[:, None]).astype(jnp.int32), axis=1), N_EXPERTS - 1)
    n_used = (pend[N_EXPERTS - 1] // MOE_BLOCK).reshape(1)
    blk_exp = jnp.where(jnp.arange(n_blk) < n_used[0], blk_exp, blk_exp[n_used[0] - 1])
    has_rows = pend > jnp.concatenate([jnp.zeros((1,), jnp.int32), pend[:-1]])
    e_ids = jnp.arange(N_EXPERTS, dtype=jnp.int32)
    later = jnp.where(jnp.logical_and(has_rows[None, :], e_ids[None, :] > e_ids[:, None]), e_ids[None, :], N_EXPERTS)
    next_exp = jnp.min(later, axis=1)
    next_exp = jnp.where(next_exp < N_EXPERTS, next_exp, -1).astype(jnp.int32)
    dest2 = dest[:, :TOP_K]
    dest_tiles = dest2.reshape(n_tiles, ROW_TILE, TOP_K).transpose(0, 2, 1).reshape(n_tiles, 1, TOP_K * ROW_TILE)
    xs = _dispatch(pend, dest_tiles, h2, n_blk * MOE_BLOCK, n_tiles)
    ys = _experts(blk_exp.astype(jnp.int32), n_used, next_exp, xs, layer, w_gate, w_up, w_down)
    return ys[dest2[:, 0]], ys[dest2[:, 1]]


def _final_kernel(x_ref, y0_ref, y1_ref, rt_ref, mod_ref, g_ref, o_ref):
    x = _moe_residual(x_ref, y0_ref, y1_ref, rt_ref, mod_ref)
    o_ref[...] = x * lax.rsqrt(jnp.mean(x * x, axis=-1, keepdims=True) + EPS) * g_ref[...]


def _final(x_lat, moe_out, mod, g_final, n_rows, seq, batch):
    d = D_MODEL
    tile = MERGE_TILE
    assert n_rows % tile == 0 and seq % tile == 0
    row = lambda i: (i, 0)
    return pl.pallas_call(
        _final_kernel,
        grid=(n_rows // tile,),
        in_specs=[pl.BlockSpec((tile, d), row), pl.BlockSpec((tile, d), row),
                  pl.BlockSpec((tile, d), row), pl.BlockSpec((tile, ROUTER_PAD), row),
                  pl.BlockSpec((1, 6, d), lambda i: (jnp.minimum(i // (seq // tile), batch), 0, 0)),
                  pl.BlockSpec((1, d), lambda i: (0, 0))],
        out_specs=pl.BlockSpec((tile, d), row),
        out_shape=jax.ShapeDtypeStruct((n_rows, d), F32),
        compiler_params=_cparams(("parallel",)),
        name="final_norm",
    )(x_lat, *moe_out, mod, g_final)


def _rope_tables(seq):
    quarter = HEAD_DIM // 4
    freqs = ROPE_THETA ** (-jnp.arange(quarter, dtype=F32) / quarter)
    pos = jnp.arange(seq)
    ang_r = (pos // GRID_W).astype(F32)[:, None] * freqs[None, :]
    ang_c = (pos % GRID_W).astype(F32)[:, None] * freqs[None, :]
    cos = jnp.concatenate([jnp.cos(ang_r)] * 2 + [jnp.cos(ang_c)] * 2, axis=-1)
    sin = jnp.concatenate([-jnp.sin(ang_r), jnp.sin(ang_r), -jnp.sin(ang_c), jnp.sin(ang_c)], axis=-1)
    reps = 128 // HEAD_DIM
    cos = jnp.tile(cos, (1, reps))
    sin = jnp.tile(sin, (1, reps))
    cos = jnp.concatenate([cos, jnp.ones((ROW_TILE, 128), F32)], axis=0)
    sin = jnp.concatenate([sin, jnp.zeros((ROW_TILE, 128), F32)], axis=0)
    return cos, sin


def kernel(x, c, ctx, c_ctx, w_mod, b_mod, g_norm1, g_norm2, w_in, attn_sink, ssm_lam_re, ssm_lam_im,
           ssm_log_dt, ssm_b_re, ssm_b_im, ssm_c_re, ssm_c_im, ssm_d, w_glu, w_br_attn, w_br_ssm, w_out,
           w_router_group, w_router_expert, w_exp_gate, w_exp_up, w_exp_down, g_final):
    batch, seq, d = x.shape
    ctx_len = ctx.shape[1]
    depth = w_mod.shape[0]
    assert d == D_MODEL and batch * SSM_STATE == 128
    assert seq % ROW_TILE == 0 and ctx_len % ROW_TILE == 0 and (batch * seq) % ctx_len == 0
    assert batch + 1 <= MOD_ROWS
    t_lat, t_ctx = batch * seq, batch * ctx_len
    tiles_per_seq = seq // ROW_TILE
    n_lat_tiles = t_lat // ROW_TILE
    n_all_tiles = (t_lat + t_ctx) // ROW_TILE
    nb_per_seq = seq // ATTN_BLOCK
    ncc, ncl = ctx_len // SSM_CHUNK, seq // SSM_CHUNK

    c_rows = jnp.zeros((MOD_ROWS, d), F32).at[:batch].set(c).at[batch].set(c_ctx)
    mod_all = _modulation(c_rows, w_mod, b_mod).reshape(depth, MOD_ROWS, 6, d)
    cos_t, sin_t = _rope_tables(seq)
    ssm_w = _ssm_weights_all(ssm_lam_re, ssm_lam_im, ssm_log_dt, ssm_b_re, ssm_b_im, ssm_c_re, ssm_c_im, ssm_d, batch)
    x_parts = (x.reshape(t_lat, d), ctx.reshape(t_ctx, d))
    f_all = None
    for l in range(depth):
        ctx_out = l < depth - 1
        mod = mod_all[l]
        x_parts, (q, k, v, u, ga, gs) = _inproj(
            x_parts, f_all, mod_all[l - 1] if l > 0 else None, mod, g_norm1[l].reshape(1, d), cos_t, sin_t,
            w_in[l].astype(BF16), n_all_tiles, tiles_per_seq, n_lat_tiles, batch)
        sink = attn_sink[l].astype(F32)
        attn = _attention(sink, q, k, v, t_lat // ATTN_BLOCK, nb_per_seq, t_lat // ctx_len, ctx_len, True, 0)
        z, z_ctx = _ssm_mixer(u, l, *ssm_w, batch, seq, ctx_len)
        ctx_parts = None
        if ctx_out:
            attn_c = _attention(sink, q, k, v, t_ctx // ATTN_BLOCK, ctx_len // ATTN_BLOCK, t_lat // ctx_len,
                                ctx_len, False, t_lat // ATTN_BLOCK)
            ctx_parts = (attn_c, z_ctx, x_parts[1])
        w_r = jnp.zeros((d, ROUTER_PAD), F32)
        w_r = w_r.at[:, :N_EXPERT_GROUPS].set(w_router_group[l])
        w_r = w_r.at[:, N_EXPERT_GROUPS:N_EXPERT_GROUPS + N_EXPERTS].set(w_router_expert[l])
        wr_hi, wr_lo = _split_bf16(w_r)
        n_rows = t_lat + t_ctx if ctx_out else t_lat
        x_all, h2, route, counts = _merge(
            attn, z, x_parts[0], ctx_parts, ga, gs, mod, g_norm2[l].reshape(1, d), w_glu[l].astype(BF16),
            w_br_attn[l].astype(BF16), w_br_ssm[l].astype(BF16), w_out[l].astype(BF16),
            wr_hi, wr_lo, n_rows, t_lat, seq, batch)
        y0, y1 = _moe(h2, route, counts, n_rows // ROW_TILE, l, w_exp_gate, w_exp_up, w_exp_down)
        f_all = (y0, y1, route)
        x_parts = (x_all,)
    out = _final(x_all, f_all, mod_all[depth - 1], g_final.reshape(1, d), t_lat, seq, batch)
    return out.reshape(batch, seq, d)
```

```python
import functools
import math

import jax
import jax.numpy as jnp
from jax import lax
from jax.experimental import pallas as pl
from jax.experimental.pallas import tpu as pltpu

F32 = jnp.float32
BF16 = jnp.bfloat16
LANES = 128

D_MODEL = 1024
GRID_W = 64
N_HEADS = 8
N_KV_HEADS = 2
HEAD_DIM = 64
Q_GROUP = N_HEADS // N_KV_HEADS
ATTN_BLOCK = 128
ATTN_STEP_BLOCKS = 4
ROPE_THETA = 10000.0
D_SSM = D_MODEL // 2
SSM_GROUP = 16
N_SSM_GROUPS = D_SSM // SSM_GROUP
SSM_STATE = 64
N_EXPERT_GROUPS = 4
EXPERTS_PER_GROUP = 8
N_EXPERTS = N_EXPERT_GROUPS * EXPERTS_PER_GROUP
TOP_K = 2
Q_W = N_HEADS * HEAD_DIM
KV_W = N_KV_HEADS * HEAD_DIM
O_K = Q_W
O_V = O_K + KV_W
O_U = O_V + KV_W
O_GA = O_U + D_SSM
O_GS = O_GA + D_MODEL
D_IN = O_GS + D_MODEL
EPS = 1e-6
NEG_INF = -1e30

ROW_TILE = 256
MERGE_TILE = 512
MERGE_SUB = 256
SSM_CHUNK = 16
SSM_CW = SSM_CHUNK * SSM_GROUP
SSM_SCAN_GROUPS = 8
SSM_LANE_GROUPS = LANES // SSM_GROUP
SSM_LANE_TILES = D_SSM // LANES
MOE_BLOCK = 256
ROW_SUBLANES = D_MODEL // LANES
PLAN_TILE = 512
ROUTER_PAD = LANES
MOD_ROWS = 8
VMEM_LIMIT = 48 * 1024 * 1024


def _cparams(sem):
    return pltpu.CompilerParams(dimension_semantics=sem, vmem_limit_bytes=VMEM_LIMIT)


def _dot(a, b):
    return jnp.dot(a, b, preferred_element_type=F32)


def _split_bf16(a):
    hi = a.astype(BF16)
    lo = (a - hi.astype(F32)).astype(BF16)
    return hi, lo


def _rms_mod(x, g, shift, scale):
    y = x * lax.rsqrt(jnp.mean(x * x, axis=-1, keepdims=True) + EPS) * g
    return y * (1.0 + scale) + shift


def _mod_kernel(c_ref, w_ref, b_ref, o_ref):
    c = c_ref[...]
    s_hi, s_lo = _split_bf16(c * jax.nn.sigmoid(c))
    w_hi, w_lo = _split_bf16(w_ref[0])
    o_ref[0] = _dot(s_hi, w_hi) + _dot(s_lo, w_hi) + _dot(s_hi, w_lo) + b_ref[0]


def _modulation(c_rows, w_mod, b_mod):
    depth, d, n = w_mod.shape
    nb = n // 4
    return pl.pallas_call(
        _mod_kernel,
        grid=(depth, n // nb),
        in_specs=[
            pl.BlockSpec((MOD_ROWS, d), lambda l, j: (0, 0)),
            pl.BlockSpec((1, d, nb), lambda l, j: (l, 0, j)),
            pl.BlockSpec((1, 1, nb), lambda l, j: (l, 0, j)),
        ],
        out_specs=pl.BlockSpec((1, MOD_ROWS, nb), lambda l, j: (l, 0, j)),
        out_shape=jax.ShapeDtypeStruct((depth, MOD_ROWS, n), F32),
        compiler_params=_cparams(("arbitrary", "arbitrary")),
        name="modulation",
    )(c_rows, w_mod, b_mod.reshape(depth, 1, n))


def _moe_residual(x_ref, y0_ref, y1_ref, rt_ref, mod_ref):
    rt = rt_ref[...]
    f = rt[:, 2:3] * y0_ref[...].astype(F32) + rt[:, 3:4] * y1_ref[...].astype(F32)
    return x_ref[...] + mod_ref[0, 5:6, :] * f


def _inproj_kernel(has_f, n_lat_tiles, *refs):
    if has_f:
        (x_ref, y0_ref, y1_ref, rt_ref, modp_ref, mod_ref, g_ref, cos_ref, sin_ref, w_ref,
         xo_ref, q_ref, k_ref, v_ref, u_ref, ga_ref, gs_ref) = refs
        x = _moe_residual(x_ref, y0_ref, y1_ref, rt_ref, modp_ref)
        xo_ref[...] = x
    else:
        (xl_ref, xc_ref, mod_ref, g_ref, cos_ref, sin_ref, w_ref,
         q_ref, k_ref, v_ref, u_ref, ga_ref, gs_ref) = refs
        x = jnp.where(pl.program_id(0) >= n_lat_tiles, xc_ref[...], xl_ref[...])
    m = mod_ref[0]
    h = _rms_mod(x, g_ref[...], m[0:1], m[1:2]).astype(BF16)
    cos = cos_ref[...]
    sin = sin_ref[...]
    lane = lax.broadcasted_iota(jnp.int32, cos.shape, 1)
    first = (lane % (HEAD_DIM // 2)) < (HEAD_DIM // 4)

    def rope(t):
        sw = jnp.where(first, pltpu.roll(t, LANES - HEAD_DIM // 4, 1), pltpu.roll(t, HEAD_DIM // 4, 1))
        return t * cos + sw * sin

    def proj(lo, hi):
        return _dot(h, w_ref[:, lo:hi])

    q = proj(0, O_K)
    for j in range(Q_W // LANES):
        q_ref[:, LANES * j:LANES * (j + 1)] = (rope(q[:, LANES * j:LANES * (j + 1)]) * HEAD_DIM ** -0.5).astype(BF16)
    kv = proj(O_K, O_U)
    k_ref[...] = rope(kv[:, :KV_W]).astype(BF16)
    v_ref[...] = kv[:, KV_W:].astype(BF16)
    u = proj(O_U, O_GA)
    for j in range(SSM_LANE_TILES):
        u_ref[j] = u[:, LANES * j:LANES * (j + 1)]
    ga_ref[...] = jax.nn.sigmoid(proj(O_GA, O_GS)).astype(BF16)
    gs_ref[...] = jax.nn.sigmoid(proj(O_GS, D_IN)).astype(BF16)


def _inproj(x_parts, moe_out, mod_prev, mod, g1, cos_t, sin_t, w_in, n_tiles, tiles_per_seq, n_lat_tiles, batch):
    d = D_MODEL
    has_f = moe_out is not None
    row = lambda i: (i, 0)
    modi = lambda i: (jnp.minimum(i // tiles_per_seq, batch), 0, 0)
    const = lambda i: (0, 0)
    ropei = lambda i: (jnp.where(i < n_lat_tiles, i % tiles_per_seq, tiles_per_seq), 0)
    if has_f:
        in_specs = [pl.BlockSpec((ROW_TILE, d), row), pl.BlockSpec((ROW_TILE, d), row),
                    pl.BlockSpec((ROW_TILE, d), row), pl.BlockSpec((ROW_TILE, ROUTER_PAD), row),
                    pl.BlockSpec((1, 6, d), modi)]
        args = [*x_parts, *moe_out, mod_prev]
    else:
        in_specs = [pl.BlockSpec((ROW_TILE, d), lambda i: (jnp.minimum(i, n_lat_tiles - 1), 0)),
                    pl.BlockSpec((ROW_TILE, d), lambda i: (jnp.maximum(i - n_lat_tiles, 0), 0))]
        args = list(x_parts)
    in_specs += [
        pl.BlockSpec((1, 6, d), modi),
        pl.BlockSpec((1, d), const),
        pl.BlockSpec((ROW_TILE, LANES), ropei),
        pl.BlockSpec((ROW_TILE, LANES), ropei),
        pl.BlockSpec((d, D_IN), const),
    ]
    args += [mod, g1, cos_t, sin_t, w_in]
    widths = [Q_W, KV_W, KV_W, D_SSM, D_MODEL, D_MODEL]
    out_specs = [pl.BlockSpec((ROW_TILE, w), row) for w in widths]
    out_shape = [jax.ShapeDtypeStruct((n_tiles * ROW_TILE, w), BF16) for w in widths]
    out_specs[3] = pl.BlockSpec((SSM_LANE_TILES, ROW_TILE, LANES), lambda i: (0, i, 0))
    out_shape[3] = jax.ShapeDtypeStruct((SSM_LANE_TILES, n_tiles * ROW_TILE, LANES), F32)
    if has_f:
        out_specs = [pl.BlockSpec((ROW_TILE, d), row)] + out_specs
        out_shape = [jax.ShapeDtypeStruct((n_tiles * ROW_TILE, d), F32)] + out_shape
    outs = pl.pallas_call(
        functools.partial(_inproj_kernel, has_f, n_lat_tiles),
        grid=(n_tiles,),
        in_specs=in_specs,
        out_specs=out_specs,
        out_shape=out_shape,
        compiler_params=_cparams(("parallel",)),
        name="inproj",
    )(*args)
    if has_f:
        return (outs[0],), outs[1:]
    return x_parts, outs


def _attn_scores(q, k_tiles, v_tiles):
    res = []
    for g in range(N_KV_HEADS):
        gs = slice(g * HEAD_DIM, (g + 1) * HEAD_DIM)
        k_all = jnp.concatenate([t[:, gs] for t in k_tiles], axis=0)
        v_all = jnp.concatenate([t[:, gs] for t in v_tiles], axis=0)
        v_ext = jnp.concatenate([v_all, jnp.ones_like(v_all)], axis=1)
        qg = jnp.concatenate(
            [q[:, (g * Q_GROUP + h) * HEAD_DIM:(g * Q_GROUP + h + 1) * HEAD_DIM] for h in range(Q_GROUP)], axis=0)
        res.append((lax.dot_general(k_all, qg, (((1,), (1,)), ((), ())), preferred_element_type=F32), v_ext))
    return res


def _attn_finish(sink_ref, scores, tile_rows, biases, blk):
    outs = []
    for g, (st, v_ext) in enumerate(scores):
        tiles, row = [], 0
        for n, bias in zip(tile_rows, biases):
            t = st[row:row + n, :]
            tiles.append(t if bias is None else t + jnp.concatenate([bias] * Q_GROUP, axis=1))
            row += n
        mx = None
        for t in tiles:
            for r0 in range(0, t.shape[0], blk):
                mx = t[r0:r0 + blk] if mx is None else jnp.maximum(mx, t[r0:r0 + blk])
        sink = jnp.concatenate([jnp.full((1, blk), sink_ref[g * Q_GROUP + h], F32) for h in range(Q_GROUP)], axis=1)
        m = jnp.maximum(jnp.max(mx, axis=0, keepdims=True), sink)
        p = jnp.exp(jnp.concatenate([(t - m).astype(BF16) for t in tiles], axis=0))
        o_t = lax.dot_general(v_ext, p, (((0,), (0,)), ((), ())), preferred_element_type=F32)
        denom = o_t[HEAD_DIM:HEAD_DIM + 1, :] + jnp.exp(sink - m)
        o_n = o_t[:HEAD_DIM, :] * (1.0 / denom)
        for h in range(Q_GROUP):
            outs.append(o_n[:, h * blk:(h + 1) * blk].T.astype(BF16))
    return jnp.concatenate(outs, axis=1)


def _attn_kernel(band, nb_per_seq, sink_ref, *refs):
    blk = ATTN_BLOCK
    if not band:
        q_ref, kx_ref, vx_ref, o_ref = refs
        n_sub = q_ref.shape[0] // blk
        scores = [_attn_scores(q_ref[a * blk:(a + 1) * blk, :], [kx_ref[...]], [vx_ref[...]]) for a in range(n_sub)]
        for a in range(n_sub):
            o_ref[a * blk:(a + 1) * blk, :] = _attn_finish(sink_ref, scores[a], [kx_ref.shape[0]], [None], blk)
        return
    q_ref, kp_ref, kc_ref, kn_ref, vp_ref, vc_ref, vn_ref, kx_ref, vx_ref, o_ref = refs
    n_sub = q_ref.shape[0] // blk
    j0 = (pl.program_id(0) * n_sub) % nb_per_seq
    r = lax.broadcasted_iota(jnp.int32, (blk, blk), 0)
    c = lax.broadcasted_iota(jnp.int32, (blk, blk), 1)
    k_blocks = [kp_ref[...]] + [kc_ref[a * blk:(a + 1) * blk, :] for a in range(n_sub)] + [kn_ref[...]]
    v_blocks = [vp_ref[...]] + [vc_ref[a * blk:(a + 1) * blk, :] for a in range(n_sub)] + [vn_ref[...]]
    scores = [_attn_scores(q_ref[a * blk:(a + 1) * blk, :], k_blocks[a:a + 3] + [kx_ref[...]],
                           v_blocks[a:a + 3] + [vx_ref[...]]) for a in range(n_sub)]
    for a in range(n_sub):
        edge_p = jnp.where(j0 + a > 0, 0.0, NEG_INF).astype(F32)
        edge_n = jnp.where(j0 + a < nb_per_seq - 1, 0.0, NEG_INF).astype(F32)
        bias_p = jnp.where(r >= c, edge_p, NEG_INF).astype(F32)
        bias_n = jnp.where(r <= c, edge_n, NEG_INF).astype(F32)
        o_ref[a * blk:(a + 1) * blk, :] = _attn_finish(
            sink_ref, scores[a], [blk, blk, blk, kx_ref.shape[0]], [bias_p, None, bias_n, None], blk)


def _attention(sink, q, k, v, n_blocks, nb_per_seq, kx_block0, ctx_len, band, q_block0):
    blk = ATTN_BLOCK
    n_sub = math.gcd(ATTN_STEP_BLOCKS, nb_per_seq, n_blocks, q_block0)
    step = n_sub * blk
    n_steps = n_blocks // n_sub
    qi = lambda i, s: (q_block0 // n_sub + i, 0)
    cur = lambda i, s: (i, 0)
    prv = lambda i, s: (jnp.maximum(i * n_sub - 1, 0), 0)
    nxt = lambda i, s: (jnp.minimum((i + 1) * n_sub, n_blocks - 1), 0)
    kxi = lambda i, s: (kx_block0 + (i * n_sub) // nb_per_seq, 0)
    kspec = lambda f: pl.BlockSpec((blk, KV_W), f)
    cspec = pl.BlockSpec((step, KV_W), cur)
    xspec = pl.BlockSpec((ctx_len, KV_W), kxi)
    if band:
        in_specs = [pl.BlockSpec((step, Q_W), qi), kspec(prv), cspec, kspec(nxt),
                    kspec(prv), cspec, kspec(nxt), xspec, xspec]
        args = (q, k, k, k, v, v, v, k, v)
    else:
        in_specs = [pl.BlockSpec((step, Q_W), qi), xspec, xspec]
        args = (q, k, v)
    return pl.pallas_call(
        functools.partial(_attn_kernel, band, nb_per_seq),
        grid_spec=pltpu.PrefetchScalarGridSpec(
            num_scalar_prefetch=1,
            grid=(n_steps,),
            in_specs=in_specs,
            out_specs=pl.BlockSpec((step, Q_W), lambda i, s: (i, 0)),
        ),
        out_shape=jax.ShapeDtypeStruct((n_blocks * blk, Q_W), BF16),
        compiler_params=_cparams(("parallel",)),
        name="band_attention" if band else "context_attention",
    )(sink, *args)


def _dot_f32(a, b_t):
    a_hi, a_lo = _split_bf16(a)
    b_hi, b_lo = _split_bf16(b_t)
    dn = (((1,), (1,)), ((), ()))
    dg = functools.partial(lax.dot_general, dimension_numbers=dn, preferred_element_type=F32)
    return dg(a_hi, b_hi) + dg(a_lo, b_hi) + dg(a_hi, b_lo)


def _ssm_weight_kernel(batch, lam_r_ref, bt_ref, c_ref, d_ref, w1_ref, abar_ref, t_ref, wo_ref):
    lc, mm, p = SSM_CHUNK, SSM_GROUP, SSM_STATE
    up_r = lax.broadcasted_iota(jnp.int32, (lc, 1), 0).astype(F32)
    lane = lax.broadcasted_iota(jnp.int32, (mm, lc * mm), 1)
    row = lax.broadcasted_iota(jnp.int32, (mm, lc * mm), 0)
    planes_in, planes_out, abar_rows, kt = [], [], [], []
    for d in range(2):
        lr, li, dt = lam_r_ref[0, d, 0:1, :], lam_r_ref[0, d, 1:2, :], jnp.exp(lam_r_ref[0, d, 2:3, :])

        def powers(expo):
            mag = jnp.exp(lr * dt * expo)
            return mag * jnp.cos(li * dt * expo), mag * jnp.sin(li * dt * expo)

        a_re, a_im = powers(jnp.ones((1, 1), F32))
        den = lr * lr + li * li
        nr = a_re - 1.0
        f_re = (nr * lr + a_im * li) / den
        f_im = (a_im * lr - nr * li) / den
        bt_re, bt_im = bt_ref[0, d, 0], bt_ref[0, d, 1]
        bbt_re = f_re * bt_re - f_im * bt_im
        bbt_im = f_re * bt_im + f_im * bt_re
        c_re, c_im = c_ref[0, d, 0], c_ref[0, d, 1]

        pr, pi = powers(lc - 1.0 - up_r if d == 0 else up_r)
        planes_in.append(jnp.concatenate(
            [pr[s:s + 1] * bbt_re - pi[s:s + 1] * bbt_im for s in range(lc)], axis=0))
        planes_in.append(jnp.concatenate(
            [pr[s:s + 1] * bbt_im + pi[s:s + 1] * bbt_re for s in range(lc)], axis=0))

        qr, qi = powers(up_r + 1.0 if d == 0 else lc - up_r)
        planes_out.append(jnp.concatenate(
            [c_re * qr[t:t + 1] - c_im * qi[t:t + 1] for t in range(lc)], axis=0).T)
        planes_out.append(jnp.concatenate(
            [-c_re * qi[t:t + 1] - c_im * qr[t:t + 1] for t in range(lc)], axis=0).T)

        kr, ki = powers(up_r if d == 0 else lc - 1.0 - up_r)
        cp_re = jnp.concatenate([c_re * kr[j:j + 1] - c_im * ki[j:j + 1] for j in range(lc)], axis=0)
        cp_im = jnp.concatenate([c_re * ki[j:j + 1] + c_im * kr[j:j + 1] for j in range(lc)], axis=0)
        kt.append(_dot_f32(jnp.concatenate([bbt_re, bbt_im], axis=1), jnp.concatenate([cp_re, -cp_im], axis=1)))

        e_re, e_im = powers(jnp.full((1, 1), float(lc), F32))
        abar_rows += [jnp.concatenate([e_re] * batch, axis=1), jnp.concatenate([e_im] * batch, axis=1)]

    dvec = d_ref[0]
    blocks = []
    for s in range(lc):
        fwd = kt[0] if s == 0 else pltpu.roll(kt[0], mm * s, 1)
        bwd = kt[1] if s == lc - 1 else pltpu.roll(kt[1], mm * (s + 1), 1)
        blk = jnp.where(lane >= mm * s, fwd, 0.0) + jnp.where(lane < mm * (s + 1), bwd, 0.0)
        blocks.append(blk + jnp.where(lane == mm * s + row, dvec, 0.0))
    t_ref[0] = jnp.concatenate(blocks, axis=0).astype(BF16)
    abar_ref[0] = jnp.concatenate(abar_rows, axis=0)
    zero_in = jnp.zeros((lc * mm, p), F32)
    zero_out = jnp.zeros((p, lc * mm), F32)
    for b in range(batch):
        w1_ref[0, b] = jnp.concatenate(
            [pl_ if bb == b else zero_in for pl_ in planes_in for bb in range(batch)], axis=1).astype(BF16)
        wo_ref[0, b] = jnp.concatenate(
            [pl_ if bb == b else zero_out for pl_ in planes_out for bb in range(batch)], axis=0).astype(BF16)


def _ssm_weights_all(lam_re, lam_im, log_dt, b_re, b_im, c_re, c_im, d_skip, batch):
    depth = lam_re.shape[0]
    g, p, mm, lc = N_SSM_GROUPS, SSM_STATE, SSM_GROUP, SSM_CHUNK
    n = depth * g
    cw, sw = lc * mm, 4 * batch * p

    def per_group(a):
        return jnp.moveaxis(a.astype(F32), 2, 1).reshape((n, 2) + a.shape[3:])

    lam = jnp.stack([per_group(lam_re), per_group(lam_im),
                     jnp.broadcast_to(per_group(log_dt)[..., None], (n, 2, p))], axis=2)
    bt = jnp.stack([per_group(b_re), per_group(b_im)], axis=2).swapaxes(-1, -2)
    c = jnp.stack([per_group(c_re), per_group(c_im)], axis=2)
    d_t = jnp.tile(d_skip.astype(F32).reshape(n, 1, mm), (1, 1, lc))
    full = lambda *shape: pl.BlockSpec((1,) + shape, lambda i: (i,) + (0,) * len(shape))
    w1, abar, tmat, wout = pl.pallas_call(
        functools.partial(_ssm_weight_kernel, batch),
        grid=(n,),
        in_specs=[full(2, 3, p), full(2, 2, mm, p), full(2, 2, mm, p), full(1, cw)],
        out_specs=[full(batch, cw, sw), full(4, batch * p), full(cw, cw), full(batch, sw, cw)],
        out_shape=[jax.ShapeDtypeStruct((n, batch, cw, sw), BF16), jax.ShapeDtypeStruct((n, 4, batch * p), F32),
                   jax.ShapeDtypeStruct((n, cw, cw), BF16), jax.ShapeDtypeStruct((n, batch, sw, cw), BF16)],
        compiler_params=_cparams(("parallel",)),
        name="ssm_weights",
    )(lam, bt, c, d_t)
    abar = abar.reshape(depth, g, 4, batch * p).transpose(0, 2, 1, 3)
    return (w1.reshape(depth, g, batch, cw, sw), abar, tmat.reshape(depth, g, cw, cw),
            wout.reshape(depth, g, batch, sw, cw))


def _ssm_p1_kernel(ul_ref, uc_ref, w_ref, v_ref, *s_refs):
    b = pl.program_id(1)
    ncc, ncl = uc_ref.shape[0] // SSM_CHUNK, ul_ref.shape[0] // SSM_CHUNK
    nc = ncc + ncl
    pw = s_refs[0].shape[1]
    xs = [jnp.concatenate([uc_ref[pl.ds(s, ncc, stride=SSM_CHUNK), :], ul_ref[pl.ds(s, ncl, stride=SSM_CHUNK), :]],
                          axis=0).astype(BF16) for s in range(SSM_CHUNK)]
    for j in range(SSM_LANE_GROUPS):
        v = jnp.concatenate([x[:, j * SSM_GROUP:(j + 1) * SSM_GROUP] for x in xs], axis=1)
        v_ref[j] = v
        acc = _dot(v, w_ref[j])
        for k, s_ref in enumerate(s_refs):
            part = acc[:, k * pw:(k + 1) * pw]
            rows = slice(j * nc, (j + 1) * nc)

            @pl.when(b == 0)
            def _():
                s_ref[rows, :] = part

            @pl.when(b > 0)
            def _():
                s_ref[rows, :] = s_ref[rows, :] + part


def _ssm_scan_kernel(ncc, ncl, a_ref, sfr_ref, sfi_ref, sbr_ref, sbi_ref, xfr_ref, xfi_ref, xbr_ref, xbi_ref):
    nc = ncc + ncl
    gb = a_ref.shape[1]
    pw = a_ref.shape[2]
    afr, afi, abr, abi = a_ref[0], a_ref[1], a_ref[2], a_ref[3]

    def rows(r):
        return pl.ds(r, gb, stride=nc)

    def step(rf, rb, carry):
        xfr, xfi, xbr, xbi = carry
        xfr_ref[rows(rf), :] = xfr
        xfi_ref[rows(rf), :] = xfi
        xbr_ref[rows(rb), :] = xbr
        xbi_ref[rows(rb), :] = xbi
        sfr = sfr_ref[rows(rf), :]
        sfi = sfi_ref[rows(rf), :]
        sbr = sbr_ref[rows(rb), :]
        sbi = sbi_ref[rows(rb), :]
        return (afr * xfr - afi * xfi + sfr, afr * xfi + afi * xfr + sfi,
                abr * xbr - abi * xbi + sbr, abr * xbi + abi * xbr + sbi)

    zero = jnp.zeros((gb, pw), F32)
    carry = lax.fori_loop(0, ncc, lambda t, c: step(t, ncc - 1 - t, c), (zero, zero, zero, zero), unroll=4)
    lax.fori_loop(0, ncl, lambda t, c: step(ncc + t, nc - 1 - t, c), carry, unroll=4)


def _ssm_p3_kernel(v_ref, xfr_ref, xfi_ref, xbr_ref, xbi_ref, t_ref, wo_ref, zl_ref, zc_ref):
    ncc, ncl = zc_ref.shape[0] // SSM_CHUNK, zl_ref.shape[0] // SSM_CHUNK
    nc = ncc + ncl
    ys = []
    for j in range(SSM_LANE_GROUPS):
        rows = slice(j * nc, (j + 1) * nc)
        xin = jnp.concatenate([xfr_ref[rows, :], xfi_ref[rows, :], xbr_ref[rows, :], xbi_ref[rows, :]],
                              axis=1).astype(BF16)
        y = _dot(v_ref[j], t_ref[j]) + _dot(xin, wo_ref[j])
        ys.append(jax.nn.gelu(y, approximate=True))
    for t in range(SSM_CHUNK):
        zt = jnp.concatenate([y[:, t * SSM_GROUP:(t + 1) * SSM_GROUP] for y in ys], axis=1)
        zc_ref[pl.ds(t, ncc, stride=SSM_CHUNK), :] = zt[:ncc]
        zl_ref[pl.ds(t, ncl, stride=SSM_CHUNK), :] = zt[ncc:]


def _ssm_mixer(u4, layer, w1, abar, tmat, wout, batch, seq, ctx_len):
    n_lt = u4.shape[0]
    g = N_SSM_GROUPS
    lg = SSM_LANE_GROUPS
    ncc, ncl = ctx_len // SSM_CHUNK, seq // SSM_CHUNK
    nc = ncc + ncl
    cw = SSM_CW
    sw = w1.shape[-1]
    pw = sw // 4
    ctx_blk0 = (batch * seq) // ctx_len
    plane_shape = [jax.ShapeDtypeStruct((g * nc, pw), F32)] * 4
    lat_spec = pl.BlockSpec((None, seq, LANES), lambda k, b: (k, b, 0))
    ctx_spec = pl.BlockSpec((None, ctx_len, LANES), lambda k, b: (k, ctx_blk0 + b, 0))
    v_spec = pl.BlockSpec((lg, None, nc, cw), lambda k, b: (k, b, 0, 0))
    plane_spec = pl.BlockSpec((lg * nc, pw), lambda k, b: (k, 0))
    v_chunks, *s_planes = pl.pallas_call(
        _ssm_p1_kernel,
        grid=(n_lt, batch),
        in_specs=[lat_spec, ctx_spec,
                  pl.BlockSpec((None, lg, None, cw, sw), lambda k, b: (layer, k, b, 0, 0))],
        out_specs=[v_spec] + [plane_spec] * 4,
        out_shape=[jax.ShapeDtypeStruct((g, batch, nc, cw), BF16)] + plane_shape,
        compiler_params=_cparams(("parallel", "arbitrary")),
        name="ssm_chunk_states",
    )(u4, u4, w1)
    gb = SSM_SCAN_GROUPS
    x_planes = pl.pallas_call(
        functools.partial(_ssm_scan_kernel, ncc, ncl),
        grid=(g // gb,),
        in_specs=[pl.BlockSpec((None, 4, gb, pw), lambda i: (layer, 0, i, 0))]
        + [pl.BlockSpec((gb * nc, pw), lambda i: (i, 0))] * 4,
        out_specs=[pl.BlockSpec((gb * nc, pw), lambda i: (i, 0))] * 4,
        out_shape=plane_shape,
        compiler_params=_cparams(("parallel",)),
        name="ssm_chunk_scan",
    )(abar, *s_planes)
    return pl.pallas_call(
        _ssm_p3_kernel,
        grid=(n_lt, batch),
        in_specs=[v_spec] + [plane_spec] * 4
        + [pl.BlockSpec((None, lg, cw, cw), lambda k, b: (layer, k, 0, 0)),
           pl.BlockSpec((None, lg, None, sw, cw), lambda k, b: (layer, k, b, 0, 0))],
        out_specs=[pl.BlockSpec((None, seq, LANES), lambda k, b: (k, b, 0)),
                   pl.BlockSpec((None, ctx_len, LANES), lambda k, b: (k, b, 0))],
        out_shape=[jax.ShapeDtypeStruct((n_lt, batch * seq, LANES), F32),
                   jax.ShapeDtypeStruct((n_lt, batch * ctx_len, LANES), F32)],
        compiler_params=_cparams(("parallel", "arbitrary")),
        name="ssm_chunk_outputs",
    )(v_chunks, *x_planes, tmat, wout)


def _merge_kernel(n_lat_tiles, *refs):
    if n_lat_tiles is None:
        attn_ref, z_ref, x_ref = refs[:3]
        refs = refs[3:]
    else:
        attn_ref, z_ref, x_ref, attn_c_ref, z_c_ref, x_c_ref = refs[:6]
        refs = refs[6:]
        is_ctx = pl.program_id(0) >= n_lat_tiles
    (ga_ref, gs_ref, mod_ref, modl_ref, g2_ref, wglu_ref, wba_ref, wbs_ref, wout_ref, wrh_ref, wrl_ref,
     xo_ref, h2_ref, rt_ref, cnt_ref, x_s) = refs
    step = pl.program_id(0)
    m = mod_ref[0]
    ml = modl_ref[0]
    tile_counts = None

    @pl.when(step == 0)
    def _():
        x_s[...] = jnp.zeros_like(x_s)

    for r in range(MERGE_TILE // MERGE_SUB):
        rows = slice(r * MERGE_SUB, (r + 1) * MERGE_SUB)
        h2 = _rms_mod(x_s[rows, :], g2_ref[...], ml[3:4], ml[4:5])
        hi, lo = _split_bf16(h2)
        h2_ref[rows, :] = hi.astype(F32)
        logits = _dot(hi, wrh_ref[...]) + _dot(lo, wrh_ref[...]) + _dot(hi, wrl_ref[...])
        rt = _route(logits)
        rt_ref[rows, :] = rt
        oh0, oh1 = _expert_onehots(rt)
        part = jnp.sum(oh0 + oh1, axis=0, keepdims=True)
        tile_counts = part if tile_counts is None else tile_counts + part

        zf = jnp.concatenate([z_ref[j, rows, :] for j in range(SSM_LANE_TILES)], axis=1)
        attn = attn_ref[rows, :]
        x_in = x_ref[rows, :]
        if n_lat_tiles is not None:
            zf = jnp.where(is_ctx, jnp.concatenate([z_c_ref[j, rows, :] for j in range(SSM_LANE_TILES)], axis=1), zf)
            attn = jnp.where(is_ctx, attn_c_ref[rows, :], attn)
            x_in = jnp.where(is_ctx, x_c_ref[rows, :], x_in)
        z = zf.astype(BF16)
        glu = (z.astype(F32) * jax.nn.sigmoid(_dot(z, wglu_ref[...]))).astype(BF16)
        mix = (ga_ref[rows, :].astype(F32) * _dot(attn, wba_ref[...])
               + gs_ref[rows, :].astype(F32) * _dot(glu, wbs_ref[...])).astype(BF16)
        x = x_in + m[2:3] * _dot(mix, wout_ref[...])
        xo_ref[rows, :] = x
        x_s[rows, :] = x

    @pl.when(step == 0)
    def _():
        cnt_ref[...] = jnp.zeros_like(cnt_ref)

    cnt_ref[0:1, :] = cnt_ref[0:1, :] + jnp.where(step > 0, tile_counts, 0.0)


def _route(lg):
    ng, epg = N_EXPERT_GROUPS, EXPERTS_PER_GROUP
    lane_i = lax.broadcasted_iota(jnp.int32, lg.shape, 1)
    lane = lane_i.astype(F32)
    big = float(ROUTER_PAD)

    def rmax(mask_val):
        return jnp.max(mask_val, axis=-1, keepdims=True)

    def first_lane(mask, val, mx):
        return jnp.min(jnp.where(mask, jnp.where(val == mx, lane, big), big), axis=-1, keepdims=True)

    gmask = lane_i < ng
    lgm = jnp.where(gmask, lg, NEG_INF)
    mg = rmax(lgm)
    g_prob = 1.0 / jnp.sum(jnp.exp(lgm - mg), axis=-1, keepdims=True)
    g_idx = first_lane(gmask, lg, mg)
    egroup = jnp.floor((lane - float(ng)) * (1.0 / epg))
    emask = egroup == g_idx
    l1 = jnp.where(emask, lg, NEG_INF)
    m1 = rmax(l1)
    i1 = first_lane(emask, lg, m1)
    l2 = jnp.where(lane == i1, NEG_INF, l1)
    m2 = rmax(l2)
    i2 = jnp.min(jnp.where(l2 == m2, jnp.where(emask, lane, big), big), axis=-1, keepdims=True)
    r = jnp.exp(m2 - m1)
    w1 = g_prob / (1.0 + r)
    w2 = w1 * r
    return jnp.where(lane_i == 0, i1 - float(ng),
                     jnp.where(lane_i == 1, i2 - float(ng),
                               jnp.where(lane_i == 2, w1, jnp.where(lane_i == 3, w2, 0.0))))


def _merge(attn, z, x, ctx_parts, ga, gs, mod, g2, w_glu, w_ba, w_bs, w_out, wr_hi, wr_lo, n_rows, t_lat, seq, batch):
    d = D_MODEL
    tile = MERGE_TILE
    assert n_rows % tile == 0 and t_lat % tile == 0 and seq % tile == 0
    n_tiles = n_rows // tile
    n_lat = t_lat // tile
    cur = lambda i: jnp.minimum(i, n_tiles - 1)
    prev = lambda i: jnp.maximum(i - 1, 0)
    row = lambda i: (cur(i), 0)
    row_lag = lambda i: (prev(i), 0)
    lat = lambda i: (jnp.minimum(i, n_lat - 1), 0)
    cxt = lambda i: (jnp.maximum(cur(i) - n_lat, 0), 0)
    const = lambda i: (0, 0)
    mod_of = lambda t: (jnp.minimum(t // (seq // tile), batch), 0, 0)
    modi = lambda i: mod_of(cur(i))
    modi_lag = lambda i: mod_of(prev(i))
    in_specs = [pl.BlockSpec((tile, Q_W), lat),
                pl.BlockSpec((SSM_LANE_TILES, tile, LANES), lambda i: (0,) + lat(i)),
                pl.BlockSpec((tile, d), lat)]
    args = [attn, z, x]
    if ctx_parts is not None:
        in_specs += [pl.BlockSpec((tile, Q_W), cxt),
                     pl.BlockSpec((SSM_LANE_TILES, tile, LANES), lambda i: (0,) + cxt(i)),
                     pl.BlockSpec((tile, d), cxt)]
        args += list(ctx_parts)
    in_specs += [
        pl.BlockSpec((tile, d), row),
        pl.BlockSpec((tile, d), row),
        pl.BlockSpec((1, 6, d), modi),
        pl.BlockSpec((1, 6, d), modi_lag),
        pl.BlockSpec((1, d), const),
        pl.BlockSpec((D_SSM, D_SSM), const),
        pl.BlockSpec((Q_W, d), const),
        pl.BlockSpec((D_SSM, d), const),
        pl.BlockSpec((d, d), const),
        pl.BlockSpec((d, ROUTER_PAD), const),
        pl.BlockSpec((d, ROUTER_PAD), const),
    ]
    args += [ga, gs, mod, mod, g2, w_glu, w_ba, w_bs, w_out, wr_hi, wr_lo]
    return pl.pallas_call(
        functools.partial(_merge_kernel, n_lat if ctx_parts is not None else None),
        grid=(n_tiles + 1,),
        in_specs=in_specs,
        out_specs=[pl.BlockSpec((tile, d), row), pl.BlockSpec((tile, d), row_lag),
                   pl.BlockSpec((tile, ROUTER_PAD), row_lag), pl.BlockSpec((8, ROUTER_PAD), const)],
        out_shape=[jax.ShapeDtypeStruct((n_rows, d), F32), jax.ShapeDtypeStruct((n_rows, d), F32),
                   jax.ShapeDtypeStruct((n_rows, ROUTER_PAD), F32), jax.ShapeDtypeStruct((8, ROUTER_PAD), F32)],
        scratch_shapes=[pltpu.VMEM((tile, d), F32)],
        compiler_params=_cparams(("arbitrary",)),
        name="merge_router",
    )(*args)


def _expert_kernel(layer, be_ref, nu_ref, nx_ref, x_ref, wg_hbm, wu_hbm, wd_hbm, y_ref,
                   wg_f, wu_f, wd_f, wg_s, wu_s, wd_s, slot_s, sems):
    i = pl.program_id(0)
    e = be_ref[i]
    changed = jnp.logical_or(i == 0, e != be_ref[jnp.maximum(i - 1, 0)])

    def weight_copies(expert, slot):
        return [pltpu.make_async_copy(src.at[layer, expert], dst.at[slot], sems.at[slot, j])
                for j, (src, dst) in enumerate(((wg_hbm, wg_f), (wu_hbm, wu_f), (wd_hbm, wd_f)))]

    @pl.when(i == 0)
    def _():
        slot_s[0] = 0
        for cp in weight_copies(e, 0):
            cp.start()

    @pl.when(changed)
    def _():
        slot = slot_s[0]
        for cp in weight_copies(e, slot):
            cp.wait()
        nxt = nx_ref[e]

        @pl.when(nxt >= 0)
        def _():
            for cp in weight_copies(nxt, 1 - slot):
                cp.start()

        wg_s[...] = wg_f[slot].astype(BF16)
        wu_s[...] = wu_f[slot].astype(BF16)
        wd_s[...] = wd_f[slot].astype(BF16)
        slot_s[0] = 1 - slot

    @pl.when(i < nu_ref[0])
    def _():
        x = jnp.concatenate([x_ref[pl.ds(j, MOE_BLOCK, stride=ROW_SUBLANES), :] for j in range(ROW_SUBLANES)],
                            axis=1).astype(BF16)
        gate = _dot(x, wg_s[...])
        up = _dot(x, wu_s[...])
        hid = (gate * jax.nn.sigmoid(gate) * up).astype(BF16)
        y_ref[...] = _dot(hid, wd_s[...]).astype(y_ref.dtype)

    @pl.when(i >= nu_ref[0])
    def _():
        y_ref[...] = jnp.zeros_like(y_ref)


def _experts(blk_exp, n_used, next_exp, xs, layer, w_gate, w_up, w_down):
    d, de = w_gate.shape[-2:]
    assert d == ROW_SUBLANES * LANES
    n_slots = xs.shape[0] // ROW_SUBLANES
    n_blk = n_slots // MOE_BLOCK
    return pl.pallas_call(
        functools.partial(_expert_kernel, layer),
        grid_spec=pltpu.PrefetchScalarGridSpec(
            num_scalar_prefetch=3,
            grid=(n_blk,),
            in_specs=[
                pl.BlockSpec((MOE_BLOCK * ROW_SUBLANES, LANES), lambda i, be, nu, nx: (jnp.minimum(i, nu[0] - 1), 0)),
                pl.BlockSpec(memory_space=pl.ANY),
                pl.BlockSpec(memory_space=pl.ANY),
                pl.BlockSpec(memory_space=pl.ANY),
            ],
            out_specs=pl.BlockSpec((MOE_BLOCK, d), lambda i, be, nu, nx: (i, 0)),
            scratch_shapes=[pltpu.VMEM((2, d, de), F32), pltpu.VMEM((2, d, de), F32), pltpu.VMEM((2, de, d), F32),
                            pltpu.VMEM((d, de), BF16), pltpu.VMEM((d, de), BF16), pltpu.VMEM((de, d), BF16),
                            pltpu.SMEM((1,), jnp.int32), pltpu.SemaphoreType.DMA((2, 3))],
        ),
        out_shape=jax.ShapeDtypeStruct((n_slots, d), BF16),
        compiler_params=_cparams(("arbitrary",)),
        name="expert_mlp",
    )(blk_exp, n_used, next_exp, xs, w_gate, w_up, w_down)


def _expert_onehots(rt):
    lane_f = lax.broadcasted_iota(jnp.int32, rt.shape, 1).astype(F32)
    return jnp.where(lane_f == rt[:, 0:1], 1.0, 0.0), jnp.where(lane_f == rt[:, 1:2], 1.0, 0.0)


def _plan_kernel(rt_ref, cnt_ref, dest_ref, pend_ref, run_s, before_s):
    i = pl.program_id(0)
    rows = rt_ref.shape[0]
    lane = lax.broadcasted_iota(jnp.int32, (rows, ROUTER_PAD), 1)
    oh0, oh1 = _expert_onehots(rt_ref[...])
    tot0 = jnp.sum(oh0, axis=0, keepdims=True)
    tot1 = jnp.sum(oh1, axis=0, keepdims=True)

    @pl.when(i == 0)
    def _():
        counts = cnt_ref[...]
        padded = jnp.floor((counts + float(MOE_BLOCK - 1)) * (1.0 / MOE_BLOCK)) * float(MOE_BLOCK)
        r = lax.broadcasted_iota(jnp.int32, (ROUTER_PAD, ROUTER_PAD), 0)
        c = lax.broadcasted_iota(jnp.int32, (ROUTER_PAD, ROUTER_PAD), 1)
        incl = jnp.where(r <= c, 1.0, 0.0).astype(BF16)
        hi, lo = _split_bf16(padded)
        pend = _dot(hi, incl) + _dot(lo, incl)
        pend_ref[...] = pend
        run_s[...] = pend - padded
        r = lax.broadcasted_iota(jnp.int32, (rows, rows), 0)
        c = lax.broadcasted_iota(jnp.int32, (rows, rows), 1)
        before_s[...] = jnp.where(c < r, 1.0, 0.0).astype(BF16)

    prior = _dot(before_s[...], jnp.concatenate([oh0, oh1], axis=1).astype(BF16))
    run = run_s[0:1, :]
    pos0 = jnp.sum(oh0 * (run + prior[:, :ROUTER_PAD]), axis=-1, keepdims=True)
    pos1 = jnp.sum(oh1 * (run + tot0 + prior[:, ROUTER_PAD:]), axis=-1, keepdims=True)
    run_s[0:1, :] = run + tot0 + tot1
    dest_ref[...] = jnp.where(lane == 0, pos0, jnp.where(lane == 1, pos1, 0.0)).astype(jnp.int32)


def _plan(route, counts, n_rows):
    rows = PLAN_TILE
    return pl.pallas_call(
        _plan_kernel,
        grid=(n_rows // rows,),
        in_specs=[pl.BlockSpec((rows, ROUTER_PAD), lambda i: (i, 0)),
                  pl.BlockSpec((8, ROUTER_PAD), lambda i: (0, 0))],
        out_specs=[pl.BlockSpec((rows, ROUTER_PAD), lambda i: (i, 0)),
                   pl.BlockSpec((8, ROUTER_PAD), lambda i: (0, 0))],
        out_shape=[jax.ShapeDtypeStruct((n_rows, ROUTER_PAD), jnp.int32),
                   jax.ShapeDtypeStruct((8, ROUTER_PAD), F32)],
        scratch_shapes=[pltpu.VMEM((8, ROUTER_PAD), F32), pltpu.VMEM((rows, rows), BF16)],
        compiler_params=_cparams(("arbitrary",)),
        name="dispatch_plan",
    )(route, counts)


def _dispatch_kernel(pend_ref, dest_ref, h_ref, xs_ref, hs, zero_s, sems, zsem):
    step = pl.program_id(0)
    rows = h_ref.shape[0] // 2
    sub = ROW_SUBLANES
    n_blk = xs_ref.shape[0] // (MOE_BLOCK * sub)
    n_used = pend_ref[N_EXPERTS - 1] // MOE_BLOCK

    def zero_block(start):
        return pltpu.make_async_copy(
            zero_s, xs_ref.at[pl.ds(pl.multiple_of(start * sub, MOE_BLOCK * sub), MOE_BLOCK * sub)], zsem)

    @pl.when(pl.program_id(0) == 0)
    def _():
        zero_s[...] = jnp.zeros_like(zero_s)
        for e in range(N_EXPERTS):
            @pl.when(pend_ref[e] > 0)
            def _():
                zero_block(pend_ref[e] - MOE_BLOCK).start()
        lax.fori_loop(n_used, n_blk, lambda j, c: (zero_block(j * MOE_BLOCK).start(), c)[1], 0)
        for e in range(N_EXPERTS):
            @pl.when(pend_ref[e] > 0)
            def _():
                zero_block(pend_ref[e] - MOE_BLOCK).wait()
        lax.fori_loop(n_used, n_blk, lambda j, c: (zero_block(j * MOE_BLOCK).wait(), c)[1], 0)

    def row_copy(half, r, d):
        return pltpu.make_async_copy(hs.at[half, pl.ds(pl.multiple_of(r * sub, sub), sub)],
                                     xs_ref.at[pl.ds(pl.multiple_of(d * sub, sub), sub)], sems.at[half])

    def drain(half):
        for _ in range(TOP_K * rows):
            row_copy(half, 0, 0).wait()

    for half in range(2):
        @pl.when(step > 0)
        def _():
            drain(half)

        for j in range(sub):
            hs[half, pl.ds(j, rows, stride=sub), :] = h_ref[half * rows:(half + 1) * rows, LANES * j:LANES * (j + 1)]

        def issue(r, carry):
            for k in range(TOP_K):
                row_copy(half, r, dest_ref[0, 0, (half * TOP_K + k) * rows + r]).start(priority=k % 2)
            return carry

        lax.fori_loop(0, rows, issue, 0, unroll=8)

    @pl.when(step == pl.num_programs(0) - 1)
    def _():
        drain(0)
        drain(1)


def _dispatch(pend, dest_tiles, h2, n_slots, n_tiles):
    d = h2.shape[1]
    assert n_tiles % 2 == 0
    return pl.pallas_call(
        _dispatch_kernel,
        grid_spec=pltpu.PrefetchScalarGridSpec(
            num_scalar_prefetch=1,
            grid=(n_tiles // 2,),
            in_specs=[pl.BlockSpec((1, 1, 2 * TOP_K * ROW_TILE), lambda i, pe: (i, 0, 0), memory_space=pltpu.SMEM),
                      pl.BlockSpec((2 * ROW_TILE, d), lambda i, pe: (i, 0))],
            out_specs=pl.BlockSpec(memory_space=pl.ANY),
            scratch_shapes=[pltpu.VMEM((2, ROW_TILE * ROW_SUBLANES, LANES), F32),
                            pltpu.VMEM((MOE_BLOCK * ROW_SUBLANES, LANES), F32),
                            pltpu.SemaphoreType.DMA((2,)), pltpu.SemaphoreType.DMA],
        ),
        out_shape=jax.ShapeDtypeStruct((n_slots * ROW_SUBLANES, LANES), F32),
        compiler_params=_cparams(("arbitrary",)),
        name="dispatch_rows",
    )(pend, dest_tiles.reshape(n_tiles // 2, 1, 2 * TOP_K * ROW_TILE), h2)


def _moe(h2, route, counts, n_tiles, layer, w_gate, w_up, w_down):
    t, d = h2.shape
    dest, pend_f = _plan(route, counts, t)
    pend = pend_f[0, :N_EXPERTS].astype(jnp.int32)
    n_blk = -(-(t * TOP_K) // MOE_BLOCK) + N_EXPERTS
    blk_start = jnp.arange(n_blk, dtype=jnp.int32) * MOE_BLOCK
    blk_exp = jnp.minimum(jnp.sum((pend[None, :] <= blk_start[:, None]).astype(jnp.int32), axis=1), N_EXPERTS - 1)
    n_used = (pend[N_EXPERTS - 1] // MOE_BLOCK).reshape(1)
    blk_exp = jnp.where(jnp.arange(n_blk) < n_used[0], blk_exp, blk_exp[n_used[0] - 1])
    has_rows = pend > jnp.concatenate([jnp.zeros((1,), jnp.int32), pend[:-1]])
    e_ids = jnp.arange(N_EXPERTS, dtype=jnp.int32)
    later = jnp.where(jnp.logical_and(has_rows[None, :], e_ids[None, :] > e_ids[:, None]), e_ids[None, :], N_EXPERTS)
    next_exp = jnp.min(later, axis=1)
    next_exp = jnp.where(next_exp < N_EXPERTS, next_exp, -1).astype(jnp.int32)
    dest2 = dest[:, :TOP_K]
    dest_tiles = dest2.reshape(n_tiles, ROW_TILE, TOP_K).transpose(0, 2, 1).reshape(n_tiles, 1, TOP_K * ROW_TILE)
    xs = _dispatch(pend, dest_tiles, h2, n_blk * MOE_BLOCK, n_tiles)
    ys = _experts(blk_exp.astype(jnp.int32), n_used, next_exp, xs, layer, w_gate, w_up, w_down)
    return ys[dest2[:, 0]], ys[dest2[:, 1]]


def _final_kernel(x_ref, y0_ref, y1_ref, rt_ref, mod_ref, g_ref, o_ref):
    x = _moe_residual(x_ref, y0_ref, y1_ref, rt_ref, mod_ref)
    o_ref[...] = x * lax.rsqrt(jnp.mean(x * x, axis=-1, keepdims=True) + EPS) * g_ref[...]


def _final(x_lat, moe_out, mod, g_final, n_rows, seq, batch):
    d = D_MODEL
    tile = MERGE_TILE
    assert n_rows % tile == 0 and seq % tile == 0
    row = lambda i: (i, 0)
    return pl.pallas_call(
        _final_kernel,
        grid=(n_rows // tile,),
        in_specs=[pl.BlockSpec((tile, d), row), pl.BlockSpec((tile, d), row),
                  pl.BlockSpec((tile, d), row), pl.BlockSpec((tile, ROUTER_PAD), row),
                  pl.BlockSpec((1, 6, d), lambda i: (jnp.minimum(i // (seq // tile), batch), 0, 0)),
                  pl.BlockSpec((1, d), lambda i: (0, 0))],
        out_specs=pl.BlockSpec((tile, d), row),
        out_shape=jax.ShapeDtypeStruct((n_rows, d), F32),
        compiler_params=_cparams(("parallel",)),
        name="final_norm",
    )(x_lat, *moe_out, mod, g_final)


def _rope_tables(seq):
    quarter = HEAD_DIM // 4
    freqs = ROPE_THETA ** (-jnp.arange(quarter, dtype=F32) / quarter)
    pos = jnp.arange(seq)
    ang_r = (pos // GRID_W).astype(F32)[:, None] * freqs[None, :]
    ang_c = (pos % GRID_W).astype(F32)[:, None] * freqs[None, :]
    cos = jnp.concatenate([jnp.cos(ang_r)] * 2 + [jnp.cos(ang_c)] * 2, axis=-1)
    sin = jnp.concatenate([-jnp.sin(ang_r), jnp.sin(ang_r), -jnp.sin(ang_c), jnp.sin(ang_c)], axis=-1)
    reps = LANES // HEAD_DIM
    cos = jnp.tile(cos, (1, reps))
    sin = jnp.tile(sin, (1, reps))
    cos = jnp.concatenate([cos, jnp.ones((ROW_TILE, LANES), F32)], axis=0)
    sin = jnp.concatenate([sin, jnp.zeros((ROW_TILE, LANES), F32)], axis=0)
    return cos, sin


def kernel(x, c, ctx, c_ctx, w_mod, b_mod, g_norm1, g_norm2, w_in, attn_sink, ssm_lam_re, ssm_lam_im,
           ssm_log_dt, ssm_b_re, ssm_b_im, ssm_c_re, ssm_c_im, ssm_d, w_glu, w_br_attn, w_br_ssm, w_out,
           w_router_group, w_router_expert, w_exp_gate, w_exp_up, w_exp_down, g_final):
    batch, seq, d = x.shape
    ctx_len = ctx.shape[1]
    depth = w_mod.shape[0]
    assert d == D_MODEL and batch * SSM_STATE == LANES and depth == 2
    assert seq % ROW_TILE == 0 and ctx_len % ROW_TILE == 0 and (batch * seq) % ctx_len == 0
    assert batch + 1 <= MOD_ROWS
    t_lat, t_ctx = batch * seq, batch * ctx_len
    tiles_per_seq = seq // ROW_TILE
    n_lat_tiles = t_lat // ROW_TILE
    n_all_tiles = (t_lat + t_ctx) // ROW_TILE
    nb_per_seq = seq // ATTN_BLOCK
    ncc, ncl = ctx_len // SSM_CHUNK, seq // SSM_CHUNK

    c_rows = jnp.zeros((MOD_ROWS, d), F32).at[:batch].set(c).at[batch].set(c_ctx)
    mod_all = _modulation(c_rows, w_mod, b_mod).reshape(depth, MOD_ROWS, 6, d)
    cos_t, sin_t = _rope_tables(seq)
    ssm_w = _ssm_weights_all(ssm_lam_re, ssm_lam_im, ssm_log_dt, ssm_b_re, ssm_b_im, ssm_c_re, ssm_c_im, ssm_d, batch)
    x_parts = (x.reshape(t_lat, d), ctx.reshape(t_ctx, d))
    f_all = None
    for l in range(depth):
        ctx_out = l < depth - 1
        mod = mod_all[l]
        x_parts, (q, k, v, u, ga, gs) = _inproj(
            x_parts, f_all, mod_all[l - 1] if l > 0 else None, mod, g_norm1[l].reshape(1, d), cos_t, sin_t,
            w_in[l].astype(BF16), n_all_tiles, tiles_per_seq, n_lat_tiles, batch)
        sink = attn_sink[l].astype(F32)
        attn = _attention(sink, q, k, v, t_lat // ATTN_BLOCK, nb_per_seq, t_lat // ctx_len, ctx_len, True, 0)
        z, z_ctx = _ssm_mixer(u, l, *ssm_w, batch, seq, ctx_len)
        ctx_parts = None
        if ctx_out:
            attn_c = _attention(sink, q, k, v, t_ctx // ATTN_BLOCK, ctx_len // ATTN_BLOCK, t_lat // ctx_len,
                                ctx_len, False, t_lat // ATTN_BLOCK)
            ctx_parts = (attn_c, z_ctx, x_parts[1])
        w_r = jnp.zeros((d, ROUTER_PAD), F32)
        w_r = w_r.at[:, :N_EXPERT_GROUPS].set(w_router_group[l])
        w_r = w_r.at[:, N_EXPERT_GROUPS:N_EXPERT_GROUPS + N_EXPERTS].set(w_router_expert[l])
        wr_hi, wr_lo = _split_bf16(w_r)
        n_rows = t_lat + t_ctx if ctx_out else t_lat
        x_all, h2, route, counts = _merge(
            attn, z, x_parts[0], ctx_parts, ga, gs, mod, g_norm2[l].reshape(1, d), w_glu[l].astype(BF16),
            w_br_attn[l].astype(BF16), w_br_ssm[l].astype(BF16), w_out[l].astype(BF16),
            wr_hi, wr_lo, n_rows, t_lat, seq, batch)
        y0, y1 = _moe(h2, route, counts, n_rows // ROW_TILE, l, w_exp_gate, w_exp_up, w_exp_down)
        f_all = (y0, y1, route)
        x_parts = (x_all,)
    out = _final(x_all, f_all, mod_all[depth - 1], g_final.reshape(1, d), t_lat, seq, batch)
    return out.reshape(batch, seq, d)
```

```python
import functools
import math

import jax
import jax.numpy as jnp
from jax import lax
from jax.experimental import pallas as pl
from jax.experimental.pallas import tpu as pltpu

F32 = jnp.float32
BF16 = jnp.bfloat16
LANES = 128

D_MODEL = 1024
GRID_W = 64
N_HEADS = 8
N_KV_HEADS = 2
HEAD_DIM = 64
Q_GROUP = N_HEADS // N_KV_HEADS
ATTN_BLOCK = 128
ATTN_STEP_BLOCKS = 4
ROPE_THETA = 10000.0
D_SSM = D_MODEL // 2
SSM_GROUP = 16
N_SSM_GROUPS = D_SSM // SSM_GROUP
SSM_STATE = 64
N_EXPERT_GROUPS = 4
EXPERTS_PER_GROUP = 8
N_EXPERTS = N_EXPERT_GROUPS * EXPERTS_PER_GROUP
TOP_K = 2
Q_W = N_HEADS * HEAD_DIM
KV_W = N_KV_HEADS * HEAD_DIM
O_K = Q_W
O_V = O_K + KV_W
O_U = O_V + KV_W
O_GA = O_U + D_SSM
O_GS = O_GA + D_MODEL
D_IN = O_GS + D_MODEL
EPS = 1e-6
NEG_INF = -1e30

ROW_TILE = 256
MERGE_TILE = 512
MERGE_SUB = 256
SSM_CHUNK = 16
SSM_CW = SSM_CHUNK * SSM_GROUP
SSM_SCAN_GROUPS = 8
SSM_LANE_GROUPS = LANES // SSM_GROUP
SSM_LANE_TILES = D_SSM // LANES
MOE_BLOCK = 256
ROW_SUBLANES = D_MODEL // LANES
PLAN_TILE = 512
ROUTER_PAD = LANES
MOD_ROWS = 8
VMEM_LIMIT = 48 * 1024 * 1024


def _cparams(sem):
    return pltpu.CompilerParams(dimension_semantics=sem, vmem_limit_bytes=VMEM_LIMIT)


def _dot(a, b):
    return jnp.dot(a, b, preferred_element_type=F32)


def _split_bf16(a):
    hi = a.astype(BF16)
    lo = (a - hi.astype(F32)).astype(BF16)
    return hi, lo


def _rms_mod(x, g, shift, scale):
    y = x * lax.rsqrt(jnp.mean(x * x, axis=-1, keepdims=True) + EPS) * g
    return y * (1.0 + scale) + shift


def _mod_kernel(c_ref, w_ref, b_ref, o_ref):
    c = c_ref[...]
    s_hi, s_lo = _split_bf16(c * jax.nn.sigmoid(c))
    w_hi, w_lo = _split_bf16(w_ref[0])
    o_ref[0] = _dot(s_hi, w_hi) + _dot(s_lo, w_hi) + _dot(s_hi, w_lo) + b_ref[0]


def _modulation(c_rows, w_mod, b_mod):
    depth, d, n = w_mod.shape
    nb = n // 4
    return pl.pallas_call(
        _mod_kernel,
        grid=(depth, n // nb),
        in_specs=[
            pl.BlockSpec((MOD_ROWS, d), lambda l, j: (0, 0)),
            pl.BlockSpec((1, d, nb), lambda l, j: (l, 0, j)),
            pl.BlockSpec((1, 1, nb), lambda l, j: (l, 0, j)),
        ],
        out_specs=pl.BlockSpec((1, MOD_ROWS, nb), lambda l, j: (l, 0, j)),
        out_shape=jax.ShapeDtypeStruct((depth, MOD_ROWS, n), F32),
        compiler_params=_cparams(("arbitrary", "arbitrary")),
        name="modulation",
    )(c_rows, w_mod, b_mod.reshape(depth, 1, n))


def _moe_residual(x_ref, y0_ref, y1_ref, rt_ref, mod_ref):
    rt = rt_ref[...]
    f = rt[:, 2:3] * y0_ref[...].astype(F32) + rt[:, 3:4] * y1_ref[...].astype(F32)
    return x_ref[...] + mod_ref[0, 5:6, :] * f


def _inproj_kernel(has_f, n_lat_tiles, *refs):
    if has_f:
        (x_ref, y0_ref, y1_ref, rt_ref, modp_ref, mod_ref, g_ref, cos_ref, sin_ref, w_ref,
         xo_ref, q_ref, k_ref, v_ref, u_ref, ga_ref, gs_ref) = refs
        x = _moe_residual(x_ref, y0_ref, y1_ref, rt_ref, modp_ref)
        xo_ref[...] = x
    else:
        (xl_ref, xc_ref, mod_ref, g_ref, cos_ref, sin_ref, w_ref,
         q_ref, k_ref, v_ref, u_ref, ga_ref, gs_ref) = refs
        x = jnp.where(pl.program_id(0) >= n_lat_tiles, xc_ref[...], xl_ref[...])
    m = mod_ref[0]
    h = _rms_mod(x, g_ref[...], m[0:1], m[1:2]).astype(BF16)
    cos = cos_ref[...]
    sin = sin_ref[...]
    lane = lax.broadcasted_iota(jnp.int32, cos.shape, 1)
    first = (lane % (HEAD_DIM // 2)) < (HEAD_DIM // 4)

    def rope(t):
        sw = jnp.where(first, pltpu.roll(t, LANES - HEAD_DIM // 4, 1), pltpu.roll(t, HEAD_DIM // 4, 1))
        return t * cos + sw * sin

    def proj(lo, hi):
        return _dot(h, w_ref[:, lo:hi])

    q = proj(0, O_K)
    for j in range(Q_W // LANES):
        q_ref[:, LANES * j:LANES * (j + 1)] = (rope(q[:, LANES * j:LANES * (j + 1)]) * HEAD_DIM ** -0.5).astype(BF16)
    kv = proj(O_K, O_U)
    k_ref[...] = rope(kv[:, :KV_W]).astype(BF16)
    v_ref[...] = kv[:, KV_W:].astype(BF16)
    u = proj(O_U, O_GA)
    for j in range(SSM_LANE_TILES):
        u_ref[j] = u[:, LANES * j:LANES * (j + 1)]
    ga_ref[...] = jax.nn.sigmoid(proj(O_GA, O_GS)).astype(BF16)
    gs_ref[...] = jax.nn.sigmoid(proj(O_GS, D_IN)).astype(BF16)


def _inproj(x_parts, moe_out, mod_prev, mod, g1, cos_t, sin_t, w_in, n_tiles, tiles_per_seq, n_lat_tiles, batch):
    d = D_MODEL
    has_f = moe_out is not None
    row = lambda i: (i, 0)
    modi = lambda i: (jnp.minimum(i // tiles_per_seq, batch), 0, 0)
    const = lambda i: (0, 0)
    ropei = lambda i: (jnp.where(i < n_lat_tiles, i % tiles_per_seq, tiles_per_seq), 0)
    if has_f:
        in_specs = [pl.BlockSpec((ROW_TILE, d), row), pl.BlockSpec((ROW_TILE, d), row),
                    pl.BlockSpec((ROW_TILE, d), row), pl.BlockSpec((ROW_TILE, ROUTER_PAD), row),
                    pl.BlockSpec((1, 6, d), modi)]
        args = [*x_parts, *moe_out, mod_prev]
    else:
        in_specs = [pl.BlockSpec((ROW_TILE, d), lambda i: (jnp.minimum(i, n_lat_tiles - 1), 0)),
                    pl.BlockSpec((ROW_TILE, d), lambda i: (jnp.maximum(i - n_lat_tiles, 0), 0))]
        args = list(x_parts)
    in_specs += [
        pl.BlockSpec((1, 6, d), modi),
        pl.BlockSpec((1, d), const),
        pl.BlockSpec((ROW_TILE, LANES), ropei),
        pl.BlockSpec((ROW_TILE, LANES), ropei),
        pl.BlockSpec((d, D_IN), const),
    ]
    args += [mod, g1, cos_t, sin_t, w_in]
    widths = [Q_W, KV_W, KV_W, D_SSM, D_MODEL, D_MODEL]
    out_specs = [pl.BlockSpec((ROW_TILE, w), row) for w in widths]
    out_shape = [jax.ShapeDtypeStruct((n_tiles * ROW_TILE, w), BF16) for w in widths]
    out_specs[3] = pl.BlockSpec((SSM_LANE_TILES, ROW_TILE, LANES), lambda i: (0, i, 0))
    out_shape[3] = jax.ShapeDtypeStruct((SSM_LANE_TILES, n_tiles * ROW_TILE, LANES), F32)
    if has_f:
        out_specs = [pl.BlockSpec((ROW_TILE, d), row)] + out_specs
        out_shape = [jax.ShapeDtypeStruct((n_tiles * ROW_TILE, d), F32)] + out_shape
    outs = pl.pallas_call(
        functools.partial(_inproj_kernel, has_f, n_lat_tiles),
        grid=(n_tiles,),
        in_specs=in_specs,
        out_specs=out_specs,
        out_shape=out_shape,
        compiler_params=_cparams(("parallel",)),
        name="inproj",
    )(*args)
    if has_f:
        return (outs[0],), outs[1:]
    return x_parts, outs


def _attn_scores(q, k_tiles, v_tiles):
    res = []
    for g in range(N_KV_HEADS):
        gs = slice(g * HEAD_DIM, (g + 1) * HEAD_DIM)
        k_all = jnp.concatenate([t[:, gs] for t in k_tiles], axis=0)
        v_all = jnp.concatenate([t[:, gs] for t in v_tiles], axis=0)
        v_ext = jnp.concatenate([v_all, jnp.ones_like(v_all)], axis=1)
        qg = jnp.concatenate(
            [q[:, (g * Q_GROUP + h) * HEAD_DIM:(g * Q_GROUP + h + 1) * HEAD_DIM] for h in range(Q_GROUP)], axis=0)
        res.append((lax.dot_general(k_all, qg, (((1,), (1,)), ((), ())), preferred_element_type=F32), v_ext))
    return res


def _attn_finish(sink_ref, scores, tile_rows, biases, blk):
    outs = []
    for g, (st, v_ext) in enumerate(scores):
        tiles, row = [], 0
        for n, bias in zip(tile_rows, biases):
            t = st[row:row + n, :]
            tiles.append(t if bias is None else t + jnp.concatenate([bias] * Q_GROUP, axis=1))
            row += n
        mx = None
        for t in tiles:
            for r0 in range(0, t.shape[0], blk):
                mx = t[r0:r0 + blk] if mx is None else jnp.maximum(mx, t[r0:r0 + blk])
        sink = jnp.concatenate([jnp.full((1, blk), sink_ref[g * Q_GROUP + h], F32) for h in range(Q_GROUP)], axis=1)
        m = jnp.maximum(jnp.max(mx, axis=0, keepdims=True), sink)
        p = jnp.exp(jnp.concatenate([(t - m).astype(BF16) for t in tiles], axis=0))
        o_t = lax.dot_general(v_ext, p, (((0,), (0,)), ((), ())), preferred_element_type=F32)
        denom = o_t[HEAD_DIM:HEAD_DIM + 1, :] + jnp.exp(sink - m)
        o_n = o_t[:HEAD_DIM, :] * (1.0 / denom)
        for h in range(Q_GROUP):
            outs.append(o_n[:, h * blk:(h + 1) * blk].T.astype(BF16))
    return jnp.concatenate(outs, axis=1)


def _attn_kernel(band, nb_per_seq, sink_ref, *refs):
    blk = ATTN_BLOCK
    if not band:
        q_ref, kx_ref, vx_ref, o_ref = refs
        n_sub = q_ref.shape[0] // blk
        scores = [_attn_scores(q_ref[a * blk:(a + 1) * blk, :], [kx_ref[...]], [vx_ref[...]]) for a in range(n_sub)]
        for a in range(n_sub):
            o_ref[a * blk:(a + 1) * blk, :] = _attn_finish(sink_ref, scores[a], [kx_ref.shape[0]], [None], blk)
        return
    q_ref, kp_ref, kc_ref, kn_ref, vp_ref, vc_ref, vn_ref, kx_ref, vx_ref, o_ref = refs
    n_sub = q_ref.shape[0] // blk
    j0 = (pl.program_id(0) * n_sub) % nb_per_seq
    r = lax.broadcasted_iota(jnp.int32, (blk, blk), 0)
    c = lax.broadcasted_iota(jnp.int32, (blk, blk), 1)
    k_blocks = [kp_ref[...]] + [kc_ref[a * blk:(a + 1) * blk, :] for a in range(n_sub)] + [kn_ref[...]]
    v_blocks = [vp_ref[...]] + [vc_ref[a * blk:(a + 1) * blk, :] for a in range(n_sub)] + [vn_ref[...]]
    scores = [_attn_scores(q_ref[a * blk:(a + 1) * blk, :], k_blocks[a:a + 3] + [kx_ref[...]],
                           v_blocks[a:a + 3] + [vx_ref[...]]) for a in range(n_sub)]
    for a in range(n_sub):
        edge_p = jnp.where(j0 + a > 0, 0.0, NEG_INF).astype(F32)
        edge_n = jnp.where(j0 + a < nb_per_seq - 1, 0.0, NEG_INF).astype(F32)
        bias_p = jnp.where(r >= c, edge_p, NEG_INF).astype(F32)
        bias_n = jnp.where(r <= c, edge_n, NEG_INF).astype(F32)
        o_ref[a * blk:(a + 1) * blk, :] = _attn_finish(
            sink_ref, scores[a], [blk, blk, blk, kx_ref.shape[0]], [bias_p, None, bias_n, None], blk)


def _attention(sink, q, k, v, n_blocks, nb_per_seq, kx_block0, ctx_len, band, q_block0):
    blk = ATTN_BLOCK
    n_sub = math.gcd(ATTN_STEP_BLOCKS, nb_per_seq, n_blocks, q_block0)
    step = n_sub * blk
    n_steps = n_blocks // n_sub
    qi = lambda i, s: (q_block0 // n_sub + i, 0)
    cur = lambda i, s: (i, 0)
    prv = lambda i, s: (jnp.maximum(i * n_sub - 1, 0), 0)
    nxt = lambda i, s: (jnp.minimum((i + 1) * n_sub, n_blocks - 1), 0)
    kxi = lambda i, s: (kx_block0 + (i * n_sub) // nb_per_seq, 0)
    kspec = lambda f: pl.BlockSpec((blk, KV_W), f)
    cspec = pl.BlockSpec((step, KV_W), cur)
    xspec = pl.BlockSpec((ctx_len, KV_W), kxi)
    if band:
        in_specs = [pl.BlockSpec((step, Q_W), qi), kspec(prv), cspec, kspec(nxt),
                    kspec(prv), cspec, kspec(nxt), xspec, xspec]
        args = (q, k, k, k, v, v, v, k, v)
    else:
        in_specs = [pl.BlockSpec((step, Q_W), qi), xspec, xspec]
        args = (q, k, v)
    return pl.pallas_call(
        functools.partial(_attn_kernel, band, nb_per_seq),
        grid_spec=pltpu.PrefetchScalarGridSpec(
            num_scalar_prefetch=1,
            grid=(n_steps,),
            in_specs=in_specs,
            out_specs=pl.BlockSpec((step, Q_W), lambda i, s: (i, 0)),
        ),
        out_shape=jax.ShapeDtypeStruct((n_blocks * blk, Q_W), BF16),
        compiler_params=_cparams(("parallel",)),
        name="band_attention" if band else "context_attention",
    )(sink, *args)


def _dot_f32(a, b_t):
    a_hi, a_lo = _split_bf16(a)
    b_hi, b_lo = _split_bf16(b_t)
    dn = (((1,), (1,)), ((), ()))
    dg = functools.partial(lax.dot_general, dimension_numbers=dn, preferred_element_type=F32)
    return dg(a_hi, b_hi) + dg(a_lo, b_hi) + dg(a_hi, b_lo)


def _ssm_weight_kernel(batch, lam_r_ref, bt_ref, c_ref, d_ref, w1_ref, abar_ref, t_ref, wo_ref):
    lc, mm, p = SSM_CHUNK, SSM_GROUP, SSM_STATE
    up_r = lax.broadcasted_iota(jnp.int32, (lc, 1), 0).astype(F32)
    lane = lax.broadcasted_iota(jnp.int32, (mm, lc * mm), 1)
    row = lax.broadcasted_iota(jnp.int32, (mm, lc * mm), 0)
    planes_in, planes_out, abar_rows, kt = [], [], [], []
    for d in range(2):
        lr, li, dt = lam_r_ref[0, d, 0:1, :], lam_r_ref[0, d, 1:2, :], jnp.exp(lam_r_ref[0, d, 2:3, :])

        def powers(expo):
            mag = jnp.exp(lr * dt * expo)
            return mag * jnp.cos(li * dt * expo), mag * jnp.sin(li * dt * expo)

        a_re, a_im = powers(jnp.ones((1, 1), F32))
        den = lr * lr + li * li
        nr = a_re - 1.0
        f_re = (nr * lr + a_im * li) / den
        f_im = (a_im * lr - nr * li) / den
        bt_re, bt_im = bt_ref[0, d, 0], bt_ref[0, d, 1]
        bbt_re = f_re * bt_re - f_im * bt_im
        bbt_im = f_re * bt_im + f_im * bt_re
        c_re, c_im = c_ref[0, d, 0], c_ref[0, d, 1]

        pr, pi = powers(lc - 1.0 - up_r if d == 0 else up_r)
        planes_in.append(jnp.concatenate(
            [pr[s:s + 1] * bbt_re - pi[s:s + 1] * bbt_im for s in range(lc)], axis=0))
        planes_in.append(jnp.concatenate(
            [pr[s:s + 1] * bbt_im + pi[s:s + 1] * bbt_re for s in range(lc)], axis=0))

        qr, qi = powers(up_r + 1.0 if d == 0 else lc - up_r)
        planes_out.append(jnp.concatenate(
            [c_re * qr[t:t + 1] - c_im * qi[t:t + 1] for t in range(lc)], axis=0).T)
        planes_out.append(jnp.concatenate(
            [-c_re * qi[t:t + 1] - c_im * qr[t:t + 1] for t in range(lc)], axis=0).T)

        kr, ki = powers(up_r if d == 0 else lc - 1.0 - up_r)
        cp_re = jnp.concatenate([c_re * kr[j:j + 1] - c_im * ki[j:j + 1] for j in range(lc)], axis=0)
        cp_im = jnp.concatenate([c_re * ki[j:j + 1] + c_im * kr[j:j + 1] for j in range(lc)], axis=0)
        kt.append(_dot_f32(jnp.concatenate([bbt_re, bbt_im], axis=1), jnp.concatenate([cp_re, -cp_im], axis=1)))

        e_re, e_im = powers(jnp.full((1, 1), float(lc), F32))
        abar_rows += [jnp.concatenate([e_re] * batch, axis=1), jnp.concatenate([e_im] * batch, axis=1)]

    dvec = d_ref[0]
    blocks = []
    for s in range(lc):
        fwd = kt[0] if s == 0 else pltpu.roll(kt[0], mm * s, 1)
        bwd = kt[1] if s == lc - 1 else pltpu.roll(kt[1], mm * (s + 1), 1)
        blk = jnp.where(lane >= mm * s, fwd, 0.0) + jnp.where(lane < mm * (s + 1), bwd, 0.0)
        blocks.append(blk + jnp.where(lane == mm * s + row, dvec, 0.0))
    t_ref[0] = jnp.concatenate(blocks, axis=0).astype(BF16)
    abar_ref[0] = jnp.concatenate(abar_rows, axis=0)
    zero_in = jnp.zeros((lc * mm, p), F32)
    zero_out = jnp.zeros((p, lc * mm), F32)
    for b in range(batch):
        w1_ref[0, b] = jnp.concatenate(
            [pl_ if bb == b else zero_in for pl_ in planes_in for bb in range(batch)], axis=1).astype(BF16)
        wo_ref[0, b] = jnp.concatenate(
            [pl_ if bb == b else zero_out for pl_ in planes_out for bb in range(batch)], axis=0).astype(BF16)


def _ssm_weights_all(lam_re, lam_im, log_dt, b_re, b_im, c_re, c_im, d_skip, batch):
    depth = lam_re.shape[0]
    g, p, mm, lc = N_SSM_GROUPS, SSM_STATE, SSM_GROUP, SSM_CHUNK
    n = depth * g
    cw, sw = lc * mm, 4 * batch * p

    def per_group(a):
        return jnp.moveaxis(a.astype(F32), 2, 1).reshape((n, 2) + a.shape[3:])

    lam = jnp.stack([per_group(lam_re), per_group(lam_im),
                     jnp.broadcast_to(per_group(log_dt)[..., None], (n, 2, p))], axis=2)
    bt = jnp.stack([per_group(b_re), per_group(b_im)], axis=2).swapaxes(-1, -2)
    c = jnp.stack([per_group(c_re), per_group(c_im)], axis=2)
    d_t = jnp.tile(d_skip.astype(F32).reshape(n, 1, mm), (1, 1, lc))
    full = lambda *shape: pl.BlockSpec((1,) + shape, lambda i: (i,) + (0,) * len(shape))
    w1, abar, tmat, wout = pl.pallas_call(
        functools.partial(_ssm_weight_kernel, batch),
        grid=(n,),
        in_specs=[full(2, 3, p), full(2, 2, mm, p), full(2, 2, mm, p), full(1, cw)],
        out_specs=[full(batch, cw, sw), full(4, batch * p), full(cw, cw), full(batch, sw, cw)],
        out_shape=[jax.ShapeDtypeStruct((n, batch, cw, sw), BF16), jax.ShapeDtypeStruct((n, 4, batch * p), F32),
                   jax.ShapeDtypeStruct((n, cw, cw), BF16), jax.ShapeDtypeStruct((n, batch, sw, cw), BF16)],
        compiler_params=_cparams(("parallel",)),
        name="ssm_weights",
    )(lam, bt, c, d_t)
    abar = abar.reshape(depth, g, 4, batch * p).transpose(0, 2, 1, 3)
    return (w1.reshape(depth, g, batch, cw, sw), abar, tmat.reshape(depth, g, cw, cw),
            wout.reshape(depth, g, batch, sw, cw))


def _ssm_p1_kernel(ul_ref, uc_ref, w_ref, v_ref, *s_refs):
    b = pl.program_id(1)
    ncc, ncl = uc_ref.shape[0] // SSM_CHUNK, ul_ref.shape[0] // SSM_CHUNK
    nc = ncc + ncl
    pw = s_refs[0].shape[1]
    xs = [jnp.concatenate([uc_ref[pl.ds(s, ncc, stride=SSM_CHUNK), :], ul_ref[pl.ds(s, ncl, stride=SSM_CHUNK), :]],
                          axis=0) for s in range(SSM_CHUNK)]
    slot = lax.broadcasted_iota(jnp.int32, (nc, LANES), 1) // SSM_GROUP
    xs = [x if s % SSM_LANE_GROUPS == 0 else pltpu.roll(x, SSM_GROUP * (s % SSM_LANE_GROUPS), 1)
          for s, x in enumerate(xs)]
    for j in range(SSM_LANE_GROUPS):
        tiles = []
        for h in range(SSM_CW // LANES):
            merged = xs[h * SSM_LANE_GROUPS]
            for ss in range(1, SSM_LANE_GROUPS):
                merged = jnp.where(slot == (j + ss) % SSM_LANE_GROUPS, xs[h * SSM_LANE_GROUPS + ss], merged)
            tiles.append(merged if j == 0 else pltpu.roll(merged, LANES - SSM_GROUP * j, 1))
        v = jnp.concatenate(tiles, axis=1).astype(BF16)
        v_ref[j] = v
        acc = _dot(v, w_ref[j])
        for k, s_ref in enumerate(s_refs):
            part = acc[:, k * pw:(k + 1) * pw]
            rows = slice(j * nc, (j + 1) * nc)

            @pl.when(b == 0)
            def _():
                s_ref[rows, :] = part

            @pl.when(b > 0)
            def _():
                s_ref[rows, :] = s_ref[rows, :] + part


def _ssm_scan_kernel(ncc, ncl, a_ref, sfr_ref, sfi_ref, sbr_ref, sbi_ref, xfr_ref, xfi_ref, xbr_ref, xbi_ref):
    nc = ncc + ncl
    gb = a_ref.shape[1]
    pw = a_ref.shape[2]
    afr, afi, abr, abi = a_ref[0], a_ref[1], a_ref[2], a_ref[3]

    def rows(r):
        return pl.ds(r, gb, stride=nc)

    def step(rf, rb, carry):
        xfr, xfi, xbr, xbi = carry
        xfr_ref[rows(rf), :] = xfr
        xfi_ref[rows(rf), :] = xfi
        xbr_ref[rows(rb), :] = xbr
        xbi_ref[rows(rb), :] = xbi
        sfr = sfr_ref[rows(rf), :]
        sfi = sfi_ref[rows(rf), :]
        sbr = sbr_ref[rows(rb), :]
        sbi = sbi_ref[rows(rb), :]
        return (afr * xfr - afi * xfi + sfr, afr * xfi + afi * xfr + sfi,
                abr * xbr - abi * xbi + sbr, abr * xbi + abi * xbr + sbi)

    zero = jnp.zeros((gb, pw), F32)
    carry = lax.fori_loop(0, ncc, lambda t, c: step(t, ncc - 1 - t, c), (zero, zero, zero, zero), unroll=4)
    lax.fori_loop(0, ncl, lambda t, c: step(ncc + t, nc - 1 - t, c), carry, unroll=4)


def _ssm_p3_kernel(v_ref, xfr_ref, xfi_ref, xbr_ref, xbi_ref, t_ref, wo_ref, zl_ref, zc_ref):
    ncc, ncl = zc_ref.shape[0] // SSM_CHUNK, zl_ref.shape[0] // SSM_CHUNK
    nc = ncc + ncl
    ys = []
    for j in range(SSM_LANE_GROUPS):
        rows = slice(j * nc, (j + 1) * nc)
        xin = jnp.concatenate([xfr_ref[rows, :], xfi_ref[rows, :], xbr_ref[rows, :], xbi_ref[rows, :]],
                              axis=1).astype(BF16)
        y = _dot(v_ref[j], t_ref[j]) + _dot(xin, wo_ref[j])
        ys.append(jax.nn.gelu(y, approximate=True))
    slot = lax.broadcasted_iota(jnp.int32, (nc, LANES), 1) // SSM_GROUP
    for h in range(SSM_CW // LANES):
        rolled = [y[:, h * LANES:(h + 1) * LANES] for y in ys]
        rolled = [y if j == 0 else pltpu.roll(y, SSM_GROUP * j, 1) for j, y in enumerate(rolled)]
        for tt in range(SSM_LANE_GROUPS):
            t = h * SSM_LANE_GROUPS + tt
            merged = rolled[0]
            for j in range(1, SSM_LANE_GROUPS):
                merged = jnp.where(slot == (tt + j) % SSM_LANE_GROUPS, rolled[j], merged)
            zt = merged if tt == 0 else pltpu.roll(merged, LANES - SSM_GROUP * tt, 1)
            zc_ref[pl.ds(t, ncc, stride=SSM_CHUNK), :] = zt[:ncc]
            zl_ref[pl.ds(t, ncl, stride=SSM_CHUNK), :] = zt[ncc:]


def _ssm_mixer(u4, layer, w1, abar, tmat, wout, batch, seq, ctx_len):
    n_lt = u4.shape[0]
    g = N_SSM_GROUPS
    lg = SSM_LANE_GROUPS
    ncc, ncl = ctx_len // SSM_CHUNK, seq // SSM_CHUNK
    nc = ncc + ncl
    cw = SSM_CW
    sw = w1.shape[-1]
    pw = sw // 4
    ctx_blk0 = (batch * seq) // ctx_len
    plane_shape = [jax.ShapeDtypeStruct((g * nc, pw), F32)] * 4
    lat_spec = pl.BlockSpec((None, seq, LANES), lambda k, b: (k, b, 0))
    ctx_spec = pl.BlockSpec((None, ctx_len, LANES), lambda k, b: (k, ctx_blk0 + b, 0))
    v_spec = pl.BlockSpec((lg, None, nc, cw), lambda k, b: (k, b, 0, 0))
    plane_spec = pl.BlockSpec((lg * nc, pw), lambda k, b: (k, 0))
    v_chunks, *s_planes = pl.pallas_call(
        _ssm_p1_kernel,
        grid=(n_lt, batch),
        in_specs=[lat_spec, ctx_spec,
                  pl.BlockSpec((None, lg, None, cw, sw), lambda k, b: (layer, k, b, 0, 0))],
        out_specs=[v_spec] + [plane_spec] * 4,
        out_shape=[jax.ShapeDtypeStruct((g, batch, nc, cw), BF16)] + plane_shape,
        compiler_params=_cparams(("parallel", "arbitrary")),
        name="ssm_chunk_states",
    )(u4, u4, w1)
    gb = SSM_SCAN_GROUPS
    x_planes = pl.pallas_call(
        functools.partial(_ssm_scan_kernel, ncc, ncl),
        grid=(g // gb,),
        in_specs=[pl.BlockSpec((None, 4, gb, pw), lambda i: (layer, 0, i, 0))]
        + [pl.BlockSpec((gb * nc, pw), lambda i: (i, 0))] * 4,
        out_specs=[pl.BlockSpec((gb * nc, pw), lambda i: (i, 0))] * 4,
        out_shape=plane_shape,
        compiler_params=_cparams(("parallel",)),
        name="ssm_chunk_scan",
    )(abar, *s_planes)
    return pl.pallas_call(
        _ssm_p3_kernel,
        grid=(n_lt, batch),
        in_specs=[v_spec] + [plane_spec] * 4
        + [pl.BlockSpec((None, lg, cw, cw), lambda k, b: (layer, k, 0, 0)),
           pl.BlockSpec((None, lg, None, sw, cw), lambda k, b: (layer, k, b, 0, 0))],
        out_specs=[pl.BlockSpec((None, seq, LANES), lambda k, b: (k, b, 0)),
                   pl.BlockSpec((None, ctx_len, LANES), lambda k, b: (k, b, 0))],
        out_shape=[jax.ShapeDtypeStruct((n_lt, batch * seq, LANES), F32),
                   jax.ShapeDtypeStruct((n_lt, batch * ctx_len, LANES), F32)],
        compiler_params=_cparams(("parallel", "arbitrary")),
        name="ssm_chunk_outputs",
    )(v_chunks, *x_planes, tmat, wout)


def _merge_kernel(n_lat_tiles, *refs):
    if n_lat_tiles is None:
        attn_ref, z_ref, x_ref = refs[:3]
        refs = refs[3:]
    else:
        attn_ref, z_ref, x_ref, attn_c_ref, z_c_ref, x_c_ref = refs[:6]
        refs = refs[6:]
        is_ctx = pl.program_id(0) >= n_lat_tiles
    (ga_ref, gs_ref, mod_ref, modl_ref, g2_ref, wglu_ref, wba_ref, wbs_ref, wout_ref, wrh_ref, wrl_ref,
     xo_ref, h2_ref, rt_ref, cnt_ref, x_s) = refs
    step = pl.program_id(0)
    m = mod_ref[0]
    ml = modl_ref[0]
    tile_counts = None

    @pl.when(step == 0)
    def _():
        x_s[...] = jnp.zeros_like(x_s)

    for r in range(MERGE_TILE // MERGE_SUB):
        rows = slice(r * MERGE_SUB, (r + 1) * MERGE_SUB)
        h2 = _rms_mod(x_s[rows, :], g2_ref[...], ml[3:4], ml[4:5])
        hi, lo = _split_bf16(h2)
        h2_ref[rows, :] = hi.astype(F32)
        logits = _dot(hi, wrh_ref[...]) + _dot(lo, wrh_ref[...]) + _dot(hi, wrl_ref[...])
        rt = _route(logits)
        rt_ref[rows, :] = rt
        oh0, oh1 = _expert_onehots(rt)
        part = jnp.sum(oh0 + oh1, axis=0, keepdims=True)
        tile_counts = part if tile_counts is None else tile_counts + part

        zf = jnp.concatenate([z_ref[j, rows, :] for j in range(SSM_LANE_TILES)], axis=1)
        attn = attn_ref[rows, :]
        x_in = x_ref[rows, :]
        if n_lat_tiles is not None:
            zf = jnp.where(is_ctx, jnp.concatenate([z_c_ref[j, rows, :] for j in range(SSM_LANE_TILES)], axis=1), zf)
            attn = jnp.where(is_ctx, attn_c_ref[rows, :], attn)
            x_in = jnp.where(is_ctx, x_c_ref[rows, :], x_in)
        z = zf.astype(BF16)
        glu = (z.astype(F32) * jax.nn.sigmoid(_dot(z, wglu_ref[...]))).astype(BF16)
        mix = (ga_ref[rows, :].astype(F32) * _dot(attn, wba_ref[...])
               + gs_ref[rows, :].astype(F32) * _dot(glu, wbs_ref[...])).astype(BF16)
        x = x_in + m[2:3] * _dot(mix, wout_ref[...])
        xo_ref[rows, :] = x
        x_s[rows, :] = x

    @pl.when(step == 0)
    def _():
        cnt_ref[...] = jnp.zeros_like(cnt_ref)

    cnt_ref[0:1, :] = cnt_ref[0:1, :] + jnp.where(step > 0, tile_counts, 0.0)


def _route(lg):
    ng, epg = N_EXPERT_GROUPS, EXPERTS_PER_GROUP
    lane_i = lax.broadcasted_iota(jnp.int32, lg.shape, 1)
    lane = lane_i.astype(F32)
    big = float(ROUTER_PAD)

    def rmax(mask_val):
        return jnp.max(mask_val, axis=-1, keepdims=True)

    def first_lane(mask, val, mx):
        return jnp.min(jnp.where(mask, jnp.where(val == mx, lane, big), big), axis=-1, keepdims=True)

    gmask = lane_i < ng
    lgm = jnp.where(gmask, lg, NEG_INF)
    mg = rmax(lgm)
    g_prob = 1.0 / jnp.sum(jnp.exp(lgm - mg), axis=-1, keepdims=True)
    g_idx = first_lane(gmask, lg, mg)
    egroup = jnp.floor((lane - float(ng)) * (1.0 / epg))
    emask = egroup == g_idx
    l1 = jnp.where(emask, lg, NEG_INF)
    m1 = rmax(l1)
    i1 = first_lane(emask, lg, m1)
    l2 = jnp.where(lane == i1, NEG_INF, l1)
    m2 = rmax(l2)
    i2 = jnp.min(jnp.where(l2 == m2, jnp.where(emask, lane, big), big), axis=-1, keepdims=True)
    r = jnp.exp(m2 - m1)
    w1 = g_prob / (1.0 + r)
    w2 = w1 * r
    return jnp.where(lane_i == 0, i1 - float(ng),
                     jnp.where(lane_i == 1, i2 - float(ng),
                               jnp.where(lane_i == 2, w1, jnp.where(lane_i == 3, w2, 0.0))))


def _merge(attn, z, x, ctx_parts, ga, gs, mod, g2, w_glu, w_ba, w_bs, w_out, wr_hi, wr_lo, n_rows, t_lat, seq, batch):
    d = D_MODEL
    tile = MERGE_TILE
    assert n_rows % tile == 0 and t_lat % tile == 0 and seq % tile == 0
    n_tiles = n_rows // tile
    n_lat = t_lat // tile
    cur = lambda i: jnp.minimum(i, n_tiles - 1)
    prev = lambda i: jnp.maximum(i - 1, 0)
    row = lambda i: (cur(i), 0)
    row_lag = lambda i: (prev(i), 0)
    lat = lambda i: (jnp.minimum(i, n_lat - 1), 0)
    cxt = lambda i: (jnp.maximum(cur(i) - n_lat, 0), 0)
    const = lambda i: (0, 0)
    mod_of = lambda t: (jnp.minimum(t // (seq // tile), batch), 0, 0)
    modi = lambda i: mod_of(cur(i))
    modi_lag = lambda i: mod_of(prev(i))
    in_specs = [pl.BlockSpec((tile, Q_W), lat),
                pl.BlockSpec((SSM_LANE_TILES, tile, LANES), lambda i: (0,) + lat(i)),
                pl.BlockSpec((tile, d), lat)]
    args = [attn, z, x]
    if ctx_parts is not None:
        in_specs += [pl.BlockSpec((tile, Q_W), cxt),
                     pl.BlockSpec((SSM_LANE_TILES, tile, LANES), lambda i: (0,) + cxt(i)),
                     pl.BlockSpec((tile, d), cxt)]
        args += list(ctx_parts)
    in_specs += [
        pl.BlockSpec((tile, d), row),
        pl.BlockSpec((tile, d), row),
        pl.BlockSpec((1, 6, d), modi),
        pl.BlockSpec((1, 6, d), modi_lag),
        pl.BlockSpec((1, d), const),
        pl.BlockSpec((D_SSM, D_SSM), const),
        pl.BlockSpec((Q_W, d), const),
        pl.BlockSpec((D_SSM, d), const),
        pl.BlockSpec((d, d), const),
        pl.BlockSpec((d, ROUTER_PAD), const),
        pl.BlockSpec((d, ROUTER_PAD), const),
    ]
    args += [ga, gs, mod, mod, g2, w_glu, w_ba, w_bs, w_out, wr_hi, wr_lo]
    return pl.pallas_call(
        functools.partial(_merge_kernel, n_lat if ctx_parts is not None else None),
        grid=(n_tiles + 1,),
        in_specs=in_specs,
        out_specs=[pl.BlockSpec((tile, d), row), pl.BlockSpec((tile, d), row_lag),
                   pl.BlockSpec((tile, ROUTER_PAD), row_lag), pl.BlockSpec((8, ROUTER_PAD), const)],
        out_shape=[jax.ShapeDtypeStruct((n_rows, d), F32), jax.ShapeDtypeStruct((n_rows, d), F32),
                   jax.ShapeDtypeStruct((n_rows, ROUTER_PAD), F32), jax.ShapeDtypeStruct((8, ROUTER_PAD), F32)],
        scratch_shapes=[pltpu.VMEM((tile, d), F32)],
        compiler_params=_cparams(("arbitrary",)),
        name="merge_router",
    )(*args)


def _expert_kernel(layer, be_ref, nu_ref, nx_ref, x_ref, wg_hbm, wu_hbm, wd_hbm, y_ref,
                   wg_f, wu_f, wd_f, wg_s, wu_s, wd_s, slot_s, sems):
    i = pl.program_id(0)
    e = be_ref[i]
    changed = jnp.logical_or(i == 0, e != be_ref[jnp.maximum(i - 1, 0)])

    def weight_copies(expert, slot):
        return [pltpu.make_async_copy(src.at[layer, expert], dst.at[slot], sems.at[slot, j])
                for j, (src, dst) in enumerate(((wg_hbm, wg_f), (wu_hbm, wu_f), (wd_hbm, wd_f)))]

    @pl.when(i == 0)
    def _():
        slot_s[0] = 0
        for cp in weight_copies(e, 0):
            cp.start()

    @pl.when(changed)
    def _():
        slot = slot_s[0]
        for cp in weight_copies(e, slot):
            cp.wait()
        nxt = nx_ref[e]

        @pl.when(nxt >= 0)
        def _():
            for cp in weight_copies(nxt, 1 - slot):
                cp.start()

        wg_s[...] = wg_f[slot].astype(BF16)
        wu_s[...] = wu_f[slot].astype(BF16)
        wd_s[...] = wd_f[slot].astype(BF16)
        slot_s[0] = 1 - slot

    @pl.when(i < nu_ref[0])
    def _():
        x = jnp.concatenate([x_ref[pl.ds(j, MOE_BLOCK, stride=ROW_SUBLANES), :] for j in range(ROW_SUBLANES)],
                            axis=1).astype(BF16)
        gate = _dot(x, wg_s[...])
        up = _dot(x, wu_s[...])
        hid = (gate * jax.nn.sigmoid(gate) * up).astype(BF16)
        y_ref[...] = _dot(hid, wd_s[...]).astype(y_ref.dtype)

    @pl.when(i >= nu_ref[0])
    def _():
        y_ref[...] = jnp.zeros_like(y_ref)


def _experts(blk_exp, n_used, next_exp, xs, layer, w_gate, w_up, w_down):
    d, de = w_gate.shape[-2:]
    assert d == ROW_SUBLANES * LANES
    n_slots = xs.shape[0] // ROW_SUBLANES
    n_blk = n_slots // MOE_BLOCK
    return pl.pallas_call(
        functools.partial(_expert_kernel, layer),
        grid_spec=pltpu.PrefetchScalarGridSpec(
            num_scalar_prefetch=3,
            grid=(n_blk,),
            in_specs=[
                pl.BlockSpec((MOE_BLOCK * ROW_SUBLANES, LANES), lambda i, be, nu, nx: (jnp.minimum(i, nu[0] - 1), 0)),
                pl.BlockSpec(memory_space=pl.ANY),
                pl.BlockSpec(memory_space=pl.ANY),
                pl.BlockSpec(memory_space=pl.ANY),
            ],
            out_specs=pl.BlockSpec((MOE_BLOCK, d), lambda i, be, nu, nx: (i, 0)),
            scratch_shapes=[pltpu.VMEM((2, d, de), F32), pltpu.VMEM((2, d, de), F32), pltpu.VMEM((2, de, d), F32),
                            pltpu.VMEM((d, de), BF16), pltpu.VMEM((d, de), BF16), pltpu.VMEM((de, d), BF16),
                            pltpu.SMEM((1,), jnp.int32), pltpu.SemaphoreType.DMA((2, 3))],
        ),
        out_shape=jax.ShapeDtypeStruct((n_slots, d), BF16),
        compiler_params=_cparams(("arbitrary",)),
        name="expert_mlp",
    )(blk_exp, n_used, next_exp, xs, w_gate, w_up, w_down)


def _expert_onehots(rt):
    lane_f = lax.broadcasted_iota(jnp.int32, rt.shape, 1).astype(F32)
    return jnp.where(lane_f == rt[:, 0:1], 1.0, 0.0), jnp.where(lane_f == rt[:, 1:2], 1.0, 0.0)


def _plan_kernel(rt_ref, cnt_ref, dest_ref, pend_ref, run_s, before_s):
    i = pl.program_id(0)
    rows = rt_ref.shape[0]
    lane = lax.broadcasted_iota(jnp.int32, (rows, ROUTER_PAD), 1)
    oh0, oh1 = _expert_onehots(rt_ref[...])
    tot0 = jnp.sum(oh0, axis=0, keepdims=True)
    tot1 = jnp.sum(oh1, axis=0, keepdims=True)

    @pl.when(i == 0)
    def _():
        counts = cnt_ref[...]
        padded = jnp.floor((counts + float(MOE_BLOCK - 1)) * (1.0 / MOE_BLOCK)) * float(MOE_BLOCK)
        r = lax.broadcasted_iota(jnp.int32, (ROUTER_PAD, ROUTER_PAD), 0)
        c = lax.broadcasted_iota(jnp.int32, (ROUTER_PAD, ROUTER_PAD), 1)
        incl = jnp.where(r <= c, 1.0, 0.0).astype(BF16)
        hi, lo = _split_bf16(padded)
        pend = _dot(hi, incl) + _dot(lo, incl)
        pend_ref[...] = pend
        run_s[...] = pend - padded
        r = lax.broadcasted_iota(jnp.int32, (rows, rows), 0)
        c = lax.broadcasted_iota(jnp.int32, (rows, rows), 1)
        before_s[...] = jnp.where(c < r, 1.0, 0.0).astype(BF16)

    prior = _dot(before_s[...], jnp.concatenate([oh0, oh1], axis=1).astype(BF16))
    run = run_s[0:1, :]
    pos0 = jnp.sum(oh0 * (run + prior[:, :ROUTER_PAD]), axis=-1, keepdims=True)
    pos1 = jnp.sum(oh1 * (run + tot0 + prior[:, ROUTER_PAD:]), axis=-1, keepdims=True)
    run_s[0:1, :] = run + tot0 + tot1
    dest_ref[...] = jnp.where(lane == 0, pos0, jnp.where(lane == 1, pos1, 0.0)).astype(jnp.int32)


def _plan(route, counts, n_rows):
    rows = PLAN_TILE
    return pl.pallas_call(
        _plan_kernel,
        grid=(n_rows // rows,),
        in_specs=[pl.BlockSpec((rows, ROUTER_PAD), lambda i: (i, 0)),
                  pl.BlockSpec((8, ROUTER_PAD), lambda i: (0, 0))],
        out_specs=[pl.BlockSpec((rows, ROUTER_PAD), lambda i: (i, 0)),
                   pl.BlockSpec((8, ROUTER_PAD), lambda i: (0, 0))],
        out_shape=[jax.ShapeDtypeStruct((n_rows, ROUTER_PAD), jnp.int32),
                   jax.ShapeDtypeStruct((8, ROUTER_PAD), F32)],
        scratch_shapes=[pltpu.VMEM((8, ROUTER_PAD), F32), pltpu.VMEM((rows, rows), BF16)],
        compiler_params=_cparams(("arbitrary",)),
        name="dispatch_plan",
    )(route, counts)


def _dispatch_kernel(pend_ref, dest_ref, h_ref, xs_ref, hs, zero_s, sems, zsem):
    step = pl.program_id(0)
    rows = h_ref.shape[0] // 2
    sub = ROW_SUBLANES
    n_blk = xs_ref.shape[0] // (MOE_BLOCK * sub)
    n_used = pend_ref[N_EXPERTS - 1] // MOE_BLOCK

    def zero_block(start):
        return pltpu.make_async_copy(
            zero_s, xs_ref.at[pl.ds(pl.multiple_of(start * sub, MOE_BLOCK * sub), MOE_BLOCK * sub)], zsem)

    @pl.when(pl.program_id(0) == 0)
    def _():
        zero_s[...] = jnp.zeros_like(zero_s)
        for e in range(N_EXPERTS):
            @pl.when(pend_ref[e] > 0)
            def _():
                zero_block(pend_ref[e] - MOE_BLOCK).start()
        lax.fori_loop(n_used, n_blk, lambda j, c: (zero_block(j * MOE_BLOCK).start(), c)[1], 0)
        for e in range(N_EXPERTS):
            @pl.when(pend_ref[e] > 0)
            def _():
                zero_block(pend_ref[e] - MOE_BLOCK).wait()
        lax.fori_loop(n_used, n_blk, lambda j, c: (zero_block(j * MOE_BLOCK).wait(), c)[1], 0)

    def row_copy(half, r, d):
        return pltpu.make_async_copy(hs.at[half, pl.ds(pl.multiple_of(r * sub, sub), sub)],
                                     xs_ref.at[pl.ds(pl.multiple_of(d * sub, sub), sub)], sems.at[half])

    def drain(half):
        for _ in range(TOP_K * rows):
            row_copy(half, 0, 0).wait()

    for half in range(2):
        @pl.when(step > 0)
        def _():
            drain(half)

        for j in range(sub):
            hs[half, pl.ds(j, rows, stride=sub), :] = h_ref[half * rows:(half + 1) * rows, LANES * j:LANES * (j + 1)]

        def issue(r, carry):
            for k in range(TOP_K):
                row_copy(half, r, dest_ref[0, 0, (half * TOP_K + k) * rows + r]).start(priority=k % 2)
            return carry

        lax.fori_loop(0, rows, issue, 0, unroll=8)

    @pl.when(step == pl.num_programs(0) - 1)
    def _():
        drain(0)
        drain(1)


def _dispatch(pend, dest_tiles, h2, n_slots, n_tiles):
    d = h2.shape[1]
    assert n_tiles % 2 == 0
    return pl.pallas_call(
        _dispatch_kernel,
        grid_spec=pltpu.PrefetchScalarGridSpec(
            num_scalar_prefetch=1,
            grid=(n_tiles // 2,),
            in_specs=[pl.BlockSpec((1, 1, 2 * TOP_K * ROW_TILE), lambda i, pe: (i, 0, 0), memory_space=pltpu.SMEM),
                      pl.BlockSpec((2 * ROW_TILE, d), lambda i, pe: (i, 0))],
            out_specs=pl.BlockSpec(memory_space=pl.ANY),
            scratch_shapes=[pltpu.VMEM((2, ROW_TILE * ROW_SUBLANES, LANES), F32),
                            pltpu.VMEM((MOE_BLOCK * ROW_SUBLANES, LANES), F32),
                            pltpu.SemaphoreType.DMA((2,)), pltpu.SemaphoreType.DMA],
        ),
        out_shape=jax.ShapeDtypeStruct((n_slots * ROW_SUBLANES, LANES), F32),
        compiler_params=_cparams(("arbitrary",)),
        name="dispatch_rows",
    )(pend, dest_tiles.reshape(n_tiles // 2, 1, 2 * TOP_K * ROW_TILE), h2)


def _moe(h2, route, counts, n_tiles, layer, w_gate, w_up, w_down):
    t, d = h2.shape
    dest, pend_f = _plan(route, counts, t)
    pend = pend_f[0, :N_EXPERTS].astype(jnp.int32)
    n_blk = -(-(t * TOP_K) // MOE_BLOCK) + N_EXPERTS
    blk_start = jnp.arange(n_blk, dtype=jnp.int32) * MOE_BLOCK
    blk_exp = jnp.minimum(jnp.sum((pend[None, :] <= blk_start[:, None]).astype(jnp.int32), axis=1), N_EXPERTS - 1)
    n_used = (pend[N_EXPERTS - 1] // MOE_BLOCK).reshape(1)
    blk_exp = jnp.where(jnp.arange(n_blk) < n_used[0], blk_exp, blk_exp[n_used[0] - 1])
    has_rows = pend > jnp.concatenate([jnp.zeros((1,), jnp.int32), pend[:-1]])
    e_ids = jnp.arange(N_EXPERTS, dtype=jnp.int32)
    later = jnp.where(jnp.logical_and(has_rows[None, :], e_ids[None, :] > e_ids[:, None]), e_ids[None, :], N_EXPERTS)
    next_exp = jnp.min(later, axis=1)
    next_exp = jnp.where(next_exp < N_EXPERTS, next_exp, -1).astype(jnp.int32)
    dest2 = dest[:, :TOP_K]
    dest_tiles = dest2.reshape(n_tiles, ROW_TILE, TOP_K).transpose(0, 2, 1).reshape(n_tiles, 1, TOP_K * ROW_TILE)
    xs = _dispatch(pend, dest_tiles, h2, n_blk * MOE_BLOCK, n_tiles)
    ys = _experts(blk_exp.astype(jnp.int32), n_used, next_exp, xs, layer, w_gate, w_up, w_down)
    return ys[dest2[:, 0]], ys[dest2[:, 1]]


def _final_kernel(x_ref, y0_ref, y1_ref, rt_ref, mod_ref, g_ref, o_ref):
    x = _moe_residual(x_ref, y0_ref, y1_ref, rt_ref, mod_ref)
    o_ref[...] = x * lax.rsqrt(jnp.mean(x * x, axis=-1, keepdims=True) + EPS) * g_ref[...]


def _final(x_lat, moe_out, mod, g_final, n_rows, seq, batch):
    d = D_MODEL
    tile = MERGE_TILE
    assert n_rows % tile == 0 and seq % tile == 0
    row = lambda i: (i, 0)
    return pl.pallas_call(
        _final_kernel,
        grid=(n_rows // tile,),
        in_specs=[pl.BlockSpec((tile, d), row), pl.BlockSpec((tile, d), row),
                  pl.BlockSpec((tile, d), row), pl.BlockSpec((tile, ROUTER_PAD), row),
                  pl.BlockSpec((1, 6, d), lambda i: (jnp.minimum(i // (seq // tile), batch), 0, 0)),
                  pl.BlockSpec((1, d), lambda i: (0, 0))],
        out_specs=pl.BlockSpec((tile, d), row),
        out_shape=jax.ShapeDtypeStruct((n_rows, d), F32),
        compiler_params=_cparams(("parallel",)),
        name="final_norm",
    )(x_lat, *moe_out, mod, g_final)


def _rope_tables(seq):
    quarter = HEAD_DIM // 4
    freqs = ROPE_THETA ** (-jnp.arange(quarter, dtype=F32) / quarter)
    pos = jnp.arange(seq)
    ang_r = (pos // GRID_W).astype(F32)[:, None] * freqs[None, :]
    ang_c = (pos % GRID_W).astype(F32)[:, None] * freqs[None, :]
    cos = jnp.concatenate([jnp.cos(ang_r)] * 2 + [jnp.cos(ang_c)] * 2, axis=-1)
    sin = jnp.concatenate([-jnp.sin(ang_r), jnp.sin(ang_r), -jnp.sin(ang_c), jnp.sin(ang_c)], axis=-1)
    reps = LANES // HEAD_DIM
    cos = jnp.tile(cos, (1, reps))
    sin = jnp.tile(sin, (1, reps))
    cos = jnp.concatenate([cos, jnp.ones((ROW_TILE, LANES), F32)], axis=0)
    sin = jnp.concatenate([sin, jnp.zeros((ROW_TILE, LANES), F32)], axis=0)
    return cos, sin


def kernel(x, c, ctx, c_ctx, w_mod, b_mod, g_norm1, g_norm2, w_in, attn_sink, ssm_lam_re, ssm_lam_im,
           ssm_log_dt, ssm_b_re, ssm_b_im, ssm_c_re, ssm_c_im, ssm_d, w_glu, w_br_attn, w_br_ssm, w_out,
           w_router_group, w_router_expert, w_exp_gate, w_exp_up, w_exp_down, g_final):
    batch, seq, d = x.shape
    ctx_len = ctx.shape[1]
    depth = w_mod.shape[0]
    assert d == D_MODEL and batch * SSM_STATE == LANES and depth == 2
    assert seq % ROW_TILE == 0 and ctx_len % ROW_TILE == 0 and (batch * seq) % ctx_len == 0
    assert batch + 1 <= MOD_ROWS
    t_lat, t_ctx = batch * seq, batch * ctx_len
    tiles_per_seq = seq // ROW_TILE
    n_lat_tiles = t_lat // ROW_TILE
    n_all_tiles = (t_lat + t_ctx) // ROW_TILE
    nb_per_seq = seq // ATTN_BLOCK
    ncc, ncl = ctx_len // SSM_CHUNK, seq // SSM_CHUNK

    c_rows = jnp.zeros((MOD_ROWS, d), F32).at[:batch].set(c).at[batch].set(c_ctx)
    mod_all = _modulation(c_rows, w_mod, b_mod).reshape(depth, MOD_ROWS, 6, d)
    cos_t, sin_t = _rope_tables(seq)
    ssm_w = _ssm_weights_all(ssm_lam_re, ssm_lam_im, ssm_log_dt, ssm_b_re, ssm_b_im, ssm_c_re, ssm_c_im, ssm_d, batch)
    x_parts = (x.reshape(t_lat, d), ctx.reshape(t_ctx, d))
    f_all = None
    for l in range(depth):
        ctx_out = l < depth - 1
        mod = mod_all[l]
        x_parts, (q, k, v, u, ga, gs) = _inproj(
            x_parts, f_all, mod_all[l - 1] if l > 0 else None, mod, g_norm1[l].reshape(1, d), cos_t, sin_t,
            w_in[l].astype(BF16), n_all_tiles, tiles_per_seq, n_lat_tiles, batch)
        sink = attn_sink[l].astype(F32)
        attn = _attention(sink, q, k, v, t_lat // ATTN_BLOCK, nb_per_seq, t_lat // ctx_len, ctx_len, True, 0)
        z, z_ctx = _ssm_mixer(u, l, *ssm_w, batch, seq, ctx_len)
        ctx_parts = None
        if ctx_out:
            attn_c = _attention(sink, q, k, v, t_ctx // ATTN_BLOCK, ctx_len // ATTN_BLOCK, t_lat // ctx_len,
                                ctx_len, False, t_lat // ATTN_BLOCK)
            ctx_parts = (attn_c, z_ctx, x_parts[1])
        w_r = jnp.zeros((d, ROUTER_PAD), F32)
        w_r = w_r.at[:, :N_EXPERT_GROUPS].set(w_router_group[l])
        w_r = w_r.at[:, N_EXPERT_GROUPS:N_EXPERT_GROUPS + N_EXPERTS].set(w_router_expert[l])
        wr_hi, wr_lo = _split_bf16(w_r)
        n_rows = t_lat + t_ctx if ctx_out else t_lat
        x_all, h2, route, counts = _merge(
            attn, z, x_parts[0], ctx_parts, ga, gs, mod, g_norm2[l].reshape(1, d), w_glu[l].astype(BF16),
            w_br_attn[l].astype(BF16), w_br_ssm[l].astype(BF16), w_out[l].astype(BF16),
            wr_hi, wr_lo, n_rows, t_lat, seq, batch)
        y0, y1 = _moe(h2, route, counts, n_rows // ROW_TILE, l, w_exp_gate, w_exp_up, w_exp_down)
        f_all = (y0, y1, route)
        x_parts = (x_all,)
    out = _final(x_all, f_all, mod_all[depth - 1], g_final.reshape(1, d), t_lat, seq, batch)
    return out.reshape(batch, seq, d)
```

```python
import functools
import math

import jax
import jax.numpy as jnp
from jax import lax
from jax.experimental import pallas as pl
from jax.experimental.pallas import tpu as pltpu

F32 = jnp.float32
BF16 = jnp.bfloat16
LANES = 128

D_MODEL = 1024
GRID_W = 64
N_HEADS = 8
N_KV_HEADS = 2
HEAD_DIM = 64
Q_GROUP = N_HEADS // N_KV_HEADS
ATTN_BLOCK = 128
ATTN_STEP_BLOCKS = 4
ROPE_THETA = 10000.0
D_SSM = D_MODEL // 2
SSM_GROUP = 16
N_SSM_GROUPS = D_SSM // SSM_GROUP
SSM_STATE = 64
N_EXPERT_GROUPS = 4
EXPERTS_PER_GROUP = 8
N_EXPERTS = N_EXPERT_GROUPS * EXPERTS_PER_GROUP
TOP_K = 2
Q_W = N_HEADS * HEAD_DIM
KV_W = N_KV_HEADS * HEAD_DIM
O_K = Q_W
O_V = O_K + KV_W
O_U = O_V + KV_W
O_GA = O_U + D_SSM
O_GS = O_GA + D_MODEL
D_IN = O_GS + D_MODEL
EPS = 1e-6
NEG_INF = -1e30

ROW_TILE = 512
DISPATCH_TILE = 256
MERGE_TILE = 512
MERGE_SUB = 256
SSM_CHUNK = 16
SSM_CW = SSM_CHUNK * SSM_GROUP
SSM_SCAN_GROUPS = 8
SSM_LANE_GROUPS = LANES // SSM_GROUP
SSM_LANE_TILES = D_SSM // LANES
MOE_BLOCK = 256
ROW_SUBLANES = D_MODEL // LANES
PLAN_TILE = 512
ROUTER_PAD = LANES
MOD_ROWS = 8
VMEM_LIMIT = 48 * 1024 * 1024


def _cparams(sem):
    return pltpu.CompilerParams(dimension_semantics=sem, vmem_limit_bytes=VMEM_LIMIT)


def _dot(a, b):
    return jnp.dot(a, b, preferred_element_type=F32)


def _split_bf16(a):
    hi = a.astype(BF16)
    lo = (a - hi.astype(F32)).astype(BF16)
    return hi, lo


def _rms_mod(x, g, shift, scale):
    y = x * lax.rsqrt(jnp.mean(x * x, axis=-1, keepdims=True) + EPS) * g
    return y * (1.0 + scale) + shift


def _mod_kernel(c_ref, w_ref, b_ref, o_ref):
    c = c_ref[...]
    s_hi, s_lo = _split_bf16(c * jax.nn.sigmoid(c))
    w_hi, w_lo = _split_bf16(w_ref[0])
    o_ref[0] = _dot(s_hi, w_hi) + _dot(s_lo, w_hi) + _dot(s_hi, w_lo) + b_ref[0]


def _modulation(c_rows, w_mod, b_mod):
    depth, d, n = w_mod.shape
    nb = n // 4
    return pl.pallas_call(
        _mod_kernel,
        grid=(depth, n // nb),
        in_specs=[
            pl.BlockSpec((MOD_ROWS, d), lambda l, j: (0, 0)),
            pl.BlockSpec((1, d, nb), lambda l, j: (l, 0, j)),
            pl.BlockSpec((1, 1, nb), lambda l, j: (l, 0, j)),
        ],
        out_specs=pl.BlockSpec((1, MOD_ROWS, nb), lambda l, j: (l, 0, j)),
        out_shape=jax.ShapeDtypeStruct((depth, MOD_ROWS, n), F32),
        compiler_params=_cparams(("arbitrary", "arbitrary")),
        name="modulation",
    )(c_rows, w_mod, b_mod.reshape(depth, 1, n))


def _moe_residual(x_ref, y0_ref, y1_ref, rt_ref, mod_ref):
    rt = rt_ref[...]
    f = rt[:, 2:3] * y0_ref[...].astype(F32) + rt[:, 3:4] * y1_ref[...].astype(F32)
    return x_ref[...] + mod_ref[0, 5:6, :] * f


def _inproj_kernel(has_f, n_lat_tiles, *refs):
    if has_f:
        (x_ref, y0_ref, y1_ref, rt_ref, modp_ref, mod_ref, g_ref, cos_ref, sin_ref, w_ref,
         xo_ref, q_ref, k_ref, v_ref, u_ref, ga_ref, gs_ref) = refs
        x = _moe_residual(x_ref, y0_ref, y1_ref, rt_ref, modp_ref)
        xo_ref[...] = x
    else:
        (xl_ref, xc_ref, mod_ref, g_ref, cos_ref, sin_ref, w_ref,
         q_ref, k_ref, v_ref, u_ref, ga_ref, gs_ref) = refs
        x = jnp.where(pl.program_id(0) >= n_lat_tiles, xc_ref[...], xl_ref[...])
    m = mod_ref[0]
    h = _rms_mod(x, g_ref[...], m[0:1], m[1:2]).astype(BF16)
    cos = cos_ref[...]
    sin = sin_ref[...]
    lane = lax.broadcasted_iota(jnp.int32, cos.shape, 1)
    first = (lane % (HEAD_DIM // 2)) < (HEAD_DIM // 4)

    def rope(t):
        sw = jnp.where(first, pltpu.roll(t, LANES - HEAD_DIM // 4, 1), pltpu.roll(t, HEAD_DIM // 4, 1))
        return t * cos + sw * sin

    def proj(lo, hi):
        return _dot(h, w_ref[:, lo:hi])

    q = proj(0, O_K)
    for j in range(Q_W // LANES):
        q_ref[:, LANES * j:LANES * (j + 1)] = (rope(q[:, LANES * j:LANES * (j + 1)]) * HEAD_DIM ** -0.5).astype(BF16)
    kv = proj(O_K, O_U)
    k_ref[...] = rope(kv[:, :KV_W]).astype(BF16)
    v_ref[...] = kv[:, KV_W:].astype(BF16)
    u = proj(O_U, O_GA)
    for j in range(SSM_LANE_TILES):
        u_ref[j] = u[:, LANES * j:LANES * (j + 1)]
    ga_ref[...] = jax.nn.sigmoid(proj(O_GA, O_GS)).astype(BF16)
    gs_ref[...] = jax.nn.sigmoid(proj(O_GS, D_IN)).astype(BF16)


def _inproj(x_parts, moe_out, mod_prev, mod, g1, cos_t, sin_t, w_in, n_tiles, tiles_per_seq, n_lat_tiles, batch):
    d = D_MODEL
    has_f = moe_out is not None
    row = lambda i: (i, 0)
    modi = lambda i: (jnp.minimum(i // tiles_per_seq, batch), 0, 0)
    const = lambda i: (0, 0)
    ropei = lambda i: (jnp.where(i < n_lat_tiles, i % tiles_per_seq, tiles_per_seq), 0)
    if has_f:
        in_specs = [pl.BlockSpec((ROW_TILE, d), row), pl.BlockSpec((ROW_TILE, d), row),
                    pl.BlockSpec((ROW_TILE, d), row), pl.BlockSpec((ROW_TILE, ROUTER_PAD), row),
                    pl.BlockSpec((1, 6, d), modi)]
        args = [*x_parts, *moe_out, mod_prev]
    else:
        in_specs = [pl.BlockSpec((ROW_TILE, d), lambda i: (jnp.minimum(i, n_lat_tiles - 1), 0)),
                    pl.BlockSpec((ROW_TILE, d), lambda i: (jnp.maximum(i - n_lat_tiles, 0), 0))]
        args = list(x_parts)
    in_specs += [
        pl.BlockSpec((1, 6, d), modi),
        pl.BlockSpec((1, d), const),
        pl.BlockSpec((ROW_TILE, LANES), ropei),
        pl.BlockSpec((ROW_TILE, LANES), ropei),
        pl.BlockSpec((d, D_IN), const),
    ]
    args += [mod, g1, cos_t, sin_t, w_in]
    widths = [Q_W, KV_W, KV_W, D_SSM, D_MODEL, D_MODEL]
    out_specs = [pl.BlockSpec((ROW_TILE, w), row) for w in widths]
    out_shape = [jax.ShapeDtypeStruct((n_tiles * ROW_TILE, w), BF16) for w in widths]
    out_specs[3] = pl.BlockSpec((SSM_LANE_TILES, ROW_TILE, LANES), lambda i: (0, i, 0))
    out_shape[3] = jax.ShapeDtypeStruct((SSM_LANE_TILES, n_tiles * ROW_TILE, LANES), F32)
    if has_f:
        out_specs = [pl.BlockSpec((ROW_TILE, d), row)] + out_specs
        out_shape = [jax.ShapeDtypeStruct((n_tiles * ROW_TILE, d), F32)] + out_shape
    outs = pl.pallas_call(
        functools.partial(_inproj_kernel, has_f, n_lat_tiles),
        grid=(n_tiles,),
        in_specs=in_specs,
        out_specs=out_specs,
        out_shape=out_shape,
        compiler_params=_cparams(("parallel",)),
        name="inproj",
    )(*args)
    if has_f:
        return (outs[0],), outs[1:]
    return x_parts, outs


def _attn_scores(q, k_tiles, v_tiles):
    res = []
    for g in range(N_KV_HEADS):
        gs = slice(g * HEAD_DIM, (g + 1) * HEAD_DIM)
        k_all = jnp.concatenate([t[:, gs] for t in k_tiles], axis=0)
        v_all = jnp.concatenate([t[:, gs] for t in v_tiles], axis=0)
        v_ext = jnp.concatenate([v_all, jnp.ones_like(v_all)], axis=1)
        qg = jnp.concatenate(
            [q[:, (g * Q_GROUP + h) * HEAD_DIM:(g * Q_GROUP + h + 1) * HEAD_DIM] for h in range(Q_GROUP)], axis=0)
        res.append((lax.dot_general(k_all, qg, (((1,), (1,)), ((), ())), preferred_element_type=F32), v_ext))
    return res


def _attn_finish(sink_ref, scores, tile_rows, biases, blk):
    outs = []
    for g, (st, v_ext) in enumerate(scores):
        tiles, row = [], 0
        for n, bias in zip(tile_rows, biases):
            t = st[row:row + n, :]
            tiles.append(t if bias is None else t + jnp.concatenate([bias] * Q_GROUP, axis=1))
            row += n
        mx = None
        for t in tiles:
            for r0 in range(0, t.shape[0], blk):
                mx = t[r0:r0 + blk] if mx is None else jnp.maximum(mx, t[r0:r0 + blk])
        sink = jnp.concatenate([jnp.full((1, blk), sink_ref[g * Q_GROUP + h], F32) for h in range(Q_GROUP)], axis=1)
        m = jnp.maximum(jnp.max(mx, axis=0, keepdims=True), sink)
        p = jnp.exp(jnp.concatenate([(t - m).astype(BF16) for t in tiles], axis=0))
        o_t = lax.dot_general(v_ext, p, (((0,), (0,)), ((), ())), preferred_element_type=F32)
        denom = o_t[HEAD_DIM:HEAD_DIM + 1, :] + jnp.exp(sink - m)
        o_n = o_t[:HEAD_DIM, :] * (1.0 / denom)
        for h in range(Q_GROUP):
            outs.append(o_n[:, h * blk:(h + 1) * blk].T.astype(BF16))
    return jnp.concatenate(outs, axis=1)


def _attn_kernel(band, nb_per_seq, sink_ref, *refs):
    blk = ATTN_BLOCK
    if not band:
        q_ref, kx_ref, vx_ref, o_ref = refs
        n_sub = q_ref.shape[0] // blk
        scores = [_attn_scores(q_ref[a * blk:(a + 1) * blk, :], [kx_ref[...]], [vx_ref[...]]) for a in range(n_sub)]
        for a in range(n_sub):
            o_ref[a * blk:(a + 1) * blk, :] = _attn_finish(sink_ref, scores[a], [kx_ref.shape[0]], [None], blk)
        return
    q_ref, kp_ref, kc_ref, kn_ref, vp_ref, vc_ref, vn_ref, kx_ref, vx_ref, o_ref = refs
    n_sub = q_ref.shape[0] // blk
    j0 = (pl.program_id(0) * n_sub) % nb_per_seq
    r = lax.broadcasted_iota(jnp.int32, (blk, blk), 0)
    c = lax.broadcasted_iota(jnp.int32, (blk, blk), 1)
    k_blocks = [kp_ref[...]] + [kc_ref[a * blk:(a + 1) * blk, :] for a in range(n_sub)] + [kn_ref[...]]
    v_blocks = [vp_ref[...]] + [vc_ref[a * blk:(a + 1) * blk, :] for a in range(n_sub)] + [vn_ref[...]]
    scores = [_attn_scores(q_ref[a * blk:(a + 1) * blk, :], k_blocks[a:a + 3] + [kx_ref[...]],
                           v_blocks[a:a + 3] + [vx_ref[...]]) for a in range(n_sub)]
    for a in range(n_sub):
        edge_p = jnp.where(j0 + a > 0, 0.0, NEG_INF).astype(F32)
        edge_n = jnp.where(j0 + a < nb_per_seq - 1, 0.0, NEG_INF).astype(F32)
        bias_p = jnp.where(r >= c, edge_p, NEG_INF).astype(F32)
        bias_n = jnp.where(r <= c, edge_n, NEG_INF).astype(F32)
        o_ref[a * blk:(a + 1) * blk, :] = _attn_finish(
            sink_ref, scores[a], [blk, blk, blk, kx_ref.shape[0]], [bias_p, None, bias_n, None], blk)


def _attention(sink, q, k, v, n_blocks, nb_per_seq, kx_block0, ctx_len, band, q_block0):
    blk = ATTN_BLOCK
    n_sub = math.gcd(ATTN_STEP_BLOCKS, nb_per_seq, n_blocks, q_block0)
    step = n_sub * blk
    n_steps = n_blocks // n_sub
    qi = lambda i, s: (q_block0 // n_sub + i, 0)
    cur = lambda i, s: (i, 0)
    prv = lambda i, s: (jnp.maximum(i * n_sub - 1, 0), 0)
    nxt = lambda i, s: (jnp.minimum((i + 1) * n_sub, n_blocks - 1), 0)
    kxi = lambda i, s: (kx_block0 + (i * n_sub) // nb_per_seq, 0)
    kspec = lambda f: pl.BlockSpec((blk, KV_W), f)
    cspec = pl.BlockSpec((step, KV_W), cur)
    xspec = pl.BlockSpec((ctx_len, KV_W), kxi)
    if band:
        in_specs = [pl.BlockSpec((step, Q_W), qi), kspec(prv), cspec, kspec(nxt),
                    kspec(prv), cspec, kspec(nxt), xspec, xspec]
        args = (q, k, k, k, v, v, v, k, v)
    else:
        in_specs = [pl.BlockSpec((step, Q_W), qi), xspec, xspec]
        args = (q, k, v)
    return pl.pallas_call(
        functools.partial(_attn_kernel, band, nb_per_seq),
        grid_spec=pltpu.PrefetchScalarGridSpec(
            num_scalar_prefetch=1,
            grid=(n_steps,),
            in_specs=in_specs,
            out_specs=pl.BlockSpec((step, Q_W), lambda i, s: (i, 0)),
        ),
        out_shape=jax.ShapeDtypeStruct((n_blocks * blk, Q_W), BF16),
        compiler_params=_cparams(("parallel",)),
        name="band_attention" if band else "context_attention",
    )(sink, *args)


def _dot_f32(a, b_t):
    a_hi, a_lo = _split_bf16(a)
    b_hi, b_lo = _split_bf16(b_t)
    dn = (((1,), (1,)), ((), ()))
    dg = functools.partial(lax.dot_general, dimension_numbers=dn, preferred_element_type=F32)
    return dg(a_hi, b_hi) + dg(a_lo, b_hi) + dg(a_hi, b_lo)


def _ssm_weight_kernel(batch, lam_r_ref, bt_ref, c_ref, d_ref, w1_ref, abar_ref, t_ref, wo_ref):
    lc, mm, p = SSM_CHUNK, SSM_GROUP, SSM_STATE
    up_r = lax.broadcasted_iota(jnp.int32, (lc, 1), 0).astype(F32)
    lane = lax.broadcasted_iota(jnp.int32, (mm, lc * mm), 1)
    row = lax.broadcasted_iota(jnp.int32, (mm, lc * mm), 0)
    planes_in, planes_out, abar_rows, kt = [], [], [], []
    for d in range(2):
        lr, li, dt = lam_r_ref[0, d, 0:1, :], lam_r_ref[0, d, 1:2, :], jnp.exp(lam_r_ref[0, d, 2:3, :])

        def powers(expo):
            mag = jnp.exp(lr * dt * expo)
            return mag * jnp.cos(li * dt * expo), mag * jnp.sin(li * dt * expo)

        a_re, a_im = powers(jnp.ones((1, 1), F32))
        den = lr * lr + li * li
        nr = a_re - 1.0
        f_re = (nr * lr + a_im * li) / den
        f_im = (a_im * lr - nr * li) / den
        bt_re, bt_im = bt_ref[0, d, 0], bt_ref[0, d, 1]
        bbt_re = f_re * bt_re - f_im * bt_im
        bbt_im = f_re * bt_im + f_im * bt_re
        c_re, c_im = c_ref[0, d, 0], c_ref[0, d, 1]

        pr, pi = powers(lc - 1.0 - up_r if d == 0 else up_r)
        planes_in.append(jnp.concatenate(
            [pr[s:s + 1] * bbt_re - pi[s:s + 1] * bbt_im for s in range(lc)], axis=0))
        planes_in.append(jnp.concatenate(
            [pr[s:s + 1] * bbt_im + pi[s:s + 1] * bbt_re for s in range(lc)], axis=0))

        qr, qi = powers(up_r + 1.0 if d == 0 else lc - up_r)
        planes_out.append(jnp.concatenate(
            [c_re * qr[t:t + 1] - c_im * qi[t:t + 1] for t in range(lc)], axis=0).T)
        planes_out.append(jnp.concatenate(
            [-c_re * qi[t:t + 1] - c_im * qr[t:t + 1] for t in range(lc)], axis=0).T)

        kr, ki = powers(up_r if d == 0 else lc - 1.0 - up_r)
        cp_re = jnp.concatenate([c_re * kr[j:j + 1] - c_im * ki[j:j + 1] for j in range(lc)], axis=0)
        cp_im = jnp.concatenate([c_re * ki[j:j + 1] + c_im * kr[j:j + 1] for j in range(lc)], axis=0)
        kt.append(_dot_f32(jnp.concatenate([bbt_re, bbt_im], axis=1), jnp.concatenate([cp_re, -cp_im], axis=1)))

        e_re, e_im = powers(jnp.full((1, 1), float(lc), F32))
        abar_rows += [jnp.concatenate([e_re] * batch, axis=1), jnp.concatenate([e_im] * batch, axis=1)]

    dvec = d_ref[0]
    blocks = []
    for s in range(lc):
        fwd = kt[0] if s == 0 else pltpu.roll(kt[0], mm * s, 1)
        bwd = kt[1] if s == lc - 1 else pltpu.roll(kt[1], mm * (s + 1), 1)
        blk = jnp.where(lane >= mm * s, fwd, 0.0) + jnp.where(lane < mm * (s + 1), bwd, 0.0)
        blocks.append(blk + jnp.where(lane == mm * s + row, dvec, 0.0))
    t_ref[0] = jnp.concatenate(blocks, axis=0).astype(BF16)
    abar_ref[0] = jnp.concatenate(abar_rows, axis=0)
    zero_in = jnp.zeros((lc * mm, p), F32)
    zero_out = jnp.zeros((p, lc * mm), F32)
    for b in range(batch):
        w1_ref[0, b] = jnp.concatenate(
            [pl_ if bb == b else zero_in for pl_ in planes_in for bb in range(batch)], axis=1).astype(BF16)
        wo_ref[0, b] = jnp.concatenate(
            [pl_ if bb == b else zero_out for pl_ in planes_out for bb in range(batch)], axis=0).astype(BF16)


def _ssm_weights_all(lam_re, lam_im, log_dt, b_re, b_im, c_re, c_im, d_skip, batch):
    depth = lam_re.shape[0]
    g, p, mm, lc = N_SSM_GROUPS, SSM_STATE, SSM_GROUP, SSM_CHUNK
    n = depth * g
    cw, sw = lc * mm, 4 * batch * p

    def per_group(a):
        return jnp.moveaxis(a.astype(F32), 2, 1).reshape((n, 2) + a.shape[3:])

    lam = jnp.stack([per_group(lam_re), per_group(lam_im),
                     jnp.broadcast_to(per_group(log_dt)[..., None], (n, 2, p))], axis=2)
    bt = jnp.stack([per_group(b_re), per_group(b_im)], axis=2).swapaxes(-1, -2)
    c = jnp.stack([per_group(c_re), per_group(c_im)], axis=2)
    d_t = jnp.tile(d_skip.astype(F32).reshape(n, 1, mm), (1, 1, lc))
    full = lambda *shape: pl.BlockSpec((1,) + shape, lambda i: (i,) + (0,) * len(shape))
    w1, abar, tmat, wout = pl.pallas_call(
        functools.partial(_ssm_weight_kernel, batch),
        grid=(n,),
        in_specs=[full(2, 3, p), full(2, 2, mm, p), full(2, 2, mm, p), full(1, cw)],
        out_specs=[full(batch, cw, sw), full(4, batch * p), full(cw, cw), full(batch, sw, cw)],
        out_shape=[jax.ShapeDtypeStruct((n, batch, cw, sw), BF16), jax.ShapeDtypeStruct((n, 4, batch * p), F32),
                   jax.ShapeDtypeStruct((n, cw, cw), BF16), jax.ShapeDtypeStruct((n, batch, sw, cw), BF16)],
        compiler_params=_cparams(("parallel",)),
        name="ssm_weights",
    )(lam, bt, c, d_t)
    abar = abar.reshape(depth, g, 4, batch * p).transpose(0, 2, 1, 3)
    return (w1.reshape(depth, g, batch, cw, sw), abar, tmat.reshape(depth, g, cw, cw),
            wout.reshape(depth, g, batch, sw, cw))


def _ssm_p1_kernel(ul_ref, uc_ref, w_ref, v_ref, *s_refs):
    b = pl.program_id(1)
    ncc, ncl = uc_ref.shape[0] // SSM_CHUNK, ul_ref.shape[0] // SSM_CHUNK
    nc = ncc + ncl
    pw = s_refs[0].shape[1]
    xs = [jnp.concatenate([uc_ref[pl.ds(s, ncc, stride=SSM_CHUNK), :], ul_ref[pl.ds(s, ncl, stride=SSM_CHUNK), :]],
                          axis=0) for s in range(SSM_CHUNK)]
    slot = lax.broadcasted_iota(jnp.int32, (nc, LANES), 1) // SSM_GROUP
    xs = [x if s % SSM_LANE_GROUPS == 0 else pltpu.roll(x, SSM_GROUP * (s % SSM_LANE_GROUPS), 1)
          for s, x in enumerate(xs)]
    for j in range(SSM_LANE_GROUPS):
        tiles = []
        for h in range(SSM_CW // LANES):
            merged = xs[h * SSM_LANE_GROUPS]
            for ss in range(1, SSM_LANE_GROUPS):
                merged = jnp.where(slot == (j + ss) % SSM_LANE_GROUPS, xs[h * SSM_LANE_GROUPS + ss], merged)
            tiles.append(merged if j == 0 else pltpu.roll(merged, LANES - SSM_GROUP * j, 1))
        v = jnp.concatenate(tiles, axis=1).astype(BF16)
        v_ref[j] = v
        acc = _dot(v, w_ref[j])
        for k, s_ref in enumerate(s_refs):
            part = acc[:, k * pw:(k + 1) * pw]
            rows = slice(j * nc, (j + 1) * nc)

            @pl.when(b == 0)
            def _():
                s_ref[rows, :] = part

            @pl.when(b > 0)
            def _():
                s_ref[rows, :] = s_ref[rows, :] + part


def _ssm_scan_kernel(ncc, ncl, a_ref, sfr_ref, sfi_ref, sbr_ref, sbi_ref, xfr_ref, xfi_ref, xbr_ref, xbi_ref):
    nc = ncc + ncl
    gb = a_ref.shape[1]
    pw = a_ref.shape[2]
    afr, afi, abr, abi = a_ref[0], a_ref[1], a_ref[2], a_ref[3]

    def rows(r):
        return pl.ds(r, gb, stride=nc)

    def step(rf, rb, carry):
        xfr, xfi, xbr, xbi = carry
        xfr_ref[rows(rf), :] = xfr
        xfi_ref[rows(rf), :] = xfi
        xbr_ref[rows(rb), :] = xbr
        xbi_ref[rows(rb), :] = xbi
        sfr = sfr_ref[rows(rf), :]
        sfi = sfi_ref[rows(rf), :]
        sbr = sbr_ref[rows(rb), :]
        sbi = sbi_ref[rows(rb), :]
        return (afr * xfr - afi * xfi + sfr, afr * xfi + afi * xfr + sfi,
                abr * xbr - abi * xbi + sbr, abr * xbi + abi * xbr + sbi)

    zero = jnp.zeros((gb, pw), F32)
    carry = lax.fori_loop(0, ncc, lambda t, c: step(t, ncc - 1 - t, c), (zero, zero, zero, zero), unroll=4)
    lax.fori_loop(0, ncl, lambda t, c: step(ncc + t, nc - 1 - t, c), carry, unroll=4)


def _ssm_p3_kernel(v_ref, xfr_ref, xfi_ref, xbr_ref, xbi_ref, t_ref, wo_ref, zl_ref, zc_ref):
    ncc, ncl = zc_ref.shape[0] // SSM_CHUNK, zl_ref.shape[0] // SSM_CHUNK
    nc = ncc + ncl
    ys = []
    for j in range(SSM_LANE_GROUPS):
        rows = slice(j * nc, (j + 1) * nc)
        xin = jnp.concatenate([xfr_ref[rows, :], xfi_ref[rows, :], xbr_ref[rows, :], xbi_ref[rows, :]],
                              axis=1).astype(BF16)
        y = _dot(v_ref[j], t_ref[j]) + _dot(xin, wo_ref[j])
        ys.append(jax.nn.gelu(y, approximate=True))
    slot = lax.broadcasted_iota(jnp.int32, (nc, LANES), 1) // SSM_GROUP
    for h in range(SSM_CW // LANES):
        rolled = [y[:, h * LANES:(h + 1) * LANES] for y in ys]
        rolled = [y if j == 0 else pltpu.roll(y, SSM_GROUP * j, 1) for j, y in enumerate(rolled)]
        for tt in range(SSM_LANE_GROUPS):
            t = h * SSM_LANE_GROUPS + tt
            merged = rolled[0]
            for j in range(1, SSM_LANE_GROUPS):
                merged = jnp.where(slot == (tt + j) % SSM_LANE_GROUPS, rolled[j], merged)
            zt = merged if tt == 0 else pltpu.roll(merged, LANES - SSM_GROUP * tt, 1)
            zc_ref[pl.ds(t, ncc, stride=SSM_CHUNK), :] = zt[:ncc]
            zl_ref[pl.ds(t, ncl, stride=SSM_CHUNK), :] = zt[ncc:]


def _ssm_mixer(u4, layer, w1, abar, tmat, wout, batch, seq, ctx_len):
    n_lt = u4.shape[0]
    g = N_SSM_GROUPS
    lg = SSM_LANE_GROUPS
    ncc, ncl = ctx_len // SSM_CHUNK, seq // SSM_CHUNK
    nc = ncc + ncl
    cw = SSM_CW
    sw = w1.shape[-1]
    pw = sw // 4
    ctx_blk0 = (batch * seq) // ctx_len
    plane_shape = [jax.ShapeDtypeStruct((g * nc, pw), F32)] * 4
    lat_spec = pl.BlockSpec((None, seq, LANES), lambda k, b: (k, b, 0))
    ctx_spec = pl.BlockSpec((None, ctx_len, LANES), lambda k, b: (k, ctx_blk0 + b, 0))
    v_spec = pl.BlockSpec((lg, None, nc, cw), lambda k, b: (k, b, 0, 0))
    plane_spec = pl.BlockSpec((lg * nc, pw), lambda k, b: (k, 0))
    v_chunks, *s_planes = pl.pallas_call(
        _ssm_p1_kernel,
        grid=(n_lt, batch),
        in_specs=[lat_spec, ctx_spec,
                  pl.BlockSpec((None, lg, None, cw, sw), lambda k, b: (layer, k, b, 0, 0))],
        out_specs=[v_spec] + [plane_spec] * 4,
        out_shape=[jax.ShapeDtypeStruct((g, batch, nc, cw), BF16)] + plane_shape,
        compiler_params=_cparams(("parallel", "arbitrary")),
        name="ssm_chunk_states",
    )(u4, u4, w1)
    gb = SSM_SCAN_GROUPS
    x_planes = pl.pallas_call(
        functools.partial(_ssm_scan_kernel, ncc, ncl),
        grid=(g // gb,),
        in_specs=[pl.BlockSpec((None, 4, gb, pw), lambda i: (layer, 0, i, 0))]
        + [pl.BlockSpec((gb * nc, pw), lambda i: (i, 0))] * 4,
        out_specs=[pl.BlockSpec((gb * nc, pw), lambda i: (i, 0))] * 4,
        out_shape=plane_shape,
        compiler_params=_cparams(("parallel",)),
        name="ssm_chunk_scan",
    )(abar, *s_planes)
    return pl.pallas_call(
        _ssm_p3_kernel,
        grid=(n_lt, batch),
        in_specs=[v_spec] + [plane_spec] * 4
        + [pl.BlockSpec((None, lg, cw, cw), lambda k, b: (layer, k, 0, 0)),
           pl.BlockSpec((None, lg, None, sw, cw), lambda k, b: (layer, k, b, 0, 0))],
        out_specs=[pl.BlockSpec((None, seq, LANES), lambda k, b: (k, b, 0)),
                   pl.BlockSpec((None, ctx_len, LANES), lambda k, b: (k, b, 0))],
        out_shape=[jax.ShapeDtypeStruct((n_lt, batch * seq, LANES), F32),
                   jax.ShapeDtypeStruct((n_lt, batch * ctx_len, LANES), F32)],
        compiler_params=_cparams(("parallel", "arbitrary")),
        name="ssm_chunk_outputs",
    )(v_chunks, *x_planes, tmat, wout)


def _merge_kernel(n_lat_tiles, *refs):
    if n_lat_tiles is None:
        attn_ref, z_ref, x_ref = refs[:3]
        refs = refs[3:]
    else:
        attn_ref, z_ref, x_ref, attn_c_ref, z_c_ref, x_c_ref = refs[:6]
        refs = refs[6:]
        is_ctx = pl.program_id(0) >= n_lat_tiles
    (ga_ref, gs_ref, mod_ref, modl_ref, g2_ref, wglu_ref, wba_ref, wbs_ref, wout_ref, wrh_ref, wrl_ref,
     xo_ref, h2_ref, rt_ref, cnt_ref, x_s) = refs
    step = pl.program_id(0)
    m = mod_ref[0]
    ml = modl_ref[0]
    tile_counts = None

    @pl.when(step == 0)
    def _():
        x_s[...] = jnp.zeros_like(x_s)

    for r in range(MERGE_TILE // MERGE_SUB):
        rows = slice(r * MERGE_SUB, (r + 1) * MERGE_SUB)
        h2 = _rms_mod(x_s[rows, :], g2_ref[...], ml[3:4], ml[4:5])
        hi, lo = _split_bf16(h2)
        h2_ref[rows, :] = hi.astype(F32)
        logits = _dot(hi, wrh_ref[...]) + _dot(lo, wrh_ref[...]) + _dot(hi, wrl_ref[...])
        rt = _route(logits)
        rt_ref[rows, :] = rt
        oh0, oh1 = _expert_onehots(rt)
        part = jnp.sum(oh0 + oh1, axis=0, keepdims=True)
        tile_counts = part if tile_counts is None else tile_counts + part

        zf = jnp.concatenate([z_ref[j, rows, :] for j in range(SSM_LANE_TILES)], axis=1)
        attn = attn_ref[rows, :]
        x_in = x_ref[rows, :]
        if n_lat_tiles is not None:
            zf = jnp.where(is_ctx, jnp.concatenate([z_c_ref[j, rows, :] for j in range(SSM_LANE_TILES)], axis=1), zf)
            attn = jnp.where(is_ctx, attn_c_ref[rows, :], attn)
            x_in = jnp.where(is_ctx, x_c_ref[rows, :], x_in)
        z = zf.astype(BF16)
        glu = (z.astype(F32) * jax.nn.sigmoid(_dot(z, wglu_ref[...]))).astype(BF16)
        mix = (ga_ref[rows, :].astype(F32) * _dot(attn, wba_ref[...])
               + gs_ref[rows, :].astype(F32) * _dot(glu, wbs_ref[...])).astype(BF16)
        x = x_in + m[2:3] * _dot(mix, wout_ref[...])
        xo_ref[rows, :] = x
        x_s[rows, :] = x

    @pl.when(step == 0)
    def _():
        cnt_ref[...] = jnp.zeros_like(cnt_ref)

    cnt_ref[0:1, :] = cnt_ref[0:1, :] + jnp.where(step > 0, tile_counts, 0.0)


def _route(lg):
    ng, epg = N_EXPERT_GROUPS, EXPERTS_PER_GROUP
    lane_i = lax.broadcasted_iota(jnp.int32, lg.shape, 1)
    lane = lane_i.astype(F32)
    big = float(ROUTER_PAD)

    def rmax(mask_val):
        return jnp.max(mask_val, axis=-1, keepdims=True)

    def first_lane(mask, val, mx):
        return jnp.min(jnp.where(mask, jnp.where(val == mx, lane, big), big), axis=-1, keepdims=True)

    gmask = lane_i < ng
    lgm = jnp.where(gmask, lg, NEG_INF)
    mg = rmax(lgm)
    g_prob = 1.0 / jnp.sum(jnp.exp(lgm - mg), axis=-1, keepdims=True)
    g_idx = first_lane(gmask, lg, mg)
    egroup = jnp.floor((lane - float(ng)) * (1.0 / epg))
    emask = egroup == g_idx
    l1 = jnp.where(emask, lg, NEG_INF)
    m1 = rmax(l1)
    i1 = first_lane(emask, lg, m1)
    l2 = jnp.where(lane == i1, NEG_INF, l1)
    m2 = rmax(l2)
    i2 = jnp.min(jnp.where(l2 == m2, jnp.where(emask, lane, big), big), axis=-1, keepdims=True)
    r = jnp.exp(m2 - m1)
    w1 = g_prob / (1.0 + r)
    w2 = w1 * r
    return jnp.where(lane_i == 0, i1 - float(ng),
                     jnp.where(lane_i == 1, i2 - float(ng),
                               jnp.where(lane_i == 2, w1, jnp.where(lane_i == 3, w2, 0.0))))


def _merge(attn, z, x, ctx_parts, ga, gs, mod, g2, w_glu, w_ba, w_bs, w_out, wr_hi, wr_lo, n_rows, t_lat, seq, batch):
    d = D_MODEL
    tile = MERGE_TILE
    assert n_rows % tile == 0 and t_lat % tile == 0 and seq % tile == 0
    n_tiles = n_rows // tile
    n_lat = t_lat // tile
    cur = lambda i: jnp.minimum(i, n_tiles - 1)
    prev = lambda i: jnp.maximum(i - 1, 0)
    row = lambda i: (cur(i), 0)
    row_lag = lambda i: (prev(i), 0)
    lat = lambda i: (jnp.minimum(i, n_lat - 1), 0)
    cxt = lambda i: (jnp.maximum(cur(i) - n_lat, 0), 0)
    const = lambda i: (0, 0)
    mod_of = lambda t: (jnp.minimum(t // (seq // tile), batch), 0, 0)
    modi = lambda i: mod_of(cur(i))
    modi_lag = lambda i: mod_of(prev(i))
    in_specs = [pl.BlockSpec((tile, Q_W), lat),
                pl.BlockSpec((SSM_LANE_TILES, tile, LANES), lambda i: (0,) + lat(i)),
                pl.BlockSpec((tile, d), lat)]
    args = [attn, z, x]
    if ctx_parts is not None:
        in_specs += [pl.BlockSpec((tile, Q_W), cxt),
                     pl.BlockSpec((SSM_LANE_TILES, tile, LANES), lambda i: (0,) + cxt(i)),
                     pl.BlockSpec((tile, d), cxt)]
        args += list(ctx_parts)
    in_specs += [
        pl.BlockSpec((tile, d), row),
        pl.BlockSpec((tile, d), row),
        pl.BlockSpec((1, 6, d), modi),
        pl.BlockSpec((1, 6, d), modi_lag),
        pl.BlockSpec((1, d), const),
        pl.BlockSpec((D_SSM, D_SSM), const),
        pl.BlockSpec((Q_W, d), const),
        pl.BlockSpec((D_SSM, d), const),
        pl.BlockSpec((d, d), const),
        pl.BlockSpec((d, ROUTER_PAD), const),
        pl.BlockSpec((d, ROUTER_PAD), const),
    ]
    args += [ga, gs, mod, mod, g2, w_glu, w_ba, w_bs, w_out, wr_hi, wr_lo]
    return pl.pallas_call(
        functools.partial(_merge_kernel, n_lat if ctx_parts is not None else None),
        grid=(n_tiles + 1,),
        in_specs=in_specs,
        out_specs=[pl.BlockSpec((tile, d), row), pl.BlockSpec((tile, d), row_lag),
                   pl.BlockSpec((tile, ROUTER_PAD), row_lag), pl.BlockSpec((8, ROUTER_PAD), const)],
        out_shape=[jax.ShapeDtypeStruct((n_rows, d), F32), jax.ShapeDtypeStruct((n_rows, d), F32),
                   jax.ShapeDtypeStruct((n_rows, ROUTER_PAD), F32), jax.ShapeDtypeStruct((8, ROUTER_PAD), F32)],
        scratch_shapes=[pltpu.VMEM((tile, d), F32)],
        compiler_params=_cparams(("arbitrary",)),
        name="merge_router",
    )(*args)


def _expert_kernel(layer, be_ref, nu_ref, nx_ref, x_ref, wg_hbm, wu_hbm, wd_hbm, y_ref,
                   wg_f, wu_f, wd_f, wg_s, wu_s, wd_s, slot_s, sems):
    i = pl.program_id(0)
    e = be_ref[i]
    changed = jnp.logical_or(i == 0, e != be_ref[jnp.maximum(i - 1, 0)])

    def weight_copies(expert, slot):
        return [pltpu.make_async_copy(src.at[layer, expert], dst.at[slot], sems.at[slot, j])
                for j, (src, dst) in enumerate(((wg_hbm, wg_f), (wu_hbm, wu_f), (wd_hbm, wd_f)))]

    @pl.when(i == 0)
    def _():
        slot_s[0] = 0
        for cp in weight_copies(e, 0):
            cp.start()

    @pl.when(changed)
    def _():
        slot = slot_s[0]
        for cp in weight_copies(e, slot):
            cp.wait()
        nxt = nx_ref[e]

        @pl.when(nxt >= 0)
        def _():
            for cp in weight_copies(nxt, 1 - slot):
                cp.start()

        wg_s[...] = wg_f[slot].astype(BF16)
        wu_s[...] = wu_f[slot].astype(BF16)
        wd_s[...] = wd_f[slot].astype(BF16)
        slot_s[0] = 1 - slot

    @pl.when(i < nu_ref[0])
    def _():
        x = jnp.concatenate([x_ref[pl.ds(j, MOE_BLOCK, stride=ROW_SUBLANES), :] for j in range(ROW_SUBLANES)],
                            axis=1).astype(BF16)
        gate = _dot(x, wg_s[...])
        up = _dot(x, wu_s[...])
        hid = (gate * jax.nn.sigmoid(gate) * up).astype(BF16)
        y_ref[...] = _dot(hid, wd_s[...]).astype(y_ref.dtype)

    @pl.when(i >= nu_ref[0])
    def _():
        y_ref[...] = jnp.zeros_like(y_ref)


def _experts(blk_exp, n_used, next_exp, xs, layer, w_gate, w_up, w_down):
    d, de = w_gate.shape[-2:]
    assert d == ROW_SUBLANES * LANES
    n_slots = xs.shape[0] // ROW_SUBLANES
    n_blk = n_slots // MOE_BLOCK
    return pl.pallas_call(
        functools.partial(_expert_kernel, layer),
        grid_spec=pltpu.PrefetchScalarGridSpec(
            num_scalar_prefetch=3,
            grid=(n_blk,),
            in_specs=[
                pl.BlockSpec((MOE_BLOCK * ROW_SUBLANES, LANES), lambda i, be, nu, nx: (jnp.minimum(i, nu[0] - 1), 0)),
                pl.BlockSpec(memory_space=pl.ANY),
                pl.BlockSpec(memory_space=pl.ANY),
                pl.BlockSpec(memory_space=pl.ANY),
            ],
            out_specs=pl.BlockSpec((MOE_BLOCK, d), lambda i, be, nu, nx: (i, 0)),
            scratch_shapes=[pltpu.VMEM((2, d, de), F32), pltpu.VMEM((2, d, de), F32), pltpu.VMEM((2, de, d), F32),
                            pltpu.VMEM((d, de), BF16), pltpu.VMEM((d, de), BF16), pltpu.VMEM((de, d), BF16),
                            pltpu.SMEM((1,), jnp.int32), pltpu.SemaphoreType.DMA((2, 3))],
        ),
        out_shape=jax.ShapeDtypeStruct((n_slots, d), BF16),
        compiler_params=_cparams(("arbitrary",)),
        name="expert_mlp",
    )(blk_exp, n_used, next_exp, xs, w_gate, w_up, w_down)


def _expert_onehots(rt):
    lane_f = lax.broadcasted_iota(jnp.int32, rt.shape, 1).astype(F32)
    return jnp.where(lane_f == rt[:, 0:1], 1.0, 0.0), jnp.where(lane_f == rt[:, 1:2], 1.0, 0.0)


def _plan_kernel(rt_ref, cnt_ref, dest_ref, pend_ref, run_s, before_s):
    i = pl.program_id(0)
    rows = rt_ref.shape[0]
    lane = lax.broadcasted_iota(jnp.int32, (rows, ROUTER_PAD), 1)
    oh0, oh1 = _expert_onehots(rt_ref[...])
    tot0 = jnp.sum(oh0, axis=0, keepdims=True)
    tot1 = jnp.sum(oh1, axis=0, keepdims=True)

    @pl.when(i == 0)
    def _():
        counts = cnt_ref[...]
        padded = jnp.floor((counts + float(MOE_BLOCK - 1)) * (1.0 / MOE_BLOCK)) * float(MOE_BLOCK)
        r = lax.broadcasted_iota(jnp.int32, (ROUTER_PAD, ROUTER_PAD), 0)
        c = lax.broadcasted_iota(jnp.int32, (ROUTER_PAD, ROUTER_PAD), 1)
        incl = jnp.where(r <= c, 1.0, 0.0).astype(BF16)
        hi, lo = _split_bf16(padded)
        pend = _dot(hi, incl) + _dot(lo, incl)
        pend_ref[...] = pend
        run_s[...] = pend - padded
        r = lax.broadcasted_iota(jnp.int32, (rows, rows), 0)
        c = lax.broadcasted_iota(jnp.int32, (rows, rows), 1)
        before_s[...] = jnp.where(c < r, 1.0, 0.0).astype(BF16)

    prior = _dot(before_s[...], jnp.concatenate([oh0, oh1], axis=1).astype(BF16))
    run = run_s[0:1, :]
    pos0 = jnp.sum(oh0 * (run + prior[:, :ROUTER_PAD]), axis=-1, keepdims=True)
    pos1 = jnp.sum(oh1 * (run + tot0 + prior[:, ROUTER_PAD:]), axis=-1, keepdims=True)
    run_s[0:1, :] = run + tot0 + tot1
    dest_ref[...] = jnp.where(lane == 0, pos0, jnp.where(lane == 1, pos1, 0.0)).astype(jnp.int32)


def _plan(route, counts, n_rows):
    rows = PLAN_TILE
    return pl.pallas_call(
        _plan_kernel,
        grid=(n_rows // rows,),
        in_specs=[pl.BlockSpec((rows, ROUTER_PAD), lambda i: (i, 0)),
                  pl.BlockSpec((8, ROUTER_PAD), lambda i: (0, 0))],
        out_specs=[pl.BlockSpec((rows, ROUTER_PAD), lambda i: (i, 0)),
                   pl.BlockSpec((8, ROUTER_PAD), lambda i: (0, 0))],
        out_shape=[jax.ShapeDtypeStruct((n_rows, ROUTER_PAD), jnp.int32),
                   jax.ShapeDtypeStruct((8, ROUTER_PAD), F32)],
        scratch_shapes=[pltpu.VMEM((8, ROUTER_PAD), F32), pltpu.VMEM((rows, rows), BF16)],
        compiler_params=_cparams(("arbitrary",)),
        name="dispatch_plan",
    )(route, counts)


def _dispatch_kernel(pend_ref, dest_ref, h_ref, xs_ref, hs, zero_s, sems, zsem):
    step = pl.program_id(0)
    rows = h_ref.shape[0] // 2
    sub = ROW_SUBLANES
    n_blk = xs_ref.shape[0] // (MOE_BLOCK * sub)
    n_used = pend_ref[N_EXPERTS - 1] // MOE_BLOCK

    def zero_block(start):
        return pltpu.make_async_copy(
            zero_s, xs_ref.at[pl.ds(pl.multiple_of(start * sub, MOE_BLOCK * sub), MOE_BLOCK * sub)], zsem)

    @pl.when(pl.program_id(0) == 0)
    def _():
        zero_s[...] = jnp.zeros_like(zero_s)
        for e in range(N_EXPERTS):
            @pl.when(pend_ref[e] > 0)
            def _():
                zero_block(pend_ref[e] - MOE_BLOCK).start()
        lax.fori_loop(n_used, n_blk, lambda j, c: (zero_block(j * MOE_BLOCK).start(), c)[1], 0)
        for e in range(N_EXPERTS):
            @pl.when(pend_ref[e] > 0)
            def _():
                zero_block(pend_ref[e] - MOE_BLOCK).wait()
        lax.fori_loop(n_used, n_blk, lambda j, c: (zero_block(j * MOE_BLOCK).wait(), c)[1], 0)

    def row_copy(half, r, d):
        return pltpu.make_async_copy(hs.at[half, pl.ds(pl.multiple_of(r * sub, sub), sub)],
                                     xs_ref.at[pl.ds(pl.multiple_of(d * sub, sub), sub)], sems.at[half])

    def drain(half):
        for _ in range(TOP_K * rows):
            row_copy(half, 0, 0).wait()

    for half in range(2):
        @pl.when(step > 0)
        def _():
            drain(half)

        for j in range(sub):
            hs[half, pl.ds(j, rows, stride=sub), :] = h_ref[half * rows:(half + 1) * rows, LANES * j:LANES * (j + 1)]

        def issue(r, carry):
            for k in range(TOP_K):
                row_copy(half, r, dest_ref[0, 0, (half * TOP_K + k) * rows + r]).start(priority=k % 2)
            return carry

        lax.fori_loop(0, rows, issue, 0, unroll=8)

    @pl.when(step == pl.num_programs(0) - 1)
    def _():
        drain(0)
        drain(1)


def _dispatch(pend, dest_tiles, h2, n_slots, n_tiles):
    d = h2.shape[1]
    assert n_tiles % 2 == 0
    return pl.pallas_call(
        _dispatch_kernel,
        grid_spec=pltpu.PrefetchScalarGridSpec(
            num_scalar_prefetch=1,
            grid=(n_tiles // 2,),
            in_specs=[pl.BlockSpec((1, 1, 2 * TOP_K * DISPATCH_TILE), lambda i, pe: (i, 0, 0), memory_space=pltpu.SMEM),
                      pl.BlockSpec((2 * DISPATCH_TILE, d), lambda i, pe: (i, 0))],
            out_specs=pl.BlockSpec(memory_space=pl.ANY),
            scratch_shapes=[pltpu.VMEM((2, DISPATCH_TILE * ROW_SUBLANES, LANES), F32),
                            pltpu.VMEM((MOE_BLOCK * ROW_SUBLANES, LANES), F32),
                            pltpu.SemaphoreType.DMA((2,)), pltpu.SemaphoreType.DMA],
        ),
        out_shape=jax.ShapeDtypeStruct((n_slots * ROW_SUBLANES, LANES), F32),
        compiler_params=_cparams(("arbitrary",)),
        name="dispatch_rows",
    )(pend, dest_tiles.reshape(n_tiles // 2, 1, 2 * TOP_K * DISPATCH_TILE), h2)


def _moe(h2, route, counts, n_tiles, layer, w_gate, w_up, w_down):
    t, d = h2.shape
    dest, pend_f = _plan(route, counts, t)
    pend = pend_f[0, :N_EXPERTS].astype(jnp.int32)
    n_blk = -(-(t * TOP_K) // MOE_BLOCK) + N_EXPERTS
    blk_start = jnp.arange(n_blk, dtype=jnp.int32) * MOE_BLOCK
    blk_exp = jnp.minimum(jnp.sum((pend[None, :] <= blk_start[:, None]).astype(jnp.int32), axis=1), N_EXPERTS - 1)
    n_used = (pend[N_EXPERTS - 1] // MOE_BLOCK).reshape(1)
    blk_exp = jnp.where(jnp.arange(n_blk) < n_used[0], blk_exp, blk_exp[n_used[0] - 1])
    has_rows = pend > jnp.concatenate([jnp.zeros((1,), jnp.int32), pend[:-1]])
    e_ids = jnp.arange(N_EXPERTS, dtype=jnp.int32)
    later = jnp.where(jnp.logical_and(has_rows[None, :], e_ids[None, :] > e_ids[:, None]), e_ids[None, :], N_EXPERTS)
    next_exp = jnp.min(later, axis=1)
    next_exp = jnp.where(next_exp < N_EXPERTS, next_exp, -1).astype(jnp.int32)
    dest2 = dest[:, :TOP_K]
    dest_tiles = dest2.reshape(n_tiles, DISPATCH_TILE, TOP_K).transpose(0, 2, 1).reshape(n_tiles, 1, TOP_K * DISPATCH_TILE)
    xs = _dispatch(pend, dest_tiles, h2, n_blk * MOE_BLOCK, n_tiles)
    ys = _experts(blk_exp.astype(jnp.int32), n_used, next_exp, xs, layer, w_gate, w_up, w_down)
    return ys[dest2[:, 0]], ys[dest2[:, 1]]


def _final_kernel(x_ref, y0_ref, y1_ref, rt_ref, mod_ref, g_ref, o_ref):
    x = _moe_residual(x_ref, y0_ref, y1_ref, rt_ref, mod_ref)
    o_ref[...] = x * lax.rsqrt(jnp.mean(x * x, axis=-1, keepdims=True) + EPS) * g_ref[...]


def _final(x_lat, moe_out, mod, g_final, n_rows, seq, batch):
    d = D_MODEL
    tile = MERGE_TILE
    assert n_rows % tile == 0 and seq % tile == 0
    row = lambda i: (i, 0)
    return pl.pallas_call(
        _final_kernel,
        grid=(n_rows // tile,),
        in_specs=[pl.BlockSpec((tile, d), row), pl.BlockSpec((tile, d), row),
                  pl.BlockSpec((tile, d), row), pl.BlockSpec((tile, ROUTER_PAD), row),
                  pl.BlockSpec((1, 6, d), lambda i: (jnp.minimum(i // (seq // tile), batch), 0, 0)),
                  pl.BlockSpec((1, d), lambda i: (0, 0))],
        out_specs=pl.BlockSpec((tile, d), row),
        out_shape=jax.ShapeDtypeStruct((n_rows, d), F32),
        compiler_params=_cparams(("parallel",)),
        name="final_norm",
    )(x_lat, *moe_out, mod, g_final)


def _rope_tables(seq):
    quarter = HEAD_DIM // 4
    freqs = ROPE_THETA ** (-jnp.arange(quarter, dtype=F32) / quarter)
    pos = jnp.arange(seq)
    ang_r = (pos // GRID_W).astype(F32)[:, None] * freqs[None, :]
    ang_c = (pos % GRID_W).astype(F32)[:, None] * freqs[None, :]
    cos = jnp.concatenate([jnp.cos(ang_r)] * 2 + [jnp.cos(ang_c)] * 2, axis=-1)
    sin = jnp.concatenate([-jnp.sin(ang_r), jnp.sin(ang_r), -jnp.sin(ang_c), jnp.sin(ang_c)], axis=-1)
    reps = LANES // HEAD_DIM
    cos = jnp.tile(cos, (1, reps))
    sin = jnp.tile(sin, (1, reps))
    cos = jnp.concatenate([cos, jnp.ones((ROW_TILE, LANES), F32)], axis=0)
    sin = jnp.concatenate([sin, jnp.zeros((ROW_TILE, LANES), F32)], axis=0)
    return cos, sin


def kernel(x, c, ctx, c_ctx, w_mod, b_mod, g_norm1, g_norm2, w_in, attn_sink, ssm_lam_re, ssm_lam_im,
           ssm_log_dt, ssm_b_re, ssm_b_im, ssm_c_re, ssm_c_im, ssm_d, w_glu, w_br_attn, w_br_ssm, w_out,
           w_router_group, w_router_expert, w_exp_gate, w_exp_up, w_exp_down, g_final):
    batch, seq, d = x.shape
    ctx_len = ctx.shape[1]
    depth = w_mod.shape[0]
    assert d == D_MODEL and batch * SSM_STATE == LANES and depth == 2
    assert seq % ROW_TILE == 0 and (batch * ctx_len) % ROW_TILE == 0 and (batch * seq) % ctx_len == 0
    assert batch + 1 <= MOD_ROWS
    t_lat, t_ctx = batch * seq, batch * ctx_len
    tiles_per_seq = seq // ROW_TILE
    n_lat_tiles = t_lat // ROW_TILE
    n_all_tiles = (t_lat + t_ctx) // ROW_TILE
    nb_per_seq = seq // ATTN_BLOCK
    ncc, ncl = ctx_len // SSM_CHUNK, seq // SSM_CHUNK

    c_rows = jnp.zeros((MOD_ROWS, d), F32).at[:batch].set(c).at[batch].set(c_ctx)
    mod_all = _modulation(c_rows, w_mod, b_mod).reshape(depth, MOD_ROWS, 6, d)
    cos_t, sin_t = _rope_tables(seq)
    ssm_w = _ssm_weights_all(ssm_lam_re, ssm_lam_im, ssm_log_dt, ssm_b_re, ssm_b_im, ssm_c_re, ssm_c_im, ssm_d, batch)
    x_parts = (x.reshape(t_lat, d), ctx.reshape(t_ctx, d))
    f_all = None
    for l in range(depth):
        ctx_out = l < depth - 1
        mod = mod_all[l]
        x_parts, (q, k, v, u, ga, gs) = _inproj(
            x_parts, f_all, mod_all[l - 1] if l > 0 else None, mod, g_norm1[l].reshape(1, d), cos_t, sin_t,
            w_in[l].astype(BF16), n_all_tiles, tiles_per_seq, n_lat_tiles, batch)
        sink = attn_sink[l].astype(F32)
        attn = _attention(sink, q, k, v, t_lat // ATTN_BLOCK, nb_per_seq, t_lat // ctx_len, ctx_len, True, 0)
        z, z_ctx = _ssm_mixer(u, l, *ssm_w, batch, seq, ctx_len)
        ctx_parts = None
        if ctx_out:
            attn_c = _attention(sink, q, k, v, t_ctx // ATTN_BLOCK, ctx_len // ATTN_BLOCK, t_lat // ctx_len,
                                ctx_len, False, t_lat // ATTN_BLOCK)
            ctx_parts = (attn_c, z_ctx, x_parts[1])
        w_r = jnp.zeros((d, ROUTER_PAD), F32)
        w_r = w_r.at[:, :N_EXPERT_GROUPS].set(w_router_group[l])
        w_r = w_r.at[:, N_EXPERT_GROUPS:N_EXPERT_GROUPS + N_EXPERTS].set(w_router_expert[l])
        wr_hi, wr_lo = _split_bf16(w_r)
        n_rows = t_lat + t_ctx if ctx_out else t_lat
        x_all, h2, route, counts = _merge(
            attn, z, x_parts[0], ctx_parts, ga, gs, mod, g_norm2[l].reshape(1, d), w_glu[l].astype(BF16),
            w_br_attn[l].astype(BF16), w_br_ssm[l].astype(BF16), w_out[l].astype(BF16),
            wr_hi, wr_lo, n_rows, t_lat, seq, batch)
        y0, y1 = _moe(h2, route, counts, n_rows // DISPATCH_TILE, l, w_exp_gate, w_exp_up, w_exp_down)
        f_all = (y0, y1, route)
        x_parts = (x_all,)
    out = _final(x_all, f_all, mod_all[depth - 1], g_final.reshape(1, d), t_lat, seq, batch)
    return out.reshape(batch, seq, d)
```
